```python
import jax, jax.numpy as jnp
from jax import lax
import numpy as np


D_MODEL = 2048
BATCH = 16
SEQ = 256
DEPTH = 2
DEC_BATCH = 8
DEC_SEQ = 1024
PAST_LEN = 512

GRID_W = 64
HGRN_HEADS = 8
HGRN_DK = 128
HGRN_DV = 128
HGRN_WIDTH = HGRN_HEADS * HGRN_DV
HGRN_CHUNK = 64
CONV_WIDTH = 512
NA_HEADS = 4
NA_HEAD_DIM = 128
NA_WIDTH = NA_HEADS * NA_HEAD_DIM
NA_KH = 8
NA_KW = 16
MIX_WIDTH = HGRN_WIDTH + CONV_WIDTH + NA_WIDTH
IN_PROJ_WIDTH = 5 * HGRN_WIDTH + 3 * CONV_WIDTH + 3 * NA_WIDTH
FFN_HIDDEN = ((8 * D_MODEL + 3 * 256 - 1) // (3 * 256)) * 256
ATTN_BLOCK = 128
EPS = 1e-6

kernel_name = 'hybrid_dit_hgrn2_shortconv_natten'


def rmsnorm(x, w):
    xf = x.astype(jnp.float32)
    y = xf * lax.rsqrt(jnp.mean(xf * xf, axis=-1, keepdims=True) + EPS)
    return (y * w.astype(jnp.float32)).astype(x.dtype)


def modulate(h, shift, scale):
    return h * (1.0 + scale) + shift


def adaln_params(cvec, w_ada_l, b_ada_l):
    mod = jax.nn.silu(cvec) @ w_ada_l + b_ada_l
    return jnp.split(mod[..., None, :], 6, axis=-1)


def split_in_proj(p):
    sizes = [HGRN_WIDTH] * 5 + [CONV_WIDTH] * 3 + [NA_WIDTH] * 3
    bounds = [int(s) for s in np.cumsum(sizes)[:-1]]
    return jnp.split(p, bounds, axis=-1)


def mixer_inputs(x, shift, scale, norm_w, w_in_l):
    h = modulate(rmsnorm(x, norm_w), shift, scale)
    return split_in_proj(h @ w_in_l)


def ffn_residual(x, shift, scale, gate, norm_w, wg, wu, wd):
    h = modulate(rmsnorm(x, norm_w), shift, scale)
    return x + gate * ((jax.nn.silu(h @ wg) * (h @ wu)) @ wd)


def hgrn_lower_bounds(lb_raw):
    p = jax.nn.softmax(lb_raw.astype(jnp.float32), axis=1)
    cp = jnp.cumsum(p, axis=1)
    return cp - cp[:, :1]


def hgrn2_gates(f_raw, lb):
    xf = f_raw.astype(jnp.float32)
    log_f = jnp.logaddexp(jnp.log(lb), jnp.log1p(-lb) + jax.nn.log_sigmoid(xf))
    k = (1.0 - lb) * jax.nn.sigmoid(-xf)
    return log_f, k


def hgrn2_chunk_scan(q, k, v, log_f, s0):
    B, N, H, K = q.shape
    nc = N // HGRN_CHUNK

    def to_chunks(t):
        return t.astype(jnp.float32).reshape(B, nc, HGRN_CHUNK, H, t.shape[-1]).transpose(1, 0, 3, 2, 4)

    causal = jnp.tril(jnp.ones((HGRN_CHUNK, HGRN_CHUNK), dtype=bool))[:, :, None]

    def step(s, inp):
        qc, kc, vc, gc = inp
        b = jnp.cumsum(gc, axis=2)
        b_last = b[:, :, -1:, :]
        o_inter = jnp.einsum('bhik,bhkv->bhiv', qc * jnp.exp(b), s)
        decay = jnp.exp(jnp.where(causal, b[:, :, :, None, :] - b[:, :, None, :, :], -jnp.inf))
        attn = jnp.einsum('bhik,bhjk,bhijk->bhij', qc, kc, decay)
        o = o_inter + jnp.einsum('bhij,bhjv->bhiv', attn, vc)
        s_new = jnp.exp(b_last[:, :, 0, :])[..., None] * s + jnp.einsum('bhjk,bhjv->bhkv', kc * jnp.exp(b_last - b), vc)
        return s_new, o

    s_fin, o = lax.scan(step, s0.astype(jnp.float32), (to_chunks(q), to_chunks(k), to_chunks(v), to_chunks(log_f)))
    o = o.transpose(1, 0, 3, 2, 4).reshape(B, N, H, v.shape[-1])
    return o, s_fin


def hgrn2_bidirectional(hq, hi, hf_f, hf_b, hg, lb_f, lb_b, gnorm_w, s_f0, s_b0):
    B, N, _ = hq.shape

    def heads(t):
        return t.reshape(B, N, HGRN_HEADS, -1)

    def flip(t):
        return jnp.flip(t, axis=1)

    q = heads(hq).astype(jnp.float32) * (HGRN_DK ** -0.5)
    v = heads(hi).astype(jnp.float32)
    lf_f, k_f = hgrn2_gates(heads(hf_f), lb_f.reshape(HGRN_HEADS, HGRN_DK))
    lf_b, k_b = hgrn2_gates(heads(hf_b), lb_b.reshape(HGRN_HEADS, HGRN_DK))
    o_f, s_f = hgrn2_chunk_scan(q, k_f, v, lf_f, s_f0)
    o_b, s_b = hgrn2_chunk_scan(flip(q), flip(k_b), flip(v), flip(lf_b), s_b0)
    o = o_f + flip(o_b)
    o = rmsnorm(o, gnorm_w) * jax.nn.silu(heads(hg).astype(jnp.float32))
    return o.reshape(B, N, HGRN_WIDTH).astype(hq.dtype), s_f, s_b


def short_conv_mixer(b_gate, c_gate, xc, conv_w_l):
    u = c_gate * xc
    up = jnp.pad(u, ((0, 0), (1, 1), (0, 0)))
    y = up[:, :-2] * conv_w_l[0] + up[:, 1:-1] * conv_w_l[1] + up[:, 2:] * conv_w_l[2]
    return b_gate * y


def context_self_attention(q, k, v):
    B, L, H, Dh = q.shape
    nb = L // ATTN_BLOCK
    scale = Dh ** -0.5
    qb = q.reshape(B, nb, ATTN_BLOCK, H, Dh).transpose(1, 0, 2, 3, 4)

    def blk(qi):
        s = jnp.einsum('bqhd,blhd->bhql', qi, k).astype(jnp.float32) * scale
        p = jax.nn.softmax(s, axis=-1).astype(v.dtype)
        return jnp.einsum('bhql,blhd->bqhd', p, v)

    o = lax.map(blk, qb)
    return o.transpose(1, 0, 2, 3, 4).reshape(B, L, H * Dh)


def latent_neighbourhood_attention(q, k, v, k_ctx, v_ctx, rpb_l):
    B, N, H, Dh = q.shape
    rows = N // GRID_W
    kh = min(NA_KH, rows)
    scale = Dh ** -0.5
    row_start = np.clip(np.arange(rows) - kh // 2, 0, rows - kh).astype(np.int32)
    col = np.arange(GRID_W)
    col_start = np.clip(col - NA_KW // 2, 0, GRID_W - NA_KW)
    col_mask = (col[None, :] >= col_start[:, None]) & (col[None, :] < col_start[:, None] + NA_KW)
    dc_idx = np.clip(col[None, :] - col[:, None] + NA_KW - 1, 0, 2 * NA_KW - 2)
    col_mask_j = jnp.asarray(col_mask)[:, None, :]
    k_grid = k.reshape(B, rows, GRID_W, H, Dh)
    v_grid = v.reshape(B, rows, GRID_W, H, Dh)
    q_rows = q.reshape(B, rows, GRID_W, H, Dh).transpose(1, 0, 2, 3, 4)
    n_loc = kh * GRID_W

    def row_fn(inp):
        q_r, rs, r = inp
        k_band = lax.dynamic_slice_in_dim(k_grid, rs, kh, axis=1)
        v_band = lax.dynamic_slice_in_dim(v_grid, rs, kh, axis=1)
        dr_idx = rs + jnp.arange(kh, dtype=jnp.int32) - r + (NA_KH - 1)
        bias = rpb_l[:, dr_idx][:, :, dc_idx].transpose(0, 2, 1, 3)
        s_loc = jnp.einsum('bqhd,bkwhd->bhqkw', q_r, k_band).astype(jnp.float32) * scale + bias.astype(jnp.float32)
        s_loc = jnp.where(col_mask_j, s_loc, -jnp.inf)
        s_ctx = jnp.einsum('bqhd,blhd->bhql', q_r, k_ctx).astype(jnp.float32) * scale
        s = jnp.concatenate([s_loc.reshape(B, H, GRID_W, n_loc), s_ctx], axis=-1)
        p = jax.nn.softmax(s, axis=-1).astype(v.dtype)
        p_loc = p[..., :n_loc].reshape(B, H, GRID_W, kh, GRID_W)
        p_ctx = p[..., n_loc:]
        return jnp.einsum('bhqkw,bkwhd->bqhd', p_loc, v_band) + jnp.einsum('bhql,blhd->bqhd', p_ctx, v_ctx)

    o = lax.map(row_fn, (q_rows, jnp.asarray(row_start), jnp.arange(rows, dtype=jnp.int32)))
    return o.transpose(1, 0, 2, 3, 4).reshape(B, N, H * Dh)


def context_layer(x, mods, norm_mix_w_l, w_in_l, lb_f, lb_b, gnorm_w_l, conv_w_l, rpb_unused_free, w_out_l,
                  norm_ffn_w_l, wg, wu, wd):
    sh1, sc1, g1, sh2, sc2, g2 = mods
    hq, hi, hff, hfb, hg, cb, cc, cx, nq, nk, nv = mixer_inputs(x, sh1, sc1, norm_mix_w_l, w_in_l)
    B, L, _ = x.shape
    s0 = jnp.zeros((B, HGRN_HEADS, HGRN_DK, HGRN_DV), jnp.float32)
    o_a, s_f, s_b = hgrn2_bidirectional(hq, hi, hff, hfb, hg, lb_f, lb_b, gnorm_w_l, s0, s0)
    o_b = short_conv_mixer(cb, cc, cx, conv_w_l)
    k_c = nk.reshape(B, L, NA_HEADS, NA_HEAD_DIM)
    v_c = nv.reshape(B, L, NA_HEADS, NA_HEAD_DIM)
    o_c = context_self_attention(nq.reshape(B, L, NA_HEADS, NA_HEAD_DIM), k_c, v_c)
    x = x + g1 * (jnp.concatenate([o_a, o_b, o_c], axis=-1) @ w_out_l)
    x = ffn_residual(x, sh2, sc2, g2, norm_ffn_w_l, wg, wu, wd)
    return x, k_c, v_c, jnp.stack([s_f, s_b], axis=1)


def latent_layer(x, mods, k_ctx, v_ctx, s_f0, s_b0, norm_mix_w_l, w_in_l, lb_f, lb_b, gnorm_w_l, conv_w_l, rpb_l,
                 w_out_l, norm_ffn_w_l, wg, wu, wd):
    sh1, sc1, g1, sh2, sc2, g2 = mods
    hq, hi, hff, hfb, hg, cb, cc, cx, nq, nk, nv = mixer_inputs(x, sh1, sc1, norm_mix_w_l, w_in_l)
    B, N, _ = x.shape
    o_a, _, _ = hgrn2_bidirectional(hq, hi, hff, hfb, hg, lb_f, lb_b, gnorm_w_l, s_f0, s_b0)
    o_b = short_conv_mixer(cb, cc, cx, conv_w_l)

    def heads(t):
        return t.reshape(B, N, NA_HEADS, NA_HEAD_DIM)

    o_c = latent_neighbourhood_attention(heads(nq), heads(nk), heads(nv), k_ctx, v_ctx, rpb_l)
    x = x + g1 * (jnp.concatenate([o_a, o_b, o_c], axis=-1) @ w_out_l)
    return ffn_residual(x, sh2, sc2, g2, norm_ffn_w_l, wg, wu, wd)


def setup_inputs(seed: int = 0) -> dict:
    key = jax.random.key(seed)
    ks = jax.random.split(key, 24)

    def nrm(k, shape, s):
        return jax.random.normal(k, shape, jnp.float32) * s

    return {
        'x_prompt': nrm(ks[0], (BATCH, SEQ, D_MODEL), 1.0),
        'x_sample': nrm(ks[1], (DEC_BATCH, DEC_SEQ, D_MODEL), 1.0),
        'cache_na_k': nrm(ks[2], (DEC_BATCH, DEPTH, PAST_LEN, NA_HEADS, NA_HEAD_DIM), 1.0),
        'cache_na_v': nrm(ks[3], (DEC_BATCH, DEPTH, PAST_LEN, NA_HEADS, NA_HEAD_DIM), 1.0),
        'state_hgrn': nrm(ks[4], (DEC_BATCH, DEPTH, 2, HGRN_HEADS, HGRN_DK, HGRN_DV), 0.5),
        'c': nrm(ks[5], (DEC_BATCH, D_MODEL), 1.0),
        'c_ctx': nrm(ks[6], (D_MODEL,), 1.0),
        'w_ada': nrm(ks[7], (DEPTH, D_MODEL, 6 * D_MODEL), 0.5 * D_MODEL ** -0.5),
        'b_ada': nrm(ks[8], (DEPTH, 6 * D_MODEL), 0.02),
        'norm_mix_w': 1.0 + nrm(ks[9], (DEPTH, D_MODEL), 0.05),
        'w_in': nrm(ks[10], (DEPTH, D_MODEL, IN_PROJ_WIDTH), D_MODEL ** -0.5),
        'hgrn_lb_raw': nrm(ks[11], (2, DEPTH, HGRN_WIDTH), 1.0),
        'hgrn_gnorm_w': 1.0 + nrm(ks[12], (DEPTH, HGRN_DV), 0.05),
        'conv_w': nrm(ks[13], (DEPTH, 3, CONV_WIDTH), 0.5),
        'na_rpb': nrm(ks[14], (DEPTH, NA_HEADS, 2 * NA_KH - 1, 2 * NA_KW - 1), 0.5),
        'w_out': nrm(ks[15], (DEPTH, MIX_WIDTH, D_MODEL), MIX_WIDTH ** -0.5),
        'norm_ffn_w': 1.0 + nrm(ks[16], (DEPTH, D_MODEL), 0.05),
        'w_ffn_gate': nrm(ks[17], (DEPTH, D_MODEL, FFN_HIDDEN), D_MODEL ** -0.5),
        'w_ffn_up': nrm(ks[18], (DEPTH, D_MODEL, FFN_HIDDEN), D_MODEL ** -0.5),
        'w_ffn_down': nrm(ks[19], (DEPTH, FFN_HIDDEN, D_MODEL), FFN_HIDDEN ** -0.5),
        'final_norm_w': 1.0 + nrm(ks[20], (D_MODEL,), 0.05),
    }


def reference(x_prompt, x_sample, cache_na_k, cache_na_v, state_hgrn, c, c_ctx, w_ada, b_ada, norm_mix_w, w_in,
              hgrn_lb_raw, hgrn_gnorm_w, conv_w, na_rpb, w_out, norm_ffn_w, w_ffn_gate, w_ffn_up, w_ffn_down,
              final_norm_w):
    lbs = hgrn_lower_bounds(hgrn_lb_raw)
    xp, xs = x_prompt, x_sample
    new_k, new_v, new_s = [], [], []
    for l in range(DEPTH):
        xp, k_c, v_c, s_c = context_layer(
            xp, adaln_params(c_ctx, w_ada[l], b_ada[l]), norm_mix_w[l], w_in[l], lbs[0, l], lbs[1, l],
            hgrn_gnorm_w[l], conv_w[l], None, w_out[l], norm_ffn_w[l], w_ffn_gate[l], w_ffn_up[l], w_ffn_down[l])
        new_k.append(k_c)
        new_v.append(v_c)
        new_s.append(s_c)
        xs = latent_layer(
            xs, adaln_params(c, w_ada[l], b_ada[l]), cache_na_k[:, l], cache_na_v[:, l], state_hgrn[:, l, 0],
            state_hgrn[:, l, 1], norm_mix_w[l], w_in[l], lbs[0, l], lbs[1, l], hgrn_gnorm_w[l], conv_w[l], na_rpb[l],
            w_out[l], norm_ffn_w[l], w_ffn_gate[l], w_ffn_up[l], w_ffn_down[l])
    y_prompt = rmsnorm(xp, final_norm_w)
    y_sample = rmsnorm(xs, final_norm_w)
    new_na_k = jnp.stack(new_k, axis=1)
    new_na_v = jnp.stack(new_v, axis=1)
    new_hgrn_state = jnp.stack(new_s, axis=1).astype(x_prompt.dtype)
    return (y_prompt, y_sample, new_na_k, new_na_v, new_hgrn_state)
```

```python
import functools

import numpy as np
import jax
import jax.numpy as jnp
from jax import lax
from jax.experimental import pallas as pl
from jax.experimental.pallas import tpu as pltpu

F32 = jnp.float32
BF16 = jnp.bfloat16

D_MODEL = 2048
BATCH = 16
SEQ = 256
DEPTH = 2
DEC_BATCH = 8
DEC_SEQ = 1024
PAST_LEN = 512
GRID_W = 64
HGRN_HEADS = 8
HGRN_DK = 128
HGRN_DV = 128
HGRN_WIDTH = HGRN_HEADS * HGRN_DV
CONV_WIDTH = 512
NA_HEADS = 4
NA_HEAD_DIM = 128
NA_WIDTH = NA_HEADS * NA_HEAD_DIM
NA_KH = 8
NA_KW = 16
MIX_WIDTH = HGRN_WIDTH + CONV_WIDTH + NA_WIDTH
IN_PROJ_WIDTH = 5 * HGRN_WIDTH + 3 * CONV_WIDTH + 3 * NA_WIDTH
FFN_HIDDEN = ((8 * D_MODEL + 3 * 256 - 1) // (3 * 256)) * 256
EPS = 1e-6

NS = DEC_BATCH * DEC_SEQ
NP = BATCH * SEQ
NTOK = NS + NP
MOD_ROWS = 16
CTX_ROW = DEC_BATCH
LANE = 128
SUBLANE = 8
GRID_ROWS = DEC_SEQ // GRID_W

COL_HQ = 0
COL_HI = HGRN_WIDTH // LANE
COL_HFF = 2 * HGRN_WIDTH // LANE
COL_HFB = 3 * HGRN_WIDTH // LANE
COL_HG = 4 * HGRN_WIDTH // LANE
COL_CB = 5 * HGRN_WIDTH // LANE
COL_CC = COL_CB + CONV_WIDTH // LANE
COL_CX = COL_CC + CONV_WIDTH // LANE
COL_NQ = COL_CX + CONV_WIDTH // LANE
COL_NK = COL_NQ + NA_WIDTH // LANE
COL_NV = COL_NK + NA_WIDTH // LANE

HGRN_CHUNK = 64
NEG_BIG = -1e30
VMEM_LIMIT = 56 * 1024 * 1024


def _cparams(sem):
    return pltpu.CompilerParams(dimension_semantics=sem, vmem_limit_bytes=VMEM_LIMIT)


def _dot(a, b):
    return jnp.dot(a, b, preferred_element_type=F32)


def _dot_nt(a, b):
    return lax.dot_general(a, b, (((1,), (1,)), ((), ())), preferred_element_type=F32)


def _dot_tn(a, b):
    return lax.dot_general(a, b, (((0,), (0,)), ((), ())), preferred_element_type=F32)


def _sigmoid(x):
    return 1.0 / (1.0 + jnp.exp(-x))


ADA_TN = 1024


def _adaln_kernel(c_ref, w_ref, b_ref, o_ref):
    c = c_ref[...]
    s = (c * _sigmoid(c)).astype(BF16)
    o_ref[...] = _dot(s, w_ref[...].astype(BF16)) + b_ref[...]


def _adaln(cvecs, w_ada, b_ada):
    n = 6 * D_MODEL
    return pl.pallas_call(
        _adaln_kernel,
        grid=(DEPTH, n // ADA_TN),
        in_specs=[
            pl.BlockSpec((MOD_ROWS, D_MODEL), lambda l, j: (0, 0)),
            pl.BlockSpec((None, D_MODEL, ADA_TN), lambda l, j: (l, 0, j)),
            pl.BlockSpec((None, 1, ADA_TN), lambda l, j: (l, 0, j)),
        ],
        out_specs=pl.BlockSpec((None, MOD_ROWS, ADA_TN), lambda l, j: (l, 0, j)),
        out_shape=jax.ShapeDtypeStruct((DEPTH, MOD_ROWS, n), F32),
        compiler_params=_cparams(("arbitrary", "arbitrary")),
        name="adaln",
    )(cvecs, w_ada, b_ada.reshape(DEPTH, 1, n))


TM = 512
IN_TN = 512
FFN_TF = 512


def _mod_row(i):
    return jnp.minimum((i * TM) // DEC_SEQ, CTX_ROW)


def _mod_spec(layer, chunk, ngrid):
    if ngrid == 1:
        return pl.BlockSpec((None, None, 1, D_MODEL), lambda i: (layer, _mod_row(i), 0, chunk))
    return pl.BlockSpec((None, None, 1, D_MODEL), lambda i, j: (layer, _mod_row(i), 0, chunk))


def _norm_modulate(x, nw, shift, scale):
    ms = jnp.mean(x * x, axis=-1, keepdims=True)
    y = x * lax.rsqrt(ms + EPS) * nw
    return y * (1.0 + scale) + shift


def _inproj_kernel(x_ref, sh_ref, sc_ref, nw_ref, w_ref, o_ref, h_ref):
    @pl.when(pl.program_id(1) == 0)
    def _():
        h_ref[...] = _norm_modulate(x_ref[...], nw_ref[...], sh_ref[...], sc_ref[...]).astype(BF16)

    o_ref[...] = _dot(h_ref[...], w_ref[...])


def _inproj(x, mods4, layer, norm_w, w_bf16):
    return pl.pallas_call(
        _inproj_kernel,
        grid=(NTOK // TM, IN_PROJ_WIDTH // IN_TN),
        in_specs=[
            pl.BlockSpec((TM, D_MODEL), lambda i, j: (i, 0)),
            _mod_spec(layer, 0, 2),
            _mod_spec(layer, 1, 2),
            pl.BlockSpec((1, D_MODEL), lambda i, j: (0, 0)),
            pl.BlockSpec((D_MODEL, IN_TN), lambda i, j: (0, j)),
        ],
        out_specs=pl.BlockSpec((TM, IN_TN), lambda i, j: (i, j)),
        out_shape=jax.ShapeDtypeStruct((NTOK, IN_PROJ_WIDTH), F32),
        scratch_shapes=[pltpu.VMEM((TM, D_MODEL), BF16)],
        compiler_params=_cparams(("arbitrary", "arbitrary")),
        name="inproj",
    )(x, mods4, mods4, norm_w.reshape(1, D_MODEL), w_bf16)


def _outproj_kernel(x_ref, a_ref, b_ref, c_ref, g_ref, w_ref, o_ref):
    acc = _dot(a_ref[...], w_ref[0:HGRN_WIDTH, :])
    acc += _dot(b_ref[...], w_ref[HGRN_WIDTH:HGRN_WIDTH + CONV_WIDTH, :])
    acc += _dot(c_ref[...], w_ref[HGRN_WIDTH + CONV_WIDTH:MIX_WIDTH, :])
    o_ref[...] = x_ref[...] + g_ref[...] * acc


def _outproj(x, o_a, o_b, o_c, mods4, layer, w_bf16):
    return pl.pallas_call(
        _outproj_kernel,
        grid=(NTOK // TM,),
        in_specs=[
            pl.BlockSpec((TM, D_MODEL), lambda i: (i, 0)),
            pl.BlockSpec((TM, HGRN_WIDTH), lambda i: (i, 0)),
            pl.BlockSpec((TM, CONV_WIDTH), lambda i: (i, 0)),
            pl.BlockSpec((TM, NA_WIDTH), lambda i: (i, 0)),
            _mod_spec(layer, 2, 1),
            pl.BlockSpec((MIX_WIDTH, D_MODEL), lambda i: (0, 0)),
        ],
        out_specs=pl.BlockSpec((TM, D_MODEL), lambda i: (i, 0)),
        out_shape=jax.ShapeDtypeStruct((NTOK, D_MODEL), F32),
        compiler_params=_cparams(("arbitrary",)),
        name="outproj",
    )(x, o_a, o_b, o_c, mods4, w_bf16)


def _ffn_kernel(x_ref, sh_ref, sc_ref, g_ref, nw_ref, fw_ref, wg_ref, wu_ref, wd_ref, o_ref, h_ref, acc_ref, *,
                final_norm):
    j = pl.program_id(1)

    @pl.when(j == 0)
    def _():
        h_ref[...] = _norm_modulate(x_ref[...], nw_ref[...], sh_ref[...], sc_ref[...]).astype(BF16)
        acc_ref[...] = jnp.zeros_like(acc_ref)

    h = h_ref[...]
    a = _dot(h, wg_ref[...])
    u = _dot(h, wu_ref[...])
    t = (a * _sigmoid(a) * u).astype(BF16)
    acc_ref[...] += _dot(t, wd_ref[...])

    @pl.when(j == pl.num_programs(1) - 1)
    def _():
        y = x_ref[...] + g_ref[...] * acc_ref[...]
        if final_norm:
            ms = jnp.mean(y * y, axis=-1, keepdims=True)
            y = y * lax.rsqrt(ms + EPS) * fw_ref[...]
        o_ref[...] = y


def _ffn(x, mods4, layer, norm_w, final_w, wg, wu, wd, final_norm):
    return pl.pallas_call(
        functools.partial(_ffn_kernel, final_norm=final_norm),
        grid=(NTOK // TM, FFN_HIDDEN // FFN_TF),
        in_specs=[
            pl.BlockSpec((TM, D_MODEL), lambda i, j: (i, 0)),
            _mod_spec(layer, 3, 2),
            _mod_spec(layer, 4, 2),
            _mod_spec(layer, 5, 2),
            pl.BlockSpec((1, D_MODEL), lambda i, j: (0, 0)),
            pl.BlockSpec((1, D_MODEL), lambda i, j: (0, 0)),
            pl.BlockSpec((D_MODEL, FFN_TF), lambda i, j: (0, j)),
            pl.BlockSpec((D_MODEL, FFN_TF), lambda i, j: (0, j)),
            pl.BlockSpec((FFN_TF, D_MODEL), lambda i, j: (j, 0)),
        ],
        out_specs=pl.BlockSpec((TM, D_MODEL), lambda i, j: (i, 0)),
        out_shape=jax.ShapeDtypeStruct((NTOK, D_MODEL), F32),
        scratch_shapes=[pltpu.VMEM((TM, D_MODEL), BF16), pltpu.VMEM((TM, D_MODEL), F32)],
        compiler_params=_cparams(("arbitrary", "arbitrary")),
        name="ffn",
    )(x, mods4, mods4, mods4, norm_w.reshape(1, D_MODEL), final_w.reshape(1, D_MODEL), wg, wu, wd)


def _hgrn_gates(x, lb):
    e = jnp.exp(-jnp.abs(x))
    ls = jnp.minimum(x, 0.0) - jnp.log1p(e)
    a = jnp.log(jnp.maximum(lb, 1e-37))
    c = jnp.log1p(-lb) + ls
    general = jnp.maximum(a, c) + jnp.log1p(jnp.exp(-jnp.abs(a - c)))
    log_f = jnp.where(lb > 0.0, general, ls)
    k = (1.0 - lb) * jnp.where(x >= 0.0, e, 1.0) / (1.0 + e)
    return log_f, k


def _hgrn_chunk(q, k, v, g, st, b_ref, k_ref, tri, masks, fwd):
    C = HGRN_CHUNK
    g_hi = g.astype(BF16)
    g_lo = (g - g_hi.astype(F32)).astype(BF16)
    b = _dot(tri, g_hi) + _dot(tri, g_lo)
    b_ref[...] = b
    k_ref[...] = k
    b_end = b[C - 1:C, :] if fwd else b[0:1, :]

    o = _dot_nt((q * jnp.exp(b)).astype(BF16), st.astype(BF16))
    kd = (k * jnp.exp(b_end - b)).astype(BF16)
    st_new = jnp.exp(b_end) * st + _dot_tn(v.astype(BF16), kd)

    attn = jnp.zeros((C, C), F32)
    m = SUBLANE
    lvl = 0
    while m < C:
        q_parts, k_parts = [], []
        zero = jnp.zeros((m, LANE), F32)
        for p in range(C // (2 * m)):
            lo = p * 2 * m
            left = slice(lo, lo + m)
            right = slice(lo + m, lo + 2 * m)
            if fwd:
                beta = b_ref[pl.ds(lo + m - 1, 1), :]
                q_parts += [zero, q[right] * jnp.exp(b[right] - beta)]
                k_parts += [k[left] * jnp.exp(beta - b[left]), zero]
            else:
                beta = b_ref[pl.ds(lo + m, 1), :]
                q_parts += [q[left] * jnp.exp(b[left] - beta), zero]
                k_parts += [zero, k[right] * jnp.exp(beta - b[right])]
        q_l = jnp.concatenate(q_parts, axis=0).astype(BF16)
        k_l = jnp.concatenate(k_parts, axis=0).astype(BF16)
        attn += _dot_nt(q_l, k_l) * masks[lvl]
        m *= 2
        lvl += 1
    o += _dot(attn.astype(BF16), v.astype(BF16))

    sub = lax.broadcasted_iota(jnp.int32, (SUBLANE, LANE), 0)
    o_blocks = []
    for u in range(C // SUBLANE):
        rows = slice(u * SUBLANE, (u + 1) * SUBLANE)
        q_u = q[rows]
        b_u = b[rows]
        acc = jnp.zeros((SUBLANE, LANE), F32)
        for jj in range(SUBLANE):
            r = u * SUBLANE + jj
            keep = (sub >= jj) if fwd else (sub <= jj)
            d = jnp.where(keep, b_u - b_ref[pl.ds(r, 1), :], NEG_BIG)
            z = q_u * jnp.exp(d) * k_ref[pl.ds(r, 1), :]
            acc += jnp.sum(z, axis=-1, keepdims=True) * v[r:r + 1, :]
        o_blocks.append(acc)
    o += jnp.concatenate(o_blocks, axis=0)
    return o, st_new


def _hgrn_kernel(*refs, T, has_s0, emit_state):
    q_ref, v_ref, ff_ref, fb_ref, g_ref, lb_ref, gnw_ref = refs[:7]
    pos = 7
    s0_ref = None
    if has_s0:
        s0_ref = refs[pos]
        pos += 1
    o_ref = refs[pos]
    pos += 1
    s_out_ref = None
    if emit_state:
        s_out_ref = refs[pos]
        pos += 1
    oacc_ref, bf_ref, kf_ref, bb_ref, kb_ref = refs[pos:pos + 5]

    C = HGRN_CHUNK
    nc = T // C
    row = lax.broadcasted_iota(jnp.int32, (C, C), 0)
    col = lax.broadcasted_iota(jnp.int32, (C, C), 1)
    tri_f = jnp.where(col <= row, 1.0, 0.0).astype(BF16)
    tri_b = jnp.where(col >= row, 1.0, 0.0).astype(BF16)
    masks = []
    m = SUBLANE
    while m < C:
        shift = int(np.log2(2 * m))
        masks.append(jnp.where((row >> shift) == (col >> shift), 1.0, 0.0).astype(F32))
        m *= 2

    lb_f = lb_ref[0]
    lb_b = lb_ref[1]
    q_scale = HGRN_DK ** -0.5

    oacc_ref[...] = jnp.zeros_like(oacc_ref)
    if has_s0:
        st_f0 = s0_ref[0].T
        st_b0 = s0_ref[1].T
    else:
        st_f0 = jnp.zeros((HGRN_DV, HGRN_DK), F32)
        st_b0 = jnp.zeros((HGRN_DV, HGRN_DK), F32)

    def body(c, carry):
        st_f, st_b = carry
        rf = pl.ds(pl.multiple_of(c * C, C), C)
        rb = pl.ds(pl.multiple_of((nc - 1 - c) * C, C), C)
        g_f, k_f = _hgrn_gates(ff_ref[rf, :], lb_f)
        o_f, st_f = _hgrn_chunk(q_ref[rf, :] * q_scale, k_f, v_ref[rf, :], g_f, st_f, bf_ref, kf_ref, tri_f, masks,
                                True)
        oacc_ref[rf, :] += o_f
        g_b, k_b = _hgrn_gates(fb_ref[rb, :], lb_b)
        o_b, st_b = _hgrn_chunk(q_ref[rb, :] * q_scale, k_b, v_ref[rb, :], g_b, st_b, bb_ref, kb_ref, tri_b, masks,
                                False)
        oacc_ref[rb, :] += o_b
        return st_f, st_b

    st_f, st_b = lax.fori_loop(0, nc, body, (st_f0, st_b0))

    if emit_state:
        s_out_ref[0] = st_f.T
        s_out_ref[1] = st_b.T

    o = oacc_ref[...]
    ms = jnp.mean(o * o, axis=-1, keepdims=True)
    gate = g_ref[...]
    y = o * lax.rsqrt(ms + EPS) * gnw_ref[...] * (gate * _sigmoid(gate))
    o_ref[...] = y.astype(BF16)


def _hgrn(proj, lbs_l, gnorm_w_l, s0, layer, *, T, nseq, row_block0, has_s0, emit_state):
    def col_spec(col0):
        return pl.BlockSpec((T, LANE), lambda b, h: (row_block0 + b, col0 + h))

    in_specs = [col_spec(COL_HQ), col_spec(COL_HI), col_spec(COL_HFF), col_spec(COL_HFB), col_spec(COL_HG),
                pl.BlockSpec((2, None, 1, LANE), lambda b, h: (0, h, 0, 0)),
                pl.BlockSpec((1, LANE), lambda b, h: (0, 0))]
    args = [proj, proj, proj, proj, proj, lbs_l.reshape(2, HGRN_HEADS, 1, LANE), gnorm_w_l.reshape(1, LANE)]
    if has_s0:
        in_specs.append(pl.BlockSpec((None, None, 2, None, HGRN_DK, HGRN_DV), lambda b, h: (b, layer, 0, h, 0, 0)))
        args.append(s0)
    out_specs = [pl.BlockSpec((T, LANE), lambda b, h: (b, h))]
    out_shape = [jax.ShapeDtypeStruct((nseq * T, HGRN_WIDTH), BF16)]
    if emit_state:
        out_specs.append(pl.BlockSpec((None, 2, None, HGRN_DK, HGRN_DV), lambda b, h: (b, 0, h, 0, 0)))
        out_shape.append(jax.ShapeDtypeStruct((nseq, 2, HGRN_HEADS, HGRN_DK, HGRN_DV), F32))
    C = HGRN_CHUNK
    res = pl.pallas_call(
        functools.partial(_hgrn_kernel, T=T, has_s0=has_s0, emit_state=emit_state),
        grid=(nseq, HGRN_HEADS),
        in_specs=in_specs,
        out_specs=out_specs,
        out_shape=out_shape,
        scratch_shapes=[pltpu.VMEM((T, LANE), F32)] + [pltpu.VMEM((C, LANE), F32)] * 4,
        compiler_params=_cparams(("arbitrary", "arbitrary")),
        name="hgrn_ctx" if emit_state else "hgrn_lat",
    )(*args)
    return res


def _conv_kernel(cb_ref, cc_ref, cx_ref, w_ref, o_ref, *, T):
    u = cc_ref[...] * cx_ref[...]
    row = lax.broadcasted_iota(jnp.int32, (T, CONV_WIDTH), 0)
    u_prev = jnp.where(row == 0, 0.0, pltpu.roll(u, 1, axis=0))
    u_next = jnp.where(row == T - 1, 0.0, pltpu.roll(u, T - 1, axis=0))
    y = u_prev * w_ref[0:1, :] + u * w_ref[1:2, :] + u_next * w_ref[2:3, :]
    o_ref[...] = (cb_ref[...] * y).astype(BF16)


def _conv(proj, conv_w_l, *, T, nseq, row_block0):
    cw = CONV_WIDTH // LANE

    def col_spec(col0):
        return pl.BlockSpec((T, CONV_WIDTH), lambda b: (row_block0 + b, col0 // cw))

    return pl.pallas_call(
        functools.partial(_conv_kernel, T=T),
        grid=(nseq,),
        in_specs=[col_spec(COL_CB), col_spec(COL_CC), col_spec(COL_CX),
                  pl.BlockSpec((3, CONV_WIDTH), lambda b: (0, 0))],
        out_specs=pl.BlockSpec((T, CONV_WIDTH), lambda b: (b, 0)),
        out_shape=jax.ShapeDtypeStruct((nseq * T, CONV_WIDTH), BF16),
        compiler_params=_cparams(("arbitrary",)),
        name="conv",
    )(proj, proj, proj, conv_w_l)


def _ctx_attn_kernel(q_ref, k_ref, v_ref, o_ref):
    scale = NA_HEAD_DIM ** -0.5
    s = _dot_nt(q_ref[...].astype(BF16), k_ref[...].astype(BF16)) * scale
    m = jnp.max(s, axis=-1, keepdims=True)
    p = jnp.exp(s - m)
    l = jnp.sum(p, axis=-1, keepdims=True)
    o = _dot(p.astype(BF16), v_ref[...].astype(BF16)) / l
    o_ref[...] = o.astype(BF16)


def _ctx_attn(proj):
    rb0 = NS // SEQ

    def col_spec(col0):
        return pl.BlockSpec((SEQ, LANE), lambda b, h: (rb0 + b, col0 + h))

    return pl.pallas_call(
        _ctx_attn_kernel,
        grid=(BATCH, NA_HEADS),
        in_specs=[col_spec(COL_NQ), col_spec(COL_NK), col_spec(COL_NV)],
        out_specs=pl.BlockSpec((SEQ, LANE), lambda b, h: (b, h)),
        out_shape=jax.ShapeDtypeStruct((NP, NA_WIDTH), BF16),
        compiler_params=_cparams(("arbitrary", "arbitrary")),
        name="ctx_attn",
    )(proj, proj, proj)


_KH = min(NA_KH, GRID_ROWS)
_ROW_START = [int(v) for v in np.clip(np.arange(GRID_ROWS) - _KH // 2, 0, GRID_ROWS - _KH)]
NLOC = _KH * GRID_W


def _nat_kernel(q_ref, k_ref, v_ref, kc_ref, vc_ref, bias_ref, o_ref, sctx_ref, pctx_ref, oloc_ref, den_ref):
    scale = NA_HEAD_DIM ** -0.5
    q_all = q_ref[...].astype(BF16)
    sctx_ref[...] = _dot_nt(q_all, kc_ref[...].astype(BF16)) * scale
    for r in range(GRID_ROWS):
        rows = slice(r * GRID_W, (r + 1) * GRID_W)
        band = slice(_ROW_START[r] * GRID_W, _ROW_START[r] * GRID_W + NLOC)
        q_r = q_ref[rows, :].astype(BF16)
        s_loc = _dot_nt(q_r, k_ref[band, :].astype(BF16)) * scale + bias_ref[r - _ROW_START[r]]
        s_ctx = sctx_ref[rows, :]
        m = jnp.maximum(jnp.max(s_loc, axis=-1, keepdims=True), jnp.max(s_ctx, axis=-1, keepdims=True))
        p_loc = jnp.exp(s_loc - m)
        p_ctx = jnp.exp(s_ctx - m)
        den_ref[rows, :] = jnp.sum(p_loc, axis=-1, keepdims=True) + jnp.sum(p_ctx, axis=-1, keepdims=True)
        pctx_ref[rows, :] = p_ctx.astype(BF16)
        oloc_ref[rows, :] = _dot(p_loc.astype(BF16), v_ref[band, :].astype(BF16))
    o = (oloc_ref[...] + _dot(pctx_ref[...], vc_ref[...].astype(BF16))) / den_ref[...]
    o_ref[...] = o.astype(BF16)


def _nat_bias(rpb_l):
    col = np.arange(GRID_W)
    col_start = np.clip(col - NA_KW // 2, 0, GRID_W - NA_KW)
    col_mask = (col[None, :] >= col_start[:, None]) & (col[None, :] < col_start[:, None] + NA_KW)
    dc_idx = np.clip(col[None, :] - col[:, None] + NA_KW - 1, 0, 2 * NA_KW - 2)
    tab = rpb_l[:, :, dc_idx]
    tab = jnp.where(jnp.asarray(col_mask)[None, None], tab, NEG_BIG)
    slabs = []
    for off in range(_KH):
        dr = np.arange(_KH) - off + (NA_KH - 1)
        slab = tab[:, dr]
        slabs.append(slab.transpose(0, 2, 1, 3).reshape(NA_HEADS, GRID_W, NLOC))
    return jnp.stack(slabs, axis=1)


def _nat(proj, cache_k4, cache_v4, bias, layer):
    def col_spec(col0):
        return pl.BlockSpec((DEC_SEQ, LANE), lambda b, h: (b, col0 + h))

    cache_spec = pl.BlockSpec((None, None, PAST_LEN, LANE), lambda b, h: (b, layer, 0, h))
    return pl.pallas_call(
        _nat_kernel,
        grid=(DEC_BATCH, NA_HEADS),
        in_specs=[col_spec(COL_NQ), col_spec(COL_NK), col_spec(COL_NV), cache_spec, cache_spec,
                  pl.BlockSpec((None, _KH, GRID_W, NLOC), lambda b, h: (h, 0, 0, 0))],
        out_specs=pl.BlockSpec((DEC_SEQ, LANE), lambda b, h: (b, h)),
        out_shape=jax.ShapeDtypeStruct((NS, NA_WIDTH), BF16),
        scratch_shapes=[pltpu.VMEM((DEC_SEQ, PAST_LEN), F32), pltpu.VMEM((DEC_SEQ, PAST_LEN), BF16),
                        pltpu.VMEM((DEC_SEQ, LANE), F32), pltpu.VMEM((DEC_SEQ, 1), F32)],
        compiler_params=_cparams(("arbitrary", "arbitrary")),
        name="nat",
    )(proj, proj, proj, cache_k4, cache_v4, bias)


def kernel(x_prompt, x_sample, cache_na_k, cache_na_v, state_hgrn, c, c_ctx, w_ada, b_ada, norm_mix_w, w_in,
           hgrn_lb_raw, hgrn_gnorm_w, conv_w, na_rpb, w_out, norm_ffn_w, w_ffn_gate, w_ffn_up, w_ffn_down,
           final_norm_w):
    x = jnp.concatenate([x_sample.reshape(NS, D_MODEL), x_prompt.reshape(NP, D_MODEL)], axis=0)
    cvecs = jnp.concatenate([c, c_ctx[None, :], jnp.zeros((MOD_ROWS - DEC_BATCH - 1, D_MODEL), F32)], axis=0)
    mods4 = _adaln(cvecs, w_ada, b_ada).reshape(DEPTH, MOD_ROWS, 1, 6 * D_MODEL)

    p_lb = jax.nn.softmax(hgrn_lb_raw.astype(F32), axis=1)
    cp = jnp.cumsum(p_lb, axis=1)
    lbs = cp - cp[:, :1]

    cache_k4 = cache_na_k.reshape(DEC_BATCH, DEPTH, PAST_LEN, NA_WIDTH)
    cache_v4 = cache_na_v.reshape(DEC_BATCH, DEPTH, PAST_LEN, NA_WIDTH)

    new_k, new_v, new_s = [], [], []
    for l in range(DEPTH):
        proj = _inproj(x, mods4, l, norm_mix_w[l], w_in[l].astype(BF16))
        oa_s = _hgrn(proj, lbs[:, l], hgrn_gnorm_w[l], state_hgrn, l, T=DEC_SEQ, nseq=DEC_BATCH, row_block0=0,
                     has_s0=True, emit_state=False)[0]
        oa_p, s_c = _hgrn(proj, lbs[:, l], hgrn_gnorm_w[l], None, l, T=SEQ, nseq=BATCH, row_block0=NS // SEQ,
                          has_s0=False, emit_state=True)
        ob_s = _conv(proj, conv_w[l], T=DEC_SEQ, nseq=DEC_BATCH, row_block0=0)
        ob_p = _conv(proj, conv_w[l], T=SEQ, nseq=BATCH, row_block0=NS // SEQ)
        oc_s = _nat(proj, cache_k4, cache_v4, _nat_bias(na_rpb[l]), l)
        oc_p = _ctx_attn(proj)
        o_a = jnp.concatenate([oa_s, oa_p], axis=0)
        o_b = jnp.concatenate([ob_s, ob_p], axis=0)
        o_c = jnp.concatenate([oc_s, oc_p], axis=0)
        x = _outproj(x, o_a, o_b, o_c, mods4, l, w_out[l].astype(BF16))
        x = _ffn(x, mods4, l, norm_ffn_w[l], final_norm_w, w_ffn_gate[l].astype(BF16), w_ffn_up[l].astype(BF16),
                 w_ffn_down[l].astype(BF16), final_norm=(l == DEPTH - 1))
        kv = proj[NS:, COL_NK * LANE:]
        new_k.append(kv[:, :NA_WIDTH].reshape(BATCH, SEQ, NA_HEADS, NA_HEAD_DIM))
        new_v.append(kv[:, NA_WIDTH:].reshape(BATCH, SEQ, NA_HEADS, NA_HEAD_DIM))
        new_s.append(s_c)

    y_sample = x[:NS].reshape(DEC_BATCH, DEC_SEQ, D_MODEL)
    y_prompt = x[NS:].reshape(BATCH, SEQ, D_MODEL)
    return (y_prompt, y_sample, jnp.stack(new_k, axis=1), jnp.stack(new_v, axis=1), jnp.stack(new_s, axis=1))
```

```python
import functools

import numpy as np
import jax
import jax.numpy as jnp
from jax import lax
from jax.experimental import pallas as pl
from jax.experimental.pallas import tpu as pltpu

F32 = jnp.float32
BF16 = jnp.bfloat16

D_MODEL = 2048
BATCH = 16
SEQ = 256
DEPTH = 2
DEC_BATCH = 8
DEC_SEQ = 1024
PAST_LEN = 512
GRID_W = 64
HGRN_HEADS = 8
HGRN_DK = 128
HGRN_DV = 128
HGRN_WIDTH = HGRN_HEADS * HGRN_DV
CONV_WIDTH = 512
NA_HEADS = 4
NA_HEAD_DIM = 128
NA_WIDTH = NA_HEADS * NA_HEAD_DIM
NA_KH = 8
NA_KW = 16
MIX_WIDTH = HGRN_WIDTH + CONV_WIDTH + NA_WIDTH
IN_PROJ_WIDTH = 5 * HGRN_WIDTH + 3 * CONV_WIDTH + 3 * NA_WIDTH
FFN_HIDDEN = ((8 * D_MODEL + 3 * 256 - 1) // (3 * 256)) * 256
EPS = 1e-6

NS = DEC_BATCH * DEC_SEQ
NP = BATCH * SEQ
NTOK = NS + NP
MOD_ROWS = 16
CTX_ROW = DEC_BATCH
LANE = 128
SUBLANE = 8
GRID_ROWS = DEC_SEQ // GRID_W

COL_HQ = 0
COL_HI = HGRN_WIDTH // LANE
COL_HFF = 2 * HGRN_WIDTH // LANE
COL_HFB = 3 * HGRN_WIDTH // LANE
COL_HG = 4 * HGRN_WIDTH // LANE
COL_CB = 5 * HGRN_WIDTH // LANE
COL_CC = COL_CB + CONV_WIDTH // LANE
COL_CX = COL_CC + CONV_WIDTH // LANE
COL_NQ = COL_CX + CONV_WIDTH // LANE
COL_NK = COL_NQ + NA_WIDTH // LANE
COL_NV = COL_NK + NA_WIDTH // LANE

NEG_BIG = -1e30
VMEM_LIMIT = 56 * 1024 * 1024


def _cparams(sem):
    return pltpu.CompilerParams(dimension_semantics=sem, vmem_limit_bytes=VMEM_LIMIT)


def _dot(a, b):
    return jnp.dot(a, b, preferred_element_type=F32)


def _dot_nt(a, b):
    return lax.dot_general(a, b, (((1,), (1,)), ((), ())), preferred_element_type=F32)


def _dot_tn(a, b):
    return lax.dot_general(a, b, (((0,), (0,)), ((), ())), preferred_element_type=F32)


def _sigmoid(x):
    return 1.0 / (1.0 + jnp.exp(-x))


ADA_TN = 1024


def _adaln_kernel(c_ref, w_ref, b_ref, o_ref):
    c = c_ref[...]
    s = (c * _sigmoid(c)).astype(BF16)
    o_ref[...] = _dot(s, w_ref[...].astype(BF16)) + b_ref[...]


def _adaln(cvecs, w_ada, b_ada):
    n = 6 * D_MODEL
    return pl.pallas_call(
        _adaln_kernel,
        grid=(DEPTH, n // ADA_TN),
        in_specs=[
            pl.BlockSpec((MOD_ROWS, D_MODEL), lambda l, j: (0, 0)),
            pl.BlockSpec((None, D_MODEL, ADA_TN), lambda l, j: (l, 0, j)),
            pl.BlockSpec((None, 1, ADA_TN), lambda l, j: (l, 0, j)),
        ],
        out_specs=pl.BlockSpec((None, MOD_ROWS, ADA_TN), lambda l, j: (l, 0, j)),
        out_shape=jax.ShapeDtypeStruct((DEPTH, MOD_ROWS, n), F32),
        compiler_params=_cparams(("arbitrary", "arbitrary")),
        name="adaln",
    )(cvecs, w_ada, b_ada.reshape(DEPTH, 1, n))


TM = 512
IN_TN = 512
FFN_TF = 512


def _mod_row(i):
    return jnp.minimum((i * TM) // DEC_SEQ, CTX_ROW)


def _mod_spec(layer, chunk, ngrid):
    if ngrid == 1:
        return pl.BlockSpec((None, None, 1, D_MODEL), lambda i: (layer, _mod_row(i), 0, chunk))
    return pl.BlockSpec((None, None, 1, D_MODEL), lambda i, j: (layer, _mod_row(i), 0, chunk))


def _norm_modulate(x, nw, shift, scale):
    ms = jnp.mean(x * x, axis=-1, keepdims=True)
    y = x * lax.rsqrt(ms + EPS) * nw
    return y * (1.0 + scale) + shift


def _inproj_kernel(x_ref, sh_ref, sc_ref, nw_ref, w_ref, o_ref, h_ref):
    @pl.when(pl.program_id(1) == 0)
    def _():
        h_ref[...] = _norm_modulate(x_ref[...], nw_ref[...], sh_ref[...], sc_ref[...]).astype(BF16)

    o_ref[...] = _dot(h_ref[...], w_ref[...])


def _inproj(x, mods4, layer, norm_w, w_bf16):
    return pl.pallas_call(
        _inproj_kernel,
        grid=(NTOK // TM, IN_PROJ_WIDTH // IN_TN),
        in_specs=[
            pl.BlockSpec((TM, D_MODEL), lambda i, j: (i, 0)),
            _mod_spec(layer, 0, 2),
            _mod_spec(layer, 1, 2),
            pl.BlockSpec((1, D_MODEL), lambda i, j: (0, 0)),
            pl.BlockSpec((D_MODEL, IN_TN), lambda i, j: (0, j)),
        ],
        out_specs=pl.BlockSpec((TM, IN_TN), lambda i, j: (i, j)),
        out_shape=jax.ShapeDtypeStruct((NTOK, IN_PROJ_WIDTH), F32),
        scratch_shapes=[pltpu.VMEM((TM, D_MODEL), BF16)],
        compiler_params=_cparams(("arbitrary", "arbitrary")),
        name="inproj",
    )(x, mods4, mods4, norm_w.reshape(1, D_MODEL), w_bf16)


def _outproj_kernel(x_ref, a_ref, b_ref, c_ref, g_ref, w_ref, o_ref):
    acc = _dot(a_ref[...], w_ref[0:HGRN_WIDTH, :])
    acc += _dot(b_ref[...], w_ref[HGRN_WIDTH:HGRN_WIDTH + CONV_WIDTH, :])
    acc += _dot(c_ref[...], w_ref[HGRN_WIDTH + CONV_WIDTH:MIX_WIDTH, :])
    o_ref[...] = x_ref[...] + g_ref[...] * acc


def _outproj(x, o_a, o_b, o_c, mods4, layer, w_bf16):
    return pl.pallas_call(
        _outproj_kernel,
        grid=(NTOK // TM,),
        in_specs=[
            pl.BlockSpec((TM, D_MODEL), lambda i: (i, 0)),
            pl.BlockSpec((TM, HGRN_WIDTH), lambda i: (i, 0)),
            pl.BlockSpec((TM, CONV_WIDTH), lambda i: (i, 0)),
            pl.BlockSpec((TM, NA_WIDTH), lambda i: (i, 0)),
            _mod_spec(layer, 2, 1),
            pl.BlockSpec((MIX_WIDTH, D_MODEL), lambda i: (0, 0)),
        ],
        out_specs=pl.BlockSpec((TM, D_MODEL), lambda i: (i, 0)),
        out_shape=jax.ShapeDtypeStruct((NTOK, D_MODEL), F32),
        compiler_params=_cparams(("arbitrary",)),
        name="outproj",
    )(x, o_a, o_b, o_c, mods4, w_bf16)


def _ffn_kernel(x_ref, sh_ref, sc_ref, g_ref, nw_ref, fw_ref, wg_ref, wu_ref, wd_ref, o_ref, h_ref, acc_ref, *,
                final_norm):
    j = pl.program_id(1)

    @pl.when(j == 0)
    def _():
        h_ref[...] = _norm_modulate(x_ref[...], nw_ref[...], sh_ref[...], sc_ref[...]).astype(BF16)
        acc_ref[...] = jnp.zeros_like(acc_ref)

    h = h_ref[...]
    a = _dot(h, wg_ref[...])
    u = _dot(h, wu_ref[...])
    t = (a * _sigmoid(a) * u).astype(BF16)
    acc_ref[...] += _dot(t, wd_ref[...])

    @pl.when(j == pl.num_programs(1) - 1)
    def _():
        y = x_ref[...] + g_ref[...] * acc_ref[...]
        if final_norm:
            ms = jnp.mean(y * y, axis=-1, keepdims=True)
            y = y * lax.rsqrt(ms + EPS) * fw_ref[...]
        o_ref[...] = y


def _ffn(x, mods4, layer, norm_w, final_w, wg, wu, wd, final_norm):
    return pl.pallas_call(
        functools.partial(_ffn_kernel, final_norm=final_norm),
        grid=(NTOK // TM, FFN_HIDDEN // FFN_TF),
        in_specs=[
            pl.BlockSpec((TM, D_MODEL), lambda i, j: (i, 0)),
            _mod_spec(layer, 3, 2),
            _mod_spec(layer, 4, 2),
            _mod_spec(layer, 5, 2),
            pl.BlockSpec((1, D_MODEL), lambda i, j: (0, 0)),
            pl.BlockSpec((1, D_MODEL), lambda i, j: (0, 0)),
            pl.BlockSpec((D_MODEL, FFN_TF), lambda i, j: (0, j)),
            pl.BlockSpec((D_MODEL, FFN_TF), lambda i, j: (0, j)),
            pl.BlockSpec((FFN_TF, D_MODEL), lambda i, j: (j, 0)),
        ],
        out_specs=pl.BlockSpec((TM, D_MODEL), lambda i, j: (i, 0)),
        out_shape=jax.ShapeDtypeStruct((NTOK, D_MODEL), F32),
        scratch_shapes=[pltpu.VMEM((TM, D_MODEL), BF16), pltpu.VMEM((TM, D_MODEL), F32)],
        compiler_params=_cparams(("arbitrary", "arbitrary")),
        name="ffn",
    )(x, mods4, mods4, mods4, norm_w.reshape(1, D_MODEL), final_w.reshape(1, D_MODEL), wg, wu, wd)


HGRN_CHUNK = 64
HGRN_HG = 2
HGRN_LEVELS = [HGRN_CHUNK >> (i + 1) for i in range(HGRN_CHUNK.bit_length() - 1)]


def _hgrn_gates(x, lb):
    e = jnp.exp(-jnp.abs(x))
    r = 1.0 / (1.0 + e)
    er = e * r
    pos = x >= 0.0
    sig = jnp.where(pos, r, er)
    nsig = jnp.where(pos, er, r)
    ls = jnp.minimum(x, 0.0) + jnp.log(r)
    f = lb + (1.0 - lb) * sig
    log_f = jnp.where(lb > 0.0, jnp.log(f), ls)
    return log_f, f, (1.0 - lb) * nsig


def _hgrn_chunk_a(x, lb, b_ref, tri):
    g, f, k = _hgrn_gates(x, lb)
    g_hi = g.astype(BF16)
    g_lo = (g - g_hi.astype(F32)).astype(BF16)
    b = _dot(tri, g_hi) + _dot(tri, g_lo)
    b_ref[...] = b
    return b, f, k


def _hgrn_chunk_b(b, f, k, q, v, st, b_ref, mask_ref, dirn):
    C = HGRN_CHUNK
    fwd = dirn == 0
    b_end = b_ref[pl.ds(C - 1 if fwd else 0, 1), :]
    vb = v.astype(BF16)
    o = _dot_nt((q * jnp.exp(b)).astype(BF16), st.astype(BF16))
    kd = (k * jnp.exp(b_end - b)).astype(BF16)
    st_new = jnp.exp(b_end) * st + _dot_tn(vb, kd)

    row = lax.broadcasted_iota(jnp.int32, (C, LANE), 0)
    sub = lax.broadcasted_iota(jnp.int32, (SUBLANE, LANE), 0)
    attn = None
    for lvl, m in enumerate(HGRN_LEVELS):
        if m >= SUBLANE:
            parts = []
            for p in range(C // (2 * m)):
                lo = p * 2 * m
                left = slice(lo, lo + m)
                right = slice(lo + m, lo + 2 * m)
                beta = b_ref[pl.ds(lo + (m - 1 if fwd else m), 1), :]
                if fwd:
                    parts += [k[left] * jnp.exp(beta - b[left]), q[right] * jnp.exp(b[right] - beta)]
                else:
                    parts += [q[left] * jnp.exp(b[left] - beta), k[right] * jnp.exp(beta - b[right])]
            w = jnp.concatenate(parts, axis=0)
        else:
            reads = ((row & m) != 0) if fwd else ((row & m) == 0)
            if m == 1:
                w = jnp.where(reads, q * f, k)
            else:
                betas = []
                for u in range(C // SUBLANE):
                    rows = []
                    for p in range(SUBLANE // (2 * m)):
                        rows.append(b_ref[pl.ds(u * SUBLANE + p * 2 * m + (m - 1 if fwd else m), 1), :])
                    beta_u = rows[-1]
                    for p in range(len(rows) - 2, -1, -1):
                        beta_u = jnp.where(sub < (p + 1) * 2 * m, rows[p], beta_u)
                    betas.append(jnp.broadcast_to(beta_u, (SUBLANE, LANE)))
                beta = jnp.concatenate(betas, axis=0)
                w = jnp.where(reads, q, k) * jnp.exp(-jnp.abs(b - beta))
        wb = w.astype(BF16)
        term = _dot_nt(wb, wb) * mask_ref[dirn * len(HGRN_LEVELS) + lvl]
        attn = term if attn is None else attn + term
    o += jnp.sum(q * k, axis=-1, keepdims=True) * v
    return o, attn, vb, st_new


def _hgrn_kernel(*refs, T, has_s0, emit_state):
    q_ref, v_ref, ff_ref, fb_ref, g_ref, lb_ref, gnw_ref = refs[:7]
    pos = 7
    s0_ref = None
    if has_s0:
        s0_ref = refs[pos]
        pos += 1
    o_ref = refs[pos]
    pos += 1
    s_out_ref = None
    if emit_state:
        s_out_ref = refs[pos]
        pos += 1
    oacc_ref, b_ref, st_ref, tri_ref, mask_ref = refs[pos:pos + 5]

    C = HGRN_CHUNK
    nc = T // C
    nl = len(HGRN_LEVELS)

    @pl.when((pl.program_id(0) == 0) & (pl.program_id(1) == 0))
    def _():
        row = lax.broadcasted_iota(jnp.int32, (C, C), 0)
        col = lax.broadcasted_iota(jnp.int32, (C, C), 1)
        tri_ref[0] = jnp.where(col <= row, 1.0, 0.0).astype(BF16)
        tri_ref[1] = jnp.where(col >= row, 1.0, 0.0).astype(BF16)
        for lvl, m in enumerate(HGRN_LEVELS):
            shift = (2 * m).bit_length() - 1
            same = (row >> shift) == (col >> shift)
            hi_lo = ((row & m) != 0) & ((col & m) == 0)
            lo_hi = ((row & m) == 0) & ((col & m) != 0)
            mask_ref[lvl] = jnp.where(same & hi_lo, 1.0, 0.0)
            mask_ref[nl + lvl] = jnp.where(same & lo_hi, 1.0, 0.0)

    oacc_ref[...] = jnp.zeros_like(oacc_ref)
    for hh in range(HGRN_HG):
        for d in range(2):
            if has_s0:
                st_ref[2 * hh + d] = s0_ref[d, hh].T
            else:
                st_ref[2 * hh + d] = jnp.zeros((HGRN_DV, HGRN_DK), F32)

    q_scale = HGRN_DK ** -0.5

    def body(c, carry):
        chains = []
        for hh in range(HGRN_HG):
            lanes = slice(hh * LANE, (hh + 1) * LANE)
            for d in range(2):
                cc = c if d == 0 else nc - 1 - c
                rows = pl.ds(pl.multiple_of(cc * C, C), C)
                chains.append((hh, d, rows, lanes))
        stage_a = []
        for hh, d, rows, lanes in chains:
            x_ref = ff_ref if d == 0 else fb_ref
            stage_a.append(_hgrn_chunk_a(x_ref[rows, lanes], lb_ref[d, hh], b_ref.at[2 * hh + d], tri_ref[d]))
        stage_b = []
        for (hh, d, rows, lanes), (b, f, k) in zip(chains, stage_a):
            idx = 2 * hh + d
            o, attn, vb, st_new = _hgrn_chunk_b(b, f, k, q_ref[rows, lanes] * q_scale, v_ref[rows, lanes],
                                                st_ref[idx], b_ref.at[idx], mask_ref, d)
            st_ref[idx] = st_new
            stage_b.append((o, attn, vb))
        for (hh, d, rows, lanes), (o, attn, vb) in zip(chains, stage_b):
            oacc_ref[rows, lanes] += o + _dot(attn.astype(BF16), vb)
        return carry

    lax.fori_loop(0, nc, body, 0)

    for hh in range(HGRN_HG):
        lanes = slice(hh * LANE, (hh + 1) * LANE)
        if emit_state:
            for d in range(2):
                s_out_ref[d, hh] = st_ref[2 * hh + d].T
        o = oacc_ref[:, lanes]
        ms = jnp.mean(o * o, axis=-1, keepdims=True)
        gate = g_ref[:, lanes]
        y = o * lax.rsqrt(ms + EPS) * gnw_ref[...] * (gate * _sigmoid(gate))
        o_ref[:, lanes] = y.astype(BF16)


def _hgrn(proj, lbs_l, gnorm_w_l, s0, layer, *, T, nseq, row_block0, has_s0, emit_state):
    hg = HGRN_HG
    wide = hg * LANE

    def col_spec(col0):
        return pl.BlockSpec((T, wide), lambda b, h: (row_block0 + b, col0 // hg + h))

    in_specs = [col_spec(COL_HQ), col_spec(COL_HI), col_spec(COL_HFF), col_spec(COL_HFB), col_spec(COL_HG),
                pl.BlockSpec((2, hg, 1, LANE), lambda b, h: (0, h, 0, 0)),
                pl.BlockSpec((1, LANE), lambda b, h: (0, 0))]
    args = [proj, proj, proj, proj, proj, lbs_l.reshape(2, HGRN_HEADS, 1, LANE), gnorm_w_l.reshape(1, LANE)]
    if has_s0:
        in_specs.append(pl.BlockSpec((None, None, 2, hg, HGRN_DK, HGRN_DV), lambda b, h: (b, layer, 0, h, 0, 0)))
        args.append(s0)
    out_specs = [pl.BlockSpec((T, wide), lambda b, h: (b, h))]
    out_shape = [jax.ShapeDtypeStruct((nseq * T, HGRN_WIDTH), BF16)]
    if emit_state:
        out_specs.append(pl.BlockSpec((None, 2, hg, HGRN_DK, HGRN_DV), lambda b, h: (b, 0, h, 0, 0)))
        out_shape.append(jax.ShapeDtypeStruct((nseq, 2, HGRN_HEADS, HGRN_DK, HGRN_DV), F32))
    C = HGRN_CHUNK
    nl = len(HGRN_LEVELS)
    return pl.pallas_call(
        functools.partial(_hgrn_kernel, T=T, has_s0=has_s0, emit_state=emit_state),
        grid=(nseq, HGRN_HEADS // hg),
        in_specs=in_specs,
        out_specs=out_specs,
        out_shape=out_shape,
        scratch_shapes=[pltpu.VMEM((T, wide), F32), pltpu.VMEM((2 * hg, C, LANE), F32),
                        pltpu.VMEM((2 * hg, HGRN_DV, HGRN_DK), F32), pltpu.VMEM((2, C, C), BF16),
                        pltpu.VMEM((2 * nl, C, C), F32)],
        compiler_params=_cparams(("arbitrary", "arbitrary")),
        name="hgrn_ctx" if emit_state else "hgrn_lat",
    )(*args)


def _conv_kernel(cb_ref, cc_ref, cx_ref, w_ref, o_ref, *, T):
    u = cc_ref[...] * cx_ref[...]
    row = lax.broadcasted_iota(jnp.int32, (T, CONV_WIDTH), 0)
    u_prev = jnp.where(row == 0, 0.0, pltpu.roll(u, 1, axis=0))
    u_next = jnp.where(row == T - 1, 0.0, pltpu.roll(u, T - 1, axis=0))
    y = u_prev * w_ref[0:1, :] + u * w_ref[1:2, :] + u_next * w_ref[2:3, :]
    o_ref[...] = (cb_ref[...] * y).astype(BF16)


def _conv(proj, conv_w_l, *, T, nseq, row_block0):
    cw = CONV_WIDTH // LANE

    def col_spec(col0):
        return pl.BlockSpec((T, CONV_WIDTH), lambda b: (row_block0 + b, col0 // cw))

    return pl.pallas_call(
        functools.partial(_conv_kernel, T=T),
        grid=(nseq,),
        in_specs=[col_spec(COL_CB), col_spec(COL_CC), col_spec(COL_CX),
                  pl.BlockSpec((3, CONV_WIDTH), lambda b: (0, 0))],
        out_specs=pl.BlockSpec((T, CONV_WIDTH), lambda b: (b, 0)),
        out_shape=jax.ShapeDtypeStruct((nseq * T, CONV_WIDTH), BF16),
        compiler_params=_cparams(("arbitrary",)),
        name="conv",
    )(proj, proj, proj, conv_w_l)


def _ctx_attn_kernel(q_ref, k_ref, v_ref, o_ref):
    scale = NA_HEAD_DIM ** -0.5
    s = _dot_nt(q_ref[...].astype(BF16), k_ref[...].astype(BF16)) * scale
    m = jnp.max(s, axis=-1, keepdims=True)
    p = jnp.exp(s - m)
    l = jnp.sum(p, axis=-1, keepdims=True)
    o = _dot(p.astype(BF16), v_ref[...].astype(BF16)) / l
    o_ref[...] = o.astype(BF16)


def _ctx_attn(proj):
    rb0 = NS // SEQ

    def col_spec(col0):
        return pl.BlockSpec((SEQ, LANE), lambda b, h: (rb0 + b, col0 + h))

    return pl.pallas_call(
        _ctx_attn_kernel,
        grid=(BATCH, NA_HEADS),
        in_specs=[col_spec(COL_NQ), col_spec(COL_NK), col_spec(COL_NV)],
        out_specs=pl.BlockSpec((SEQ, LANE), lambda b, h: (b, h)),
        out_shape=jax.ShapeDtypeStruct((NP, NA_WIDTH), BF16),
        compiler_params=_cparams(("arbitrary", "arbitrary")),
        name="ctx_attn",
    )(proj, proj, proj)


_KH = min(NA_KH, GRID_ROWS)
_ROW_START = [int(v) for v in np.clip(np.arange(GRID_ROWS) - _KH // 2, 0, GRID_ROWS - _KH)]
NLOC = _KH * GRID_W


def _nat_kernel(q_ref, k_ref, v_ref, kc_ref, vc_ref, bias_ref, o_ref, sctx_ref, pctx_ref, oloc_ref, den_ref):
    scale = NA_HEAD_DIM ** -0.5
    q_all = q_ref[...].astype(BF16)
    sctx_ref[...] = _dot_nt(q_all, kc_ref[...].astype(BF16)) * scale
    for r in range(GRID_ROWS):
        rows = slice(r * GRID_W, (r + 1) * GRID_W)
        band = slice(_ROW_START[r] * GRID_W, _ROW_START[r] * GRID_W + NLOC)
        q_r = q_ref[rows, :].astype(BF16)
        s_loc = _dot_nt(q_r, k_ref[band, :].astype(BF16)) * scale + bias_ref[r - _ROW_START[r]]
        s_ctx = sctx_ref[rows, :]
        m = jnp.maximum(jnp.max(s_loc, axis=-1, keepdims=True), jnp.max(s_ctx, axis=-1, keepdims=True))
        p_loc = jnp.exp(s_loc - m)
        p_ctx = jnp.exp(s_ctx - m)
        den_ref[rows, :] = jnp.sum(p_loc, axis=-1, keepdims=True) + jnp.sum(p_ctx, axis=-1, keepdims=True)
        pctx_ref[rows, :] = p_ctx.astype(BF16)
        oloc_ref[rows, :] = _dot(p_loc.astype(BF16), v_ref[band, :].astype(BF16))
    o = (oloc_ref[...] + _dot(pctx_ref[...], vc_ref[...].astype(BF16))) / den_ref[...]
    o_ref[...] = o.astype(BF16)


def _nat_bias(rpb_l):
    col = np.arange(GRID_W)
    col_start = np.clip(col - NA_KW // 2, 0, GRID_W - NA_KW)
    col_mask = (col[None, :] >= col_start[:, None]) & (col[None, :] < col_start[:, None] + NA_KW)
    dc_idx = np.clip(col[None, :] - col[:, None] + NA_KW - 1, 0, 2 * NA_KW - 2)
    tab = rpb_l[:, :, dc_idx]
    tab = jnp.where(jnp.asarray(col_mask)[None, None], tab, NEG_BIG)
    slabs = []
    for off in range(_KH):
        dr = np.arange(_KH) - off + (NA_KH - 1)
        slab = tab[:, dr]
        slabs.append(slab.transpose(0, 2, 1, 3).reshape(NA_HEADS, GRID_W, NLOC))
    return jnp.stack(slabs, axis=1)


def _nat(proj, cache_k4, cache_v4, bias, layer):
    def col_spec(col0):
        return pl.BlockSpec((DEC_SEQ, LANE), lambda b, h: (b, col0 + h))

    cache_spec = pl.BlockSpec((None, None, PAST_LEN, LANE), lambda b, h: (b, layer, 0, h))
    return pl.pallas_call(
        _nat_kernel,
        grid=(DEC_BATCH, NA_HEADS),
        in_specs=[col_spec(COL_NQ), col_spec(COL_NK), col_spec(COL_NV), cache_spec, cache_spec,
                  pl.BlockSpec((None, _KH, GRID_W, NLOC), lambda b, h: (h, 0, 0, 0))],
        out_specs=pl.BlockSpec((DEC_SEQ, LANE), lambda b, h: (b, h)),
        out_shape=jax.ShapeDtypeStruct((NS, NA_WIDTH), BF16),
        scratch_shapes=[pltpu.VMEM((DEC_SEQ, PAST_LEN), F32), pltpu.VMEM((DEC_SEQ, PAST_LEN), BF16),
                        pltpu.VMEM((DEC_SEQ, LANE), F32), pltpu.VMEM((DEC_SEQ, 1), F32)],
        compiler_params=_cparams(("arbitrary", "arbitrary")),
        name="nat",
    )(proj, proj, proj, cache_k4, cache_v4, bias)


def kernel(x_prompt, x_sample, cache_na_k, cache_na_v, state_hgrn, c, c_ctx, w_ada, b_ada, norm_mix_w, w_in,
           hgrn_lb_raw, hgrn_gnorm_w, conv_w, na_rpb, w_out, norm_ffn_w, w_ffn_gate, w_ffn_up, w_ffn_down,
           final_norm_w):
    x = jnp.concatenate([x_sample.reshape(NS, D_MODEL), x_prompt.reshape(NP, D_MODEL)], axis=0)
    cvecs = jnp.concatenate([c, c_ctx[None, :], jnp.zeros((MOD_ROWS - DEC_BATCH - 1, D_MODEL), F32)], axis=0)
    mods4 = _adaln(cvecs, w_ada, b_ada).reshape(DEPTH, MOD_ROWS, 1, 6 * D_MODEL)

    p_lb = jax.nn.softmax(hgrn_lb_raw.astype(F32), axis=1)
    cp = jnp.cumsum(p_lb, axis=1)
    lbs = cp - cp[:, :1]

    cache_k4 = cache_na_k.reshape(DEC_BATCH, DEPTH, PAST_LEN, NA_WIDTH)
    cache_v4 = cache_na_v.reshape(DEC_BATCH, DEPTH, PAST_LEN, NA_WIDTH)

    new_k, new_v, new_s = [], [], []
    for l in range(DEPTH):
        proj = _inproj(x, mods4, l, norm_mix_w[l], w_in[l].astype(BF16))
        oa_s = _hgrn(proj, lbs[:, l], hgrn_gnorm_w[l], state_hgrn, l, T=DEC_SEQ, nseq=DEC_BATCH, row_block0=0,
                     has_s0=True, emit_state=False)[0]
        oa_p, s_c = _hgrn(proj, lbs[:, l], hgrn_gnorm_w[l], None, l, T=SEQ, nseq=BATCH, row_block0=NS // SEQ,
                          has_s0=False, emit_state=True)
        ob_s = _conv(proj, conv_w[l], T=DEC_SEQ, nseq=DEC_BATCH, row_block0=0)
        ob_p = _conv(proj, conv_w[l], T=SEQ, nseq=BATCH, row_block0=NS // SEQ)
        oc_s = _nat(proj, cache_k4, cache_v4, _nat_bias(na_rpb[l]), l)
        oc_p = _ctx_attn(proj)
        o_a = jnp.concatenate([oa_s, oa_p], axis=0)
        o_b = jnp.concatenate([ob_s, ob_p], axis=0)
        o_c = jnp.concatenate([oc_s, oc_p], axis=0)
        x = _outproj(x, o_a, o_b, o_c, mods4, l, w_out[l].astype(BF16))
        x = _ffn(x, mods4, l, norm_ffn_w[l], final_norm_w, w_ffn_gate[l].astype(BF16), w_ffn_up[l].astype(BF16),
                 w_ffn_down[l].astype(BF16), final_norm=(l == DEPTH - 1))
        kv = proj[NS:, COL_NK * LANE:]
        new_k.append(kv[:, :NA_WIDTH].reshape(BATCH, SEQ, NA_HEADS, NA_HEAD_DIM))
        new_v.append(kv[:, NA_WIDTH:].reshape(BATCH, SEQ, NA_HEADS, NA_HEAD_DIM))
        new_s.append(s_c)

    y_sample = x[:NS].reshape(DEC_BATCH, DEC_SEQ, D_MODEL)
    y_prompt = x[NS:].reshape(BATCH, SEQ, D_MODEL)
    return (y_prompt, y_sample, jnp.stack(new_k, axis=1), jnp.stack(new_v, axis=1), jnp.stack(new_s, axis=1))
```

```python
import collections
import functools

import numpy as np
import jax
import jax.numpy as jnp
from jax import lax
from jax.experimental import pallas as pl
from jax.experimental.pallas import tpu as pltpu

F32 = jnp.float32
BF16 = jnp.bfloat16

D_MODEL = 2048
BATCH = 16
SEQ = 256
DEPTH = 2
DEC_BATCH = 8
DEC_SEQ = 1024
PAST_LEN = 512
GRID_W = 64
HGRN_HEADS = 8
HGRN_DK = 128
HGRN_DV = 128
HGRN_WIDTH = HGRN_HEADS * HGRN_DV
CONV_WIDTH = 512
NA_HEADS = 4
NA_HEAD_DIM = 128
NA_WIDTH = NA_HEADS * NA_HEAD_DIM
NA_KH = 8
NA_KW = 16
MIX_WIDTH = HGRN_WIDTH + CONV_WIDTH + NA_WIDTH
IN_PROJ_WIDTH = 5 * HGRN_WIDTH + 3 * CONV_WIDTH + 3 * NA_WIDTH
FFN_HIDDEN = ((8 * D_MODEL + 3 * 256 - 1) // (3 * 256)) * 256
EPS = 1e-6

MOD_ROWS = 16
CTX_ROW = DEC_BATCH
LANE = 128
SUBLANE = 8
GRID_ROWS = DEC_SEQ // GRID_W

COL_HQ = 0
COL_HI = HGRN_WIDTH // LANE
COL_HFF = 2 * HGRN_WIDTH // LANE
COL_HFB = 3 * HGRN_WIDTH // LANE
COL_HG = 4 * HGRN_WIDTH // LANE
COL_CB = 5 * HGRN_WIDTH // LANE
COL_CC = COL_CB + CONV_WIDTH // LANE
COL_CX = COL_CC + CONV_WIDTH // LANE
COL_NQ = COL_CX + CONV_WIDTH // LANE
COL_NK = COL_NQ + NA_WIDTH // LANE
COL_NV = COL_NK + NA_WIDTH // LANE

NEG_BIG = -1e30
VMEM_LIMIT = 56 * 1024 * 1024

Stream = collections.namedtuple("Stream", ["name", "seq", "nseq", "latent"])
LATENT = Stream("lat", DEC_SEQ, DEC_BATCH, True)
CONTEXT = Stream("ctx", SEQ, BATCH, False)


def _ntok(st):
    return st.seq * st.nseq


def _cparams(sem):
    return pltpu.CompilerParams(dimension_semantics=sem, vmem_limit_bytes=VMEM_LIMIT)


def _dot(a, b):
    return jnp.dot(a, b, preferred_element_type=F32)


def _dot_nt(a, b):
    return lax.dot_general(a, b, (((1,), (1,)), ((), ())), preferred_element_type=F32)


def _dot_tn(a, b):
    return lax.dot_general(a, b, (((0,), (0,)), ((), ())), preferred_element_type=F32)


def _sigmoid(x):
    return 1.0 / (1.0 + jnp.exp(-x))


ADA_TN = 1024


def _adaln_kernel(c_ref, w_ref, b_ref, o_ref):
    c = c_ref[...]
    s = (c * _sigmoid(c)).astype(BF16)
    o_ref[...] = _dot(s, w_ref[...].astype(BF16)) + b_ref[...]


def _adaln(cvecs, w_ada, b_ada):
    n = 6 * D_MODEL
    return pl.pallas_call(
        _adaln_kernel,
        grid=(DEPTH, n // ADA_TN),
        in_specs=[
            pl.BlockSpec((MOD_ROWS, D_MODEL), lambda l, j: (0, 0)),
            pl.BlockSpec((None, D_MODEL, ADA_TN), lambda l, j: (l, 0, j)),
            pl.BlockSpec((None, 1, ADA_TN), lambda l, j: (l, 0, j)),
        ],
        out_specs=pl.BlockSpec((None, MOD_ROWS, ADA_TN), lambda l, j: (l, 0, j)),
        out_shape=jax.ShapeDtypeStruct((DEPTH, MOD_ROWS, n), F32),
        compiler_params=_cparams(("arbitrary", "arbitrary")),
        name="adaln",
    )(cvecs, w_ada, b_ada.reshape(DEPTH, 1, n))


IN_TM = 1024
IN_TN = 1024
TM = 512
FFN_TF = 512


def _mod_spec(st, layer, chunk, tm, ngrid):
    def row(i):
        return (i * tm) // DEC_SEQ if st.latent else CTX_ROW

    if ngrid == 1:
        return pl.BlockSpec((None, None, 1, D_MODEL), lambda i: (layer, row(i), 0, chunk))
    return pl.BlockSpec((None, None, 1, D_MODEL), lambda i, j: (layer, row(i), 0, chunk))


PROLOGUE_ROWS = 256


def _norm_modulate(x_ref, h_ref, nw_ref, sh_ref, sc_ref):
    gain = nw_ref[...] * (1.0 + sc_ref[...])
    shift = sh_ref[...]
    for r in range(0, x_ref.shape[0], PROLOGUE_ROWS):
        x = x_ref[r:r + PROLOGUE_ROWS, :]
        ms = jnp.mean(x * x, axis=-1, keepdims=True)
        h_ref[r:r + PROLOGUE_ROWS, :] = (x * lax.rsqrt(ms + EPS) * gain + shift).astype(BF16)


def _inproj_kernel(*refs, emit_kv):
    x_ref, sh_ref, sc_ref, nw_ref, w_ref = refs[:5]
    outs = refs[-4:] if emit_kv else refs[-2:]
    o_ref = outs[0]
    h_ref = refs[-1]

    @pl.when(pl.program_id(1) == 0)
    def _():
        _norm_modulate(x_ref, h_ref, nw_ref, sh_ref, sc_ref)

    o_ref[...] = _dot(h_ref[...], w_ref[...])

    if emit_kv:
        @pl.when(pl.program_id(1) == pl.num_programs(1) - 1)
        def _():
            k_ref, v_ref = outs[1], outs[2]
            k_ref[...] = o_ref[:, :NA_WIDTH].reshape(k_ref.shape)
            v_ref[...] = o_ref[:, NA_WIDTH:].reshape(v_ref.shape)


def _inproj(st, x, mods4, layer, norm_w, w_bf16, kv_prev):
    emit_kv = not st.latent
    assert COL_NK * LANE == IN_PROJ_WIDTH - IN_TN and 2 * NA_WIDTH == IN_TN
    ntok = _ntok(st)
    in_specs = [
        pl.BlockSpec((IN_TM, D_MODEL), lambda i, j: (i, 0)),
        _mod_spec(st, layer, 0, IN_TM, 2),
        _mod_spec(st, layer, 1, IN_TM, 2),
        pl.BlockSpec((1, D_MODEL), lambda i, j: (0, 0)),
        pl.BlockSpec((D_MODEL, IN_TN), lambda i, j: (0, j)),
    ]
    args = [x, mods4, mods4, norm_w.reshape(1, D_MODEL), w_bf16]
    out_specs = [pl.BlockSpec((IN_TM, IN_TN), lambda i, j: (i, j))]
    out_shape = [jax.ShapeDtypeStruct((ntok, IN_PROJ_WIDTH), F32)]
    aliases = {}
    if emit_kv:
        nb = IN_TM // st.seq
        kv_spec = pl.BlockSpec((nb, None, st.seq, NA_WIDTH), lambda i, j: (i, layer, 0, 0))
        out_specs += [kv_spec, kv_spec]
        out_shape += [jax.ShapeDtypeStruct((st.nseq, DEPTH, st.seq, NA_WIDTH), F32)] * 2
        if kv_prev is not None:
            in_specs += [pl.BlockSpec(memory_space=pl.ANY)] * 2
            args += list(kv_prev)
            aliases = {5: 1, 6: 2}
    return pl.pallas_call(
        functools.partial(_inproj_kernel, emit_kv=emit_kv),
        grid=(ntok // IN_TM, IN_PROJ_WIDTH // IN_TN),
        in_specs=in_specs,
        out_specs=out_specs,
        out_shape=out_shape,
        scratch_shapes=[pltpu.VMEM((IN_TM, D_MODEL), BF16)],
        input_output_aliases=aliases,
        compiler_params=_cparams(("arbitrary", "arbitrary")),
        name="inproj_" + st.name,
    )(*args)


def _outproj_kernel(x_ref, m_ref, g_ref, w_ref, o_ref):
    o_ref[...] = x_ref[...] + g_ref[...] * _dot(m_ref[...], w_ref[...])


def _outproj(st, x, mix, mods4, layer, w_bf16):
    ntok = _ntok(st)
    return pl.pallas_call(
        _outproj_kernel,
        grid=(ntok // TM,),
        in_specs=[
            pl.BlockSpec((TM, D_MODEL), lambda i: (i, 0)),
            pl.BlockSpec((TM, MIX_WIDTH), lambda i: (i, 0)),
            _mod_spec(st, layer, 2, TM, 1),
            pl.BlockSpec((MIX_WIDTH, D_MODEL), lambda i: (0, 0)),
        ],
        out_specs=pl.BlockSpec((TM, D_MODEL), lambda i: (i, 0)),
        out_shape=jax.ShapeDtypeStruct((ntok, D_MODEL), F32),
        compiler_params=_cparams(("arbitrary",)),
        name="outproj_" + st.name,
    )(x, mix, mods4, w_bf16)


def _ffn_kernel(x_ref, sh_ref, sc_ref, g_ref, nw_ref, fw_ref, wg_ref, wu_ref, wd_ref, o_ref, h_ref, acc_ref, *,
                final_norm):
    j = pl.program_id(1)

    @pl.when(j == 0)
    def _():
        _norm_modulate(x_ref, h_ref, nw_ref, sh_ref, sc_ref)
        acc_ref[...] = jnp.zeros_like(acc_ref)

    h = h_ref[...]
    a = _dot(h, wg_ref[...])
    u = _dot(h, wu_ref[...])
    t = (a * _sigmoid(a) * u).astype(BF16)
    acc_ref[...] += _dot(t, wd_ref[...])

    @pl.when(j == pl.num_programs(1) - 1)
    def _():
        y = x_ref[...] + g_ref[...] * acc_ref[...]
        if final_norm:
            ms = jnp.mean(y * y, axis=-1, keepdims=True)
            y = y * lax.rsqrt(ms + EPS) * fw_ref[...]
        o_ref[...] = y


def _ffn(st, x, mods4, layer, norm_w, final_w, wg, wu, wd, final_norm):
    ntok = _ntok(st)
    return pl.pallas_call(
        functools.partial(_ffn_kernel, final_norm=final_norm),
        grid=(ntok // TM, FFN_HIDDEN // FFN_TF),
        in_specs=[
            pl.BlockSpec((TM, D_MODEL), lambda i, j: (i, 0)),
            _mod_spec(st, layer, 3, TM, 2),
            _mod_spec(st, layer, 4, TM, 2),
            _mod_spec(st, layer, 5, TM, 2),
            pl.BlockSpec((1, D_MODEL), lambda i, j: (0, 0)),
            pl.BlockSpec((1, D_MODEL), lambda i, j: (0, 0)),
            pl.BlockSpec((D_MODEL, FFN_TF), lambda i, j: (0, j)),
            pl.BlockSpec((D_MODEL, FFN_TF), lambda i, j: (0, j)),
            pl.BlockSpec((FFN_TF, D_MODEL), lambda i, j: (j, 0)),
        ],
        out_specs=pl.BlockSpec((TM, D_MODEL), lambda i, j: (i, 0)),
        out_shape=jax.ShapeDtypeStruct((ntok, D_MODEL), F32),
        scratch_shapes=[pltpu.VMEM((TM, D_MODEL), BF16), pltpu.VMEM((TM, D_MODEL), F32)],
        compiler_params=_cparams(("arbitrary", "arbitrary")),
        name="ffn_" + st.name,
    )(x, mods4, mods4, mods4, norm_w.reshape(1, D_MODEL), final_w.reshape(1, D_MODEL), wg, wu, wd)


HGRN_CHUNK = 64
HGRN_HG = 2
HGRN_LEVELS = [HGRN_CHUNK >> (i + 1) for i in range(HGRN_CHUNK.bit_length() - 1)]


def _hgrn_gates(x, lb):
    e = jnp.exp(-jnp.abs(x))
    r = 1.0 / (1.0 + e)
    er = e * r
    pos = x >= 0.0
    sig = jnp.where(pos, r, er)
    nsig = jnp.where(pos, er, r)
    ls = jnp.minimum(x, 0.0) + jnp.log(r)
    f = lb + (1.0 - lb) * sig
    log_f = jnp.where(lb > 0.0, jnp.log(f), ls)
    return log_f, f, (1.0 - lb) * nsig


def _hgrn_chunk_a(x, lb, b_ref, tri):
    g, f, k = _hgrn_gates(x, lb)
    g_hi = g.astype(BF16)
    g_lo = (g - g_hi.astype(F32)).astype(BF16)
    b = _dot(tri, g_hi) + _dot(tri, g_lo)
    b_ref[...] = b
    return b, f, k


def _hgrn_chunk_b(b, f, k, q, v, st, b_ref, mask_ref, dirn):
    C = HGRN_CHUNK
    fwd = dirn == 0
    b_end = b_ref[pl.ds(C - 1 if fwd else 0, 1), :]
    vb = v.astype(BF16)
    o = _dot_nt((q * jnp.exp(b)).astype(BF16), st.astype(BF16))
    kd = (k * jnp.exp(b_end - b)).astype(BF16)
    st_new = jnp.exp(b_end) * st + _dot_tn(vb, kd)

    row = lax.broadcasted_iota(jnp.int32, (C, LANE), 0)
    sub = lax.broadcasted_iota(jnp.int32, (SUBLANE, LANE), 0)
    attn = None
    for lvl, m in enumerate(HGRN_LEVELS):
        if m >= SUBLANE:
            parts = []
            for p in range(C // (2 * m)):
                lo = p * 2 * m
                left = slice(lo, lo + m)
                right = slice(lo + m, lo + 2 * m)
                beta = b_ref[pl.ds(lo + (m - 1 if fwd else m), 1), :]
                if fwd:
                    parts += [k[left] * jnp.exp(beta - b[left]), q[right] * jnp.exp(b[right] - beta)]
                else:
                    parts += [q[left] * jnp.exp(b[left] - beta), k[right] * jnp.exp(beta - b[right])]
            w = jnp.concatenate(parts, axis=0)
        else:
            reads = ((row & m) != 0) if fwd else ((row & m) == 0)
            if m == 1:
                w = jnp.where(reads, q * f, k)
            else:
                betas = []
                for u in range(C // SUBLANE):
                    rows = []
                    for p in range(SUBLANE // (2 * m)):
                        rows.append(b_ref[pl.ds(u * SUBLANE + p * 2 * m + (m - 1 if fwd else m), 1), :])
                    beta_u = rows[-1]
                    for p in range(len(rows) - 2, -1, -1):
                        beta_u = jnp.where(sub < (p + 1) * 2 * m, rows[p], beta_u)
                    betas.append(jnp.broadcast_to(beta_u, (SUBLANE, LANE)))
                beta = jnp.concatenate(betas, axis=0)
                w = jnp.where(reads, q, k) * jnp.exp(-jnp.abs(b - beta))
        wb = w.astype(BF16)
        term = _dot_nt(wb, wb) * mask_ref[dirn * len(HGRN_LEVELS) + lvl]
        attn = term if attn is None else attn + term
    o += jnp.sum(q * k, axis=-1, keepdims=True) * v
    return o, attn, vb, st_new


def _hgrn_kernel(*refs, T, has_s0, emit_state, n_alias):
    q_ref, v_ref, ff_ref, fb_ref, g_ref, lb_ref, gnw_ref = refs[:7]
    s0_ref = refs[7] if has_s0 else None
    n_out = 2 if emit_state else 1
    outs = refs[len(refs) - 5 - n_out:len(refs) - 5]
    o_ref = outs[0]
    s_out_ref = outs[1] if emit_state else None
    oacc_ref, b_ref, st_ref, tri_ref, mask_ref = refs[-5:]
    assert len(refs) == 7 + int(has_s0) + n_alias + n_out + 5

    C = HGRN_CHUNK
    nc = T // C
    nl = len(HGRN_LEVELS)

    @pl.when((pl.program_id(0) == 0) & (pl.program_id(1) == 0))
    def _():
        row = lax.broadcasted_iota(jnp.int32, (C, C), 0)
        col = lax.broadcasted_iota(jnp.int32, (C, C), 1)
        tri_ref[0] = jnp.where(col <= row, 1.0, 0.0).astype(BF16)
        tri_ref[1] = jnp.where(col >= row, 1.0, 0.0).astype(BF16)
        for lvl, m in enumerate(HGRN_LEVELS):
            shift = (2 * m).bit_length() - 1
            same = (row >> shift) == (col >> shift)
            hi_lo = ((row & m) != 0) & ((col & m) == 0)
            lo_hi = ((row & m) == 0) & ((col & m) != 0)
            mask_ref[lvl] = jnp.where(same & hi_lo, 1.0, 0.0)
            mask_ref[nl + lvl] = jnp.where(same & lo_hi, 1.0, 0.0)

    oacc_ref[...] = jnp.zeros_like(oacc_ref)
    for hh in range(HGRN_HG):
        for d in range(2):
            if has_s0:
                st_ref[2 * hh + d] = s0_ref[d, hh].T
            else:
                st_ref[2 * hh + d] = jnp.zeros((HGRN_DV, HGRN_DK), F32)

    q_scale = HGRN_DK ** -0.5

    def body(c, carry):
        chains = []
        for hh in range(HGRN_HG):
            lanes = slice(hh * LANE, (hh + 1) * LANE)
            for d in range(2):
                cc = c if d == 0 else nc - 1 - c
                rows = pl.ds(pl.multiple_of(cc * C, C), C)
                chains.append((hh, d, rows, lanes))
        stage_a = []
        for hh, d, rows, lanes in chains:
            x_ref = ff_ref if d == 0 else fb_ref
            stage_a.append(_hgrn_chunk_a(x_ref[rows, lanes], lb_ref[d, hh], b_ref.at[2 * hh + d], tri_ref[d]))
        stage_b = []
        for (hh, d, rows, lanes), (b, f, k) in zip(chains, stage_a):
            idx = 2 * hh + d
            o, attn, vb, st_new = _hgrn_chunk_b(b, f, k, q_ref[rows, lanes] * q_scale, v_ref[rows, lanes],
                                                st_ref[idx], b_ref.at[idx], mask_ref, d)
            st_ref[idx] = st_new
            stage_b.append((o, attn, vb))
        for (hh, d, rows, lanes), (o, attn, vb) in zip(chains, stage_b):
            oacc_ref[rows, lanes] += o + _dot(attn.astype(BF16), vb)
        return carry

    lax.fori_loop(0, nc, body, 0)

    for hh in range(HGRN_HG):
        lanes = slice(hh * LANE, (hh + 1) * LANE)
        if emit_state:
            for d in range(2):
                s_out_ref[d, hh] = st_ref[2 * hh + d].T
        o = oacc_ref[:, lanes]
        ms = jnp.mean(o * o, axis=-1, keepdims=True)
        gate = g_ref[:, lanes]
        y = o * lax.rsqrt(ms + EPS) * gnw_ref[...] * (gate * _sigmoid(gate))
        o_ref[:, lanes] = y.astype(BF16)


def _hgrn(st, proj, lbs_l, gnorm_w_l, s0, layer, states_prev):
    hg = HGRN_HG
    wide = hg * LANE
    T = st.seq
    has_s0 = st.latent
    emit_state = not st.latent

    def col_spec(col0):
        return pl.BlockSpec((T, wide), lambda b, h: (b, col0 // hg + h))

    in_specs = [col_spec(COL_HQ), col_spec(COL_HI), col_spec(COL_HFF), col_spec(COL_HFB), col_spec(COL_HG),
                pl.BlockSpec((2, hg, 1, LANE), lambda b, h: (0, h, 0, 0)),
                pl.BlockSpec((1, LANE), lambda b, h: (0, 0))]
    args = [proj, proj, proj, proj, proj, lbs_l.reshape(2, HGRN_HEADS, 1, LANE), gnorm_w_l.reshape(1, LANE)]
    if has_s0:
        in_specs.append(pl.BlockSpec((None, None, 2, hg, HGRN_DK, HGRN_DV), lambda b, h: (b, layer, 0, h, 0, 0)))
        args.append(s0)
    out_specs = [pl.BlockSpec((T, wide), lambda b, h: (b, h))]
    out_shape = [jax.ShapeDtypeStruct((_ntok(st), MIX_WIDTH), BF16)]
    aliases = {}
    n_alias = 0
    if emit_state:
        out_specs.append(pl.BlockSpec((None, None, 2, hg, HGRN_DK, HGRN_DV), lambda b, h: (b, layer, 0, h, 0, 0)))
        out_shape.append(jax.ShapeDtypeStruct((st.nseq, DEPTH, 2, HGRN_HEADS, HGRN_DK, HGRN_DV), F32))
        if states_prev is not None:
            aliases = {len(args): 1}
            in_specs.append(pl.BlockSpec(memory_space=pl.ANY))
            args.append(states_prev)
            n_alias = 1
    C = HGRN_CHUNK
    nl = len(HGRN_LEVELS)
    return pl.pallas_call(
        functools.partial(_hgrn_kernel, T=T, has_s0=has_s0, emit_state=emit_state, n_alias=n_alias),
        grid=(st.nseq, HGRN_HEADS // hg),
        in_specs=in_specs,
        out_specs=out_specs,
        out_shape=out_shape,
        scratch_shapes=[pltpu.VMEM((T, wide), F32), pltpu.VMEM((2 * hg, C, LANE), F32),
                        pltpu.VMEM((2 * hg, HGRN_DV, HGRN_DK), F32), pltpu.VMEM((2, C, C), BF16),
                        pltpu.VMEM((2 * nl, C, C), F32)],
        input_output_aliases=aliases,
        compiler_params=_cparams(("arbitrary", "arbitrary")),
        name="hgrn_" + st.name,
    )(*args)


def _conv_kernel(cb_ref, cc_ref, cx_ref, w_ref, mix_ref, o_ref, *, T):
    del mix_ref
    u = cc_ref[...] * cx_ref[...]
    row = lax.broadcasted_iota(jnp.int32, (T, CONV_WIDTH), 0)
    u_prev = jnp.where(row == 0, 0.0, pltpu.roll(u, 1, axis=0))
    u_next = jnp.where(row == T - 1, 0.0, pltpu.roll(u, T - 1, axis=0))
    y = u_prev * w_ref[0:1, :] + u * w_ref[1:2, :] + u_next * w_ref[2:3, :]
    o_ref[...] = (cb_ref[...] * y).astype(BF16)


def _conv(st, proj, conv_w_l, mix):
    cw = CONV_WIDTH // LANE
    T = st.seq

    def col_spec(col0):
        return pl.BlockSpec((T, CONV_WIDTH), lambda b: (b, col0 // cw))

    return pl.pallas_call(
        functools.partial(_conv_kernel, T=T),
        grid=(st.nseq,),
        in_specs=[col_spec(COL_CB), col_spec(COL_CC), col_spec(COL_CX),
                  pl.BlockSpec((3, CONV_WIDTH), lambda b: (0, 0)),
                  pl.BlockSpec(memory_space=pl.ANY)],
        out_specs=pl.BlockSpec((T, CONV_WIDTH), lambda b: (b, HGRN_WIDTH // CONV_WIDTH)),
        out_shape=jax.ShapeDtypeStruct(mix.shape, mix.dtype),
        input_output_aliases={4: 0},
        compiler_params=_cparams(("arbitrary",)),
        name="conv_" + st.name,
    )(proj, proj, proj, conv_w_l, mix)


MIX_COL_ATTN = (HGRN_WIDTH + CONV_WIDTH) // LANE


def _ctx_attn_kernel(q_ref, k_ref, v_ref, mix_ref, o_ref):
    del mix_ref
    scale = NA_HEAD_DIM ** -0.5
    s = _dot_nt(q_ref[...].astype(BF16), k_ref[...].astype(BF16)) * scale
    m = jnp.max(s, axis=-1, keepdims=True)
    p = jnp.exp(s - m)
    l = jnp.sum(p, axis=-1, keepdims=True)
    o = _dot(p.astype(BF16), v_ref[...].astype(BF16)) / l
    o_ref[...] = o.astype(BF16)


def _ctx_attn(proj, mix):
    def col_spec(col0):
        return pl.BlockSpec((SEQ, LANE), lambda b, h: (b, col0 + h))

    return pl.pallas_call(
        _ctx_attn_kernel,
        grid=(BATCH, NA_HEADS),
        in_specs=[col_spec(COL_NQ), col_spec(COL_NK), col_spec(COL_NV), pl.BlockSpec(memory_space=pl.ANY)],
        out_specs=pl.BlockSpec((SEQ, LANE), lambda b, h: (b, MIX_COL_ATTN + h)),
        out_shape=jax.ShapeDtypeStruct(mix.shape, mix.dtype),
        input_output_aliases={3: 0},
        compiler_params=_cparams(("arbitrary", "arbitrary")),
        name="ctx_attn",
    )(proj, proj, proj, mix)


_KH = min(NA_KH, GRID_ROWS)
_ROW_START = [int(v) for v in np.clip(np.arange(GRID_ROWS) - _KH // 2, 0, GRID_ROWS - _KH)]
NLOC = _KH * GRID_W


def _nat_kernel(q_ref, k_ref, v_ref, kc_ref, vc_ref, bias_ref, mix_ref, o_ref, sctx_ref, pctx_ref, oloc_ref,
                den_ref):
    del mix_ref
    scale = NA_HEAD_DIM ** -0.5
    q_all = q_ref[...].astype(BF16)
    sctx_ref[...] = _dot_nt(q_all, kc_ref[...].astype(BF16)) * scale
    for r in range(GRID_ROWS):
        rows = slice(r * GRID_W, (r + 1) * GRID_W)
        band = slice(_ROW_START[r] * GRID_W, _ROW_START[r] * GRID_W + NLOC)
        q_r = q_ref[rows, :].astype(BF16)
        s_loc = _dot_nt(q_r, k_ref[band, :].astype(BF16)) * scale + bias_ref[r - _ROW_START[r]]
        s_ctx = sctx_ref[rows, :]
        m = jnp.maximum(jnp.max(s_loc, axis=-1, keepdims=True), jnp.max(s_ctx, axis=-1, keepdims=True))
        p_loc = jnp.exp(s_loc - m)
        p_ctx = jnp.exp(s_ctx - m)
        den_ref[rows, :] = jnp.sum(p_loc, axis=-1, keepdims=True) + jnp.sum(p_ctx, axis=-1, keepdims=True)
        pctx_ref[rows, :] = p_ctx.astype(BF16)
        oloc_ref[rows, :] = _dot(p_loc.astype(BF16), v_ref[band, :].astype(BF16))
    o = (oloc_ref[...] + _dot(pctx_ref[...], vc_ref[...].astype(BF16))) / den_ref[...]
    o_ref[...] = o.astype(BF16)


def _nat_bias(rpb_l):
    col = np.arange(GRID_W)
    col_start = np.clip(col - NA_KW // 2, 0, GRID_W - NA_KW)
    col_mask = (col[None, :] >= col_start[:, None]) & (col[None, :] < col_start[:, None] + NA_KW)
    dc_idx = np.clip(col[None, :] - col[:, None] + NA_KW - 1, 0, 2 * NA_KW - 2)
    tab = rpb_l[:, :, dc_idx]
    tab = jnp.where(jnp.asarray(col_mask)[None, None], tab, NEG_BIG)
    slabs = []
    for off in range(_KH):
        dr = np.arange(_KH) - off + (NA_KH - 1)
        slab = tab[:, dr]
        slabs.append(slab.transpose(0, 2, 1, 3).reshape(NA_HEADS, GRID_W, NLOC))
    return jnp.stack(slabs, axis=1)


def _nat(proj, cache_k4, cache_v4, bias, layer, mix):
    def col_spec(col0):
        return pl.BlockSpec((DEC_SEQ, LANE), lambda b, h: (b, col0 + h))

    cache_spec = pl.BlockSpec((None, None, PAST_LEN, LANE), lambda b, h: (b, layer, 0, h))
    return pl.pallas_call(
        _nat_kernel,
        grid=(DEC_BATCH, NA_HEADS),
        in_specs=[col_spec(COL_NQ), col_spec(COL_NK), col_spec(COL_NV), cache_spec, cache_spec,
                  pl.BlockSpec((None, _KH, GRID_W, NLOC), lambda b, h: (h, 0, 0, 0)),
                  pl.BlockSpec(memory_space=pl.ANY)],
        out_specs=pl.BlockSpec((DEC_SEQ, LANE), lambda b, h: (b, MIX_COL_ATTN + h)),
        out_shape=jax.ShapeDtypeStruct(mix.shape, mix.dtype),
        scratch_shapes=[pltpu.VMEM((DEC_SEQ, PAST_LEN), F32), pltpu.VMEM((DEC_SEQ, PAST_LEN), BF16),
                        pltpu.VMEM((DEC_SEQ, LANE), F32), pltpu.VMEM((DEC_SEQ, 1), F32)],
        input_output_aliases={6: 0},
        compiler_params=_cparams(("arbitrary", "arbitrary")),
        name="nat",
    )(proj, proj, proj, cache_k4, cache_v4, bias, mix)


def kernel(x_prompt, x_sample, cache_na_k, cache_na_v, state_hgrn, c, c_ctx, w_ada, b_ada, norm_mix_w, w_in,
           hgrn_lb_raw, hgrn_gnorm_w, conv_w, na_rpb, w_out, norm_ffn_w, w_ffn_gate, w_ffn_up, w_ffn_down,
           final_norm_w):
    xs = {LATENT: x_sample.reshape(_ntok(LATENT), D_MODEL), CONTEXT: x_prompt.reshape(_ntok(CONTEXT), D_MODEL)}
    cvecs = jnp.concatenate([c, c_ctx[None, :], jnp.zeros((MOD_ROWS - DEC_BATCH - 1, D_MODEL), F32)], axis=0)
    mods4 = _adaln(cvecs, w_ada, b_ada).reshape(DEPTH, MOD_ROWS, 1, 6 * D_MODEL)

    p_lb = jax.nn.softmax(hgrn_lb_raw.astype(F32), axis=1)
    cp = jnp.cumsum(p_lb, axis=1)
    lbs = cp - cp[:, :1]

    cache_k4 = cache_na_k.reshape(DEC_BATCH, DEPTH, PAST_LEN, NA_WIDTH)
    cache_v4 = cache_na_v.reshape(DEC_BATCH, DEPTH, PAST_LEN, NA_WIDTH)

    kv = None
    states = None
    for l in range(DEPTH):
        w_in_l = w_in[l].astype(BF16)
        w_out_l = w_out[l].astype(BF16)
        wg, wu, wd = w_ffn_gate[l].astype(BF16), w_ffn_up[l].astype(BF16), w_ffn_down[l].astype(BF16)
        for st in (LATENT, CONTEXT):
            res = _inproj(st, xs[st], mods4, l, norm_mix_w[l], w_in_l, kv)
            proj = res[0]
            if st.latent:
                mix = _hgrn(st, proj, lbs[:, l], hgrn_gnorm_w[l], state_hgrn, l, None)[0]
                mix = _conv(st, proj, conv_w[l], mix)
                mix = _nat(proj, cache_k4, cache_v4, _nat_bias(na_rpb[l]), l, mix)
            else:
                kv = (res[1], res[2])
                mix, states = _hgrn(st, proj, lbs[:, l], hgrn_gnorm_w[l], None, l, states)
                mix = _conv(st, proj, conv_w[l], mix)
                mix = _ctx_attn(proj, mix)
            x1 = _outproj(st, xs[st], mix, mods4, l, w_out_l)
            xs[st] = _ffn(st, x1, mods4, l, norm_ffn_w[l], final_norm_w, wg, wu, wd, final_norm=(l == DEPTH - 1))

    y_sample = xs[LATENT].reshape(DEC_BATCH, DEC_SEQ, D_MODEL)
    y_prompt = xs[CONTEXT].reshape(BATCH, SEQ, D_MODEL)
    new_k = kv[0].reshape(BATCH, DEPTH, SEQ, NA_HEADS, NA_HEAD_DIM)
    new_v = kv[1].reshape(BATCH, DEPTH, SEQ, NA_HEADS, NA_HEAD_DIM)
    return (y_prompt, y_sample, new_k, new_v, states)
```

```python
import collections
import functools

import numpy as np
import jax
import jax.numpy as jnp
from jax import lax
from jax.experimental import pallas as pl
from jax.experimental.pallas import tpu as pltpu

F32 = jnp.float32
BF16 = jnp.bfloat16

D_MODEL = 2048
BATCH = 16
SEQ = 256
DEPTH = 2
DEC_BATCH = 8
DEC_SEQ = 1024
PAST_LEN = 512
GRID_W = 64
HGRN_HEADS = 8
HGRN_DK = 128
HGRN_DV = 128
HGRN_WIDTH = HGRN_HEADS * HGRN_DV
CONV_WIDTH = 512
NA_HEADS = 4
NA_HEAD_DIM = 128
NA_WIDTH = NA_HEADS * NA_HEAD_DIM
NA_KH = 8
NA_KW = 16
MIX_WIDTH = HGRN_WIDTH + CONV_WIDTH + NA_WIDTH
IN_PROJ_WIDTH = 5 * HGRN_WIDTH + 3 * CONV_WIDTH + 3 * NA_WIDTH
FFN_HIDDEN = ((8 * D_MODEL + 3 * 256 - 1) // (3 * 256)) * 256
EPS = 1e-6

MOD_ROWS = 16
CTX_ROW = DEC_BATCH
LANE = 128
SUBLANE = 8
GRID_ROWS = DEC_SEQ // GRID_W

COL_HQ = 0
COL_HI = HGRN_WIDTH // LANE
COL_HFF = 2 * HGRN_WIDTH // LANE
COL_HFB = 3 * HGRN_WIDTH // LANE
COL_HG = 4 * HGRN_WIDTH // LANE
COL_CB = 5 * HGRN_WIDTH // LANE
COL_CC = COL_CB + CONV_WIDTH // LANE
COL_CX = COL_CC + CONV_WIDTH // LANE
COL_NQ = COL_CX + CONV_WIDTH // LANE
COL_NK = COL_NQ + NA_WIDTH // LANE
COL_NV = COL_NK + NA_WIDTH // LANE

NEG_BIG = -1e30
VMEM_LIMIT = 56 * 1024 * 1024

Stream = collections.namedtuple("Stream", ["name", "seq", "nseq", "latent"])
LATENT = Stream("lat", DEC_SEQ, DEC_BATCH, True)
CONTEXT = Stream("ctx", SEQ, BATCH, False)


def _ntok(st):
    return st.seq * st.nseq


def _cparams(sem):
    return pltpu.CompilerParams(dimension_semantics=sem, vmem_limit_bytes=VMEM_LIMIT)


def _dot(a, b):
    return jnp.dot(a, b, preferred_element_type=F32)


def _dot_nt(a, b):
    return lax.dot_general(a, b, (((1,), (1,)), ((), ())), preferred_element_type=F32)


def _dot_tn(a, b):
    return lax.dot_general(a, b, (((0,), (0,)), ((), ())), preferred_element_type=F32)


def _sigmoid(x):
    return 1.0 / (1.0 + jnp.exp(-x))


ADA_TN = 1024


def _adaln_kernel(c_ref, w_ref, b_ref, o_ref):
    c = c_ref[...]
    s = (c * _sigmoid(c)).astype(BF16)
    o_ref[...] = _dot(s, w_ref[...].astype(BF16)) + b_ref[...]


def _adaln(cvecs, w_ada, b_ada):
    n = 6 * D_MODEL
    return pl.pallas_call(
        _adaln_kernel,
        grid=(DEPTH, n // ADA_TN),
        in_specs=[
            pl.BlockSpec((MOD_ROWS, D_MODEL), lambda l, j: (0, 0)),
            pl.BlockSpec((None, D_MODEL, ADA_TN), lambda l, j: (l, 0, j)),
            pl.BlockSpec((None, 1, ADA_TN), lambda l, j: (l, 0, j)),
        ],
        out_specs=pl.BlockSpec((None, MOD_ROWS, ADA_TN), lambda l, j: (l, 0, j)),
        out_shape=jax.ShapeDtypeStruct((DEPTH, MOD_ROWS, n), F32),
        compiler_params=_cparams(("arbitrary", "arbitrary")),
        name="adaln",
    )(cvecs, w_ada, b_ada.reshape(DEPTH, 1, n))


IN_TM = 1024
IN_TN = 1024
TM = 512
FFN_TF = 512


def _mod_spec(st, layer, chunk, tm, ngrid):
    def row(i):
        return (i * tm) // DEC_SEQ if st.latent else CTX_ROW

    if ngrid == 1:
        return pl.BlockSpec((None, None, 1, D_MODEL), lambda i: (layer, row(i), 0, chunk))
    return pl.BlockSpec((None, None, 1, D_MODEL), lambda i, j: (layer, row(i), 0, chunk))


PROLOGUE_ROWS = 256


def _norm_modulate(x_ref, h_ref, nw_ref, sh_ref, sc_ref):
    gain = nw_ref[...] * (1.0 + sc_ref[...])
    shift = sh_ref[...]
    for r in range(0, x_ref.shape[0], PROLOGUE_ROWS):
        x = x_ref[r:r + PROLOGUE_ROWS, :]
        ms = jnp.mean(x * x, axis=-1, keepdims=True)
        h_ref[r:r + PROLOGUE_ROWS, :] = (x * lax.rsqrt(ms + EPS) * gain + shift).astype(BF16)


def _inproj_kernel(*refs, emit_kv):
    x_ref, sh_ref, sc_ref, nw_ref, w_ref = refs[:5]
    outs = refs[-4:] if emit_kv else refs[-2:]
    o_ref = outs[0]
    h_ref = refs[-1]

    @pl.when(pl.program_id(1) == 0)
    def _():
        _norm_modulate(x_ref, h_ref, nw_ref, sh_ref, sc_ref)

    o_ref[...] = _dot(h_ref[...], w_ref[...])

    if emit_kv:
        @pl.when(pl.program_id(1) == pl.num_programs(1) - 1)
        def _():
            k_ref, v_ref = outs[1], outs[2]
            k_ref[...] = o_ref[:, :NA_WIDTH].reshape(k_ref.shape)
            v_ref[...] = o_ref[:, NA_WIDTH:].reshape(v_ref.shape)


def _layer_row_spec(layer, width, ngrid):
    if ngrid == 1:
        return pl.BlockSpec((None, 1, width), lambda i: (layer, 0, 0))
    return pl.BlockSpec((None, 1, width), lambda i, j: (layer, 0, 0))


def _inproj(st, x, mods4, layer, norm_w, w_bf16, kv_prev):
    emit_kv = not st.latent
    assert COL_NK * LANE == IN_PROJ_WIDTH - IN_TN and 2 * NA_WIDTH == IN_TN
    ntok = _ntok(st)
    in_specs = [
        pl.BlockSpec((IN_TM, D_MODEL), lambda i, j: (i, 0)),
        _mod_spec(st, layer, 0, IN_TM, 2),
        _mod_spec(st, layer, 1, IN_TM, 2),
        _layer_row_spec(layer, D_MODEL, 2),
        pl.BlockSpec((None, D_MODEL, IN_TN), lambda i, j: (layer, 0, j)),
    ]
    args = [x, mods4, mods4, norm_w, w_bf16]
    out_specs = [pl.BlockSpec((IN_TM, IN_TN), lambda i, j: (i, j))]
    out_shape = [jax.ShapeDtypeStruct((ntok, IN_PROJ_WIDTH), F32)]
    aliases = {}
    if emit_kv:
        nb = IN_TM // st.seq
        kv_spec = pl.BlockSpec((nb, None, st.seq, NA_WIDTH), lambda i, j: (i, layer, 0, 0))
        out_specs += [kv_spec, kv_spec]
        out_shape += [jax.ShapeDtypeStruct((st.nseq, DEPTH, st.seq, NA_WIDTH), F32)] * 2
        if kv_prev is not None:
            in_specs += [pl.BlockSpec(memory_space=pl.ANY)] * 2
            args += list(kv_prev)
            aliases = {5: 1, 6: 2}
    return pl.pallas_call(
        functools.partial(_inproj_kernel, emit_kv=emit_kv),
        grid=(ntok // IN_TM, IN_PROJ_WIDTH // IN_TN),
        in_specs=in_specs,
        out_specs=out_specs,
        out_shape=out_shape,
        scratch_shapes=[pltpu.VMEM((IN_TM, D_MODEL), BF16)],
        input_output_aliases=aliases,
        compiler_params=_cparams(("arbitrary", "arbitrary")),
        name="inproj_" + st.name,
    )(*args)


def _outproj_kernel(x_ref, m_ref, g_ref, w_ref, o_ref):
    o_ref[...] = x_ref[...] + g_ref[...] * _dot(m_ref[...], w_ref[...])


def _outproj(st, x, mix, mods4, layer, w_bf16):
    ntok = _ntok(st)
    return pl.pallas_call(
        _outproj_kernel,
        grid=(ntok // TM,),
        in_specs=[
            pl.BlockSpec((TM, D_MODEL), lambda i: (i, 0)),
            pl.BlockSpec((TM, MIX_WIDTH), lambda i: (i, 0)),
            _mod_spec(st, layer, 2, TM, 1),
            pl.BlockSpec((None, MIX_WIDTH, D_MODEL), lambda i: (layer, 0, 0)),
        ],
        out_specs=pl.BlockSpec((TM, D_MODEL), lambda i: (i, 0)),
        out_shape=jax.ShapeDtypeStruct((ntok, D_MODEL), F32),
        compiler_params=_cparams(("arbitrary",)),
        name="outproj_" + st.name,
    )(x, mix, mods4, w_bf16)


def _ffn_kernel(x_ref, sh_ref, sc_ref, g_ref, nw_ref, fw_ref, wg_ref, wu_ref, wd_ref, o_ref, h_ref, acc_ref, *,
                final_norm):
    j = pl.program_id(1)

    @pl.when(j == 0)
    def _():
        _norm_modulate(x_ref, h_ref, nw_ref, sh_ref, sc_ref)
        acc_ref[...] = jnp.zeros_like(acc_ref)

    h = h_ref[...]
    a = _dot(h, wg_ref[...])
    u = _dot(h, wu_ref[...])
    t = (a * _sigmoid(a) * u).astype(BF16)
    acc_ref[...] += _dot(t, wd_ref[...])

    @pl.when(j == pl.num_programs(1) - 1)
    def _():
        y = x_ref[...] + g_ref[...] * acc_ref[...]
        if final_norm:
            ms = jnp.mean(y * y, axis=-1, keepdims=True)
            y = y * lax.rsqrt(ms + EPS) * fw_ref[...]
        o_ref[...] = y


def _ffn(st, x, mods4, layer, norm_w, final_w, wg, wu, wd, final_norm):
    ntok = _ntok(st)
    return pl.pallas_call(
        functools.partial(_ffn_kernel, final_norm=final_norm),
        grid=(ntok // TM, FFN_HIDDEN // FFN_TF),
        in_specs=[
            pl.BlockSpec((TM, D_MODEL), lambda i, j: (i, 0)),
            _mod_spec(st, layer, 3, TM, 2),
            _mod_spec(st, layer, 4, TM, 2),
            _mod_spec(st, layer, 5, TM, 2),
            _layer_row_spec(layer, D_MODEL, 2),
            pl.BlockSpec((1, D_MODEL), lambda i, j: (0, 0)),
            pl.BlockSpec((None, D_MODEL, FFN_TF), lambda i, j: (layer, 0, j)),
            pl.BlockSpec((None, D_MODEL, FFN_TF), lambda i, j: (layer, 0, j)),
            pl.BlockSpec((None, FFN_TF, D_MODEL), lambda i, j: (layer, j, 0)),
        ],
        out_specs=pl.BlockSpec((TM, D_MODEL), lambda i, j: (i, 0)),
        out_shape=jax.ShapeDtypeStruct((ntok, D_MODEL), F32),
        scratch_shapes=[pltpu.VMEM((TM, D_MODEL), BF16), pltpu.VMEM((TM, D_MODEL), F32)],
        compiler_params=_cparams(("arbitrary", "arbitrary")),
        name="ffn_" + st.name,
    )(x, mods4, mods4, mods4, norm_w, final_w.reshape(1, D_MODEL), wg, wu, wd)


HGRN_CHUNK = 64
HGRN_HG = 2
HGRN_UNROLL = 4
HGRN_LEVELS = [HGRN_CHUNK >> (i + 1) for i in range(HGRN_CHUNK.bit_length() - 1)]


LOG2E = 1.4426950408889634


def _hgrn_gates(x, lb):
    e = jnp.exp(-jnp.abs(x))
    r = 1.0 / (1.0 + e)
    er = e * r
    pos = x >= 0.0
    sig = jnp.where(pos, r, er)
    nsig = jnp.where(pos, er, r)
    ls2 = jnp.minimum(x, 0.0) * LOG2E + jnp.log2(r)
    f = lb + (1.0 - lb) * sig
    log2_f = jnp.where(lb > 0.0, jnp.log2(f), ls2)
    return log2_f, f, (1.0 - lb) * nsig


def _hgrn_chunk_a(x, lb, b_ref, tri):
    g, f, k = _hgrn_gates(x, lb)
    g_hi = g.astype(BF16)
    g_lo = (g - g_hi.astype(F32)).astype(BF16)
    b = _dot(tri, g_hi) + _dot(tri, g_lo)
    b_ref[...] = b
    return b, f, k


def _hgrn_chunk_b(b, f, k, q, v, st, b_ref, mask_ref, dirn):
    C = HGRN_CHUNK
    fwd = dirn == 0
    b_end = b_ref[pl.ds(C - 1 if fwd else 0, 1), :]
    vb = v.astype(BF16)
    o = _dot_nt((q * jnp.exp2(b)).astype(BF16), st.astype(BF16))
    kd = (k * jnp.exp2(b_end - b)).astype(BF16)
    st_new = jnp.exp2(b_end) * st + _dot_tn(vb, kd)

    row = lax.broadcasted_iota(jnp.int32, (C, LANE), 0)
    sub = lax.broadcasted_iota(jnp.int32, (SUBLANE, LANE), 0)
    attn = None
    for lvl, m in enumerate(HGRN_LEVELS):
        if m >= SUBLANE:
            parts = []
            for p in range(C // (2 * m)):
                lo = p * 2 * m
                left = slice(lo, lo + m)
                right = slice(lo + m, lo + 2 * m)
                beta = b_ref[pl.ds(lo + (m - 1 if fwd else m), 1), :]
                if fwd:
                    parts += [k[left] * jnp.exp2(beta - b[left]), q[right] * jnp.exp2(b[right] - beta)]
                else:
                    parts += [q[left] * jnp.exp2(b[left] - beta), k[right] * jnp.exp2(beta - b[right])]
            w = jnp.concatenate(parts, axis=0)
        else:
            reads = ((row & m) != 0) if fwd else ((row & m) == 0)
            if m == 1:
                w = jnp.where(reads, q * f, k)
            else:
                betas = []
                for u in range(C // SUBLANE):
                    rows = []
                    for p in range(SUBLANE // (2 * m)):
                        rows.append(b_ref[pl.ds(u * SUBLANE + p * 2 * m + (m - 1 if fwd else m), 1), :])
                    beta_u = rows[-1]
                    for p in range(len(rows) - 2, -1, -1):
                        beta_u = jnp.where(sub < (p + 1) * 2 * m, rows[p], beta_u)
                    betas.append(jnp.broadcast_to(beta_u, (SUBLANE, LANE)))
                beta = jnp.concatenate(betas, axis=0)
                w = jnp.where(reads, q, k) * jnp.exp2(-jnp.abs(b - beta))
        wb = w.astype(BF16)
        term = _dot_nt(wb, wb) * mask_ref[dirn * len(HGRN_LEVELS) + lvl]
        attn = term if attn is None else attn + term
    o += jnp.sum(q * k, axis=-1, keepdims=True) * v
    return o, attn, vb, st_new


def _hgrn_kernel(*refs, T, has_s0, emit_state, n_alias):
    q_ref, v_ref, ff_ref, fb_ref, g_ref, lb_ref, gnw_ref = refs[:7]
    s0_ref = refs[7] if has_s0 else None
    n_out = 2 if emit_state else 1
    outs = refs[len(refs) - 5 - n_out:len(refs) - 5]
    o_ref = outs[0]
    s_out_ref = outs[1] if emit_state else None
    oacc_ref, b_ref, st_ref, tri_ref, mask_ref = refs[-5:]
    assert len(refs) == 7 + int(has_s0) + n_alias + n_out + 5

    C = HGRN_CHUNK
    U = HGRN_UNROLL
    nc = T // C
    assert nc % U == 0
    nl = len(HGRN_LEVELS)

    @pl.when((pl.program_id(0) == 0) & (pl.program_id(1) == 0))
    def _():
        row = lax.broadcasted_iota(jnp.int32, (C, C), 0)
        col = lax.broadcasted_iota(jnp.int32, (C, C), 1)
        tri_ref[0] = jnp.where(col <= row, 1.0, 0.0).astype(BF16)
        tri_ref[1] = jnp.where(col >= row, 1.0, 0.0).astype(BF16)
        for lvl, m in enumerate(HGRN_LEVELS):
            shift = (2 * m).bit_length() - 1
            same = (row >> shift) == (col >> shift)
            hi_lo = ((row & m) != 0) & ((col & m) == 0)
            lo_hi = ((row & m) == 0) & ((col & m) != 0)
            mask_ref[lvl] = jnp.where(same & hi_lo, 1.0, 0.0)
            mask_ref[nl + lvl] = jnp.where(same & lo_hi, 1.0, 0.0)

    oacc_ref[...] = jnp.zeros_like(oacc_ref)
    for hh in range(HGRN_HG):
        for d in range(2):
            if has_s0:
                st_ref[2 * hh + d] = s0_ref[d, hh].T
            else:
                st_ref[2 * hh + d] = jnp.zeros((HGRN_DV, HGRN_DK), F32)

    q_scale = HGRN_DK ** -0.5

    def body(c, carry):
        chains = []
        for hh in range(HGRN_HG):
            lanes = slice(hh * LANE, (hh + 1) * LANE)
            for d in range(2):
                for u in range(U):
                    cc = c * U + u if d == 0 else nc - 1 - (c * U + u)
                    rows = pl.ds(pl.multiple_of(cc * C, C), C)
                    chains.append((hh, d, u, rows, lanes))
        stage_a = []
        for n, (hh, d, u, rows, lanes) in enumerate(chains):
            x_ref = ff_ref if d == 0 else fb_ref
            stage_a.append(_hgrn_chunk_a(x_ref[rows, lanes], lb_ref[d, hh], b_ref.at[n], tri_ref[d]))
        stage_b = []
        st = None
        for n, ((hh, d, u, rows, lanes), (b, f, k)) in enumerate(zip(chains, stage_a)):
            idx = 2 * hh + d
            if u == 0:
                st = st_ref[idx]
            o, attn, vb, st = _hgrn_chunk_b(b, f, k, q_ref[rows, lanes] * q_scale, v_ref[rows, lanes], st,
                                            b_ref.at[n], mask_ref, d)
            if u == U - 1:
                st_ref[idx] = st
            stage_b.append((o, attn, vb))
        for (hh, d, u, rows, lanes), (o, attn, vb) in zip(chains, stage_b):
            oacc_ref[rows, lanes] += o + _dot(attn.astype(BF16), vb)
        return carry

    lax.fori_loop(0, nc // U, body, 0)

    for hh in range(HGRN_HG):
        lanes = slice(hh * LANE, (hh + 1) * LANE)
        if emit_state:
            for d in range(2):
                s_out_ref[d, hh] = st_ref[2 * hh + d].T
        o = oacc_ref[:, lanes]
        ms = jnp.mean(o * o, axis=-1, keepdims=True)
        gate = g_ref[:, lanes]
        y = o * lax.rsqrt(ms + EPS) * gnw_ref[...] * (gate * _sigmoid(gate))
        o_ref[:, lanes] = y.astype(BF16)


def _hgrn(st, proj, lbs, gnorm_w, s0, layer, states_prev):
    hg = HGRN_HG
    wide = hg * LANE
    T = st.seq
    has_s0 = st.latent
    emit_state = not st.latent

    def col_spec(col0):
        return pl.BlockSpec((T, wide), lambda b, h: (b, col0 // hg + h))

    in_specs = [col_spec(COL_HQ), col_spec(COL_HI), col_spec(COL_HFF), col_spec(COL_HFB), col_spec(COL_HG),
                pl.BlockSpec((2, None, hg, 1, LANE), lambda b, h: (0, layer, h, 0, 0)),
                _layer_row_spec(layer, LANE, 2)]
    args = [proj, proj, proj, proj, proj, lbs, gnorm_w]
    if has_s0:
        in_specs.append(pl.BlockSpec((None, None, 2, hg, HGRN_DK, HGRN_DV), lambda b, h: (b, layer, 0, h, 0, 0)))
        args.append(s0)
    out_specs = [pl.BlockSpec((T, wide), lambda b, h: (b, h))]
    out_shape = [jax.ShapeDtypeStruct((_ntok(st), MIX_WIDTH), BF16)]
    aliases = {}
    n_alias = 0
    if emit_state:
        out_specs.append(pl.BlockSpec((None, None, 2, hg, HGRN_DK, HGRN_DV), lambda b, h: (b, layer, 0, h, 0, 0)))
        out_shape.append(jax.ShapeDtypeStruct((st.nseq, DEPTH, 2, HGRN_HEADS, HGRN_DK, HGRN_DV), F32))
        if states_prev is not None:
            aliases = {len(args): 1}
            in_specs.append(pl.BlockSpec(memory_space=pl.ANY))
            args.append(states_prev)
            n_alias = 1
    C = HGRN_CHUNK
    nl = len(HGRN_LEVELS)
    return pl.pallas_call(
        functools.partial(_hgrn_kernel, T=T, has_s0=has_s0, emit_state=emit_state, n_alias=n_alias),
        grid=(st.nseq, HGRN_HEADS // hg),
        in_specs=in_specs,
        out_specs=out_specs,
        out_shape=out_shape,
        scratch_shapes=[pltpu.VMEM((T, wide), F32), pltpu.VMEM((2 * hg * HGRN_UNROLL, C, LANE), F32),
                        pltpu.VMEM((2 * hg, HGRN_DV, HGRN_DK), F32), pltpu.VMEM((2, C, C), BF16),
                        pltpu.VMEM((2 * nl, C, C), F32)],
        input_output_aliases=aliases,
        compiler_params=_cparams(("arbitrary", "arbitrary")),
        name="hgrn_" + st.name,
    )(*args)


def _conv_kernel(cb_ref, cc_ref, cx_ref, w_ref, mix_ref, o_ref, *, T):
    del mix_ref
    u = cc_ref[...] * cx_ref[...]
    row = lax.broadcasted_iota(jnp.int32, (T, CONV_WIDTH), 0)
    u_prev = jnp.where(row == 0, 0.0, pltpu.roll(u, 1, axis=0))
    u_next = jnp.where(row == T - 1, 0.0, pltpu.roll(u, T - 1, axis=0))
    y = u_prev * w_ref[0:1, :] + u * w_ref[1:2, :] + u_next * w_ref[2:3, :]
    o_ref[...] = (cb_ref[...] * y).astype(BF16)


def _conv(st, proj, conv_w, layer, mix):
    cw = CONV_WIDTH // LANE
    T = st.seq

    def col_spec(col0):
        return pl.BlockSpec((T, CONV_WIDTH), lambda b: (b, col0 // cw))

    return pl.pallas_call(
        functools.partial(_conv_kernel, T=T),
        grid=(st.nseq,),
        in_specs=[col_spec(COL_CB), col_spec(COL_CC), col_spec(COL_CX),
                  pl.BlockSpec((None, 3, CONV_WIDTH), lambda b: (layer, 0, 0)),
                  pl.BlockSpec(memory_space=pl.ANY)],
        out_specs=pl.BlockSpec((T, CONV_WIDTH), lambda b: (b, HGRN_WIDTH // CONV_WIDTH)),
        out_shape=jax.ShapeDtypeStruct(mix.shape, mix.dtype),
        input_output_aliases={4: 0},
        compiler_params=_cparams(("arbitrary",)),
        name="conv_" + st.name,
    )(proj, proj, proj, conv_w, mix)


MIX_COL_ATTN = (HGRN_WIDTH + CONV_WIDTH) // LANE


def _ctx_attn_kernel(q_ref, k_ref, v_ref, mix_ref, o_ref):
    del mix_ref
    scale = NA_HEAD_DIM ** -0.5
    s = _dot_nt(q_ref[...].astype(BF16), k_ref[...].astype(BF16)) * scale
    m = jnp.max(s, axis=-1, keepdims=True)
    p = jnp.exp(s - m)
    l = jnp.sum(p, axis=-1, keepdims=True)
    o = _dot(p.astype(BF16), v_ref[...].astype(BF16)) / l
    o_ref[...] = o.astype(BF16)


def _ctx_attn(proj, mix):
    def col_spec(col0):
        return pl.BlockSpec((SEQ, LANE), lambda b, h: (b, col0 + h))

    return pl.pallas_call(
        _ctx_attn_kernel,
        grid=(BATCH, NA_HEADS),
        in_specs=[col_spec(COL_NQ), col_spec(COL_NK), col_spec(COL_NV), pl.BlockSpec(memory_space=pl.ANY)],
        out_specs=pl.BlockSpec((SEQ, LANE), lambda b, h: (b, MIX_COL_ATTN + h)),
        out_shape=jax.ShapeDtypeStruct(mix.shape, mix.dtype),
        input_output_aliases={3: 0},
        compiler_params=_cparams(("arbitrary", "arbitrary")),
        name="ctx_attn",
    )(proj, proj, proj, mix)


_KH = min(NA_KH, GRID_ROWS)
_ROW_START = [int(v) for v in np.clip(np.arange(GRID_ROWS) - _KH // 2, 0, GRID_ROWS - _KH)]
NLOC = _KH * GRID_W


def _nat_kernel(q_ref, k_ref, v_ref, kc_ref, vc_ref, bias_ref, mix_ref, o_ref, sctx_ref, pctx_ref, oloc_ref,
                den_ref):
    del mix_ref
    scale = NA_HEAD_DIM ** -0.5
    q_all = q_ref[...].astype(BF16)
    sctx_ref[...] = _dot_nt(q_all, kc_ref[...].astype(BF16)) * scale
    for r in range(GRID_ROWS):
        rows = slice(r * GRID_W, (r + 1) * GRID_W)
        band = slice(_ROW_START[r] * GRID_W, _ROW_START[r] * GRID_W + NLOC)
        q_r = q_ref[rows, :].astype(BF16)
        s_loc = _dot_nt(q_r, k_ref[band, :].astype(BF16)) * scale + bias_ref[r - _ROW_START[r]]
        s_ctx = sctx_ref[rows, :]
        m = jnp.maximum(jnp.max(s_loc, axis=-1, keepdims=True), jnp.max(s_ctx, axis=-1, keepdims=True))
        p_loc = jnp.exp(s_loc - m)
        p_ctx = jnp.exp(s_ctx - m)
        den_ref[rows, :] = jnp.sum(p_loc, axis=-1, keepdims=True) + jnp.sum(p_ctx, axis=-1, keepdims=True)
        pctx_ref[rows, :] = p_ctx.astype(BF16)
        oloc_ref[rows, :] = _dot(p_loc.astype(BF16), v_ref[band, :].astype(BF16))
    o = (oloc_ref[...] + _dot(pctx_ref[...], vc_ref[...].astype(BF16))) / den_ref[...]
    o_ref[...] = o.astype(BF16)


def _nat_bias(rpb_l):
    col = np.arange(GRID_W)
    col_start = np.clip(col - NA_KW // 2, 0, GRID_W - NA_KW)
    col_mask = (col[None, :] >= col_start[:, None]) & (col[None, :] < col_start[:, None] + NA_KW)
    dc_idx = np.clip(col[None, :] - col[:, None] + NA_KW - 1, 0, 2 * NA_KW - 2)
    tab = rpb_l[:, :, dc_idx]
    tab = jnp.where(jnp.asarray(col_mask)[None, None], tab, NEG_BIG)
    slabs = []
    for off in range(_KH):
        dr = np.arange(_KH) - off + (NA_KH - 1)
        slab = tab[:, dr]
        slabs.append(slab.transpose(0, 2, 1, 3).reshape(NA_HEADS, GRID_W, NLOC))
    return jnp.stack(slabs, axis=1)


def _nat(proj, cache_k4, cache_v4, bias, layer, mix):
    def col_spec(col0):
        return pl.BlockSpec((DEC_SEQ, LANE), lambda b, h: (b, col0 + h))

    cache_spec = pl.BlockSpec((None, None, PAST_LEN, LANE), lambda b, h: (b, layer, 0, h))
    return pl.pallas_call(
        _nat_kernel,
        grid=(DEC_BATCH, NA_HEADS),
        in_specs=[col_spec(COL_NQ), col_spec(COL_NK), col_spec(COL_NV), cache_spec, cache_spec,
                  pl.BlockSpec((None, _KH, GRID_W, NLOC), lambda b, h: (h, 0, 0, 0)),
                  pl.BlockSpec(memory_space=pl.ANY)],
        out_specs=pl.BlockSpec((DEC_SEQ, LANE), lambda b, h: (b, MIX_COL_ATTN + h)),
        out_shape=jax.ShapeDtypeStruct(mix.shape, mix.dtype),
        scratch_shapes=[pltpu.VMEM((DEC_SEQ, PAST_LEN), F32), pltpu.VMEM((DEC_SEQ, PAST_LEN), BF16),
                        pltpu.VMEM((DEC_SEQ, LANE), F32), pltpu.VMEM((DEC_SEQ, 1), F32)],
        input_output_aliases={6: 0},
        compiler_params=_cparams(("arbitrary", "arbitrary")),
        name="nat",
    )(proj, proj, proj, cache_k4, cache_v4, bias, mix)


def kernel(x_prompt, x_sample, cache_na_k, cache_na_v, state_hgrn, c, c_ctx, w_ada, b_ada, norm_mix_w, w_in,
           hgrn_lb_raw, hgrn_gnorm_w, conv_w, na_rpb, w_out, norm_ffn_w, w_ffn_gate, w_ffn_up, w_ffn_down,
           final_norm_w):
    xs = {LATENT: x_sample.reshape(_ntok(LATENT), D_MODEL), CONTEXT: x_prompt.reshape(_ntok(CONTEXT), D_MODEL)}
    cvecs = jnp.concatenate([c, c_ctx[None, :], jnp.zeros((MOD_ROWS - DEC_BATCH - 1, D_MODEL), F32)], axis=0)
    mods4 = _adaln(cvecs, w_ada, b_ada).reshape(DEPTH, MOD_ROWS, 1, 6 * D_MODEL)

    p_lb = jax.nn.softmax(hgrn_lb_raw.astype(F32), axis=1)
    cp = jnp.cumsum(p_lb, axis=1)
    lbs = (cp - cp[:, :1]).reshape(2, DEPTH, HGRN_HEADS, 1, LANE)
    gnorm_w = hgrn_gnorm_w.reshape(DEPTH, 1, LANE)
    norm_mix = norm_mix_w.reshape(DEPTH, 1, D_MODEL)
    norm_ffn = norm_ffn_w.reshape(DEPTH, 1, D_MODEL)

    cache_k4 = cache_na_k.reshape(DEC_BATCH, DEPTH, PAST_LEN, NA_WIDTH)
    cache_v4 = cache_na_v.reshape(DEC_BATCH, DEPTH, PAST_LEN, NA_WIDTH)

    w_in_b, w_out_b = w_in.astype(BF16), w_out.astype(BF16)
    wg, wu, wd = w_ffn_gate.astype(BF16), w_ffn_up.astype(BF16), w_ffn_down.astype(BF16)

    kv = None
    states = None
    for l in range(DEPTH):
        for st in (LATENT, CONTEXT):
            res = _inproj(st, xs[st], mods4, l, norm_mix, w_in_b, kv)
            proj = res[0]
            if st.latent:
                mix = _hgrn(st, proj, lbs, gnorm_w, state_hgrn, l, None)[0]
                mix = _conv(st, proj, conv_w, l, mix)
                mix = _nat(proj, cache_k4, cache_v4, _nat_bias(na_rpb[l]), l, mix)
            else:
                kv = (res[1], res[2])
                mix, states = _hgrn(st, proj, lbs, gnorm_w, None, l, states)
                mix = _conv(st, proj, conv_w, l, mix)
                mix = _ctx_attn(proj, mix)
            x1 = _outproj(st, xs[st], mix, mods4, l, w_out_b)
            xs[st] = _ffn(st, x1, mods4, l, norm_ffn, final_norm_w, wg, wu, wd, final_norm=(l == DEPTH - 1))

    y_sample = xs[LATENT].reshape(DEC_BATCH, DEC_SEQ, D_MODEL)
    y_prompt = xs[CONTEXT].reshape(BATCH, SEQ, D_MODEL)
    new_k = kv[0].reshape(BATCH, DEPTH, SEQ, NA_HEADS, NA_HEAD_DIM)
    new_v = kv[1].reshape(BATCH, DEPTH, SEQ, NA_HEADS, NA_HEAD_DIM)
    return (y_prompt, y_sample, new_k, new_v, states)
```

```python
import collections
import functools

import numpy as np
import jax
import jax.numpy as jnp
from jax import lax
from jax.experimental import pallas as pl
from jax.experimental.pallas import tpu as pltpu

F32 = jnp.float32
BF16 = jnp.bfloat16

D_MODEL = 2048
BATCH = 16
SEQ = 256
DEPTH = 2
DEC_BATCH = 8
DEC_SEQ = 1024
PAST_LEN = 512
GRID_W = 64
HGRN_HEADS = 8
HGRN_DK = 128
HGRN_DV = 128
HGRN_WIDTH = HGRN_HEADS * HGRN_DV
CONV_WIDTH = 512
NA_HEADS = 4
NA_HEAD_DIM = 128
NA_WIDTH = NA_HEADS * NA_HEAD_DIM
NA_KH = 8
NA_KW = 16
MIX_WIDTH = HGRN_WIDTH + CONV_WIDTH + NA_WIDTH
IN_PROJ_WIDTH = 5 * HGRN_WIDTH + 3 * CONV_WIDTH + 3 * NA_WIDTH
FFN_HIDDEN = ((8 * D_MODEL + 3 * 256 - 1) // (3 * 256)) * 256
EPS = 1e-6

MOD_ROWS = 16
CTX_ROW = DEC_BATCH
LANE = 128
SUBLANE = 8
GRID_ROWS = DEC_SEQ // GRID_W

COL_HQ = 0
COL_HI = HGRN_WIDTH // LANE
COL_HFF = 2 * HGRN_WIDTH // LANE
COL_HFB = 3 * HGRN_WIDTH // LANE
COL_HG = 4 * HGRN_WIDTH // LANE
COL_CB = 5 * HGRN_WIDTH // LANE
COL_CC = COL_CB + CONV_WIDTH // LANE
COL_CX = COL_CC + CONV_WIDTH // LANE
COL_NQ = COL_CX + CONV_WIDTH // LANE
COL_NK = COL_NQ + NA_WIDTH // LANE
COL_NV = COL_NK + NA_WIDTH // LANE

NEG_BIG = -1e30
VMEM_LIMIT = 60 * 1024 * 1024

Stream = collections.namedtuple("Stream", ["name", "seq", "nseq", "latent"])
LATENT = Stream("lat", DEC_SEQ, DEC_BATCH, True)
CONTEXT = Stream("ctx", SEQ, BATCH, False)


def _ntok(st):
    return st.seq * st.nseq


def _cparams(sem):
    return pltpu.CompilerParams(dimension_semantics=sem, vmem_limit_bytes=VMEM_LIMIT)


def _dot(a, b):
    return jnp.dot(a, b, preferred_element_type=F32)


def _dot_nt(a, b):
    return lax.dot_general(a, b, (((1,), (1,)), ((), ())), preferred_element_type=F32)


def _dot_tn(a, b):
    return lax.dot_general(a, b, (((0,), (0,)), ((), ())), preferred_element_type=F32)


def _sigmoid(x):
    return 1.0 / (1.0 + jnp.exp(-x))


ADA_TN = 1024


def _adaln_kernel(c_ref, w_ref, b_ref, o_ref):
    c = c_ref[...]
    s = (c * _sigmoid(c)).astype(BF16)
    o_ref[...] = _dot(s, w_ref[...].astype(BF16)) + b_ref[...]


def _adaln(cvecs, w_ada, b_ada):
    n = 6 * D_MODEL
    return pl.pallas_call(
        _adaln_kernel,
        grid=(DEPTH, n // ADA_TN),
        in_specs=[
            pl.BlockSpec((MOD_ROWS, D_MODEL), lambda l, j: (0, 0)),
            pl.BlockSpec((None, D_MODEL, ADA_TN), lambda l, j: (l, 0, j)),
            pl.BlockSpec((None, 1, ADA_TN), lambda l, j: (l, 0, j)),
        ],
        out_specs=pl.BlockSpec((None, MOD_ROWS, ADA_TN), lambda l, j: (l, 0, j)),
        out_shape=jax.ShapeDtypeStruct((DEPTH, MOD_ROWS, n), F32),
        compiler_params=_cparams(("arbitrary", "arbitrary")),
        name="adaln",
    )(cvecs, w_ada, b_ada.reshape(DEPTH, 1, n))


IN_TM = 1024
IN_TN = 1024
TM = 512
FFN_TM = 1024
FFN_TF = 512


def _mod_spec(st, layer, chunk, tm, ngrid):
    def row(i):
        return (i * tm) // DEC_SEQ if st.latent else CTX_ROW

    if ngrid == 1:
        return pl.BlockSpec((None, None, 1, D_MODEL), lambda i: (layer, row(i), 0, chunk))
    return pl.BlockSpec((None, None, 1, D_MODEL), lambda i, j: (layer, row(i), 0, chunk))


PROLOGUE_ROWS = 128


def _norm_modulate(x_ref, h_ref, nw_ref, sh_ref, sc_ref):
    gain = nw_ref[...] * (1.0 + sc_ref[...])
    shift = sh_ref[...]

    def chunk(r, carry):
        rows = pl.ds(pl.multiple_of(r * PROLOGUE_ROWS, PROLOGUE_ROWS), PROLOGUE_ROWS)
        x = x_ref[rows, :]
        ms = jnp.mean(x * x, axis=-1, keepdims=True)
        h_ref[rows, :] = (x * lax.rsqrt(ms + EPS) * gain + shift).astype(BF16)
        return carry

    lax.fori_loop(0, x_ref.shape[0] // PROLOGUE_ROWS, chunk, 0)


def _inproj_kernel(*refs, emit_kv):
    x_ref, sh_ref, sc_ref, nw_ref, w_ref = refs[:5]
    outs = refs[-4:] if emit_kv else refs[-2:]
    o_ref = outs[0]
    h_ref = refs[-1]

    @pl.when(pl.program_id(1) == 0)
    def _():
        _norm_modulate(x_ref, h_ref, nw_ref, sh_ref, sc_ref)

    o_ref[...] = _dot(h_ref[...], w_ref[...])

    if emit_kv:
        @pl.when(pl.program_id(1) == pl.num_programs(1) - 1)
        def _():
            k_ref, v_ref = outs[1], outs[2]
            k_ref[...] = o_ref[:, :NA_WIDTH].reshape(k_ref.shape)
            v_ref[...] = o_ref[:, NA_WIDTH:].reshape(v_ref.shape)


def _layer_row_spec(layer, width, ngrid):
    if ngrid == 1:
        return pl.BlockSpec((None, 1, width), lambda i: (layer, 0, 0))
    return pl.BlockSpec((None, 1, width), lambda i, j: (layer, 0, 0))


def _inproj(st, x, mods4, layer, norm_w, w_bf16, kv_prev):
    emit_kv = not st.latent
    assert COL_NK * LANE == IN_PROJ_WIDTH - IN_TN and 2 * NA_WIDTH == IN_TN
    ntok = _ntok(st)
    in_specs = [
        pl.BlockSpec((IN_TM, D_MODEL), lambda i, j: (i, 0)),
        _mod_spec(st, layer, 0, IN_TM, 2),
        _mod_spec(st, layer, 1, IN_TM, 2),
        _layer_row_spec(layer, D_MODEL, 2),
        pl.BlockSpec((None, D_MODEL, IN_TN), lambda i, j: (layer, 0, j)),
    ]
    args = [x, mods4, mods4, norm_w, w_bf16]
    out_specs = [pl.BlockSpec((IN_TM, IN_TN), lambda i, j: (i, j))]
    out_shape = [jax.ShapeDtypeStruct((ntok, IN_PROJ_WIDTH), F32)]
    aliases = {}
    if emit_kv:
        nb = IN_TM // st.seq
        kv_spec = pl.BlockSpec((nb, None, st.seq, NA_WIDTH), lambda i, j: (i, layer, 0, 0))
        out_specs += [kv_spec, kv_spec]
        out_shape += [jax.ShapeDtypeStruct((st.nseq, DEPTH, st.seq, NA_WIDTH), F32)] * 2
        if kv_prev is not None:
            in_specs += [pl.BlockSpec(memory_space=pl.ANY)] * 2
            args += list(kv_prev)
            aliases = {5: 1, 6: 2}
    return pl.pallas_call(
        functools.partial(_inproj_kernel, emit_kv=emit_kv),
        grid=(ntok // IN_TM, IN_PROJ_WIDTH // IN_TN),
        in_specs=in_specs,
        out_specs=out_specs,
        out_shape=out_shape,
        scratch_shapes=[pltpu.VMEM((IN_TM, D_MODEL), BF16)],
        input_output_aliases=aliases,
        compiler_params=_cparams(("arbitrary", "arbitrary")),
        name="inproj_" + st.name,
    )(*args)


def _outproj_kernel(x_ref, m_ref, g_ref, w_ref, o_ref):
    o_ref[...] = x_ref[...] + g_ref[...] * _dot(m_ref[...], w_ref[...])


def _outproj(st, x, mix, mods4, layer, w_bf16):
    ntok = _ntok(st)
    return pl.pallas_call(
        _outproj_kernel,
        grid=(ntok // TM,),
        in_specs=[
            pl.BlockSpec((TM, D_MODEL), lambda i: (i, 0)),
            pl.BlockSpec((TM, MIX_WIDTH), lambda i: (i, 0)),
            _mod_spec(st, layer, 2, TM, 1),
            pl.BlockSpec((None, MIX_WIDTH, D_MODEL), lambda i: (layer, 0, 0)),
        ],
        out_specs=pl.BlockSpec((TM, D_MODEL), lambda i: (i, 0)),
        out_shape=jax.ShapeDtypeStruct((ntok, D_MODEL), F32),
        compiler_params=_cparams(("arbitrary",)),
        name="outproj_" + st.name,
    )(x, mix, mods4, w_bf16)


def _ffn_kernel(x_ref, sh_ref, sc_ref, g_ref, nw_ref, fw_ref, wg_ref, wu_ref, wd_ref, o_ref, h_ref, *, final_norm):
    j = pl.program_id(1)

    @pl.when(j == 0)
    def _():
        _norm_modulate(x_ref, h_ref, nw_ref, sh_ref, sc_ref)
        o_ref[...] = jnp.zeros_like(o_ref)

    h = h_ref[...]
    a = _dot(h, wg_ref[...])
    u = _dot(h, wu_ref[...])
    t = (a * _sigmoid(a) * u).astype(BF16)
    for n in range(0, D_MODEL, FFN_TF):
        o_ref[:, n:n + FFN_TF] += _dot(t, wd_ref[:, n:n + FFN_TF])

    @pl.when(j == pl.num_programs(1) - 1)
    def _():
        gate = g_ref[...]

        def chunk(r, carry):
            rows = pl.ds(pl.multiple_of(r * PROLOGUE_ROWS, PROLOGUE_ROWS), PROLOGUE_ROWS)
            y = x_ref[rows, :] + gate * o_ref[rows, :]
            if final_norm:
                ms = jnp.mean(y * y, axis=-1, keepdims=True)
                y = y * lax.rsqrt(ms + EPS) * fw_ref[...]
            o_ref[rows, :] = y
            return carry

        lax.fori_loop(0, FFN_TM // PROLOGUE_ROWS, chunk, 0)


def _ffn(st, x, mods4, layer, norm_w, final_w, wg, wu, wd, final_norm):
    ntok = _ntok(st)
    TM = FFN_TM
    return pl.pallas_call(
        functools.partial(_ffn_kernel, final_norm=final_norm),
        grid=(ntok // TM, FFN_HIDDEN // FFN_TF),
        in_specs=[
            pl.BlockSpec((TM, D_MODEL), lambda i, j: (i, 0)),
            _mod_spec(st, layer, 3, TM, 2),
            _mod_spec(st, layer, 4, TM, 2),
            _mod_spec(st, layer, 5, TM, 2),
            _layer_row_spec(layer, D_MODEL, 2),
            pl.BlockSpec((1, D_MODEL), lambda i, j: (0, 0)),
            pl.BlockSpec((None, D_MODEL, FFN_TF), lambda i, j: (layer, 0, j)),
            pl.BlockSpec((None, D_MODEL, FFN_TF), lambda i, j: (layer, 0, j)),
            pl.BlockSpec((None, FFN_TF, D_MODEL), lambda i, j: (layer, j, 0)),
        ],
        out_specs=pl.BlockSpec((TM, D_MODEL), lambda i, j: (i, 0)),
        out_shape=jax.ShapeDtypeStruct((ntok, D_MODEL), F32),
        scratch_shapes=[pltpu.VMEM((TM, D_MODEL), BF16)],
        compiler_params=_cparams(("arbitrary", "arbitrary")),
        name="ffn_" + st.name,
    )(x, mods4, mods4, mods4, norm_w, final_w.reshape(1, D_MODEL), wg, wu, wd)


HGRN_CHUNK = 64
HGRN_HG = 2
HGRN_UNROLL = 4
HGRN_LEVELS = [HGRN_CHUNK >> (i + 1) for i in range(HGRN_CHUNK.bit_length() - 1)]


LOG2E = 1.4426950408889634


def _hgrn_gates(x, lb):
    e = jnp.exp(-jnp.abs(x))
    r = 1.0 / (1.0 + e)
    er = e * r
    pos = x >= 0.0
    sig = jnp.where(pos, r, er)
    nsig = jnp.where(pos, er, r)
    ls2 = jnp.minimum(x, 0.0) * LOG2E + jnp.log2(r)
    f = lb + (1.0 - lb) * sig
    log2_f = jnp.where(lb > 0.0, jnp.log2(f), ls2)
    return log2_f, f, (1.0 - lb) * nsig


def _hgrn_chunk_a(x, lb, b_ref, tri):
    g, f, k = _hgrn_gates(x, lb)
    g_hi = g.astype(BF16)
    g_lo = (g - g_hi.astype(F32)).astype(BF16)
    b = _dot(tri, g_hi) + _dot(tri, g_lo)
    b_ref[...] = b
    return b, f, k


def _hgrn_chunk_b(b, f, k, q, v, st, b_ref, mask_ref, dirn):
    C = HGRN_CHUNK
    fwd = dirn == 0
    b_end = b_ref[pl.ds(C - 1 if fwd else 0, 1), :]
    vb = v.astype(BF16)
    o = _dot_nt((q * jnp.exp2(b)).astype(BF16), st.astype(BF16))
    kd = (k * jnp.exp2(b_end - b)).astype(BF16)
    st_new = jnp.exp2(b_end) * st + _dot_tn(vb, kd)

    row = lax.broadcasted_iota(jnp.int32, (C, LANE), 0)
    sub = lax.broadcasted_iota(jnp.int32, (SUBLANE, LANE), 0)
    attn = None
    for lvl, m in enumerate(HGRN_LEVELS):
        if m >= SUBLANE:
            parts = []
            for p in range(C // (2 * m)):
                lo = p * 2 * m
                left = slice(lo, lo + m)
                right = slice(lo + m, lo + 2 * m)
                beta = b_ref[pl.ds(lo + (m - 1 if fwd else m), 1), :]
                if fwd:
                    parts += [k[left] * jnp.exp2(beta - b[left]), q[right] * jnp.exp2(b[right] - beta)]
                else:
                    parts += [q[left] * jnp.exp2(b[left] - beta), k[right] * jnp.exp2(beta - b[right])]
            w = jnp.concatenate(parts, axis=0)
        else:
            reads = ((row & m) != 0) if fwd else ((row & m) == 0)
            if m == 1:
                w = jnp.where(reads, q * f, k)
            else:
                betas = []
                for u in range(C // SUBLANE):
                    rows = []
                    for p in range(SUBLANE // (2 * m)):
                        rows.append(b_ref[pl.ds(u * SUBLANE + p * 2 * m + (m - 1 if fwd else m), 1), :])
                    beta_u = rows[-1]
                    for p in range(len(rows) - 2, -1, -1):
                        beta_u = jnp.where(sub < (p + 1) * 2 * m, rows[p], beta_u)
                    betas.append(jnp.broadcast_to(beta_u, (SUBLANE, LANE)))
                beta = jnp.concatenate(betas, axis=0)
                w = jnp.where(reads, q, k) * jnp.exp2(-jnp.abs(b - beta))
        wb = w.astype(BF16)
        term = _dot_nt(wb, wb) * mask_ref[dirn * len(HGRN_LEVELS) + lvl]
        attn = term if attn is None else attn + term
    o += jnp.sum(q * k, axis=-1, keepdims=True) * v
    return o, attn, vb, st_new


def _hgrn_kernel(*refs, T, has_s0, emit_state, n_alias):
    q_ref, v_ref, ff_ref, fb_ref, g_ref, lb_ref, gnw_ref = refs[:7]
    s0_ref = refs[7] if has_s0 else None
    n_out = 2 if emit_state else 1
    outs = refs[len(refs) - 5 - n_out:len(refs) - 5]
    o_ref = outs[0]
    s_out_ref = outs[1] if emit_state else None
    oacc_ref, b_ref, st_ref, tri_ref, mask_ref = refs[-5:]
    assert len(refs) == 7 + int(has_s0) + n_alias + n_out + 5

    C = HGRN_CHUNK
    U = HGRN_UNROLL
    nc = T // C
    assert nc % U == 0
    nl = len(HGRN_LEVELS)

    @pl.when((pl.program_id(0) == 0) & (pl.program_id(1) == 0))
    def _():
        row = lax.broadcasted_iota(jnp.int32, (C, C), 0)
        col = lax.broadcasted_iota(jnp.int32, (C, C), 1)
        tri_ref[0] = jnp.where(col <= row, 1.0, 0.0).astype(BF16)
        tri_ref[1] = jnp.where(col >= row, 1.0, 0.0).astype(BF16)
        for lvl, m in enumerate(HGRN_LEVELS):
            shift = (2 * m).bit_length() - 1
            same = (row >> shift) == (col >> shift)
            hi_lo = ((row & m) != 0) & ((col & m) == 0)
            lo_hi = ((row & m) == 0) & ((col & m) != 0)
            mask_ref[lvl] = jnp.where(same & hi_lo, 1.0, 0.0)
            mask_ref[nl + lvl] = jnp.where(same & lo_hi, 1.0, 0.0)

    oacc_ref[...] = jnp.zeros_like(oacc_ref)
    for hh in range(HGRN_HG):
        for d in range(2):
            if has_s0:
                st_ref[2 * hh + d] = s0_ref[d, hh].T
            else:
                st_ref[2 * hh + d] = jnp.zeros((HGRN_DV, HGRN_DK), F32)

    q_scale = HGRN_DK ** -0.5

    def body(c, carry):
        chains = []
        for hh in range(HGRN_HG):
            lanes = slice(hh * LANE, (hh + 1) * LANE)
            for d in range(2):
                for u in range(U):
                    cc = c * U + u if d == 0 else nc - 1 - (c * U + u)
                    rows = pl.ds(pl.multiple_of(cc * C, C), C)
                    chains.append((hh, d, u, rows, lanes))
        stage_a = []
        for n, (hh, d, u, rows, lanes) in enumerate(chains):
            x_ref = ff_ref if d == 0 else fb_ref
            stage_a.append(_hgrn_chunk_a(x_ref[rows, lanes], lb_ref[d, hh], b_ref.at[n], tri_ref[d]))
        stage_b = []
        st = None
        for n, ((hh, d, u, rows, lanes), (b, f, k)) in enumerate(zip(chains, stage_a)):
            idx = 2 * hh + d
            if u == 0:
                st = st_ref[idx]
            o, attn, vb, st = _hgrn_chunk_b(b, f, k, q_ref[rows, lanes] * q_scale, v_ref[rows, lanes], st,
                                            b_ref.at[n], mask_ref, d)
            if u == U - 1:
                st_ref[idx] = st
            stage_b.append((o, attn, vb))
        for (hh, d, u, rows, lanes), (o, attn, vb) in zip(chains, stage_b):
            oacc_ref[rows, lanes] += o + _dot(attn.astype(BF16), vb)
        return carry

    lax.fori_loop(0, nc // U, body, 0)

    for hh in range(HGRN_HG):
        lanes = slice(hh * LANE, (hh + 1) * LANE)
        if emit_state:
            for d in range(2):
                s_out_ref[d, hh] = st_ref[2 * hh + d].T
        o = oacc_ref[:, lanes]
        ms = jnp.mean(o * o, axis=-1, keepdims=True)
        gate = g_ref[:, lanes]
        y = o * lax.rsqrt(ms + EPS) * gnw_ref[...] * (gate * _sigmoid(gate))
        o_ref[:, lanes] = y.astype(BF16)


def _hgrn(st, proj, lbs, gnorm_w, s0, layer, states_prev):
    hg = HGRN_HG
    wide = hg * LANE
    T = st.seq
    has_s0 = st.latent
    emit_state = not st.latent

    def col_spec(col0):
        return pl.BlockSpec((T, wide), lambda b, h: (b, col0 // hg + h))

    in_specs = [col_spec(COL_HQ), col_spec(COL_HI), col_spec(COL_HFF), col_spec(COL_HFB), col_spec(COL_HG),
                pl.BlockSpec((2, None, hg, 1, LANE), lambda b, h: (0, layer, h, 0, 0)),
                _layer_row_spec(layer, LANE, 2)]
    args = [proj, proj, proj, proj, proj, lbs, gnorm_w]
    if has_s0:
        in_specs.append(pl.BlockSpec((None, None, 2, hg, HGRN_DK, HGRN_DV), lambda b, h: (b, layer, 0, h, 0, 0)))
        args.append(s0)
    out_specs = [pl.BlockSpec((T, wide), lambda b, h: (b, h))]
    out_shape = [jax.ShapeDtypeStruct((_ntok(st), MIX_WIDTH), BF16)]
    aliases = {}
    n_alias = 0
    if emit_state:
        out_specs.append(pl.BlockSpec((None, None, 2, hg, HGRN_DK, HGRN_DV), lambda b, h: (b, layer, 0, h, 0, 0)))
        out_shape.append(jax.ShapeDtypeStruct((st.nseq, DEPTH, 2, HGRN_HEADS, HGRN_DK, HGRN_DV), F32))
        if states_prev is not None:
            aliases = {len(args): 1}
            in_specs.append(pl.BlockSpec(memory_space=pl.ANY))
            args.append(states_prev)
            n_alias = 1
    C = HGRN_CHUNK
    nl = len(HGRN_LEVELS)
    return pl.pallas_call(
        functools.partial(_hgrn_kernel, T=T, has_s0=has_s0, emit_state=emit_state, n_alias=n_alias),
        grid=(st.nseq, HGRN_HEADS // hg),
        in_specs=in_specs,
        out_specs=out_specs,
        out_shape=out_shape,
        scratch_shapes=[pltpu.VMEM((T, wide), F32), pltpu.VMEM((2 * hg * HGRN_UNROLL, C, LANE), F32),
                        pltpu.VMEM((2 * hg, HGRN_DV, HGRN_DK), F32), pltpu.VMEM((2, C, C), BF16),
                        pltpu.VMEM((2 * nl, C, C), F32)],
        input_output_aliases=aliases,
        compiler_params=_cparams(("arbitrary", "arbitrary")),
        name="hgrn_" + st.name,
    )(*args)


def _conv_kernel(cb_ref, cc_ref, cx_ref, w_ref, mix_ref, o_ref, *, T):
    del mix_ref
    u = cc_ref[...] * cx_ref[...]
    row = lax.broadcasted_iota(jnp.int32, (T, CONV_WIDTH), 0)
    u_prev = jnp.where(row == 0, 0.0, pltpu.roll(u, 1, axis=0))
    u_next = jnp.where(row == T - 1, 0.0, pltpu.roll(u, T - 1, axis=0))
    y = u_prev * w_ref[0:1, :] + u * w_ref[1:2, :] + u_next * w_ref[2:3, :]
    o_ref[...] = (cb_ref[...] * y).astype(BF16)


def _conv(st, proj, conv_w, layer, mix):
    cw = CONV_WIDTH // LANE
    T = st.seq

    def col_spec(col0):
        return pl.BlockSpec((T, CONV_WIDTH), lambda b: (b, col0 // cw))

    return pl.pallas_call(
        functools.partial(_conv_kernel, T=T),
        grid=(st.nseq,),
        in_specs=[col_spec(COL_CB), col_spec(COL_CC), col_spec(COL_CX),
                  pl.BlockSpec((None, 3, CONV_WIDTH), lambda b: (layer, 0, 0)),
                  pl.BlockSpec(memory_space=pl.ANY)],
        out_specs=pl.BlockSpec((T, CONV_WIDTH), lambda b: (b, HGRN_WIDTH // CONV_WIDTH)),
        out_shape=jax.ShapeDtypeStruct(mix.shape, mix.dtype),
        input_output_aliases={4: 0},
        compiler_params=_cparams(("arbitrary",)),
        name="conv_" + st.name,
    )(proj, proj, proj, conv_w, mix)


MIX_COL_ATTN = (HGRN_WIDTH + CONV_WIDTH) // LANE


def _ctx_attn_kernel(q_ref, k_ref, v_ref, mix_ref, o_ref):
    del mix_ref
    scale = NA_HEAD_DIM ** -0.5
    heads = [slice(h * LANE, (h + 1) * LANE) for h in range(NA_HEADS)]
    scores = [_dot_nt(q_ref[:, hs].astype(BF16), k_ref[:, hs].astype(BF16)) * scale for hs in heads]
    probs = []
    for s in scores:
        p = jnp.exp(s - jnp.max(s, axis=-1, keepdims=True))
        probs.append((p.astype(BF16), jnp.sum(p, axis=-1, keepdims=True)))
    for hs, (p, l) in zip(heads, probs):
        o_ref[:, hs] = (_dot(p, v_ref[:, hs].astype(BF16)) / l).astype(BF16)


def _ctx_attn(proj, mix):
    def col_spec(col0):
        return pl.BlockSpec((SEQ, NA_WIDTH), lambda b: (b, col0 * LANE // NA_WIDTH))

    return pl.pallas_call(
        _ctx_attn_kernel,
        grid=(BATCH,),
        in_specs=[col_spec(COL_NQ), col_spec(COL_NK), col_spec(COL_NV), pl.BlockSpec(memory_space=pl.ANY)],
        out_specs=pl.BlockSpec((SEQ, NA_WIDTH), lambda b: (b, MIX_COL_ATTN * LANE // NA_WIDTH)),
        out_shape=jax.ShapeDtypeStruct(mix.shape, mix.dtype),
        input_output_aliases={3: 0},
        compiler_params=_cparams(("arbitrary",)),
        name="ctx_attn",
    )(proj, proj, proj, mix)


_KH = min(NA_KH, GRID_ROWS)
_ROW_START = [int(v) for v in np.clip(np.arange(GRID_ROWS) - _KH // 2, 0, GRID_ROWS - _KH)]
NLOC = _KH * GRID_W
RPB_ROWS = 2 * NA_KH - 1
RPB_COLS = 2 * NA_KW - 1


def _nat_kernel(q_ref, k_ref, v_ref, kc_ref, vc_ref, bias_ref, mix_ref, o_ref, sctx_ref, sloc_ref, pctx_ref,
                ploc_ref, den_ref):
    del mix_ref
    scale = NA_HEAD_DIM ** -0.5
    q_all = q_ref[...].astype(BF16)
    k_all = k_ref[...].astype(BF16)
    v_all = v_ref[...].astype(BF16)
    row_slices = [slice(r * GRID_W, (r + 1) * GRID_W) for r in range(GRID_ROWS)]
    bands = [slice(rs * GRID_W, rs * GRID_W + NLOC) for rs in _ROW_START]
    sctx_ref[...] = _dot_nt(q_all, kc_ref[...].astype(BF16)) * scale
    for r, (rows, band) in enumerate(zip(row_slices, bands)):
        sloc_ref[rows, :] = _dot_nt(q_all[rows], k_all[band]) * scale + bias_ref[r - _ROW_START[r]]
    for rows in row_slices:
        s_loc = sloc_ref[rows, :]
        s_ctx = sctx_ref[rows, :]
        m = jnp.maximum(jnp.max(s_loc, axis=-1, keepdims=True), jnp.max(s_ctx, axis=-1, keepdims=True))
        p_loc = jnp.exp(s_loc - m)
        p_ctx = jnp.exp(s_ctx - m)
        den_ref[rows, :] = jnp.sum(p_loc, axis=-1, keepdims=True) + jnp.sum(p_ctx, axis=-1, keepdims=True)
        ploc_ref[rows, :] = p_loc.astype(BF16)
        pctx_ref[rows, :] = p_ctx.astype(BF16)
    o_ctx = _dot(pctx_ref[...], vc_ref[...].astype(BF16))
    for rows, band in zip(row_slices, bands):
        o = (_dot(ploc_ref[rows, :], v_all[band]) + o_ctx[rows]) / den_ref[rows, :]
        o_ref[rows, :] = o.astype(BF16)


def _nat_bias_kernel(rpb_ref, o_ref):
    base = (pl.program_id(0) * NA_HEADS + pl.program_id(1)) * (RPB_ROWS * RPB_COLS)
    q = lax.broadcasted_iota(jnp.int32, (GRID_W, GRID_W), 0)
    k = lax.broadcasted_iota(jnp.int32, (GRID_W, GRID_W), 1)
    col_start = jnp.clip(q - NA_KW // 2, 0, GRID_W - NA_KW)
    in_window = (k >= col_start) & (k < col_start + NA_KW)
    dc = jnp.clip(k - q + NA_KW - 1, 0, RPB_COLS - 1)
    tabs = []
    for dr in range(RPB_ROWS):
        t = jnp.zeros((GRID_W, GRID_W), F32)
        for c in range(RPB_COLS):
            t = jnp.where(dc == c, rpb_ref[base + dr * RPB_COLS + c], t)
        tabs.append(jnp.where(in_window, t, NEG_BIG))
    for off in range(_KH):
        for p in range(_KH // 2):
            pair = [tabs[kr - off + NA_KH - 1] for kr in (2 * p, 2 * p + 1)]
            o_ref[off, :, 2 * p * GRID_W:(2 * p + 2) * GRID_W] = jnp.concatenate(pair, axis=1)


def _nat_bias(na_rpb):
    return pl.pallas_call(
        _nat_bias_kernel,
        grid=(DEPTH, NA_HEADS),
        in_specs=[pl.BlockSpec(memory_space=pltpu.SMEM)],
        out_specs=pl.BlockSpec((None, None, _KH, GRID_W, NLOC), lambda l, h: (l, h, 0, 0, 0)),
        out_shape=jax.ShapeDtypeStruct((DEPTH, NA_HEADS, _KH, GRID_W, NLOC), F32),
        compiler_params=_cparams(("arbitrary", "arbitrary")),
        name="nat_bias",
    )(na_rpb.reshape(-1))


def _nat(proj, cache_k4, cache_v4, bias, layer, mix):
    def col_spec(col0):
        return pl.BlockSpec((DEC_SEQ, LANE), lambda b, h: (b, col0 + h))

    cache_spec = pl.BlockSpec((None, None, PAST_LEN, LANE), lambda b, h: (b, layer, 0, h))
    return pl.pallas_call(
        _nat_kernel,
        grid=(DEC_BATCH, NA_HEADS),
        in_specs=[col_spec(COL_NQ), col_spec(COL_NK), col_spec(COL_NV), cache_spec, cache_spec,
                  pl.BlockSpec((None, None, _KH, GRID_W, NLOC), lambda b, h: (layer, h, 0, 0, 0)),
                  pl.BlockSpec(memory_space=pl.ANY)],
        out_specs=pl.BlockSpec((DEC_SEQ, LANE), lambda b, h: (b, MIX_COL_ATTN + h)),
        out_shape=jax.ShapeDtypeStruct(mix.shape, mix.dtype),
        scratch_shapes=[pltpu.VMEM((DEC_SEQ, PAST_LEN), F32), pltpu.VMEM((DEC_SEQ, NLOC), F32),
                        pltpu.VMEM((DEC_SEQ, PAST_LEN), BF16), pltpu.VMEM((DEC_SEQ, NLOC), BF16),
                        pltpu.VMEM((DEC_SEQ, 1), F32)],
        input_output_aliases={6: 0},
        compiler_params=_cparams(("arbitrary", "arbitrary")),
        name="nat",
    )(proj, proj, proj, cache_k4, cache_v4, bias, mix)


def kernel(x_prompt, x_sample, cache_na_k, cache_na_v, state_hgrn, c, c_ctx, w_ada, b_ada, norm_mix_w, w_in,
           hgrn_lb_raw, hgrn_gnorm_w, conv_w, na_rpb, w_out, norm_ffn_w, w_ffn_gate, w_ffn_up, w_ffn_down,
           final_norm_w):
    xs = {LATENT: x_sample.reshape(_ntok(LATENT), D_MODEL), CONTEXT: x_prompt.reshape(_ntok(CONTEXT), D_MODEL)}
    cvecs = jnp.concatenate([c, c_ctx[None, :], jnp.zeros((MOD_ROWS - DEC_BATCH - 1, D_MODEL), F32)], axis=0)
    mods4 = _adaln(cvecs, w_ada, b_ada).reshape(DEPTH, MOD_ROWS, 1, 6 * D_MODEL)

    p_lb = jax.nn.softmax(hgrn_lb_raw.astype(F32), axis=1)
    cp = jnp.cumsum(p_lb, axis=1)
    lbs = (cp - cp[:, :1]).reshape(2, DEPTH, HGRN_HEADS, 1, LANE)
    gnorm_w = hgrn_gnorm_w.reshape(DEPTH, 1, LANE)
    norm_mix = norm_mix_w.reshape(DEPTH, 1, D_MODEL)
    norm_ffn = norm_ffn_w.reshape(DEPTH, 1, D_MODEL)

    cache_k4 = cache_na_k.reshape(DEC_BATCH, DEPTH, PAST_LEN, NA_WIDTH)
    cache_v4 = cache_na_v.reshape(DEC_BATCH, DEPTH, PAST_LEN, NA_WIDTH)

    w_in_b, w_out_b = w_in.astype(BF16), w_out.astype(BF16)
    wg, wu, wd = w_ffn_gate.astype(BF16), w_ffn_up.astype(BF16), w_ffn_down.astype(BF16)

    nat_bias = _nat_bias(na_rpb)

    kv = None
    states = None
    for l in range(DEPTH):
        for st in (LATENT, CONTEXT):
            res = _inproj(st, xs[st], mods4, l, norm_mix, w_in_b, kv)
            proj = res[0]
            if st.latent:
                mix = _hgrn(st, proj, lbs, gnorm_w, state_hgrn, l, None)[0]
                mix = _conv(st, proj, conv_w, l, mix)
                mix = _nat(proj, cache_k4, cache_v4, nat_bias, l, mix)
            else:
                kv = (res[1], res[2])
                mix, states = _hgrn(st, proj, lbs, gnorm_w, None, l, states)
                mix = _conv(st, proj, conv_w, l, mix)
                mix = _ctx_attn(proj, mix)
            x1 = _outproj(st, xs[st], mix, mods4, l, w_out_b)
            xs[st] = _ffn(st, x1, mods4, l, norm_ffn, final_norm_w, wg, wu, wd, final_norm=(l == DEPTH - 1))

    y_sample = xs[LATENT].reshape(DEC_BATCH, DEC_SEQ, D_MODEL)
    y_prompt = xs[CONTEXT].reshape(BATCH, SEQ, D_MODEL)
    new_k = kv[0].reshape(BATCH, DEPTH, SEQ, NA_HEADS, NA_HEAD_DIM)
    new_v = kv[1].reshape(BATCH, DEPTH, SEQ, NA_HEADS, NA_HEAD_DIM)
    return (y_prompt, y_sample, new_k, new_v, states)
```

```python
import collections
import functools

import numpy as np
import jax
import jax.numpy as jnp
from jax import lax
from jax.experimental import pallas as pl
from jax.experimental.pallas import tpu as pltpu

F32 = jnp.float32
BF16 = jnp.bfloat16

D_MODEL = 2048
BATCH = 16
SEQ = 256
DEPTH = 2
DEC_BATCH = 8
DEC_SEQ = 1024
PAST_LEN = 512
GRID_W = 64
HGRN_HEADS = 8
HGRN_DK = 128
HGRN_DV = 128
HGRN_WIDTH = HGRN_HEADS * HGRN_DV
CONV_WIDTH = 512
NA_HEADS = 4
NA_HEAD_DIM = 128
NA_WIDTH = NA_HEADS * NA_HEAD_DIM
NA_KH = 8
NA_KW = 16
MIX_WIDTH = HGRN_WIDTH + CONV_WIDTH + NA_WIDTH
IN_PROJ_WIDTH = 5 * HGRN_WIDTH + 3 * CONV_WIDTH + 3 * NA_WIDTH
FFN_HIDDEN = ((8 * D_MODEL + 3 * 256 - 1) // (3 * 256)) * 256
EPS = 1e-6

MOD_ROWS = 16
CTX_ROW = DEC_BATCH
LANE = 128
SUBLANE = 8
GRID_ROWS = DEC_SEQ // GRID_W

COL_HQ = 0
COL_HI = HGRN_WIDTH // LANE
COL_HFF = 2 * HGRN_WIDTH // LANE
COL_HFB = 3 * HGRN_WIDTH // LANE
COL_HG = 4 * HGRN_WIDTH // LANE
COL_CB = 5 * HGRN_WIDTH // LANE
COL_CC = COL_CB + CONV_WIDTH // LANE
COL_CX = COL_CC + CONV_WIDTH // LANE
COL_NQ = COL_CX + CONV_WIDTH // LANE
COL_NK = COL_NQ + NA_WIDTH // LANE
COL_NV = COL_NK + NA_WIDTH // LANE

NEG_BIG = -1e30
VMEM_LIMIT = 60 * 1024 * 1024

Stream = collections.namedtuple("Stream", ["name", "seq", "nseq", "latent"])
LATENT = Stream("lat", DEC_SEQ, DEC_BATCH, True)
CONTEXT = Stream("ctx", SEQ, BATCH, False)


def _ntok(st):
    return st.seq * st.nseq


def _cparams(sem):
    return pltpu.CompilerParams(dimension_semantics=sem, vmem_limit_bytes=VMEM_LIMIT)


def _dot(a, b):
    return jnp.dot(a, b, preferred_element_type=F32)


def _dot_nt(a, b):
    return lax.dot_general(a, b, (((1,), (1,)), ((), ())), preferred_element_type=F32)


def _dot_tn(a, b):
    return lax.dot_general(a, b, (((0,), (0,)), ((), ())), preferred_element_type=F32)


def _sigmoid(x):
    return 1.0 / (1.0 + jnp.exp(-x))


ADA_TN = 1024


def _adaln_kernel(c_ref, w_ref, b_ref, o_ref):
    c = c_ref[...]
    s = (c * _sigmoid(c)).astype(BF16)
    o_ref[...] = _dot(s, w_ref[...].astype(BF16)) + b_ref[...]


def _adaln(cvecs, w_ada, b_ada):
    n = 6 * D_MODEL
    return pl.pallas_call(
        _adaln_kernel,
        grid=(DEPTH, n // ADA_TN),
        in_specs=[
            pl.BlockSpec((MOD_ROWS, D_MODEL), lambda l, j: (0, 0)),
            pl.BlockSpec((None, D_MODEL, ADA_TN), lambda l, j: (l, 0, j)),
            pl.BlockSpec((None, 1, ADA_TN), lambda l, j: (l, 0, j)),
        ],
        out_specs=pl.BlockSpec((None, MOD_ROWS, ADA_TN), lambda l, j: (l, 0, j)),
        out_shape=jax.ShapeDtypeStruct((DEPTH, MOD_ROWS, n), F32),
        compiler_params=_cparams(("arbitrary", "arbitrary")),
        name="adaln",
    )(cvecs, w_ada, b_ada.reshape(DEPTH, 1, n))


IN_TM = 1024
IN_TN = 1024
TM = 512
FFN_TM = 1024
FFN_TF = 512


def _mod_spec(st, layer, chunk, tm, ngrid, block0=0):
    def row(i):
        return ((block0 + i) * tm) // DEC_SEQ if st.latent else CTX_ROW

    if ngrid == 1:
        return pl.BlockSpec((None, None, 1, D_MODEL), lambda i: (layer, row(i), 0, chunk))
    return pl.BlockSpec((None, None, 1, D_MODEL), lambda i, j: (layer, row(i), 0, chunk))


PROLOGUE_ROWS = 128


def _norm_modulate(x_ref, h_ref, nw_ref, sh_ref, sc_ref):
    gain = nw_ref[...] * (1.0 + sc_ref[...])
    shift = sh_ref[...]

    def chunk(r, carry):
        rows = pl.ds(pl.multiple_of(r * PROLOGUE_ROWS, PROLOGUE_ROWS), PROLOGUE_ROWS)
        x = x_ref[rows, :]
        ms = jnp.mean(x * x, axis=-1, keepdims=True)
        h_ref[rows, :] = (x * lax.rsqrt(ms + EPS) * gain + shift).astype(BF16)
        return carry

    lax.fori_loop(0, x_ref.shape[0] // PROLOGUE_ROWS, chunk, 0)


def _inproj_kernel(*refs, emit_kv):
    x_ref, sh_ref, sc_ref, nw_ref, w_ref = refs[:5]
    outs = refs[-4:] if emit_kv else refs[-2:]
    o_ref = outs[0]
    h_ref = refs[-1]

    @pl.when(pl.program_id(1) == 0)
    def _():
        _norm_modulate(x_ref, h_ref, nw_ref, sh_ref, sc_ref)

    o_ref[...] = _dot(h_ref[...], w_ref[...])

    if emit_kv:
        @pl.when(pl.program_id(1) == pl.num_programs(1) - 1)
        def _():
            k_ref, v_ref = outs[1], outs[2]
            k_ref[...] = o_ref[:, :NA_WIDTH].reshape(k_ref.shape)
            v_ref[...] = o_ref[:, NA_WIDTH:].reshape(v_ref.shape)


def _layer_row_spec(layer, width, ngrid):
    if ngrid == 1:
        return pl.BlockSpec((None, 1, width), lambda i: (layer, 0, 0))
    return pl.BlockSpec((None, 1, width), lambda i, j: (layer, 0, 0))


def _inproj(st, x, mods4, layer, norm_w, w_bf16, kv_prev):
    emit_kv = not st.latent
    assert COL_NK * LANE == IN_PROJ_WIDTH - IN_TN and 2 * NA_WIDTH == IN_TN
    ntok = _ntok(st)
    in_specs = [
        pl.BlockSpec((IN_TM, D_MODEL), lambda i, j: (i, 0)),
        _mod_spec(st, layer, 0, IN_TM, 2),
        _mod_spec(st, layer, 1, IN_TM, 2),
        _layer_row_spec(layer, D_MODEL, 2),
        pl.BlockSpec((None, D_MODEL, IN_TN), lambda i, j: (layer, 0, j)),
    ]
    args = [x, mods4, mods4, norm_w, w_bf16]
    out_specs = [pl.BlockSpec((IN_TM, IN_TN), lambda i, j: (i, j))]
    out_shape = [jax.ShapeDtypeStruct((ntok, IN_PROJ_WIDTH), F32)]
    aliases = {}
    if emit_kv:
        nb = IN_TM // st.seq
        kv_spec = pl.BlockSpec((nb, None, st.seq, NA_WIDTH), lambda i, j: (i, layer, 0, 0))
        out_specs += [kv_spec, kv_spec]
        out_shape += [jax.ShapeDtypeStruct((st.nseq, DEPTH, st.seq, NA_WIDTH), F32)] * 2
        if kv_prev is not None:
            in_specs += [pl.BlockSpec(memory_space=pl.ANY)] * 2
            args += list(kv_prev)
            aliases = {5: 1, 6: 2}
    return pl.pallas_call(
        functools.partial(_inproj_kernel, emit_kv=emit_kv),
        grid=(ntok // IN_TM, IN_PROJ_WIDTH // IN_TN),
        in_specs=in_specs,
        out_specs=out_specs,
        out_shape=out_shape,
        scratch_shapes=[pltpu.VMEM((IN_TM, D_MODEL), BF16)],
        input_output_aliases=aliases,
        compiler_params=_cparams(("arbitrary", "arbitrary")),
        name="inproj_" + st.name,
    )(*args)


def _outproj_kernel(x_ref, m_ref, g_ref, w_ref, o_ref):
    o_ref[...] = x_ref[...] + g_ref[...] * _dot(m_ref[...], w_ref[...])


def _outproj(st, x, mix, mods4, layer, w_bf16):
    ntok = _ntok(st)
    return pl.pallas_call(
        _outproj_kernel,
        grid=(ntok // TM,),
        in_specs=[
            pl.BlockSpec((TM, D_MODEL), lambda i: (i, 0)),
            pl.BlockSpec((TM, MIX_WIDTH), lambda i: (i, 0)),
            _mod_spec(st, layer, 2, TM, 1),
            pl.BlockSpec((None, MIX_WIDTH, D_MODEL), lambda i: (layer, 0, 0)),
        ],
        out_specs=pl.BlockSpec((TM, D_MODEL), lambda i: (i, 0)),
        out_shape=jax.ShapeDtypeStruct((ntok, D_MODEL), F32),
        compiler_params=_cparams(("arbitrary",)),
        name="outproj_" + st.name,
    )(x, mix, mods4, w_bf16)


FFN_CAST_TF = 256


def _ffn_kernel(*refs, final_norm, cast_weights, n_alias):
    x_ref, sh_ref, sc_ref, g_ref, nw_ref, fw_ref, wg_ref, wu_ref, wd_ref = refs[:9]
    outs = refs[9 + n_alias:-1]
    o_ref = outs[0]
    h_ref = refs[-1]
    j = pl.program_id(1)

    @pl.when(j == 0)
    def _():
        _norm_modulate(x_ref, h_ref, nw_ref, sh_ref, sc_ref)
        o_ref[...] = jnp.zeros_like(o_ref)

    if cast_weights:
        wg, wu, wd = (w[...].astype(BF16) for w in (wg_ref, wu_ref, wd_ref))
        for w, w_out_ref in zip((wg, wu, wd), outs[1:]):
            w_out_ref[...] = w
    else:
        wg, wu, wd = wg_ref[...], wu_ref[...], wd_ref[...]

    h = h_ref[...]
    a = _dot(h, wg)
    u = _dot(h, wu)
    t = (a * _sigmoid(a) * u).astype(BF16)
    for n in range(0, D_MODEL, FFN_TF):
        o_ref[:, n:n + FFN_TF] += _dot(t, wd[:, n:n + FFN_TF])

    @pl.when(j == pl.num_programs(1) - 1)
    def _():
        gate = g_ref[...]

        def chunk(r, carry):
            rows = pl.ds(pl.multiple_of(r * PROLOGUE_ROWS, PROLOGUE_ROWS), PROLOGUE_ROWS)
            y = x_ref[rows, :] + gate * o_ref[rows, :]
            if final_norm:
                ms = jnp.mean(y * y, axis=-1, keepdims=True)
                y = y * lax.rsqrt(ms + EPS) * fw_ref[...]
            o_ref[rows, :] = y
            return carry

        lax.fori_loop(0, FFN_TM // PROLOGUE_ROWS, chunk, 0)


def _ffn(st, x, mods4, layer, norm_w, final_w, weights, final_norm, *, cast_weights=False, block0=0, nblocks=None,
         out_prev=None):
    ntok = _ntok(st)
    TM = FFN_TM
    tf = FFN_CAST_TF if cast_weights else FFN_TF
    nblocks = ntok // TM if nblocks is None else nblocks
    if cast_weights:
        w_specs = [pl.BlockSpec((None, D_MODEL, tf), lambda i, j: (layer, 0, j)),
                   pl.BlockSpec((None, D_MODEL, tf), lambda i, j: (layer, 0, j)),
                   pl.BlockSpec((None, tf, D_MODEL), lambda i, j: (layer, j, 0))]
    else:
        w_specs = [pl.BlockSpec((D_MODEL, tf), lambda i, j: (0, j)),
                   pl.BlockSpec((D_MODEL, tf), lambda i, j: (0, j)),
                   pl.BlockSpec((tf, D_MODEL), lambda i, j: (j, 0))]
    in_specs = [
        pl.BlockSpec((TM, D_MODEL), lambda i, j: (block0 + i, 0)),
        _mod_spec(st, layer, 3, TM, 2, block0),
        _mod_spec(st, layer, 4, TM, 2, block0),
        _mod_spec(st, layer, 5, TM, 2, block0),
        _layer_row_spec(layer, D_MODEL, 2),
        pl.BlockSpec((1, D_MODEL), lambda i, j: (0, 0)),
    ] + w_specs
    args = [x, mods4, mods4, mods4, norm_w, final_w.reshape(1, D_MODEL)] + list(weights)
    out_specs = [pl.BlockSpec((TM, D_MODEL), lambda i, j: (block0 + i, 0))]
    out_shape = [jax.ShapeDtypeStruct((ntok, D_MODEL), F32)]
    if cast_weights:
        out_specs += [pl.BlockSpec((D_MODEL, tf), lambda i, j: (0, j)),
                      pl.BlockSpec((D_MODEL, tf), lambda i, j: (0, j)),
                      pl.BlockSpec((tf, D_MODEL), lambda i, j: (j, 0))]
        out_shape += [jax.ShapeDtypeStruct((D_MODEL, FFN_HIDDEN), BF16)] * 2
        out_shape += [jax.ShapeDtypeStruct((FFN_HIDDEN, D_MODEL), BF16)]
    aliases = {}
    if out_prev is not None:
        aliases = {len(args): 0}
        in_specs.append(pl.BlockSpec(memory_space=pl.ANY))
        args.append(out_prev)
    return pl.pallas_call(
        functools.partial(_ffn_kernel, final_norm=final_norm, cast_weights=cast_weights, n_alias=len(aliases)),
        grid=(nblocks, FFN_HIDDEN // tf),
        in_specs=in_specs,
        out_specs=out_specs,
        out_shape=out_shape,
        scratch_shapes=[pltpu.VMEM((TM, D_MODEL), BF16)],
        input_output_aliases=aliases,
        compiler_params=_cparams(("arbitrary", "arbitrary")),
        name="ffn_" + st.name + ("_cast" if cast_weights else ""),
    )(*args)


HGRN_CHUNK = 64
HGRN_HG = 2
HGRN_UNROLL = 4
HGRN_LEVELS = [HGRN_CHUNK >> (i + 1) for i in range(HGRN_CHUNK.bit_length() - 1)]


LOG2E = 1.4426950408889634


def _hgrn_gates(x, lb):
    e = jnp.exp(-jnp.abs(x))
    r = 1.0 / (1.0 + e)
    er = e * r
    pos = x >= 0.0
    sig = jnp.where(pos, r, er)
    nsig = jnp.where(pos, er, r)
    ls2 = jnp.minimum(x, 0.0) * LOG2E + jnp.log2(r)
    f = lb + (1.0 - lb) * sig
    log2_f = jnp.where(lb > 0.0, jnp.log2(f), ls2)
    return log2_f, f, (1.0 - lb) * nsig


def _hgrn_chunk_a(x, lb, b_ref, tri):
    g, f, k = _hgrn_gates(x, lb)
    g_hi = g.astype(BF16)
    g_lo = (g - g_hi.astype(F32)).astype(BF16)
    b = _dot(tri, g_hi) + _dot(tri, g_lo)
    b_ref[...] = b
    return b, f, k


def _hgrn_chunk_b(b, f, k, q, v, st, b_ref, mask_ref, dirn):
    C = HGRN_CHUNK
    fwd = dirn == 0
    b_end = b_ref[pl.ds(C - 1 if fwd else 0, 1), :]
    vb = v.astype(BF16)
    q0 = (q * jnp.exp2(b)).astype(BF16)
    stb = st.astype(BF16)
    kd = (k * jnp.exp2(b_end - b)).astype(BF16)
    st_new = jnp.exp2(b_end) * st + _dot_tn(vb, kd)

    row = lax.broadcasted_iota(jnp.int32, (C, LANE), 0)
    sub = lax.broadcasted_iota(jnp.int32, (SUBLANE, LANE), 0)
    attn = [None] * (C // SUBLANE)

    def add_rows(first_row, term):
        for t in range(term.shape[0] // SUBLANE):
            blk = first_row // SUBLANE + t
            piece = term[t * SUBLANE:(t + 1) * SUBLANE]
            attn[blk] = piece if attn[blk] is None else attn[blk] + piece

    for lvl, m in enumerate(HGRN_LEVELS):
        mask_idx = dirn * len(HGRN_LEVELS) + lvl
        if m >= SUBLANE:
            parts, readers = [], []
            for p in range(C // (2 * m)):
                lo = p * 2 * m
                left = slice(lo, lo + m)
                right = slice(lo + m, lo + 2 * m)
                beta = b_ref[pl.ds(lo + (m - 1 if fwd else m), 1), :]
                if fwd:
                    parts += [k[left] * jnp.exp2(beta - b[left]), q[right] * jnp.exp2(b[right] - beta)]
                    readers.append((lo + m, parts[-1]))
                else:
                    parts += [q[left] * jnp.exp2(b[left] - beta), k[right] * jnp.exp2(beta - b[right])]
                    readers.append((lo, parts[-2]))
            wb = jnp.concatenate(parts, axis=0).astype(BF16)
            wq = jnp.concatenate([r for _, r in readers], axis=0).astype(BF16)
            term = _dot_nt(wq, wb)
            for n, (first_row, _) in enumerate(readers):
                add_rows(first_row, term[n * m:(n + 1) * m] * mask_ref[mask_idx, first_row:first_row + m, :])
        else:
            reads = ((row & m) != 0) if fwd else ((row & m) == 0)
            if m == 1:
                w = jnp.where(reads, q * f, k)
            else:
                betas = []
                for u in range(C // SUBLANE):
                    rows = []
                    for p in range(SUBLANE // (2 * m)):
                        rows.append(b_ref[pl.ds(u * SUBLANE + p * 2 * m + (m - 1 if fwd else m), 1), :])
                    beta_u = rows[-1]
                    for p in range(len(rows) - 2, -1, -1):
                        beta_u = jnp.where(sub < (p + 1) * 2 * m, rows[p], beta_u)
                    betas.append(jnp.broadcast_to(beta_u, (SUBLANE, LANE)))
                beta = jnp.concatenate(betas, axis=0)
                w = jnp.where(reads, q, k) * jnp.exp2(-jnp.abs(b - beta))
            wb = w.astype(BF16)
            add_rows(0, _dot_nt(wb, wb) * mask_ref[mask_idx])
    attn_b = jnp.concatenate(attn, axis=0).astype(BF16)
    diag = jnp.sum(q * k, axis=-1, keepdims=True) * v
    return q0, stb, attn_b, vb, diag, st_new


def _hgrn_kernel(*refs, T, has_s0, emit_state, n_alias):
    q_ref, v_ref, ff_ref, fb_ref, g_ref, lb_ref, gnw_ref = refs[:7]
    s0_ref = refs[7] if has_s0 else None
    n_out = 2 if emit_state else 1
    outs = refs[len(refs) - 5 - n_out:len(refs) - 5]
    o_ref = outs[0]
    s_out_ref = outs[1] if emit_state else None
    oacc_ref, b_ref, st_ref, tri_ref, mask_ref = refs[-5:]
    assert len(refs) == 7 + int(has_s0) + n_alias + n_out + 5

    C = HGRN_CHUNK
    U = HGRN_UNROLL
    nc = T // C
    assert nc % U == 0
    nl = len(HGRN_LEVELS)

    @pl.when((pl.program_id(0) == 0) & (pl.program_id(1) == 0))
    def _():
        row = lax.broadcasted_iota(jnp.int32, (C, C), 0)
        col = lax.broadcasted_iota(jnp.int32, (C, C), 1)
        tri_ref[0] = jnp.where(col <= row, 1.0, 0.0).astype(BF16)
        tri_ref[1] = jnp.where(col >= row, 1.0, 0.0).astype(BF16)
        for lvl, m in enumerate(HGRN_LEVELS):
            shift = (2 * m).bit_length() - 1
            same = (row >> shift) == (col >> shift)
            hi_lo = ((row & m) != 0) & ((col & m) == 0)
            lo_hi = ((row & m) == 0) & ((col & m) != 0)
            mask_ref[lvl] = jnp.where(same & hi_lo, 1.0, 0.0)
            mask_ref[nl + lvl] = jnp.where(same & lo_hi, 1.0, 0.0)

    oacc_ref[...] = jnp.zeros_like(oacc_ref)
    for hh in range(HGRN_HG):
        for d in range(2):
            if has_s0:
                st_ref[2 * hh + d] = s0_ref[d, hh].T
            else:
                st_ref[2 * hh + d] = jnp.zeros((HGRN_DV, HGRN_DK), F32)

    q_scale = HGRN_DK ** -0.5

    def body(c, carry):
        chains = []
        for hh in range(HGRN_HG):
            lanes = slice(hh * LANE, (hh + 1) * LANE)
            for d in range(2):
                for u in range(U):
                    cc = c * U + u if d == 0 else nc - 1 - (c * U + u)
                    rows = pl.ds(pl.multiple_of(cc * C, C), C)
                    chains.append((hh, d, u, rows, lanes))
        stage_a = {}
        states = {}
        pending = []
        for step in range(U + 2):
            for n, (hh, d, u, rows, lanes) in enumerate(chains):
                if u == step:
                    x_ref = ff_ref if d == 0 else fb_ref
                    stage_a[n] = _hgrn_chunk_a(x_ref[rows, lanes], lb_ref[d, hh], b_ref.at[n], tri_ref[d])
            ready, pending = pending, []
            for n, (hh, d, u, rows, lanes) in enumerate(chains):
                if u != step - 1:
                    continue
                b, f, k = stage_a[n]
                idx = 2 * hh + d
                st = st_ref[idx] if u == 0 else states[idx]
                q0, stb, attn_b, vb, diag, st = _hgrn_chunk_b(b, f, k, q_ref[rows, lanes], v_ref[rows, lanes], st,
                                                              b_ref.at[n], mask_ref, d)
                states[idx] = st
                if u == U - 1:
                    st_ref[idx] = st
                pending.append((rows, lanes, q0, stb, attn_b, vb, diag))
            for rows, lanes, q0, stb, attn_b, vb, diag in ready:
                oacc_ref[rows, lanes] += (_dot_nt(q0, stb) + _dot(attn_b, vb)) + diag
        assert not pending
        return carry

    lax.fori_loop(0, nc // U, body, 0)

    for hh in range(HGRN_HG):
        lanes = slice(hh * LANE, (hh + 1) * LANE)
        if emit_state:
            for d in range(2):
                s_out_ref[d, hh] = st_ref[2 * hh + d].T
        o = oacc_ref[:, lanes]
        ms = jnp.mean(o * o, axis=-1, keepdims=True) * (q_scale * q_scale)
        gate = g_ref[:, lanes]
        y = o * (q_scale * lax.rsqrt(ms + EPS)) * gnw_ref[...] * (gate * _sigmoid(gate))
        o_ref[:, lanes] = y.astype(BF16)


def _hgrn(st, proj, lbs, gnorm_w, s0, layer, states_prev):
    hg = HGRN_HG
    wide = hg * LANE
    T = st.seq
    has_s0 = st.latent
    emit_state = not st.latent

    def col_spec(col0):
        return pl.BlockSpec((T, wide), lambda b, h: (b, col0 // hg + h))

    in_specs = [col_spec(COL_HQ), col_spec(COL_HI), col_spec(COL_HFF), col_spec(COL_HFB), col_spec(COL_HG),
                pl.BlockSpec((2, None, hg, 1, LANE), lambda b, h: (0, layer, h, 0, 0)),
                _layer_row_spec(layer, LANE, 2)]
    args = [proj, proj, proj, proj, proj, lbs, gnorm_w]
    if has_s0:
        in_specs.append(pl.BlockSpec((None, None, 2, hg, HGRN_DK, HGRN_DV), lambda b, h: (b, layer, 0, h, 0, 0)))
        args.append(s0)
    out_specs = [pl.BlockSpec((T, wide), lambda b, h: (b, h))]
    out_shape = [jax.ShapeDtypeStruct((_ntok(st), MIX_WIDTH), BF16)]
    aliases = {}
    n_alias = 0
    if emit_state:
        out_specs.append(pl.BlockSpec((None, None, 2, hg, HGRN_DK, HGRN_DV), lambda b, h: (b, layer, 0, h, 0, 0)))
        out_shape.append(jax.ShapeDtypeStruct((st.nseq, DEPTH, 2, HGRN_HEADS, HGRN_DK, HGRN_DV), F32))
        if states_prev is not None:
            aliases = {len(args): 1}
            in_specs.append(pl.BlockSpec(memory_space=pl.ANY))
            args.append(states_prev)
            n_alias = 1
    C = HGRN_CHUNK
    nl = len(HGRN_LEVELS)
    return pl.pallas_call(
        functools.partial(_hgrn_kernel, T=T, has_s0=has_s0, emit_state=emit_state, n_alias=n_alias),
        grid=(st.nseq, HGRN_HEADS // hg),
        in_specs=in_specs,
        out_specs=out_specs,
        out_shape=out_shape,
        scratch_shapes=[pltpu.VMEM((T, wide), F32), pltpu.VMEM((2 * hg * HGRN_UNROLL, C, LANE), F32),
                        pltpu.VMEM((2 * hg, HGRN_DV, HGRN_DK), F32), pltpu.VMEM((2, C, C), BF16),
                        pltpu.VMEM((2 * nl, C, C), F32)],
        input_output_aliases=aliases,
        compiler_params=_cparams(("arbitrary", "arbitrary")),
        name="hgrn_" + st.name,
    )(*args)


def _conv_kernel(cb_ref, cc_ref, cx_ref, w_ref, mix_ref, o_ref, *, T):
    del mix_ref
    u = cc_ref[...] * cx_ref[...]
    row = lax.broadcasted_iota(jnp.int32, (T, CONV_WIDTH), 0)
    u_prev = jnp.where(row == 0, 0.0, pltpu.roll(u, 1, axis=0))
    u_next = jnp.where(row == T - 1, 0.0, pltpu.roll(u, T - 1, axis=0))
    y = u_prev * w_ref[0:1, :] + u * w_ref[1:2, :] + u_next * w_ref[2:3, :]
    o_ref[...] = (cb_ref[...] * y).astype(BF16)


def _conv(st, proj, conv_w, layer, mix):
    cw = CONV_WIDTH // LANE
    T = st.seq

    def col_spec(col0):
        return pl.BlockSpec((T, CONV_WIDTH), lambda b: (b, col0 // cw))

    return pl.pallas_call(
        functools.partial(_conv_kernel, T=T),
        grid=(st.nseq,),
        in_specs=[col_spec(COL_CB), col_spec(COL_CC), col_spec(COL_CX),
                  pl.BlockSpec((None, 3, CONV_WIDTH), lambda b: (layer, 0, 0)),
                  pl.BlockSpec(memory_space=pl.ANY)],
        out_specs=pl.BlockSpec((T, CONV_WIDTH), lambda b: (b, HGRN_WIDTH // CONV_WIDTH)),
        out_shape=jax.ShapeDtypeStruct(mix.shape, mix.dtype),
        input_output_aliases={4: 0},
        compiler_params=_cparams(("arbitrary",)),
        name="conv_" + st.name,
    )(proj, proj, proj, conv_w, mix)


MIX_COL_ATTN = (HGRN_WIDTH + CONV_WIDTH) // LANE


def _ctx_attn_kernel(q_ref, k_ref, v_ref, mix_ref, o_ref):
    del mix_ref
    scale = NA_HEAD_DIM ** -0.5
    heads = [slice(h * LANE, (h + 1) * LANE) for h in range(NA_HEADS)]
    scores = [_dot_nt(q_ref[:, hs].astype(BF16), k_ref[:, hs].astype(BF16)) * scale for hs in heads]
    probs = []
    for s in scores:
        p = jnp.exp(s - jnp.max(s, axis=-1, keepdims=True))
        probs.append((p.astype(BF16), jnp.sum(p, axis=-1, keepdims=True)))
    for hs, (p, l) in zip(heads, probs):
        o_ref[:, hs] = (_dot(p, v_ref[:, hs].astype(BF16)) / l).astype(BF16)


def _ctx_attn(proj, mix):
    def col_spec(col0):
        return pl.BlockSpec((SEQ, NA_WIDTH), lambda b: (b, col0 * LANE // NA_WIDTH))

    return pl.pallas_call(
        _ctx_attn_kernel,
        grid=(BATCH,),
        in_specs=[col_spec(COL_NQ), col_spec(COL_NK), col_spec(COL_NV), pl.BlockSpec(memory_space=pl.ANY)],
        out_specs=pl.BlockSpec((SEQ, NA_WIDTH), lambda b: (b, MIX_COL_ATTN * LANE // NA_WIDTH)),
        out_shape=jax.ShapeDtypeStruct(mix.shape, mix.dtype),
        input_output_aliases={3: 0},
        compiler_params=_cparams(("arbitrary",)),
        name="ctx_attn",
    )(proj, proj, proj, mix)


_KH = min(NA_KH, GRID_ROWS)
_ROW_START = [int(v) for v in np.clip(np.arange(GRID_ROWS) - _KH // 2, 0, GRID_ROWS - _KH)]
NLOC = _KH * GRID_W
RPB_ROWS = 2 * NA_KH - 1
RPB_COLS = 2 * NA_KW - 1


def _nat_kernel(q_ref, k_ref, v_ref, kc_ref, vc_ref, bias_ref, mix_ref, o_ref, sctx_ref, sloc_ref, pctx_ref,
                ploc_ref, den_ref):
    del mix_ref
    scale = NA_HEAD_DIM ** -0.5
    q_all = q_ref[...].astype(BF16)
    k_all = k_ref[...].astype(BF16)
    v_all = v_ref[...].astype(BF16)
    row_slices = [slice(r * GRID_W, (r + 1) * GRID_W) for r in range(GRID_ROWS)]
    bands = [slice(rs * GRID_W, rs * GRID_W + NLOC) for rs in _ROW_START]
    sctx_ref[...] = _dot_nt(q_all, kc_ref[...].astype(BF16)) * scale
    for r, (rows, band) in enumerate(zip(row_slices, bands)):
        sloc_ref[rows, :] = _dot_nt(q_all[rows], k_all[band]) * scale + bias_ref[r - _ROW_START[r]]
    for rows in row_slices:
        s_loc = sloc_ref[rows, :]
        s_ctx = sctx_ref[rows, :]
        m = jnp.maximum(jnp.max(s_loc, axis=-1, keepdims=True), jnp.max(s_ctx, axis=-1, keepdims=True))
        p_loc = jnp.exp(s_loc - m)
        p_ctx = jnp.exp(s_ctx - m)
        den_ref[rows, :] = jnp.sum(p_loc, axis=-1, keepdims=True) + jnp.sum(p_ctx, axis=-1, keepdims=True)
        ploc_ref[rows, :] = p_loc.astype(BF16)
        pctx_ref[rows, :] = p_ctx.astype(BF16)
    o_ctx = _dot(pctx_ref[...], vc_ref[...].astype(BF16))
    for rows, band in zip(row_slices, bands):
        o = (_dot(ploc_ref[rows, :], v_all[band]) + o_ctx[rows]) / den_ref[rows, :]
        o_ref[rows, :] = o.astype(BF16)


def _nat_bias_kernel(rpb_ref, o_ref):
    base = (pl.program_id(0) * NA_HEADS + pl.program_id(1)) * (RPB_ROWS * RPB_COLS)
    q = lax.broadcasted_iota(jnp.int32, (GRID_W, GRID_W), 0)
    k = lax.broadcasted_iota(jnp.int32, (GRID_W, GRID_W), 1)
    col_start = jnp.clip(q - NA_KW // 2, 0, GRID_W - NA_KW)
    in_window = (k >= col_start) & (k < col_start + NA_KW)
    dc = jnp.clip(k - q + NA_KW - 1, 0, RPB_COLS - 1)
    tabs = []
    for dr in range(RPB_ROWS):
        t = jnp.zeros((GRID_W, GRID_W), F32)
        for c in range(RPB_COLS):
            t = jnp.where(dc == c, rpb_ref[base + dr * RPB_COLS + c], t)
        tabs.append(jnp.where(in_window, t, NEG_BIG))
    for off in range(_KH):
        for p in range(_KH // 2):
            pair = [tabs[kr - off + NA_KH - 1] for kr in (2 * p, 2 * p + 1)]
            o_ref[off, :, 2 * p * GRID_W:(2 * p + 2) * GRID_W] = jnp.concatenate(pair, axis=1)


def _nat_bias(na_rpb):
    return pl.pallas_call(
        _nat_bias_kernel,
        grid=(DEPTH, NA_HEADS),
        in_specs=[pl.BlockSpec(memory_space=pltpu.SMEM)],
        out_specs=pl.BlockSpec((None, None, _KH, GRID_W, NLOC), lambda l, h: (l, h, 0, 0, 0)),
        out_shape=jax.ShapeDtypeStruct((DEPTH, NA_HEADS, _KH, GRID_W, NLOC), F32),
        compiler_params=_cparams(("arbitrary", "arbitrary")),
        name="nat_bias",
    )(na_rpb.reshape(-1))


def _nat(proj, cache_k, cache_v, bias, layer, mix):
    def col_spec(col0):
        return pl.BlockSpec((DEC_SEQ, LANE), lambda b, h: (b, col0 + h))

    cache_spec = pl.BlockSpec((None, None, PAST_LEN, LANE), lambda b, h: (b, layer, 0, h))
    return pl.pallas_call(
        _nat_kernel,
        grid=(DEC_BATCH, NA_HEADS),
        in_specs=[col_spec(COL_NQ), col_spec(COL_NK), col_spec(COL_NV), cache_spec, cache_spec,
                  pl.BlockSpec((None, None, _KH, GRID_W, NLOC), lambda b, h: (layer, h, 0, 0, 0)),
                  pl.BlockSpec(memory_space=pl.ANY)],
        out_specs=pl.BlockSpec((DEC_SEQ, LANE), lambda b, h: (b, MIX_COL_ATTN + h)),
        out_shape=jax.ShapeDtypeStruct(mix.shape, mix.dtype),
        scratch_shapes=[pltpu.VMEM((DEC_SEQ, PAST_LEN), F32), pltpu.VMEM((DEC_SEQ, NLOC), F32),
                        pltpu.VMEM((DEC_SEQ, PAST_LEN), BF16), pltpu.VMEM((DEC_SEQ, NLOC), BF16),
                        pltpu.VMEM((DEC_SEQ, 1), F32)],
        input_output_aliases={6: 0},
        compiler_params=_cparams(("arbitrary", "arbitrary")),
        name="nat",
    )(proj, proj, proj, cache_k, cache_v, bias, mix)


def kernel(x_prompt, x_sample, cache_na_k, cache_na_v, state_hgrn, c, c_ctx, w_ada, b_ada, norm_mix_w, w_in,
           hgrn_lb_raw, hgrn_gnorm_w, conv_w, na_rpb, w_out, norm_ffn_w, w_ffn_gate, w_ffn_up, w_ffn_down,
           final_norm_w):
    xs = {LATENT: x_sample.reshape(_ntok(LATENT), D_MODEL), CONTEXT: x_prompt.reshape(_ntok(CONTEXT), D_MODEL)}
    cvecs = jnp.concatenate([c, c_ctx[None, :], jnp.zeros((MOD_ROWS - DEC_BATCH - 1, D_MODEL), F32)], axis=0)
    mods4 = _adaln(cvecs, w_ada, b_ada).reshape(DEPTH, MOD_ROWS, 1, 6 * D_MODEL)

    p_lb = jax.nn.softmax(hgrn_lb_raw.astype(F32), axis=1)
    cp = jnp.cumsum(p_lb, axis=1)
    lbs = (cp - cp[:, :1]).reshape(2, DEPTH, HGRN_HEADS, 1, LANE)
    gnorm_w = hgrn_gnorm_w.reshape(DEPTH, 1, LANE)
    norm_mix = norm_mix_w.reshape(DEPTH, 1, D_MODEL)
    norm_ffn = norm_ffn_w.reshape(DEPTH, 1, D_MODEL)

    cache_k4 = cache_na_k.reshape(DEC_BATCH, DEPTH, PAST_LEN, NA_WIDTH)
    cache_v4 = cache_na_v.reshape(DEC_BATCH, DEPTH, PAST_LEN, NA_WIDTH)

    w_in_b, w_out_b = w_in.astype(BF16), w_out.astype(BF16)

    nat_bias = _nat_bias(na_rpb)

    kv = None
    states = None
    for l in range(DEPTH):
        for st in (LATENT, CONTEXT):
            res = _inproj(st, xs[st], mods4, l, norm_mix, w_in_b, kv)
            proj = res[0]
            if st.latent:
                mix = _hgrn(st, proj, lbs, gnorm_w, state_hgrn, l, None)[0]
                mix = _conv(st, proj, conv_w, l, mix)
                mix = _nat(proj, cache_k4, cache_v4, nat_bias, l, mix)
            else:
                kv = (res[1], res[2])
                mix, states = _hgrn(st, proj, lbs, gnorm_w, None, l, states)
                mix = _conv(st, proj, conv_w, l, mix)
                mix = _ctx_attn(proj, mix)
            x1 = _outproj(st, xs[st], mix, mods4, l, w_out_b)
            last = l == DEPTH - 1
            if st.latent:
                y, *ffn_w = _ffn(st, x1, mods4, l, norm_ffn, final_norm_w, (w_ffn_gate, w_ffn_up, w_ffn_down), last,
                                 cast_weights=True, nblocks=1)
                xs[st] = _ffn(st, x1, mods4, l, norm_ffn, final_norm_w, ffn_w, last, block0=1,
                              nblocks=_ntok(st) // FFN_TM - 1, out_prev=y)[0]
            else:
                xs[st] = _ffn(st, x1, mods4, l, norm_ffn, final_norm_w, ffn_w, last)[0]

    y_sample = xs[LATENT].reshape(DEC_BATCH, DEC_SEQ, D_MODEL)
    y_prompt = xs[CONTEXT].reshape(BATCH, SEQ, D_MODEL)
    new_k = kv[0].reshape(BATCH, DEPTH, SEQ, NA_HEADS, NA_HEAD_DIM)
    new_v = kv[1].reshape(BATCH, DEPTH, SEQ, NA_HEADS, NA_HEAD_DIM)
    return (y_prompt, y_sample, new_k, new_v, states)
```

```python
import collections
import functools

import numpy as np
import jax
import jax.numpy as jnp
from jax import lax
from jax.experimental import pallas as pl
from jax.experimental.pallas import tpu as pltpu

F32 = jnp.float32
BF16 = jnp.bfloat16

D_MODEL = 2048
BATCH = 16
SEQ = 256
DEPTH = 2
DEC_BATCH = 8
DEC_SEQ = 1024
PAST_LEN = 512
GRID_W = 64
HGRN_HEADS = 8
HGRN_DK = 128
HGRN_DV = 128
HGRN_WIDTH = HGRN_HEADS * HGRN_DV
CONV_WIDTH = 512
NA_HEADS = 4
NA_HEAD_DIM = 128
NA_WIDTH = NA_HEADS * NA_HEAD_DIM
NA_KH = 8
NA_KW = 16
MIX_WIDTH = HGRN_WIDTH + CONV_WIDTH + NA_WIDTH
IN_PROJ_WIDTH = 5 * HGRN_WIDTH + 3 * CONV_WIDTH + 3 * NA_WIDTH
FFN_HIDDEN = ((8 * D_MODEL + 3 * 256 - 1) // (3 * 256)) * 256
EPS = 1e-6

MOD_ROWS = 16
CTX_ROW = DEC_BATCH
LANE = 128
SUBLANE = 8
GRID_ROWS = DEC_SEQ // GRID_W

COL_HQ = 0
COL_HI = HGRN_WIDTH // LANE
COL_HFF = 2 * HGRN_WIDTH // LANE
COL_HFB = 3 * HGRN_WIDTH // LANE
COL_HG = 4 * HGRN_WIDTH // LANE
COL_CB = 5 * HGRN_WIDTH // LANE
COL_CC = COL_CB + CONV_WIDTH // LANE
COL_CX = COL_CC + CONV_WIDTH // LANE
COL_NQ = COL_CX + CONV_WIDTH // LANE
COL_NK = COL_NQ + NA_WIDTH // LANE
COL_NV = COL_NK + NA_WIDTH // LANE

NEG_BIG = -1e30
VMEM_LIMIT = 60 * 1024 * 1024

Stream = collections.namedtuple("Stream", ["name", "seq", "nseq", "latent"])
LATENT = Stream("lat", DEC_SEQ, DEC_BATCH, True)
CONTEXT = Stream("ctx", SEQ, BATCH, False)


def _ntok(st):
    return st.seq * st.nseq


def _cparams(sem):
    return pltpu.CompilerParams(dimension_semantics=sem, vmem_limit_bytes=VMEM_LIMIT)


def _dot(a, b):
    return jnp.dot(a, b, preferred_element_type=F32)


def _dot_nt(a, b):
    return lax.dot_general(a, b, (((1,), (1,)), ((), ())), preferred_element_type=F32)


def _dot_tn(a, b):
    return lax.dot_general(a, b, (((0,), (0,)), ((), ())), preferred_element_type=F32)


def _sigmoid(x):
    return 1.0 / (1.0 + jnp.exp(-x))


ADA_TN = 1024


def _adaln_kernel(c_ref, w_ref, b_ref, o_ref):
    c = c_ref[...]
    s = (c * _sigmoid(c)).astype(BF16)
    o_ref[...] = _dot(s, w_ref[...].astype(BF16)) + b_ref[...]


def _adaln(cvecs, w_ada, b_ada):
    n = 6 * D_MODEL
    return pl.pallas_call(
        _adaln_kernel,
        grid=(DEPTH, n // ADA_TN),
        in_specs=[
            pl.BlockSpec((MOD_ROWS, D_MODEL), lambda l, j: (0, 0)),
            pl.BlockSpec((None, D_MODEL, ADA_TN), lambda l, j: (l, 0, j)),
            pl.BlockSpec((None, 1, ADA_TN), lambda l, j: (l, 0, j)),
        ],
        out_specs=pl.BlockSpec((None, MOD_ROWS, ADA_TN), lambda l, j: (l, 0, j)),
        out_shape=jax.ShapeDtypeStruct((DEPTH, MOD_ROWS, n), F32),
        compiler_params=_cparams(("arbitrary", "arbitrary")),
        name="adaln",
    )(cvecs, w_ada, b_ada.reshape(DEPTH, 1, n))


IN_TM = 1024
IN_TN = 1024
TM = 512
FFN_TM = 1024
FFN_TF = 512


def _mod_spec(st, layer, chunk, tm, ngrid, block0=0):
    def row(i):
        return ((block0 + i) * tm) // DEC_SEQ if st.latent else CTX_ROW

    if ngrid == 1:
        return pl.BlockSpec((None, None, 1, D_MODEL), lambda i: (layer, row(i), 0, chunk))
    return pl.BlockSpec((None, None, 1, D_MODEL), lambda i, j: (layer, row(i), 0, chunk))


PROLOGUE_ROWS = 128


def _norm_modulate(x_ref, h_ref, nw_ref, sh_ref, sc_ref):
    gain = nw_ref[...] * (1.0 + sc_ref[...])
    shift = sh_ref[...]

    def chunk(r, carry):
        rows = pl.ds(pl.multiple_of(r * PROLOGUE_ROWS, PROLOGUE_ROWS), PROLOGUE_ROWS)
        x = x_ref[rows, :]
        ms = jnp.mean(x * x, axis=-1, keepdims=True)
        h_ref[rows, :] = (x * lax.rsqrt(ms + EPS) * gain + shift).astype(BF16)
        return carry

    lax.fori_loop(0, x_ref.shape[0] // PROLOGUE_ROWS, chunk, 0)


def _inproj_kernel(*refs, emit_kv):
    x_ref, sh_ref, sc_ref, nw_ref, w_ref = refs[:5]
    outs = refs[-4:] if emit_kv else refs[-2:]
    o_ref = outs[0]
    h_ref = refs[-1]

    @pl.when(pl.program_id(1) == 0)
    def _():
        _norm_modulate(x_ref, h_ref, nw_ref, sh_ref, sc_ref)

    o_ref[...] = _dot(h_ref[...], w_ref[...])

    if emit_kv:
        @pl.when(pl.program_id(1) == pl.num_programs(1) - 1)
        def _():
            k_ref, v_ref = outs[1], outs[2]
            k_ref[...] = o_ref[:, :NA_WIDTH].reshape(k_ref.shape)
            v_ref[...] = o_ref[:, NA_WIDTH:].reshape(v_ref.shape)


def _layer_row_spec(layer, width, ngrid):
    if ngrid == 1:
        return pl.BlockSpec((None, 1, width), lambda i: (layer, 0, 0))
    return pl.BlockSpec((None, 1, width), lambda i, j: (layer, 0, 0))


def _inproj(st, x, mods4, layer, norm_w, w_bf16, kv_prev):
    emit_kv = not st.latent
    assert COL_NK * LANE == IN_PROJ_WIDTH - IN_TN and 2 * NA_WIDTH == IN_TN
    ntok = _ntok(st)
    in_specs = [
        pl.BlockSpec((IN_TM, D_MODEL), lambda i, j: (i, 0)),
        _mod_spec(st, layer, 0, IN_TM, 2),
        _mod_spec(st, layer, 1, IN_TM, 2),
        _layer_row_spec(layer, D_MODEL, 2),
        pl.BlockSpec((None, D_MODEL, IN_TN), lambda i, j: (layer, 0, j)),
    ]
    args = [x, mods4, mods4, norm_w, w_bf16]
    out_specs = [pl.BlockSpec((IN_TM, IN_TN), lambda i, j: (i, j))]
    out_shape = [jax.ShapeDtypeStruct((ntok, IN_PROJ_WIDTH), F32)]
    aliases = {}
    if emit_kv:
        nb = IN_TM // st.seq
        kv_spec = pl.BlockSpec((nb, None, st.seq, NA_WIDTH), lambda i, j: (i, layer, 0, 0))
        out_specs += [kv_spec, kv_spec]
        out_shape += [jax.ShapeDtypeStruct((st.nseq, DEPTH, st.seq, NA_WIDTH), F32)] * 2
        if kv_prev is not None:
            in_specs += [pl.BlockSpec(memory_space=pl.ANY)] * 2
            args += list(kv_prev)
            aliases = {5: 1, 6: 2}
    return pl.pallas_call(
        functools.partial(_inproj_kernel, emit_kv=emit_kv),
        grid=(ntok // IN_TM, IN_PROJ_WIDTH // IN_TN),
        in_specs=in_specs,
        out_specs=out_specs,
        out_shape=out_shape,
        scratch_shapes=[pltpu.VMEM((IN_TM, D_MODEL), BF16)],
        input_output_aliases=aliases,
        compiler_params=_cparams(("arbitrary", "arbitrary")),
        name="inproj_" + st.name,
    )(*args)


def _outproj_kernel(x_ref, m_ref, g_ref, w_ref, o_ref):
    o_ref[...] = x_ref[...] + g_ref[...] * _dot(m_ref[...], w_ref[...])


def _outproj(st, x, mix, mods4, layer, w_bf16):
    ntok = _ntok(st)
    return pl.pallas_call(
        _outproj_kernel,
        grid=(ntok // TM,),
        in_specs=[
            pl.BlockSpec((TM, D_MODEL), lambda i: (i, 0)),
            pl.BlockSpec((TM, MIX_WIDTH), lambda i: (i, 0)),
            _mod_spec(st, layer, 2, TM, 1),
            pl.BlockSpec((None, MIX_WIDTH, D_MODEL), lambda i: (layer, 0, 0)),
        ],
        out_specs=pl.BlockSpec((TM, D_MODEL), lambda i: (i, 0)),
        out_shape=jax.ShapeDtypeStruct((ntok, D_MODEL), F32),
        compiler_params=_cparams(("arbitrary",)),
        name="outproj_" + st.name,
    )(x, mix, mods4, w_bf16)


FFN_CAST_TF = 256


def _ffn_kernel(*refs, final_norm, cast_weights, n_alias):
    x_ref, sh_ref, sc_ref, g_ref, nw_ref, fw_ref, wg_ref, wu_ref, wd_ref = refs[:9]
    outs = refs[9 + n_alias:-1]
    o_ref = outs[0]
    h_ref = refs[-1]
    j = pl.program_id(1)

    @pl.when(j == 0)
    def _():
        _norm_modulate(x_ref, h_ref, nw_ref, sh_ref, sc_ref)
        o_ref[...] = jnp.zeros_like(o_ref)

    if cast_weights:
        wg, wu, wd = (w[...].astype(BF16) for w in (wg_ref, wu_ref, wd_ref))
        for w, w_out_ref in zip((wg, wu, wd), outs[1:]):
            w_out_ref[...] = w
    else:
        wg, wu, wd = wg_ref[...], wu_ref[...], wd_ref[...]

    h = h_ref[...]
    a = _dot(h, wg)
    u = _dot(h, wu)
    t = (a * _sigmoid(a) * u).astype(BF16)
    for n in range(0, D_MODEL, FFN_TF):
        o_ref[:, n:n + FFN_TF] += _dot(t, wd[:, n:n + FFN_TF])

    @pl.when(j == pl.num_programs(1) - 1)
    def _():
        gate = g_ref[...]

        def chunk(r, carry):
            rows = pl.ds(pl.multiple_of(r * PROLOGUE_ROWS, PROLOGUE_ROWS), PROLOGUE_ROWS)
            y = x_ref[rows, :] + gate * o_ref[rows, :]
            if final_norm:
                ms = jnp.mean(y * y, axis=-1, keepdims=True)
                y = y * lax.rsqrt(ms + EPS) * fw_ref[...]
            o_ref[rows, :] = y
            return carry

        lax.fori_loop(0, FFN_TM // PROLOGUE_ROWS, chunk, 0)


def _ffn(st, x, mods4, layer, norm_w, final_w, weights, final_norm, *, cast_weights=False, block0=0, nblocks=None,
         out_prev=None):
    ntok = _ntok(st)
    TM = FFN_TM
    tf = FFN_CAST_TF if cast_weights else FFN_TF
    nblocks = ntok // TM if nblocks is None else nblocks
    if cast_weights:
        w_specs = [pl.BlockSpec((None, D_MODEL, tf), lambda i, j: (layer, 0, j)),
                   pl.BlockSpec((None, D_MODEL, tf), lambda i, j: (layer, 0, j)),
                   pl.BlockSpec((None, tf, D_MODEL), lambda i, j: (layer, j, 0))]
    else:
        w_specs = [pl.BlockSpec((D_MODEL, tf), lambda i, j: (0, j)),
                   pl.BlockSpec((D_MODEL, tf), lambda i, j: (0, j)),
                   pl.BlockSpec((tf, D_MODEL), lambda i, j: (j, 0))]
    in_specs = [
        pl.BlockSpec((TM, D_MODEL), lambda i, j: (block0 + i, 0)),
        _mod_spec(st, layer, 3, TM, 2, block0),
        _mod_spec(st, layer, 4, TM, 2, block0),
        _mod_spec(st, layer, 5, TM, 2, block0),
        _layer_row_spec(layer, D_MODEL, 2),
        pl.BlockSpec((1, D_MODEL), lambda i, j: (0, 0)),
    ] + w_specs
    args = [x, mods4, mods4, mods4, norm_w, final_w.reshape(1, D_MODEL)] + list(weights)
    out_specs = [pl.BlockSpec((TM, D_MODEL), lambda i, j: (block0 + i, 0))]
    out_shape = [jax.ShapeDtypeStruct((ntok, D_MODEL), F32)]
    if cast_weights:
        out_specs += [pl.BlockSpec((D_MODEL, tf), lambda i, j: (0, j)),
                      pl.BlockSpec((D_MODEL, tf), lambda i, j: (0, j)),
                      pl.BlockSpec((tf, D_MODEL), lambda i, j: (j, 0))]
        out_shape += [jax.ShapeDtypeStruct((D_MODEL, FFN_HIDDEN), BF16)] * 2
        out_shape += [jax.ShapeDtypeStruct((FFN_HIDDEN, D_MODEL), BF16)]
    aliases = {}
    if out_prev is not None:
        aliases = {len(args): 0}
        in_specs.append(pl.BlockSpec(memory_space=pl.ANY))
        args.append(out_prev)
    return pl.pallas_call(
        functools.partial(_ffn_kernel, final_norm=final_norm, cast_weights=cast_weights, n_alias=len(aliases)),
        grid=(nblocks, FFN_HIDDEN // tf),
        in_specs=in_specs,
        out_specs=out_specs,
        out_shape=out_shape,
        scratch_shapes=[pltpu.VMEM((TM, D_MODEL), BF16)],
        input_output_aliases=aliases,
        compiler_params=_cparams(("arbitrary", "arbitrary")),
        name="ffn_" + st.name + ("_cast" if cast_weights else ""),
    )(*args)


HGRN_CHUNK = 64
HGRN_HG = 2
HGRN_UNROLL = 4
HGRN_LEVELS = [HGRN_CHUNK >> (i + 1) for i in range(HGRN_CHUNK.bit_length() - 1)]


LOG2E = 1.4426950408889634


def _hgrn_gates(x, lb):
    e = jnp.exp(-jnp.abs(x))
    r = 1.0 / (1.0 + e)
    er = e * r
    pos = x >= 0.0
    sig = jnp.where(pos, r, er)
    nsig = jnp.where(pos, er, r)
    ls2 = jnp.minimum(x, 0.0) * LOG2E + jnp.log2(r)
    f = lb + (1.0 - lb) * sig
    log2_f = jnp.where(lb > 0.0, jnp.log2(f), ls2)
    return log2_f, f, (1.0 - lb) * nsig


def _hgrn_chunk_a(x, lb, b_ref, tri):
    g, f, k = _hgrn_gates(x, lb)
    g_hi = g.astype(BF16)
    g_lo = (g - g_hi.astype(F32)).astype(BF16)
    b = _dot(tri, g_hi) + _dot(tri, g_lo)
    b_ref[...] = b
    return b, f, k


def _hgrn_chunk_b(b, f, k, q, v, st, b_ref, mask_ref, dirn):
    C = HGRN_CHUNK
    fwd = dirn == 0
    b_end = b_ref[pl.ds(C - 1 if fwd else 0, 1), :]
    vb = v.astype(BF16)
    q0 = (q * jnp.exp2(b)).astype(BF16)
    stb = st.astype(BF16)
    kd = (k * jnp.exp2(b_end - b)).astype(BF16)
    st_new = jnp.exp2(b_end) * st + _dot_tn(vb, kd)

    row = lax.broadcasted_iota(jnp.int32, (C, LANE), 0)
    sub = lax.broadcasted_iota(jnp.int32, (SUBLANE, LANE), 0)
    attn = [None] * (C // SUBLANE)

    def add_rows(first_row, term):
        for t in range(term.shape[0] // SUBLANE):
            blk = first_row // SUBLANE + t
            piece = term[t * SUBLANE:(t + 1) * SUBLANE]
            attn[blk] = piece if attn[blk] is None else attn[blk] + piece

    for lvl, m in enumerate(HGRN_LEVELS):
        mask_idx = dirn * len(HGRN_LEVELS) + lvl
        if m >= SUBLANE:
            parts, readers = [], []
            for p in range(C // (2 * m)):
                lo = p * 2 * m
                left = slice(lo, lo + m)
                right = slice(lo + m, lo + 2 * m)
                beta = b_ref[pl.ds(lo + (m - 1 if fwd else m), 1), :]
                if fwd:
                    parts += [k[left] * jnp.exp2(beta - b[left]), q[right] * jnp.exp2(b[right] - beta)]
                    readers.append((lo + m, parts[-1]))
                else:
                    parts += [q[left] * jnp.exp2(b[left] - beta), k[right] * jnp.exp2(beta - b[right])]
                    readers.append((lo, parts[-2]))
            wb = jnp.concatenate(parts, axis=0).astype(BF16)
            wq = jnp.concatenate([r for _, r in readers], axis=0).astype(BF16)
            term = _dot_nt(wq, wb)
            for n, (first_row, _) in enumerate(readers):
                add_rows(first_row, term[n * m:(n + 1) * m] * mask_ref[mask_idx, first_row:first_row + m, :])
        else:
            reads = ((row & m) != 0) if fwd else ((row & m) == 0)
            if m == 1:
                w = jnp.where(reads, q * f, k)
            else:
                betas = []
                for u in range(C // SUBLANE):
                    rows = []
                    for p in range(SUBLANE // (2 * m)):
                        rows.append(b_ref[pl.ds(u * SUBLANE + p * 2 * m + (m - 1 if fwd else m), 1), :])
                    beta_u = rows[-1]
                    for p in range(len(rows) - 2, -1, -1):
                        beta_u = jnp.where(sub < (p + 1) * 2 * m, rows[p], beta_u)
                    betas.append(jnp.broadcast_to(beta_u, (SUBLANE, LANE)))
                beta = jnp.concatenate(betas, axis=0)
                w = jnp.where(reads, q, k) * jnp.exp2(-jnp.abs(b - beta))
            wb = w.astype(BF16)
            add_rows(0, _dot_nt(wb, wb) * mask_ref[mask_idx])
    attn_b = jnp.concatenate(attn, axis=0).astype(BF16)
    diag = jnp.sum(q * k, axis=-1, keepdims=True) * v
    return q0, stb, attn_b, vb, diag, st_new


def _hgrn_tables():
    C = HGRN_CHUNK
    row, col = np.indices((C, C))
    tri = np.stack([col <= row, col >= row])
    fwd, bwd = [], []
    for m in HGRN_LEVELS:
        same = (row // (2 * m)) == (col // (2 * m))
        fwd.append(same & ((row & m) != 0) & ((col & m) == 0))
        bwd.append(same & ((row & m) == 0) & ((col & m) != 0))
    return jnp.asarray(tri, BF16), jnp.asarray(np.stack(fwd + bwd), F32)


def _hgrn_unit(q_ref, v_ref, ff_ref, fb_ref, g_ref, lb_ref, gnw_ref, tri_ref, mask_ref, s0_ref, o_ref, s_out_ref,
               oacc_ref, b_ref, st_ref, *, T):
    C = HGRN_CHUNK
    U = HGRN_UNROLL
    nc = T // C
    assert nc % U == 0

    oacc_ref[...] = jnp.zeros_like(oacc_ref)
    for hh in range(HGRN_HG):
        for d in range(2):
            if s0_ref is not None:
                st_ref[2 * hh + d] = s0_ref[d, hh].T
            else:
                st_ref[2 * hh + d] = jnp.zeros((HGRN_DV, HGRN_DK), F32)

    q_scale = HGRN_DK ** -0.5

    def body(c, carry):
        chains = []
        for hh in range(HGRN_HG):
            lanes = slice(hh * LANE, (hh + 1) * LANE)
            for d in range(2):
                for u in range(U):
                    cc = c * U + u if d == 0 else nc - 1 - (c * U + u)
                    rows = pl.ds(pl.multiple_of(cc * C, C), C)
                    chains.append((hh, d, u, rows, lanes))
        stage_a = {}
        states = {}
        pending = []
        for step in range(U + 2):
            for n, (hh, d, u, rows, lanes) in enumerate(chains):
                if u == step:
                    x_ref = ff_ref if d == 0 else fb_ref
                    stage_a[n] = _hgrn_chunk_a(x_ref[rows, lanes], lb_ref[d, hh], b_ref.at[n], tri_ref[d])
            ready, pending = pending, []
            for n, (hh, d, u, rows, lanes) in enumerate(chains):
                if u != step - 1:
                    continue
                b, f, k = stage_a[n]
                idx = 2 * hh + d
                st = st_ref[idx] if u == 0 else states[idx]
                q0, stb, attn_b, vb, diag, st = _hgrn_chunk_b(b, f, k, q_ref[rows, lanes], v_ref[rows, lanes], st,
                                                              b_ref.at[n], mask_ref, d)
                states[idx] = st
                if u == U - 1:
                    st_ref[idx] = st
                pending.append((rows, lanes, q0, stb, attn_b, vb, diag))
            for rows, lanes, q0, stb, attn_b, vb, diag in ready:
                oacc_ref[rows, lanes] += (_dot_nt(q0, stb) + _dot(attn_b, vb)) + diag
        assert not pending
        return carry

    lax.fori_loop(0, nc // U, body, 0)

    for hh in range(HGRN_HG):
        lanes = slice(hh * LANE, (hh + 1) * LANE)
        if s_out_ref is not None:
            for d in range(2):
                s_out_ref[d, hh] = st_ref[2 * hh + d].T
        o = oacc_ref[:, lanes]
        ms = jnp.mean(o * o, axis=-1, keepdims=True) * (q_scale * q_scale)
        gate = g_ref[:, lanes]
        y = o * (q_scale * lax.rsqrt(ms + EPS)) * gnw_ref[...] * (gate * _sigmoid(gate))
        o_ref[:, lanes] = y.astype(BF16)


def _hgrn_kernel(*refs, T, has_s0, emit_state, n_alias):
    n_in = 9 + int(has_s0)
    n_out = 2 if emit_state else 1
    assert len(refs) == n_in + n_alias + n_out + 3
    s0_ref = refs[9] if has_s0 else None
    outs = refs[n_in + n_alias:n_in + n_alias + n_out]
    _hgrn_unit(*refs[:9], s0_ref, outs[0], outs[1] if emit_state else None, *refs[-3:], T=T)


def _hgrn_io(st, proj, lbs, gnorm_w, tables, s0, layer, states_prev, unit):
    hg = HGRN_HG
    wide = hg * LANE
    T = st.seq
    C = HGRN_CHUNK

    def at(fn):
        return lambda *g: fn(*unit(*g))

    def col_spec(col0):
        return pl.BlockSpec((T, wide), at(lambda b, h: (b, col0 // hg + h)))

    tri, masks = tables
    in_specs = [col_spec(COL_HQ), col_spec(COL_HI), col_spec(COL_HFF), col_spec(COL_HFB), col_spec(COL_HG),
                pl.BlockSpec((2, None, hg, 1, LANE), at(lambda b, h: (0, layer, h, 0, 0))),
                pl.BlockSpec((None, 1, LANE), at(lambda b, h: (layer, 0, 0))),
                pl.BlockSpec(tri.shape, at(lambda b, h: (0, 0, 0))),
                pl.BlockSpec(masks.shape, at(lambda b, h: (0, 0, 0)))]
    args = [proj, proj, proj, proj, proj, lbs, gnorm_w, tri, masks]
    state_spec = pl.BlockSpec((None, None, 2, hg, HGRN_DK, HGRN_DV), at(lambda b, h: (b, layer, 0, h, 0, 0)))
    if st.latent:
        in_specs.append(state_spec)
        args.append(s0)
    out_specs = [pl.BlockSpec((T, wide), at(lambda b, h: (b, h)))]
    out_shape = [jax.ShapeDtypeStruct((_ntok(st), MIX_WIDTH), BF16)]
    aliases = {}
    if not st.latent:
        out_specs.append(state_spec)
        out_shape.append(jax.ShapeDtypeStruct((st.nseq, DEPTH, 2, HGRN_HEADS, HGRN_DK, HGRN_DV), F32))
        if states_prev is not None:
            aliases = {len(args): 1}
            in_specs.append(pl.BlockSpec(memory_space=pl.ANY))
            args.append(states_prev)
    scratch = [pltpu.VMEM((T, wide), F32), pltpu.VMEM((2 * hg * HGRN_UNROLL, C, LANE), F32),
               pltpu.VMEM((2 * hg, HGRN_DV, HGRN_DK), F32)]
    return in_specs, args, out_specs, out_shape, scratch, aliases


def _hgrn(st, proj, lbs, gnorm_w, tables, s0, layer, states_prev):
    in_specs, args, out_specs, out_shape, scratch, aliases = _hgrn_io(
        st, proj, lbs, gnorm_w, tables, s0, layer, states_prev, lambda b, h: (b, h))
    return pl.pallas_call(
        functools.partial(_hgrn_kernel, T=st.seq, has_s0=st.latent, emit_state=not st.latent, n_alias=len(aliases)),
        grid=(st.nseq, HGRN_HEADS // HGRN_HG),
        in_specs=in_specs,
        out_specs=out_specs,
        out_shape=out_shape,
        scratch_shapes=scratch,
        input_output_aliases=aliases,
        compiler_params=_cparams(("arbitrary", "arbitrary")),
        name="hgrn_" + st.name,
    )(*args)


def _inproj_hgrn_kernel(*refs, T, n_hgrn_in):
    x_ref, sh_ref, sc_ref, nw_ref, w_ref = refs[:5]
    hgrn_in = refs[5:5 + n_hgrn_in]
    o_ref, mix_ref, s_out_ref = refs[5 + n_hgrn_in:5 + n_hgrn_in + 3]
    h_ref = refs[-4]

    @pl.when(pl.program_id(1) == 0)
    def _():
        _norm_modulate(x_ref, h_ref, nw_ref, sh_ref, sc_ref)

    o_ref[...] = _dot(h_ref[...], w_ref[...])
    _hgrn_unit(*hgrn_in[:9], None, mix_ref, s_out_ref, *refs[-3:], T=T)


def _inproj_hgrn(st, x, mods4, layer, norm_w, w_bf16, st2, proj2, lbs, gnorm_w, tables, states_prev):
    assert st.latent and not st2.latent
    ntok = _ntok(st)
    ni, nj = ntok // IN_TM, IN_PROJ_WIDTH // IN_TN
    nh = HGRN_HEADS // HGRN_HG
    assert ni * nj == st2.nseq * nh

    def unit(i, j):
        s = i * nj + j
        return s // nh, s % nh

    h_in, h_args, h_out, h_shape, h_scratch, h_alias = _hgrn_io(st2, proj2, lbs, gnorm_w, tables, None, layer,
                                                                states_prev, unit)
    in_specs = [
        pl.BlockSpec((IN_TM, D_MODEL), lambda i, j: (i, 0)),
        _mod_spec(st, layer, 0, IN_TM, 2),
        _mod_spec(st, layer, 1, IN_TM, 2),
        _layer_row_spec(layer, D_MODEL, 2),
        pl.BlockSpec((None, D_MODEL, IN_TN), lambda i, j: (layer, 0, j)),
    ] + h_in
    args = [x, mods4, mods4, norm_w, w_bf16] + h_args
    return pl.pallas_call(
        functools.partial(_inproj_hgrn_kernel, T=st2.seq, n_hgrn_in=len(h_args)),
        grid=(ni, nj),
        in_specs=in_specs,
        out_specs=[pl.BlockSpec((IN_TM, IN_TN), lambda i, j: (i, j))] + h_out,
        out_shape=[jax.ShapeDtypeStruct((ntok, IN_PROJ_WIDTH), F32)] + h_shape,
        scratch_shapes=[pltpu.VMEM((IN_TM, D_MODEL), BF16)] + h_scratch,
        input_output_aliases={5 + a: 1 + o for a, o in h_alias.items()},
        compiler_params=_cparams(("arbitrary", "arbitrary")),
        name="inproj_" + st.name + "_hgrn_" + st2.name,
    )(*args)


def _conv_kernel(cb_ref, cc_ref, cx_ref, w_ref, mix_ref, o_ref, *, T):
    del mix_ref
    u = cc_ref[...] * cx_ref[...]
    row = lax.broadcasted_iota(jnp.int32, (T, CONV_WIDTH), 0)
    u_prev = jnp.where(row == 0, 0.0, pltpu.roll(u, 1, axis=0))
    u_next = jnp.where(row == T - 1, 0.0, pltpu.roll(u, T - 1, axis=0))
    y = u_prev * w_ref[0:1, :] + u * w_ref[1:2, :] + u_next * w_ref[2:3, :]
    o_ref[...] = (cb_ref[...] * y).astype(BF16)


def _conv(st, proj, conv_w, layer, mix):
    cw = CONV_WIDTH // LANE
    T = st.seq

    def col_spec(col0):
        return pl.BlockSpec((T, CONV_WIDTH), lambda b: (b, col0 // cw))

    return pl.pallas_call(
        functools.partial(_conv_kernel, T=T),
        grid=(st.nseq,),
        in_specs=[col_spec(COL_CB), col_spec(COL_CC), col_spec(COL_CX),
                  pl.BlockSpec((None, 3, CONV_WIDTH), lambda b: (layer, 0, 0)),
                  pl.BlockSpec(memory_space=pl.ANY)],
        out_specs=pl.BlockSpec((T, CONV_WIDTH), lambda b: (b, HGRN_WIDTH // CONV_WIDTH)),
        out_shape=jax.ShapeDtypeStruct(mix.shape, mix.dtype),
        input_output_aliases={4: 0},
        compiler_params=_cparams(("arbitrary",)),
        name="conv_" + st.name,
    )(proj, proj, proj, conv_w, mix)


MIX_COL_ATTN = (HGRN_WIDTH + CONV_WIDTH) // LANE


def _ctx_attn_kernel(q_ref, k_ref, v_ref, mix_ref, o_ref):
    del mix_ref
    scale = NA_HEAD_DIM ** -0.5
    heads = [slice(h * LANE, (h + 1) * LANE) for h in range(NA_HEADS)]
    scores = [_dot_nt(q_ref[:, hs].astype(BF16), k_ref[:, hs].astype(BF16)) * scale for hs in heads]
    probs = []
    for s in scores:
        p = jnp.exp(s - jnp.max(s, axis=-1, keepdims=True))
        probs.append((p.astype(BF16), jnp.sum(p, axis=-1, keepdims=True)))
    for hs, (p, l) in zip(heads, probs):
        o_ref[:, hs] = (_dot(p, v_ref[:, hs].astype(BF16)) / l).astype(BF16)


def _ctx_attn(proj, mix):
    def col_spec(col0):
        return pl.BlockSpec((SEQ, NA_WIDTH), lambda b: (b, col0 * LANE // NA_WIDTH))

    return pl.pallas_call(
        _ctx_attn_kernel,
        grid=(BATCH,),
        in_specs=[col_spec(COL_NQ), col_spec(COL_NK), col_spec(COL_NV), pl.BlockSpec(memory_space=pl.ANY)],
        out_specs=pl.BlockSpec((SEQ, NA_WIDTH), lambda b: (b, MIX_COL_ATTN * LANE // NA_WIDTH)),
        out_shape=jax.ShapeDtypeStruct(mix.shape, mix.dtype),
        input_output_aliases={3: 0},
        compiler_params=_cparams(("arbitrary",)),
        name="ctx_attn",
    )(proj, proj, proj, mix)


_KH = min(NA_KH, GRID_ROWS)
_ROW_START = [int(v) for v in np.clip(np.arange(GRID_ROWS) - _KH // 2, 0, GRID_ROWS - _KH)]
NLOC = _KH * GRID_W
RPB_ROWS = 2 * NA_KH - 1
RPB_COLS = 2 * NA_KW - 1


def _nat_kernel(q_ref, k_ref, v_ref, kc_ref, vc_ref, bias_ref, mix_ref, o_ref, sctx_ref, sloc_ref, pctx_ref,
                ploc_ref, den_ref):
    del mix_ref
    scale = NA_HEAD_DIM ** -0.5
    q_all = q_ref[...].astype(BF16)
    k_all = k_ref[...].astype(BF16)
    v_all = v_ref[...].astype(BF16)
    row_slices = [slice(r * GRID_W, (r + 1) * GRID_W) for r in range(GRID_ROWS)]
    bands = [slice(rs * GRID_W, rs * GRID_W + NLOC) for rs in _ROW_START]
    sctx_ref[...] = _dot_nt(q_all, kc_ref[...].astype(BF16)) * scale
    for r, (rows, band) in enumerate(zip(row_slices, bands)):
        sloc_ref[rows, :] = _dot_nt(q_all[rows], k_all[band]) * scale + bias_ref[r - _ROW_START[r]]
    for rows in row_slices:
        s_loc = sloc_ref[rows, :]
        s_ctx = sctx_ref[rows, :]
        m = jnp.maximum(jnp.max(s_loc, axis=-1, keepdims=True), jnp.max(s_ctx, axis=-1, keepdims=True))
        p_loc = jnp.exp(s_loc - m)
        p_ctx = jnp.exp(s_ctx - m)
        den_ref[rows, :] = jnp.sum(p_loc, axis=-1, keepdims=True) + jnp.sum(p_ctx, axis=-1, keepdims=True)
        ploc_ref[rows, :] = p_loc.astype(BF16)
        pctx_ref[rows, :] = p_ctx.astype(BF16)
    o_ctx = _dot(pctx_ref[...], vc_ref[...].astype(BF16))
    for rows, band in zip(row_slices, bands):
        o = (_dot(ploc_ref[rows, :], v_all[band]) + o_ctx[rows]) / den_ref[rows, :]
        o_ref[rows, :] = o.astype(BF16)


def _nat_bias_kernel(rpb_ref, o_ref):
    base = (pl.program_id(0) * NA_HEADS + pl.program_id(1)) * (RPB_ROWS * RPB_COLS)
    q = lax.broadcasted_iota(jnp.int32, (GRID_W, GRID_W), 0)
    k = lax.broadcasted_iota(jnp.int32, (GRID_W, GRID_W), 1)
    col_start = jnp.clip(q - NA_KW // 2, 0, GRID_W - NA_KW)
    in_window = (k >= col_start) & (k < col_start + NA_KW)
    dc = jnp.clip(k - q + NA_KW - 1, 0, RPB_COLS - 1)
    tabs = []
    for dr in range(RPB_ROWS):
        t = jnp.zeros((GRID_W, GRID_W), F32)
        for c in range(RPB_COLS):
            t = jnp.where(dc == c, rpb_ref[base + dr * RPB_COLS + c], t)
        tabs.append(jnp.where(in_window, t, NEG_BIG))
    for off in range(_KH):
        for p in range(_KH // 2):
            pair = [tabs[kr - off + NA_KH - 1] for kr in (2 * p, 2 * p + 1)]
            o_ref[off, :, 2 * p * GRID_W:(2 * p + 2) * GRID_W] = jnp.concatenate(pair, axis=1)


def _nat_bias(na_rpb):
    return pl.pallas_call(
        _nat_bias_kernel,
        grid=(DEPTH, NA_HEADS),
        in_specs=[pl.BlockSpec(memory_space=pltpu.SMEM)],
        out_specs=pl.BlockSpec((None, None, _KH, GRID_W, NLOC), lambda l, h: (l, h, 0, 0, 0)),
        out_shape=jax.ShapeDtypeStruct((DEPTH, NA_HEADS, _KH, GRID_W, NLOC), F32),
        compiler_params=_cparams(("arbitrary", "arbitrary")),
        name="nat_bias",
    )(na_rpb.reshape(-1))


def _nat(proj, cache_k, cache_v, bias, layer, mix):
    def col_spec(col0):
        return pl.BlockSpec((DEC_SEQ, LANE), lambda b, h: (b, col0 + h))

    cache_spec = pl.BlockSpec((None, None, PAST_LEN, LANE), lambda b, h: (b, layer, 0, h))
    return pl.pallas_call(
        _nat_kernel,
        grid=(DEC_BATCH, NA_HEADS),
        in_specs=[col_spec(COL_NQ), col_spec(COL_NK), col_spec(COL_NV), cache_spec, cache_spec,
                  pl.BlockSpec((None, None, _KH, GRID_W, NLOC), lambda b, h: (layer, h, 0, 0, 0)),
                  pl.BlockSpec(memory_space=pl.ANY)],
        out_specs=pl.BlockSpec((DEC_SEQ, LANE), lambda b, h: (b, MIX_COL_ATTN + h)),
        out_shape=jax.ShapeDtypeStruct(mix.shape, mix.dtype),
        scratch_shapes=[pltpu.VMEM((DEC_SEQ, PAST_LEN), F32), pltpu.VMEM((DEC_SEQ, NLOC), F32),
                        pltpu.VMEM((DEC_SEQ, PAST_LEN), BF16), pltpu.VMEM((DEC_SEQ, NLOC), BF16),
                        pltpu.VMEM((DEC_SEQ, 1), F32)],
        input_output_aliases={6: 0},
        compiler_params=_cparams(("arbitrary", "arbitrary")),
        name="nat",
    )(proj, proj, proj, cache_k, cache_v, bias, mix)


def kernel(x_prompt, x_sample, cache_na_k, cache_na_v, state_hgrn, c, c_ctx, w_ada, b_ada, norm_mix_w, w_in,
           hgrn_lb_raw, hgrn_gnorm_w, conv_w, na_rpb, w_out, norm_ffn_w, w_ffn_gate, w_ffn_up, w_ffn_down,
           final_norm_w):
    xs = {LATENT: x_sample.reshape(_ntok(LATENT), D_MODEL), CONTEXT: x_prompt.reshape(_ntok(CONTEXT), D_MODEL)}
    cvecs = jnp.concatenate([c, c_ctx[None, :], jnp.zeros((MOD_ROWS - DEC_BATCH - 1, D_MODEL), F32)], axis=0)
    mods4 = _adaln(cvecs, w_ada, b_ada).reshape(DEPTH, MOD_ROWS, 1, 6 * D_MODEL)

    p_lb = jax.nn.softmax(hgrn_lb_raw.astype(F32), axis=1)
    cp = jnp.cumsum(p_lb, axis=1)
    lbs = (cp - cp[:, :1]).reshape(2, DEPTH, HGRN_HEADS, 1, LANE)
    gnorm_w = hgrn_gnorm_w.reshape(DEPTH, 1, LANE)
    norm_mix = norm_mix_w.reshape(DEPTH, 1, D_MODEL)
    norm_ffn = norm_ffn_w.reshape(DEPTH, 1, D_MODEL)

    cache_k4 = cache_na_k.reshape(DEC_BATCH, DEPTH, PAST_LEN, NA_WIDTH)
    cache_v4 = cache_na_v.reshape(DEC_BATCH, DEPTH, PAST_LEN, NA_WIDTH)

    w_in_b, w_out_b = w_in.astype(BF16), w_out.astype(BF16)

    nat_bias = _nat_bias(na_rpb)
    tables = _hgrn_tables()

    kv = None
    states = None
    for l in range(DEPTH):
        last = l == DEPTH - 1
        proj_c, *kv = _inproj(CONTEXT, xs[CONTEXT], mods4, l, norm_mix, w_in_b, kv)
        proj_l, mix_c, states = _inproj_hgrn(LATENT, xs[LATENT], mods4, l, norm_mix, w_in_b, CONTEXT, proj_c, lbs,
                                             gnorm_w, tables, states)

        mix_l = _hgrn(LATENT, proj_l, lbs, gnorm_w, tables, state_hgrn, l, None)[0]
        mix_l = _conv(LATENT, proj_l, conv_w, l, mix_l)
        mix_l = _nat(proj_l, cache_k4, cache_v4, nat_bias, l, mix_l)
        x1 = _outproj(LATENT, xs[LATENT], mix_l, mods4, l, w_out_b)
        y, *ffn_w = _ffn(LATENT, x1, mods4, l, norm_ffn, final_norm_w, (w_ffn_gate, w_ffn_up, w_ffn_down), last,
                         cast_weights=True, nblocks=1)
        xs[LATENT] = _ffn(LATENT, x1, mods4, l, norm_ffn, final_norm_w, ffn_w, last, block0=1,
                          nblocks=_ntok(LATENT) // FFN_TM - 1, out_prev=y)[0]

        mix_c = _conv(CONTEXT, proj_c, conv_w, l, mix_c)
        mix_c = _ctx_attn(proj_c, mix_c)
        x1 = _outproj(CONTEXT, xs[CONTEXT], mix_c, mods4, l, w_out_b)
        xs[CONTEXT] = _ffn(CONTEXT, x1, mods4, l, norm_ffn, final_norm_w, ffn_w, last)[0]

    y_sample = xs[LATENT].reshape(DEC_BATCH, DEC_SEQ, D_MODEL)
    y_prompt = xs[CONTEXT].reshape(BATCH, SEQ, D_MODEL)
    new_k = kv[0].reshape(BATCH, DEPTH, SEQ, NA_HEADS, NA_HEAD_DIM)
    new_v = kv[1].reshape(BATCH, DEPTH, SEQ, NA_HEADS, NA_HEAD_DIM)
    return (y_prompt, y_sample, new_k, new_v, states)
```

```python
import collections
import functools

import numpy as np
import jax
import jax.numpy as jnp
from jax import lax
from jax.experimental import pallas as pl
from jax.experimental.pallas import tpu as pltpu

F32 = jnp.float32
BF16 = jnp.bfloat16

D_MODEL = 2048
BATCH = 16
SEQ = 256
DEPTH = 2
DEC_BATCH = 8
DEC_SEQ = 1024
PAST_LEN = 512
GRID_W = 64
HGRN_HEADS = 8
HGRN_DK = 128
HGRN_DV = 128
HGRN_WIDTH = HGRN_HEADS * HGRN_DV
CONV_WIDTH = 512
NA_HEADS = 4
NA_HEAD_DIM = 128
NA_WIDTH = NA_HEADS * NA_HEAD_DIM
NA_KH = 8
NA_KW = 16
MIX_WIDTH = HGRN_WIDTH + CONV_WIDTH + NA_WIDTH
IN_PROJ_WIDTH = 5 * HGRN_WIDTH + 3 * CONV_WIDTH + 3 * NA_WIDTH
FFN_HIDDEN = ((8 * D_MODEL + 3 * 256 - 1) // (3 * 256)) * 256
EPS = 1e-6

MOD_ROWS = 16
CTX_ROW = DEC_BATCH
LANE = 128
SUBLANE = 8
GRID_ROWS = DEC_SEQ // GRID_W

COL_HQ = 0
COL_HI = HGRN_WIDTH // LANE
COL_HFF = 2 * HGRN_WIDTH // LANE
COL_HFB = 3 * HGRN_WIDTH // LANE
COL_HG = 4 * HGRN_WIDTH // LANE
COL_CB = 5 * HGRN_WIDTH // LANE
COL_CC = COL_CB + CONV_WIDTH // LANE
COL_CX = COL_CC + CONV_WIDTH // LANE
COL_NQ = COL_CX + CONV_WIDTH // LANE
COL_NK = COL_NQ + NA_WIDTH // LANE
COL_NV = COL_NK + NA_WIDTH // LANE

NEG_BIG = -1e30
VMEM_LIMIT = 60 * 1024 * 1024

Stream = collections.namedtuple("Stream", ["name", "seq", "nseq", "latent"])
LATENT = Stream("lat", DEC_SEQ, DEC_BATCH, True)
CONTEXT = Stream("ctx", SEQ, BATCH, False)


def _ntok(st):
    return st.seq * st.nseq


def _cparams(sem):
    return pltpu.CompilerParams(dimension_semantics=sem, vmem_limit_bytes=VMEM_LIMIT)


def _dot(a, b):
    return jnp.dot(a, b, preferred_element_type=F32)


def _dot_nt(a, b):
    return lax.dot_general(a, b, (((1,), (1,)), ((), ())), preferred_element_type=F32)


def _dot_tn(a, b):
    return lax.dot_general(a, b, (((0,), (0,)), ((), ())), preferred_element_type=F32)


def _sigmoid(x):
    return 1.0 / (1.0 + jnp.exp(-x))


ADA_TN = 1024


def _adaln_kernel(c_ref, w_ref, b_ref, o_ref):
    c = c_ref[...]
    s = (c * _sigmoid(c)).astype(BF16)
    o_ref[...] = _dot(s, w_ref[...].astype(BF16)) + b_ref[...]


def _adaln(cvecs, w_ada, b_ada):
    n = 6 * D_MODEL
    return pl.pallas_call(
        _adaln_kernel,
        grid=(DEPTH, n // ADA_TN),
        in_specs=[
            pl.BlockSpec((MOD_ROWS, D_MODEL), lambda l, j: (0, 0)),
            pl.BlockSpec((None, D_MODEL, ADA_TN), lambda l, j: (l, 0, j)),
            pl.BlockSpec((None, 1, ADA_TN), lambda l, j: (l, 0, j)),
        ],
        out_specs=pl.BlockSpec((None, MOD_ROWS, ADA_TN), lambda l, j: (l, 0, j)),
        out_shape=jax.ShapeDtypeStruct((DEPTH, MOD_ROWS, n), F32),
        compiler_params=_cparams(("arbitrary", "arbitrary")),
        name="adaln",
    )(cvecs, w_ada, b_ada.reshape(DEPTH, 1, n))


IN_TM = 1024
IN_TN = 1024
IN_PANEL = 256
TM = 512
FFN_TM = 1024
FFN_TF = 512


def _mod_spec(st, layer, chunk, tm, ngrid, block0=0):
    def row(i):
        return ((block0 + i) * tm) // DEC_SEQ if st.latent else CTX_ROW

    if ngrid == 1:
        return pl.BlockSpec((None, None, 1, D_MODEL), lambda i: (layer, row(i), 0, chunk))
    return pl.BlockSpec((None, None, 1, D_MODEL), lambda i, j: (layer, row(i), 0, chunk))


PROLOGUE_ROWS = 128


def _norm_modulate(x_ref, h_ref, nw_ref, sh_ref, sc_ref):
    gain = nw_ref[...] * (1.0 + sc_ref[...])
    shift = sh_ref[...]

    def chunk(r, carry):
        rows = pl.ds(pl.multiple_of(r * PROLOGUE_ROWS, PROLOGUE_ROWS), PROLOGUE_ROWS)
        x = x_ref[rows, :]
        ms = jnp.mean(x * x, axis=-1, keepdims=True)
        h_ref[rows, :] = (x * lax.rsqrt(ms + EPS) * gain + shift).astype(BF16)
        return carry

    lax.fori_loop(0, x_ref.shape[0] // PROLOGUE_ROWS, chunk, 0)


def _inproj_kernel(*refs, emit_kv):
    x_ref, sh_ref, sc_ref, nw_ref, w_ref = refs[:5]
    outs = refs[-4:] if emit_kv else refs[-2:]
    o_ref = outs[0]
    h_ref = refs[-1]

    @pl.when(pl.program_id(1) == 0)
    def _():
        _norm_modulate(x_ref, h_ref, nw_ref, sh_ref, sc_ref)

    o_ref[...] = _dot(h_ref[...], w_ref[...])

    if emit_kv:
        @pl.when(pl.program_id(1) == pl.num_programs(1) - 1)
        def _():
            k_ref, v_ref = outs[1], outs[2]
            k_ref[...] = o_ref[:, :NA_WIDTH].reshape(k_ref.shape)
            v_ref[...] = o_ref[:, NA_WIDTH:].reshape(v_ref.shape)


def _layer_row_spec(layer, width, ngrid):
    if ngrid == 1:
        return pl.BlockSpec((None, 1, width), lambda i: (layer, 0, 0))
    return pl.BlockSpec((None, 1, width), lambda i, j: (layer, 0, 0))


def _inproj(st, x, mods4, layer, norm_w, w_bf16, kv_prev):
    emit_kv = not st.latent
    assert COL_NK * LANE == IN_PROJ_WIDTH - IN_TN and 2 * NA_WIDTH == IN_TN
    ntok = _ntok(st)
    in_specs = [
        pl.BlockSpec((IN_TM, D_MODEL), lambda i, j: (i, 0)),
        _mod_spec(st, layer, 0, IN_TM, 2),
        _mod_spec(st, layer, 1, IN_TM, 2),
        _layer_row_spec(layer, D_MODEL, 2),
        pl.BlockSpec((None, D_MODEL, IN_TN), lambda i, j: (layer, 0, j)),
    ]
    args = [x, mods4, mods4, norm_w, w_bf16]
    out_specs = [pl.BlockSpec((IN_TM, IN_TN), lambda i, j: (i, j))]
    out_shape = [jax.ShapeDtypeStruct((ntok, IN_PROJ_WIDTH), F32)]
    aliases = {}
    if emit_kv:
        nb = IN_TM // st.seq
        kv_spec = pl.BlockSpec((nb, None, st.seq, NA_WIDTH), lambda i, j: (i, layer, 0, 0))
        out_specs += [kv_spec, kv_spec]
        out_shape += [jax.ShapeDtypeStruct((st.nseq, DEPTH, st.seq, NA_WIDTH), F32)] * 2
        if kv_prev is not None:
            in_specs += [pl.BlockSpec(memory_space=pl.ANY)] * 2
            args += list(kv_prev)
            aliases = {5: 1, 6: 2}
    return pl.pallas_call(
        functools.partial(_inproj_kernel, emit_kv=emit_kv),
        grid=(ntok // IN_TM, IN_PROJ_WIDTH // IN_TN),
        in_specs=in_specs,
        out_specs=out_specs,
        out_shape=out_shape,
        scratch_shapes=[pltpu.VMEM((IN_TM, D_MODEL), BF16)],
        input_output_aliases=aliases,
        compiler_params=_cparams(("arbitrary", "arbitrary")),
        name="inproj_" + st.name,
    )(*args)


def _outproj_kernel(x_ref, m_ref, g_ref, w_ref, o_ref):
    o_ref[...] = x_ref[...] + g_ref[...] * _dot(m_ref[...], w_ref[...])


def _outproj(st, x, mix, mods4, layer, w_bf16):
    ntok = _ntok(st)
    return pl.pallas_call(
        _outproj_kernel,
        grid=(ntok // TM,),
        in_specs=[
            pl.BlockSpec((TM, D_MODEL), lambda i: (i, 0)),
            pl.BlockSpec((TM, MIX_WIDTH), lambda i: (i, 0)),
            _mod_spec(st, layer, 2, TM, 1),
            pl.BlockSpec((None, MIX_WIDTH, D_MODEL), lambda i: (layer, 0, 0)),
        ],
        out_specs=pl.BlockSpec((TM, D_MODEL), lambda i: (i, 0)),
        out_shape=jax.ShapeDtypeStruct((ntok, D_MODEL), F32),
        compiler_params=_cparams(("arbitrary",)),
        name="outproj_" + st.name,
    )(x, mix, mods4, w_bf16)


FFN_CAST_TF = 256


def _ffn_kernel(*refs, final_norm, cast_weights, n_alias):
    x_ref, sh_ref, sc_ref, g_ref, nw_ref, fw_ref, wg_ref, wu_ref, wd_ref = refs[:9]
    outs = refs[9 + n_alias:-1]
    o_ref = outs[0]
    h_ref = refs[-1]
    j = pl.program_id(1)

    @pl.when(j == 0)
    def _():
        _norm_modulate(x_ref, h_ref, nw_ref, sh_ref, sc_ref)
        o_ref[...] = jnp.zeros_like(o_ref)

    if cast_weights:
        wg, wu, wd = (w[...].astype(BF16) for w in (wg_ref, wu_ref, wd_ref))
        for w, w_out_ref in zip((wg, wu, wd), outs[1:]):
            w_out_ref[...] = w
    else:
        wg, wu, wd = wg_ref[...], wu_ref[...], wd_ref[...]

    h = h_ref[...]
    a = _dot(h, wg)
    u = _dot(h, wu)
    t = (a * _sigmoid(a) * u).astype(BF16)
    for n in range(0, D_MODEL, FFN_TF):
        o_ref[:, n:n + FFN_TF] += _dot(t, wd[:, n:n + FFN_TF])

    @pl.when(j == pl.num_programs(1) - 1)
    def _():
        gate = g_ref[...]

        def chunk(r, carry):
            rows = pl.ds(pl.multiple_of(r * PROLOGUE_ROWS, PROLOGUE_ROWS), PROLOGUE_ROWS)
            y = x_ref[rows, :] + gate * o_ref[rows, :]
            if final_norm:
                ms = jnp.mean(y * y, axis=-1, keepdims=True)
                y = y * lax.rsqrt(ms + EPS) * fw_ref[...]
            o_ref[rows, :] = y
            return carry

        lax.fori_loop(0, FFN_TM // PROLOGUE_ROWS, chunk, 0)


def _ffn(st, x, mods4, layer, norm_w, final_w, weights, final_norm, *, cast_weights=False, block0=0, nblocks=None,
         out_prev=None):
    ntok = _ntok(st)
    TM = FFN_TM
    tf = FFN_CAST_TF if cast_weights else FFN_TF
    nblocks = ntok // TM if nblocks is None else nblocks
    if cast_weights:
        w_specs = [pl.BlockSpec((None, D_MODEL, tf), lambda i, j: (layer, 0, j)),
                   pl.BlockSpec((None, D_MODEL, tf), lambda i, j: (layer, 0, j)),
                   pl.BlockSpec((None, tf, D_MODEL), lambda i, j: (layer, j, 0))]
    else:
        w_specs = [pl.BlockSpec((D_MODEL, tf), lambda i, j: (0, j)),
                   pl.BlockSpec((D_MODEL, tf), lambda i, j: (0, j)),
                   pl.BlockSpec((tf, D_MODEL), lambda i, j: (j, 0))]
    in_specs = [
        pl.BlockSpec((TM, D_MODEL), lambda i, j: (block0 + i, 0)),
        _mod_spec(st, layer, 3, TM, 2, block0),
        _mod_spec(st, layer, 4, TM, 2, block0),
        _mod_spec(st, layer, 5, TM, 2, block0),
        _layer_row_spec(layer, D_MODEL, 2),
        pl.BlockSpec((1, D_MODEL), lambda i, j: (0, 0)),
    ] + w_specs
    args = [x, mods4, mods4, mods4, norm_w, final_w.reshape(1, D_MODEL)] + list(weights)
    out_specs = [pl.BlockSpec((TM, D_MODEL), lambda i, j: (block0 + i, 0))]
    out_shape = [jax.ShapeDtypeStruct((ntok, D_MODEL), F32)]
    if cast_weights:
        out_specs += [pl.BlockSpec((D_MODEL, tf), lambda i, j: (0, j)),
                      pl.BlockSpec((D_MODEL, tf), lambda i, j: (0, j)),
                      pl.BlockSpec((tf, D_MODEL), lambda i, j: (j, 0))]
        out_shape += [jax.ShapeDtypeStruct((D_MODEL, FFN_HIDDEN), BF16)] * 2
        out_shape += [jax.ShapeDtypeStruct((FFN_HIDDEN, D_MODEL), BF16)]
    aliases = {}
    if out_prev is not None:
        aliases = {len(args): 0}
        in_specs.append(pl.BlockSpec(memory_space=pl.ANY))
        args.append(out_prev)
    return pl.pallas_call(
        functools.partial(_ffn_kernel, final_norm=final_norm, cast_weights=cast_weights, n_alias=len(aliases)),
        grid=(nblocks, FFN_HIDDEN // tf),
        in_specs=in_specs,
        out_specs=out_specs,
        out_shape=out_shape,
        scratch_shapes=[pltpu.VMEM((TM, D_MODEL), BF16)],
        input_output_aliases=aliases,
        compiler_params=_cparams(("arbitrary", "arbitrary")),
        name="ffn_" + st.name + ("_cast" if cast_weights else ""),
    )(*args)


HGRN_CHUNK = 64
HGRN_HG = 2
HGRN_UNROLL = 4
HGRN_LEVELS = [HGRN_CHUNK >> (i + 1) for i in range(HGRN_CHUNK.bit_length() - 1)]


LOG2E = 1.4426950408889634


def _hgrn_gates(x, lb):
    e = jnp.exp(-jnp.abs(x))
    r = 1.0 / (1.0 + e)
    er = e * r
    pos = x >= 0.0
    sig = jnp.where(pos, r, er)
    nsig = jnp.where(pos, er, r)
    ls2 = jnp.minimum(x, 0.0) * LOG2E + jnp.log2(r)
    f = lb + (1.0 - lb) * sig
    log2_f = jnp.where(lb > 0.0, jnp.log2(f), ls2)
    return log2_f, f, (1.0 - lb) * nsig


def _hgrn_chunk_a(x, lb, b_ref, tri):
    g, f, k = _hgrn_gates(x, lb)
    g_hi = g.astype(BF16)
    g_lo = (g - g_hi.astype(F32)).astype(BF16)
    b = _dot(tri, g_hi) + _dot(tri, g_lo)
    b_ref[...] = b
    return b, f, k


def _hgrn_chunk_b(b, f, k, q, v, st, b_ref, mask_ref, dirn):
    C = HGRN_CHUNK
    fwd = dirn == 0
    b_end = b_ref[pl.ds(C - 1 if fwd else 0, 1), :]
    vb = v.astype(BF16)
    q0 = (q * jnp.exp2(b)).astype(BF16)
    stb = st.astype(BF16)
    kd = (k * jnp.exp2(b_end - b)).astype(BF16)
    st_new = jnp.exp2(b_end) * st + _dot_tn(vb, kd)

    row = lax.broadcasted_iota(jnp.int32, (C, LANE), 0)
    sub = lax.broadcasted_iota(jnp.int32, (SUBLANE, LANE), 0)
    attn = [None] * (C // SUBLANE)

    def add_rows(first_row, term):
        for t in range(term.shape[0] // SUBLANE):
            blk = first_row // SUBLANE + t
            piece = term[t * SUBLANE:(t + 1) * SUBLANE]
            attn[blk] = piece if attn[blk] is None else attn[blk] + piece

    for lvl, m in enumerate(HGRN_LEVELS):
        mask_idx = dirn * len(HGRN_LEVELS) + lvl
        if m >= SUBLANE:
            parts, readers = [], []
            for p in range(C // (2 * m)):
                lo = p * 2 * m
                left = slice(lo, lo + m)
                right = slice(lo + m, lo + 2 * m)
                beta = b_ref[pl.ds(lo + (m - 1 if fwd else m), 1), :]
                if fwd:
                    parts += [k[left] * jnp.exp2(beta - b[left]), q[right] * jnp.exp2(b[right] - beta)]
                    readers.append((lo + m, parts[-1]))
                else:
                    parts += [q[left] * jnp.exp2(b[left] - beta), k[right] * jnp.exp2(beta - b[right])]
                    readers.append((lo, parts[-2]))
            wb = jnp.concatenate(parts, axis=0).astype(BF16)
            wq = jnp.concatenate([r for _, r in readers], axis=0).astype(BF16)
            term = _dot_nt(wq, wb)
            for n, (first_row, _) in enumerate(readers):
                add_rows(first_row, term[n * m:(n + 1) * m] * mask_ref[mask_idx, first_row:first_row + m, :])
        else:
            reads = ((row & m) != 0) if fwd else ((row & m) == 0)
            if m == 1:
                w = jnp.where(reads, q * f, k)
            else:
                betas = []
                for u in range(C // SUBLANE):
                    rows = []
                    for p in range(SUBLANE // (2 * m)):
                        rows.append(b_ref[pl.ds(u * SUBLANE + p * 2 * m + (m - 1 if fwd else m), 1), :])
                    beta_u = rows[-1]
                    for p in range(len(rows) - 2, -1, -1):
                        beta_u = jnp.where(sub < (p + 1) * 2 * m, rows[p], beta_u)
                    betas.append(jnp.broadcast_to(beta_u, (SUBLANE, LANE)))
                beta = jnp.concatenate(betas, axis=0)
                w = jnp.where(reads, q, k) * jnp.exp2(-jnp.abs(b - beta))
            wb = w.astype(BF16)
            add_rows(0, _dot_nt(wb, wb) * mask_ref[mask_idx])
    attn_b = jnp.concatenate(attn, axis=0).astype(BF16)
    diag = jnp.sum(q * k, axis=-1, keepdims=True) * v
    return q0, stb, attn_b, vb, diag, st_new


def _hgrn_tables():
    C = HGRN_CHUNK
    row, col = np.indices((C, C))
    tri = np.stack([col <= row, col >= row])
    fwd, bwd = [], []
    for m in HGRN_LEVELS:
        same = (row // (2 * m)) == (col // (2 * m))
        fwd.append(same & ((row & m) != 0) & ((col & m) == 0))
        bwd.append(same & ((row & m) == 0) & ((col & m) != 0))
    return jnp.asarray(tri, BF16), jnp.asarray(np.stack(fwd + bwd), F32)


def _hgrn_unit(q_ref, v_ref, ff_ref, fb_ref, g_ref, lb_ref, gnw_ref, tri_ref, mask_ref, s0_ref, o_ref, s_out_ref,
               oacc_ref, b_ref, st_ref, *, T, fillers=()):
    C = HGRN_CHUNK
    U = HGRN_UNROLL
    nc = T // C
    assert nc % U == 0

    oacc_ref[...] = jnp.zeros_like(oacc_ref)
    for hh in range(HGRN_HG):
        for d in range(2):
            if s0_ref is not None:
                st_ref[2 * hh + d] = s0_ref[d, hh].T
            else:
                st_ref[2 * hh + d] = jnp.zeros((HGRN_DV, HGRN_DK), F32)

    q_scale = HGRN_DK ** -0.5

    def body(c, carry):
        chains = []
        for hh in range(HGRN_HG):
            lanes = slice(hh * LANE, (hh + 1) * LANE)
            for d in range(2):
                for u in range(U):
                    cc = c * U + u if d == 0 else nc - 1 - (c * U + u)
                    rows = pl.ds(pl.multiple_of(cc * C, C), C)
                    chains.append((hh, d, u, rows, lanes))
        stage_a = {}
        states = {}
        pending = []
        for step in range(U + 2):
            for fill in step_fillers[step::U + 2]:
                fill()
            for n, (hh, d, u, rows, lanes) in enumerate(chains):
                if u == step:
                    x_ref = ff_ref if d == 0 else fb_ref
                    stage_a[n] = _hgrn_chunk_a(x_ref[rows, lanes], lb_ref[d, hh], b_ref.at[n], tri_ref[d])
            ready, pending = pending, []
            for n, (hh, d, u, rows, lanes) in enumerate(chains):
                if u != step - 1:
                    continue
                b, f, k = stage_a[n]
                idx = 2 * hh + d
                st = st_ref[idx] if u == 0 else states[idx]
                q0, stb, attn_b, vb, diag, st = _hgrn_chunk_b(b, f, k, q_ref[rows, lanes], v_ref[rows, lanes], st,
                                                              b_ref.at[n], mask_ref, d)
                states[idx] = st
                if u == U - 1:
                    st_ref[idx] = st
                pending.append((rows, lanes, q0, stb, attn_b, vb, diag))
            for rows, lanes, q0, stb, attn_b, vb, diag in ready:
                oacc_ref[rows, lanes] += (_dot_nt(q0, stb) + _dot(attn_b, vb)) + diag
        assert not pending
        return carry

    if nc == U:
        step_fillers = tuple(fillers)
        body(0, 0)
    else:
        step_fillers = ()
        for fill in fillers:
            fill()
        lax.fori_loop(0, nc // U, body, 0)

    for hh in range(HGRN_HG):
        lanes = slice(hh * LANE, (hh + 1) * LANE)
        if s_out_ref is not None:
            for d in range(2):
                s_out_ref[d, hh] = st_ref[2 * hh + d].T
        o = oacc_ref[:, lanes]
        ms = jnp.mean(o * o, axis=-1, keepdims=True) * (q_scale * q_scale)
        gate = g_ref[:, lanes]
        y = o * (q_scale * lax.rsqrt(ms + EPS)) * gnw_ref[...] * (gate * _sigmoid(gate))
        o_ref[:, lanes] = y.astype(BF16)


def _hgrn_kernel(*refs, T, has_s0, emit_state, n_alias):
    n_in = 9 + int(has_s0)
    n_out = 2 if emit_state else 1
    assert len(refs) == n_in + n_alias + n_out + 3
    s0_ref = refs[9] if has_s0 else None
    outs = refs[n_in + n_alias:n_in + n_alias + n_out]
    _hgrn_unit(*refs[:9], s0_ref, outs[0], outs[1] if emit_state else None, *refs[-3:], T=T)


def _hgrn_io(st, proj, lbs, gnorm_w, tables, s0, layer, states_prev, unit):
    hg = HGRN_HG
    wide = hg * LANE
    T = st.seq
    C = HGRN_CHUNK

    def at(fn):
        return lambda *g: fn(*unit(*g))

    def col_spec(col0):
        return pl.BlockSpec((T, wide), at(lambda b, h: (b, col0 // hg + h)))

    tri, masks = tables
    in_specs = [col_spec(COL_HQ), col_spec(COL_HI), col_spec(COL_HFF), col_spec(COL_HFB), col_spec(COL_HG),
                pl.BlockSpec((2, None, hg, 1, LANE), at(lambda b, h: (0, layer, h, 0, 0))),
                pl.BlockSpec((None, 1, LANE), at(lambda b, h: (layer, 0, 0))),
                pl.BlockSpec(tri.shape, at(lambda b, h: (0, 0, 0))),
                pl.BlockSpec(masks.shape, at(lambda b, h: (0, 0, 0)))]
    args = [proj, proj, proj, proj, proj, lbs, gnorm_w, tri, masks]
    state_spec = pl.BlockSpec((None, None, 2, hg, HGRN_DK, HGRN_DV), at(lambda b, h: (b, layer, 0, h, 0, 0)))
    if st.latent:
        in_specs.append(state_spec)
        args.append(s0)
    out_specs = [pl.BlockSpec((T, wide), at(lambda b, h: (b, h)))]
    out_shape = [jax.ShapeDtypeStruct((_ntok(st), MIX_WIDTH), BF16)]
    aliases = {}
    if not st.latent:
        out_specs.append(state_spec)
        out_shape.append(jax.ShapeDtypeStruct((st.nseq, DEPTH, 2, HGRN_HEADS, HGRN_DK, HGRN_DV), F32))
        if states_prev is not None:
            aliases = {len(args): 1}
            in_specs.append(pl.BlockSpec(memory_space=pl.ANY))
            args.append(states_prev)
    scratch = [pltpu.VMEM((T, wide), F32), pltpu.VMEM((2 * hg * HGRN_UNROLL, C, LANE), F32),
               pltpu.VMEM((2 * hg, HGRN_DV, HGRN_DK), F32)]
    return in_specs, args, out_specs, out_shape, scratch, aliases


def _hgrn(st, proj, lbs, gnorm_w, tables, s0, layer, states_prev):
    in_specs, args, out_specs, out_shape, scratch, aliases = _hgrn_io(
        st, proj, lbs, gnorm_w, tables, s0, layer, states_prev, lambda b, h: (b, h))
    return pl.pallas_call(
        functools.partial(_hgrn_kernel, T=st.seq, has_s0=st.latent, emit_state=not st.latent, n_alias=len(aliases)),
        grid=(st.nseq, HGRN_HEADS // HGRN_HG),
        in_specs=in_specs,
        out_specs=out_specs,
        out_shape=out_shape,
        scratch_shapes=scratch,
        input_output_aliases=aliases,
        compiler_params=_cparams(("arbitrary", "arbitrary")),
        name="hgrn_" + st.name,
    )(*args)


def _inproj_hgrn_kernel(*refs, T, n_hgrn_in):
    x_ref, sh_ref, sc_ref, nw_ref, w_ref = refs[:5]
    hgrn_in = refs[5:5 + n_hgrn_in]
    o_ref, mix_ref, s_out_ref = refs[5 + n_hgrn_in:5 + n_hgrn_in + 3]
    h_ref = refs[-4]

    @pl.when(pl.program_id(1) == 0)
    def _():
        _norm_modulate(x_ref, h_ref, nw_ref, sh_ref, sc_ref)

    def panel(n):
        cols = slice(n * IN_PANEL, (n + 1) * IN_PANEL)

        def fill():
            o_ref[:, cols] = _dot(h_ref[...], w_ref[:, cols])

        return fill

    _hgrn_unit(*hgrn_in[:9], None, mix_ref, s_out_ref, *refs[-3:], T=T,
               fillers=[panel(n) for n in range(IN_TN // IN_PANEL)])


def _inproj_hgrn(st, x, mods4, layer, norm_w, w_bf16, st2, proj2, lbs, gnorm_w, tables, states_prev):
    assert st.latent and not st2.latent
    ntok = _ntok(st)
    ni, nj = ntok // IN_TM, IN_PROJ_WIDTH // IN_TN
    nh = HGRN_HEADS // HGRN_HG
    assert ni * nj == st2.nseq * nh

    def unit(i, j):
        s = i * nj + j
        return s // nh, s % nh

    h_in, h_args, h_out, h_shape, h_scratch, h_alias = _hgrn_io(st2, proj2, lbs, gnorm_w, tables, None, layer,
                                                                states_prev, unit)
    in_specs = [
        pl.BlockSpec((IN_TM, D_MODEL), lambda i, j: (i, 0)),
        _mod_spec(st, layer, 0, IN_TM, 2),
        _mod_spec(st, layer, 1, IN_TM, 2),
        _layer_row_spec(layer, D_MODEL, 2),
        pl.BlockSpec((None, D_MODEL, IN_TN), lambda i, j: (layer, 0, j)),
    ] + h_in
    args = [x, mods4, mods4, norm_w, w_bf16] + h_args
    return pl.pallas_call(
        functools.partial(_inproj_hgrn_kernel, T=st2.seq, n_hgrn_in=len(h_args)),
        grid=(ni, nj),
        in_specs=in_specs,
        out_specs=[pl.BlockSpec((IN_TM, IN_TN), lambda i, j: (i, j))] + h_out,
        out_shape=[jax.ShapeDtypeStruct((ntok, IN_PROJ_WIDTH), F32)] + h_shape,
        scratch_shapes=[pltpu.VMEM((IN_TM, D_MODEL), BF16)] + h_scratch,
        input_output_aliases={5 + a: 1 + o for a, o in h_alias.items()},
        compiler_params=_cparams(("arbitrary", "arbitrary")),
        name="inproj_" + st.name + "_hgrn_" + st2.name,
    )(*args)


def _conv_kernel(cb_ref, cc_ref, cx_ref, w_ref, mix_ref, o_ref, *, T):
    del mix_ref
    u = cc_ref[...] * cx_ref[...]
    row = lax.broadcasted_iota(jnp.int32, (T, CONV_WIDTH), 0)
    u_prev = jnp.where(row == 0, 0.0, pltpu.roll(u, 1, axis=0))
    u_next = jnp.where(row == T - 1, 0.0, pltpu.roll(u, T - 1, axis=0))
    y = u_prev * w_ref[0:1, :] + u * w_ref[1:2, :] + u_next * w_ref[2:3, :]
    o_ref[...] = (cb_ref[...] * y).astype(BF16)


def _conv(st, proj, conv_w, layer, mix):
    cw = CONV_WIDTH // LANE
    T = st.seq

    def col_spec(col0):
        return pl.BlockSpec((T, CONV_WIDTH), lambda b: (b, col0 // cw))

    return pl.pallas_call(
        functools.partial(_conv_kernel, T=T),
        grid=(st.nseq,),
        in_specs=[col_spec(COL_CB), col_spec(COL_CC), col_spec(COL_CX),
                  pl.BlockSpec((None, 3, CONV_WIDTH), lambda b: (layer, 0, 0)),
                  pl.BlockSpec(memory_space=pl.ANY)],
        out_specs=pl.BlockSpec((T, CONV_WIDTH), lambda b: (b, HGRN_WIDTH // CONV_WIDTH)),
        out_shape=jax.ShapeDtypeStruct(mix.shape, mix.dtype),
        input_output_aliases={4: 0},
        compiler_params=_cparams(("arbitrary",)),
        name="conv_" + st.name,
    )(proj, proj, proj, conv_w, mix)


MIX_COL_ATTN = (HGRN_WIDTH + CONV_WIDTH) // LANE


def _ctx_attn_kernel(q_ref, k_ref, v_ref, mix_ref, o_ref):
    del mix_ref
    scale = NA_HEAD_DIM ** -0.5
    heads = [slice(h * LANE, (h + 1) * LANE) for h in range(NA_HEADS)]
    scores = [_dot_nt(q_ref[:, hs].astype(BF16), k_ref[:, hs].astype(BF16)) * scale for hs in heads]
    probs = []
    for s in scores:
        p = jnp.exp(s - jnp.max(s, axis=-1, keepdims=True))
        probs.append((p.astype(BF16), jnp.sum(p, axis=-1, keepdims=True)))
    for hs, (p, l) in zip(heads, probs):
        o_ref[:, hs] = (_dot(p, v_ref[:, hs].astype(BF16)) / l).astype(BF16)


def _ctx_attn(proj, mix):
    def col_spec(col0):
        return pl.BlockSpec((SEQ, NA_WIDTH), lambda b: (b, col0 * LANE // NA_WIDTH))

    return pl.pallas_call(
        _ctx_attn_kernel,
        grid=(BATCH,),
        in_specs=[col_spec(COL_NQ), col_spec(COL_NK), col_spec(COL_NV), pl.BlockSpec(memory_space=pl.ANY)],
        out_specs=pl.BlockSpec((SEQ, NA_WIDTH), lambda b: (b, MIX_COL_ATTN * LANE // NA_WIDTH)),
        out_shape=jax.ShapeDtypeStruct(mix.shape, mix.dtype),
        input_output_aliases={3: 0},
        compiler_params=_cparams(("arbitrary",)),
        name="ctx_attn",
    )(proj, proj, proj, mix)


_KH = min(NA_KH, GRID_ROWS)
_ROW_START = [int(v) for v in np.clip(np.arange(GRID_ROWS) - _KH // 2, 0, GRID_ROWS - _KH)]
NLOC = _KH * GRID_W
RPB_ROWS = 2 * NA_KH - 1
RPB_COLS = 2 * NA_KW - 1


def _nat_kernel(q_ref, k_ref, v_ref, kc_ref, vc_ref, bias_ref, mix_ref, o_ref, sctx_ref, sloc_ref, pctx_ref,
                ploc_ref, den_ref):
    del mix_ref
    scale = NA_HEAD_DIM ** -0.5
    q_all = q_ref[...].astype(BF16)
    k_all = k_ref[...].astype(BF16)
    v_all = v_ref[...].astype(BF16)
    row_slices = [slice(r * GRID_W, (r + 1) * GRID_W) for r in range(GRID_ROWS)]
    bands = [slice(rs * GRID_W, rs * GRID_W + NLOC) for rs in _ROW_START]
    sctx_ref[...] = _dot_nt(q_all, kc_ref[...].astype(BF16)) * scale
    for r, (rows, band) in enumerate(zip(row_slices, bands)):
        sloc_ref[rows, :] = _dot_nt(q_all[rows], k_all[band]) * scale + bias_ref[r - _ROW_START[r]]
    for rows in row_slices:
        s_loc = sloc_ref[rows, :]
        s_ctx = sctx_ref[rows, :]
        m = jnp.maximum(jnp.max(s_loc, axis=-1, keepdims=True), jnp.max(s_ctx, axis=-1, keepdims=True))
        p_loc = jnp.exp(s_loc - m)
        p_ctx = jnp.exp(s_ctx - m)
        den_ref[rows, :] = jnp.sum(p_loc, axis=-1, keepdims=True) + jnp.sum(p_ctx, axis=-1, keepdims=True)
        ploc_ref[rows, :] = p_loc.astype(BF16)
        pctx_ref[rows, :] = p_ctx.astype(BF16)
    o_ctx = _dot(pctx_ref[...], vc_ref[...].astype(BF16))
    for rows, band in zip(row_slices, bands):
        o = (_dot(ploc_ref[rows, :], v_all[band]) + o_ctx[rows]) / den_ref[rows, :]
        o_ref[rows, :] = o.astype(BF16)


def _nat_bias_kernel(rpb_ref, o_ref):
    base = (pl.program_id(0) * NA_HEADS + pl.program_id(1)) * (RPB_ROWS * RPB_COLS)
    q = lax.broadcasted_iota(jnp.int32, (GRID_W, GRID_W), 0)
    k = lax.broadcasted_iota(jnp.int32, (GRID_W, GRID_W), 1)
    col_start = jnp.clip(q - NA_KW // 2, 0, GRID_W - NA_KW)
    in_window = (k >= col_start) & (k < col_start + NA_KW)
    dc = jnp.clip(k - q + NA_KW - 1, 0, RPB_COLS - 1)
    tabs = []
    for dr in range(RPB_ROWS):
        t = jnp.zeros((GRID_W, GRID_W), F32)
        for c in range(RPB_COLS):
            t = jnp.where(dc == c, rpb_ref[base + dr * RPB_COLS + c], t)
        tabs.append(jnp.where(in_window, t, NEG_BIG))
    for off in range(_KH):
        for p in range(_KH // 2):
            pair = [tabs[kr - off + NA_KH - 1] for kr in (2 * p, 2 * p + 1)]
            o_ref[off, :, 2 * p * GRID_W:(2 * p + 2) * GRID_W] = jnp.concatenate(pair, axis=1)


def _nat_bias(na_rpb):
    return pl.pallas_call(
        _nat_bias_kernel,
        grid=(DEPTH, NA_HEADS),
        in_specs=[pl.BlockSpec(memory_space=pltpu.SMEM)],
        out_specs=pl.BlockSpec((None, None, _KH, GRID_W, NLOC), lambda l, h: (l, h, 0, 0, 0)),
        out_shape=jax.ShapeDtypeStruct((DEPTH, NA_HEADS, _KH, GRID_W, NLOC), F32),
        compiler_params=_cparams(("arbitrary", "arbitrary")),
        name="nat_bias",
    )(na_rpb.reshape(-1))


def _nat(proj, cache_k, cache_v, bias, layer, mix):
    def col_spec(col0):
        return pl.BlockSpec((DEC_SEQ, LANE), lambda b, h: (b, col0 + h))

    cache_spec = pl.BlockSpec((None, None, PAST_LEN, LANE), lambda b, h: (b, layer, 0, h))
    return pl.pallas_call(
        _nat_kernel,
        grid=(DEC_BATCH, NA_HEADS),
        in_specs=[col_spec(COL_NQ), col_spec(COL_NK), col_spec(COL_NV), cache_spec, cache_spec,
                  pl.BlockSpec((None, None, _KH, GRID_W, NLOC), lambda b, h: (layer, h, 0, 0, 0)),
                  pl.BlockSpec(memory_space=pl.ANY)],
        out_specs=pl.BlockSpec((DEC_SEQ, LANE), lambda b, h: (b, MIX_COL_ATTN + h)),
        out_shape=jax.ShapeDtypeStruct(mix.shape, mix.dtype),
        scratch_shapes=[pltpu.VMEM((DEC_SEQ, PAST_LEN), F32), pltpu.VMEM((DEC_SEQ, NLOC), F32),
                        pltpu.VMEM((DEC_SEQ, PAST_LEN), BF16), pltpu.VMEM((DEC_SEQ, NLOC), BF16),
                        pltpu.VMEM((DEC_SEQ, 1), F32)],
        input_output_aliases={6: 0},
        compiler_params=_cparams(("arbitrary", "arbitrary")),
        name="nat",
    )(proj, proj, proj, cache_k, cache_v, bias, mix)


def kernel(x_prompt, x_sample, cache_na_k, cache_na_v, state_hgrn, c, c_ctx, w_ada, b_ada, norm_mix_w, w_in,
           hgrn_lb_raw, hgrn_gnorm_w, conv_w, na_rpb, w_out, norm_ffn_w, w_ffn_gate, w_ffn_up, w_ffn_down,
           final_norm_w):
    xs = {LATENT: x_sample.reshape(_ntok(LATENT), D_MODEL), CONTEXT: x_prompt.reshape(_ntok(CONTEXT), D_MODEL)}
    cvecs = jnp.concatenate([c, c_ctx[None, :], jnp.zeros((MOD_ROWS - DEC_BATCH - 1, D_MODEL), F32)], axis=0)
    mods4 = _adaln(cvecs, w_ada, b_ada).reshape(DEPTH, MOD_ROWS, 1, 6 * D_MODEL)

    p_lb = jax.nn.softmax(hgrn_lb_raw.astype(F32), axis=1)
    cp = jnp.cumsum(p_lb, axis=1)
    lbs = (cp - cp[:, :1]).reshape(2, DEPTH, HGRN_HEADS, 1, LANE)
    gnorm_w = hgrn_gnorm_w.reshape(DEPTH, 1, LANE)
    norm_mix = norm_mix_w.reshape(DEPTH, 1, D_MODEL)
    norm_ffn = norm_ffn_w.reshape(DEPTH, 1, D_MODEL)

    cache_k4 = cache_na_k.reshape(DEC_BATCH, DEPTH, PAST_LEN, NA_WIDTH)
    cache_v4 = cache_na_v.reshape(DEC_BATCH, DEPTH, PAST_LEN, NA_WIDTH)

    w_in_b, w_out_b = w_in.astype(BF16), w_out.astype(BF16)

    nat_bias = _nat_bias(na_rpb)
    tables = _hgrn_tables()

    kv = None
    states = None
    for l in range(DEPTH):
        last = l == DEPTH - 1
        proj_c, *kv = _inproj(CONTEXT, xs[CONTEXT], mods4, l, norm_mix, w_in_b, kv)
        proj_l, mix_c, states = _inproj_hgrn(LATENT, xs[LATENT], mods4, l, norm_mix, w_in_b, CONTEXT, proj_c, lbs,
                                             gnorm_w, tables, states)

        mix_l = _hgrn(LATENT, proj_l, lbs, gnorm_w, tables, state_hgrn, l, None)[0]
        mix_l = _conv(LATENT, proj_l, conv_w, l, mix_l)
        mix_l = _nat(proj_l, cache_k4, cache_v4, nat_bias, l, mix_l)
        x1 = _outproj(LATENT, xs[LATENT], mix_l, mods4, l, w_out_b)
        y, *ffn_w = _ffn(LATENT, x1, mods4, l, norm_ffn, final_norm_w, (w_ffn_gate, w_ffn_up, w_ffn_down), last,
                         cast_weights=True, nblocks=1)
        xs[LATENT] = _ffn(LATENT, x1, mods4, l, norm_ffn, final_norm_w, ffn_w, last, block0=1,
                          nblocks=_ntok(LATENT) // FFN_TM - 1, out_prev=y)[0]

        mix_c = _conv(CONTEXT, proj_c, conv_w, l, mix_c)
        mix_c = _ctx_attn(proj_c, mix_c)
        x1 = _outproj(CONTEXT, xs[CONTEXT], mix_c, mods4, l, w_out_b)
        xs[CONTEXT] = _ffn(CONTEXT, x1, mods4, l, norm_ffn, final_norm_w, ffn_w, last)[0]

    y_sample = xs[LATENT].reshape(DEC_BATCH, DEC_SEQ, D_MODEL)
    y_prompt = xs[CONTEXT].reshape(BATCH, SEQ, D_MODEL)
    new_k = kv[0].reshape(BATCH, DEPTH, SEQ, NA_HEADS, NA_HEAD_DIM)
    new_v = kv[1].reshape(BATCH, DEPTH, SEQ, NA_HEADS, NA_HEAD_DIM)
    return (y_prompt, y_sample, new_k, new_v, states)
```

```python
import collections
import functools

import numpy as np
import jax
import jax.numpy as jnp
from jax import lax
from jax.experimental import pallas as pl
from jax.experimental.pallas import tpu as pltpu

F32 = jnp.float32
BF16 = jnp.bfloat16

D_MODEL = 2048
BATCH = 16
SEQ = 256
DEPTH = 2
DEC_BATCH = 8
DEC_SEQ = 1024
PAST_LEN = 512
GRID_W = 64
HGRN_HEADS = 8
HGRN_DK = 128
HGRN_DV = 128
HGRN_WIDTH = HGRN_HEADS * HGRN_DV
CONV_WIDTH = 512
NA_HEADS = 4
NA_HEAD_DIM = 128
NA_WIDTH = NA_HEADS * NA_HEAD_DIM
NA_KH = 8
NA_KW = 16
MIX_WIDTH = HGRN_WIDTH + CONV_WIDTH + NA_WIDTH
IN_PROJ_WIDTH = 5 * HGRN_WIDTH + 3 * CONV_WIDTH + 3 * NA_WIDTH
FFN_HIDDEN = ((8 * D_MODEL + 3 * 256 - 1) // (3 * 256)) * 256
EPS = 1e-6

MOD_ROWS = 16
CTX_ROW = DEC_BATCH
LANE = 128
SUBLANE = 8
GRID_ROWS = DEC_SEQ // GRID_W

COL_HQ = 0
COL_HI = HGRN_WIDTH // LANE
COL_HFF = 2 * HGRN_WIDTH // LANE
COL_HFB = 3 * HGRN_WIDTH // LANE
COL_HG = 4 * HGRN_WIDTH // LANE
COL_CB = 5 * HGRN_WIDTH // LANE
COL_CC = COL_CB + CONV_WIDTH // LANE
COL_CX = COL_CC + CONV_WIDTH // LANE
COL_NQ = COL_CX + CONV_WIDTH // LANE
COL_NK = COL_NQ + NA_WIDTH // LANE
COL_NV = COL_NK + NA_WIDTH // LANE

NEG_BIG = -1e30
VMEM_LIMIT = 60 * 1024 * 1024

Stream = collections.namedtuple("Stream", ["name", "seq", "nseq", "latent"])
LATENT = Stream("lat", DEC_SEQ, DEC_BATCH, True)
CONTEXT = Stream("ctx", SEQ, BATCH, False)


def _ntok(st):
    return st.seq * st.nseq


def _cparams(sem):
    return pltpu.CompilerParams(dimension_semantics=sem, vmem_limit_bytes=VMEM_LIMIT)


def _dot(a, b):
    return jnp.dot(a, b, preferred_element_type=F32)


def _dot_nt(a, b):
    return lax.dot_general(a, b, (((1,), (1,)), ((), ())), preferred_element_type=F32)


def _dot_tn(a, b):
    return lax.dot_general(a, b, (((0,), (0,)), ((), ())), preferred_element_type=F32)


def _sigmoid(x):
    return 1.0 / (1.0 + jnp.exp(-x))


ADA_TN = 1024


def _adaln_kernel(c_ref, w_ref, b_ref, o_ref):
    c = c_ref[...]
    s = (c * _sigmoid(c)).astype(BF16)
    o_ref[...] = _dot(s, w_ref[...].astype(BF16)) + b_ref[...]


def _adaln(cvecs, w_ada, b_ada):
    n = 6 * D_MODEL
    return pl.pallas_call(
        _adaln_kernel,
        grid=(DEPTH, n // ADA_TN),
        in_specs=[
            pl.BlockSpec((MOD_ROWS, D_MODEL), lambda l, j: (0, 0)),
            pl.BlockSpec((None, D_MODEL, ADA_TN), lambda l, j: (l, 0, j)),
            pl.BlockSpec((None, 1, ADA_TN), lambda l, j: (l, 0, j)),
        ],
        out_specs=pl.BlockSpec((None, MOD_ROWS, ADA_TN), lambda l, j: (l, 0, j)),
        out_shape=jax.ShapeDtypeStruct((DEPTH, MOD_ROWS, n), F32),
        compiler_params=_cparams(("arbitrary", "arbitrary")),
        name="adaln",
    )(cvecs, w_ada, b_ada.reshape(DEPTH, 1, n))


IN_TM = 1024
IN_TN = 1024
IN_PANEL = 256
TM = 512
FFN_TM = 1024
FFN_TF = 512


def _mod_spec(st, layer, chunk, tm, ngrid, block0=0):
    def row(i):
        return ((block0 + i) * tm) // DEC_SEQ if st.latent else CTX_ROW

    if ngrid == 1:
        return pl.BlockSpec((None, None, 1, D_MODEL), lambda i: (layer, row(i), 0, chunk))
    return pl.BlockSpec((None, None, 1, D_MODEL), lambda i, j: (layer, row(i), 0, chunk))


PROLOGUE_ROWS = 128


def _norm_modulate(x_ref, h_ref, nw_ref, sh_ref, sc_ref):
    gain = nw_ref[...] * (1.0 + sc_ref[...])
    shift = sh_ref[...]

    def chunk(r, carry):
        rows = pl.ds(pl.multiple_of(r * PROLOGUE_ROWS, PROLOGUE_ROWS), PROLOGUE_ROWS)
        x = x_ref[rows, :]
        ms = jnp.mean(x * x, axis=-1, keepdims=True)
        h_ref[rows, :] = (x * lax.rsqrt(ms + EPS) * gain + shift).astype(BF16)
        return carry

    lax.fori_loop(0, x_ref.shape[0] // PROLOGUE_ROWS, chunk, 0)


IN_CAST_TN = 512


def _inproj_kernel(*refs, emit_kv, cast_weights, n_alias):
    x_ref, sh_ref, sc_ref, nw_ref, w_ref = refs[:5]
    outs = refs[5 + n_alias:-1]
    o_ref = outs[0]
    h_ref = refs[-1]
    j = pl.program_id(1)
    nj = pl.num_programs(1)

    @pl.when(j == 0)
    def _():
        _norm_modulate(x_ref, h_ref, nw_ref, sh_ref, sc_ref)

    if cast_weights:
        w = w_ref[...].astype(BF16)
        outs[-1][...] = w
    else:
        w = w_ref[...]
    o_ref[...] = _dot(h_ref[...], w)

    if emit_kv:
        k_ref, v_ref = outs[1], outs[2]
        tn = o_ref.shape[1]
        if tn == 2 * NA_WIDTH:
            @pl.when(j == nj - 1)
            def _():
                k_ref[...] = o_ref[:, :NA_WIDTH].reshape(k_ref.shape)
                v_ref[...] = o_ref[:, NA_WIDTH:].reshape(v_ref.shape)
        else:
            assert tn == NA_WIDTH

            @pl.when(j == nj - 2)
            def _():
                k_ref[...] = o_ref[...].reshape(k_ref.shape)

            @pl.when(j == nj - 1)
            def _():
                v_ref[...] = o_ref[...].reshape(v_ref.shape)


def _layer_row_spec(layer, width, ngrid):
    if ngrid == 1:
        return pl.BlockSpec((None, 1, width), lambda i: (layer, 0, 0))
    return pl.BlockSpec((None, 1, width), lambda i, j: (layer, 0, 0))


def _inproj(st, x, mods4, layer, norm_w, w, kv_prev, *, cast_weights=False, block0=0, nblocks=None, proj_prev=None):
    emit_kv = not st.latent
    tn = IN_CAST_TN if cast_weights else IN_TN
    assert COL_NK * LANE == IN_PROJ_WIDTH - 2 * NA_WIDTH and tn in (NA_WIDTH, 2 * NA_WIDTH)
    ntok = _ntok(st)
    nblocks = ntok // IN_TM if nblocks is None else nblocks
    if cast_weights:
        w_spec = pl.BlockSpec((None, D_MODEL, tn), lambda i, j: (layer, 0, j))
    else:
        w_spec = pl.BlockSpec((D_MODEL, tn), lambda i, j: (0, j))
    in_specs = [
        pl.BlockSpec((IN_TM, D_MODEL), lambda i, j: (block0 + i, 0)),
        _mod_spec(st, layer, 0, IN_TM, 2, block0),
        _mod_spec(st, layer, 1, IN_TM, 2, block0),
        _layer_row_spec(layer, D_MODEL, 2),
        w_spec,
    ]
    args = [x, mods4, mods4, norm_w, w]
    out_specs = [pl.BlockSpec((IN_TM, tn), lambda i, j: (block0 + i, j))]
    out_shape = [jax.ShapeDtypeStruct((ntok, IN_PROJ_WIDTH), F32)]
    aliases = {}
    if proj_prev is not None:
        aliases[len(args)] = 0
        in_specs.append(pl.BlockSpec(memory_space=pl.ANY))
        args.append(proj_prev)
    if emit_kv:
        nb = IN_TM // st.seq
        kv_spec = pl.BlockSpec((nb, None, st.seq, NA_WIDTH), lambda i, j: (block0 + i, layer, 0, 0))
        out_specs += [kv_spec, kv_spec]
        out_shape += [jax.ShapeDtypeStruct((st.nseq, DEPTH, st.seq, NA_WIDTH), F32)] * 2
        if kv_prev is not None:
            aliases.update({len(args): 1, len(args) + 1: 2})
            in_specs += [pl.BlockSpec(memory_space=pl.ANY)] * 2
            args += list(kv_prev)
    if cast_weights:
        out_specs.append(pl.BlockSpec((D_MODEL, tn), lambda i, j: (0, j)))
        out_shape.append(jax.ShapeDtypeStruct((D_MODEL, IN_PROJ_WIDTH), BF16))
    return pl.pallas_call(
        functools.partial(_inproj_kernel, emit_kv=emit_kv, cast_weights=cast_weights, n_alias=len(aliases)),
        grid=(nblocks, IN_PROJ_WIDTH // tn),
        in_specs=in_specs,
        out_specs=out_specs,
        out_shape=out_shape,
        scratch_shapes=[pltpu.VMEM((IN_TM, D_MODEL), BF16)],
        input_output_aliases=aliases,
        compiler_params=_cparams(("arbitrary", "arbitrary")),
        name="inproj_" + st.name + ("_cast" if cast_weights else ""),
    )(*args)


def _outproj_kernel(x_ref, m_ref, g_ref, w_ref, o_ref):
    o_ref[...] = x_ref[...] + g_ref[...] * _dot(m_ref[...], w_ref[...])


def _outproj(st, x, mix, mods4, layer, w_bf16):
    ntok = _ntok(st)
    return pl.pallas_call(
        _outproj_kernel,
        grid=(ntok // TM,),
        in_specs=[
            pl.BlockSpec((TM, D_MODEL), lambda i: (i, 0)),
            pl.BlockSpec((TM, MIX_WIDTH), lambda i: (i, 0)),
            _mod_spec(st, layer, 2, TM, 1),
            pl.BlockSpec((None, MIX_WIDTH, D_MODEL), lambda i: (layer, 0, 0)),
        ],
        out_specs=pl.BlockSpec((TM, D_MODEL), lambda i: (i, 0)),
        out_shape=jax.ShapeDtypeStruct((ntok, D_MODEL), F32),
        compiler_params=_cparams(("arbitrary",)),
        name="outproj_" + st.name,
    )(x, mix, mods4, w_bf16)


FFN_CAST_TF = 256
FFN_PANEL = 256


def _ffn_prologue(x_ref, h_ref, o_ref, nw_ref, sh_ref, sc_ref):
    _norm_modulate(x_ref, h_ref, nw_ref, sh_ref, sc_ref)
    o_ref[...] = jnp.zeros_like(o_ref)


def _ffn_fillers(h_ref, wg, wu, wd, o_ref):
    hidden = []

    def gate_up(n):
        def fill():
            h = h_ref[...]
            a = _dot(h, wg[:, n:n + FFN_PANEL])
            u = _dot(h, wu[:, n:n + FFN_PANEL])
            hidden.append((a * _sigmoid(a) * u).astype(BF16))
        return fill

    def down(n):
        def fill():
            t = hidden[0] if len(hidden) == 1 else jnp.concatenate(hidden, axis=1)
            o_ref[:, n:n + FFN_TF] += _dot(t, wd[:, n:n + FFN_TF])
        return fill

    return [gate_up(n) for n in range(0, wg.shape[1], FFN_PANEL)] + [down(n) for n in range(0, D_MODEL, FFN_TF)]


def _ffn_epilogue(x_ref, o_ref, g_ref, fw_ref, final_norm):
    gate = g_ref[...]

    def chunk(r, carry):
        rows = pl.ds(pl.multiple_of(r * PROLOGUE_ROWS, PROLOGUE_ROWS), PROLOGUE_ROWS)
        y = x_ref[rows, :] + gate * o_ref[rows, :]
        if final_norm:
            ms = jnp.mean(y * y, axis=-1, keepdims=True)
            y = y * lax.rsqrt(ms + EPS) * fw_ref[...]
        o_ref[rows, :] = y
        return carry

    lax.fori_loop(0, x_ref.shape[0] // PROLOGUE_ROWS, chunk, 0)


def _ffn_kernel(*refs, final_norm, cast_weights, n_alias):
    x_ref, sh_ref, sc_ref, g_ref, nw_ref, fw_ref, wg_ref, wu_ref, wd_ref = refs[:9]
    outs = refs[9 + n_alias:-1]
    o_ref = outs[0]
    h_ref = refs[-1]
    j = pl.program_id(1)

    @pl.when(j == 0)
    def _():
        _ffn_prologue(x_ref, h_ref, o_ref, nw_ref, sh_ref, sc_ref)

    if cast_weights:
        wg, wu, wd = (w[...].astype(BF16) for w in (wg_ref, wu_ref, wd_ref))
        for w, w_out_ref in zip((wg, wu, wd), outs[1:]):
            w_out_ref[...] = w
    else:
        wg, wu, wd = wg_ref[...], wu_ref[...], wd_ref[...]
    for fill in _ffn_fillers(h_ref, wg, wu, wd, o_ref):
        fill()

    @pl.when(j == pl.num_programs(1) - 1)
    def _():
        _ffn_epilogue(x_ref, o_ref, g_ref, fw_ref, final_norm)


def _ffn(st, x, mods4, layer, norm_w, final_w, weights, final_norm, *, cast_weights=False, block0=0, nblocks=None,
         out_prev=None, tm=FFN_TM):
    ntok = _ntok(st)
    TM = tm
    tf = FFN_CAST_TF if cast_weights else FFN_TF
    nblocks = ntok // TM if nblocks is None else nblocks
    if cast_weights:
        w_specs = [pl.BlockSpec((None, D_MODEL, tf), lambda i, j: (layer, 0, j)),
                   pl.BlockSpec((None, D_MODEL, tf), lambda i, j: (layer, 0, j)),
                   pl.BlockSpec((None, tf, D_MODEL), lambda i, j: (layer, j, 0))]
    else:
        w_specs = [pl.BlockSpec((D_MODEL, tf), lambda i, j: (0, j)),
                   pl.BlockSpec((D_MODEL, tf), lambda i, j: (0, j)),
                   pl.BlockSpec((tf, D_MODEL), lambda i, j: (j, 0))]
    in_specs = [
        pl.BlockSpec((TM, D_MODEL), lambda i, j: (block0 + i, 0)),
        _mod_spec(st, layer, 3, TM, 2, block0),
        _mod_spec(st, layer, 4, TM, 2, block0),
        _mod_spec(st, layer, 5, TM, 2, block0),
        _layer_row_spec(layer, D_MODEL, 2),
        pl.BlockSpec((1, D_MODEL), lambda i, j: (0, 0)),
    ] + w_specs
    args = [x, mods4, mods4, mods4, norm_w, final_w.reshape(1, D_MODEL)] + list(weights)
    out_specs = [pl.BlockSpec((TM, D_MODEL), lambda i, j: (block0 + i, 0))]
    out_shape = [jax.ShapeDtypeStruct((ntok, D_MODEL), F32)]
    if cast_weights:
        out_specs += [pl.BlockSpec((D_MODEL, tf), lambda i, j: (0, j)),
                      pl.BlockSpec((D_MODEL, tf), lambda i, j: (0, j)),
                      pl.BlockSpec((tf, D_MODEL), lambda i, j: (j, 0))]
        out_shape += [jax.ShapeDtypeStruct((D_MODEL, FFN_HIDDEN), BF16)] * 2
        out_shape += [jax.ShapeDtypeStruct((FFN_HIDDEN, D_MODEL), BF16)]
    aliases = {}
    if out_prev is not None:
        aliases = {len(args): 0}
        in_specs.append(pl.BlockSpec(memory_space=pl.ANY))
        args.append(out_prev)
    return pl.pallas_call(
        functools.partial(_ffn_kernel, final_norm=final_norm, cast_weights=cast_weights, n_alias=len(aliases)),
        grid=(nblocks, FFN_HIDDEN // tf),
        in_specs=in_specs,
        out_specs=out_specs,
        out_shape=out_shape,
        scratch_shapes=[pltpu.VMEM((TM, D_MODEL), BF16)],
        input_output_aliases=aliases,
        compiler_params=_cparams(("arbitrary", "arbitrary")),
        name="ffn_" + st.name + ("_cast" if cast_weights else ""),
    )(*args)


HGRN_CHUNK = 64
HGRN_HG = 2
HGRN_UNROLL = 4
HGRN_LEVELS = [HGRN_CHUNK >> (i + 1) for i in range(HGRN_CHUNK.bit_length() - 1)]


LOG2E = 1.4426950408889634


def _hgrn_gates(x, lb):
    e = jnp.exp(-jnp.abs(x))
    r = 1.0 / (1.0 + e)
    er = e * r
    pos = x >= 0.0
    sig = jnp.where(pos, r, er)
    nsig = jnp.where(pos, er, r)
    ls2 = jnp.minimum(x, 0.0) * LOG2E + jnp.log2(r)
    f = lb + (1.0 - lb) * sig
    log2_f = jnp.where(lb > 0.0, jnp.log2(f), ls2)
    return log2_f, f, (1.0 - lb) * nsig


def _hgrn_chunk_a(x, lb, b_ref, tri):
    g, f, k = _hgrn_gates(x, lb)
    g_hi = g.astype(BF16)
    g_lo = (g - g_hi.astype(F32)).astype(BF16)
    b = _dot(tri, jnp.concatenate([g_hi, g_lo], axis=0))
    b_ref[...] = b
    return b, f, k


def _hgrn_chunk_b(b, f, k, q, v, st, b_ref, mask_ref, dirn):
    C = HGRN_CHUNK
    fwd = dirn == 0
    b_end = b_ref[pl.ds(C - 1 if fwd else 0, 1), :]
    vb = v.astype(BF16)
    q0 = (q * jnp.exp2(b)).astype(BF16)
    stb = st.astype(BF16)
    kd = (k * jnp.exp2(b_end - b)).astype(BF16)
    st_new = jnp.exp2(b_end) * st + _dot_tn(vb, kd)

    row = lax.broadcasted_iota(jnp.int32, (C, LANE), 0)
    sub = lax.broadcasted_iota(jnp.int32, (SUBLANE, LANE), 0)
    attn = [None] * (C // SUBLANE)

    def add_rows(first_row, term):
        for t in range(term.shape[0] // SUBLANE):
            blk = first_row // SUBLANE + t
            piece = term[t * SUBLANE:(t + 1) * SUBLANE]
            attn[blk] = piece if attn[blk] is None else attn[blk] + piece

    for lvl, m in enumerate(HGRN_LEVELS):
        mask_idx = dirn * len(HGRN_LEVELS) + lvl
        if m >= SUBLANE:
            parts, readers = [], []
            for p in range(C // (2 * m)):
                lo = p * 2 * m
                left = slice(lo, lo + m)
                right = slice(lo + m, lo + 2 * m)
                beta = b_ref[pl.ds(lo + (m - 1 if fwd else m), 1), :]
                if fwd:
                    parts += [k[left] * jnp.exp2(beta - b[left]), q[right] * jnp.exp2(b[right] - beta)]
                    readers.append((lo + m, parts[-1]))
                else:
                    parts += [q[left] * jnp.exp2(b[left] - beta), k[right] * jnp.exp2(beta - b[right])]
                    readers.append((lo, parts[-2]))
            wb = jnp.concatenate(parts, axis=0).astype(BF16)
            wq = jnp.concatenate([r for _, r in readers], axis=0).astype(BF16)
            term = _dot_nt(wq, wb)
            for n, (first_row, _) in enumerate(readers):
                add_rows(first_row, term[n * m:(n + 1) * m] * mask_ref[mask_idx, first_row:first_row + m, :])
        else:
            reads = ((row & m) != 0) if fwd else ((row & m) == 0)
            if m == 1:
                w = jnp.where(reads, q * f, k)
            else:
                betas = []
                for u in range(C // SUBLANE):
                    rows = []
                    for p in range(SUBLANE // (2 * m)):
                        rows.append(b_ref[pl.ds(u * SUBLANE + p * 2 * m + (m - 1 if fwd else m), 1), :])
                    beta_u = rows[-1]
                    for p in range(len(rows) - 2, -1, -1):
                        beta_u = jnp.where(sub < (p + 1) * 2 * m, rows[p], beta_u)
                    betas.append(jnp.broadcast_to(beta_u, (SUBLANE, LANE)))
                beta = jnp.concatenate(betas, axis=0)
                w = jnp.where(reads, q, k) * jnp.exp2(-jnp.abs(b - beta))
            wb = w.astype(BF16)
            add_rows(0, _dot_nt(wb, wb) * mask_ref[mask_idx])
    attn_b = jnp.concatenate(attn, axis=0).astype(BF16)
    diag = jnp.sum(q * k, axis=-1, keepdims=True) * v
    return q0, stb, attn_b, vb, diag, st_new


def _hgrn_tables():
    C = HGRN_CHUNK
    row, col = np.indices((C, C))
    tri = np.stack([col <= row, col >= row])
    tri = np.concatenate([tri, tri], axis=2)
    fwd, bwd = [], []
    for m in HGRN_LEVELS:
        same = (row // (2 * m)) == (col // (2 * m))
        fwd.append(same & ((row & m) != 0) & ((col & m) == 0))
        bwd.append(same & ((row & m) == 0) & ((col & m) != 0))
    return jnp.asarray(tri, BF16), jnp.asarray(np.stack(fwd + bwd), F32)


def _hgrn_unit(q_ref, v_ref, ff_ref, fb_ref, g_ref, lb_ref, gnw_ref, tri_ref, mask_ref, s0_ref, o_ref, s_out_ref,
               oacc_ref, b_ref, st_ref, *, T, fillers=()):
    C = HGRN_CHUNK
    U = HGRN_UNROLL
    nc = T // C
    assert nc % U == 0 and len(fillers) <= U + 2
    q_scale = HGRN_DK ** -0.5

    oacc_ref[...] = jnp.zeros_like(oacc_ref)
    for hh in range(HGRN_HG):
        for d in range(2):
            if s0_ref is not None:
                st_ref[2 * hh + d] = s0_ref[d, hh].T
            else:
                st_ref[2 * hh + d] = jnp.zeros((HGRN_DV, HGRN_DK), F32)

    def run_trip(c, stage_fillers):
        chains = []
        for hh in range(HGRN_HG):
            lanes = slice(hh * LANE, (hh + 1) * LANE)
            for d in range(2):
                for u in range(U):
                    cc = c * U + u if d == 0 else nc - 1 - (c * U + u)
                    rows = pl.ds(pl.multiple_of(cc * C, C), C)
                    chains.append((hh, d, u, rows, lanes))
        stage_a = {}
        states = {}
        pending = []
        for step in range(U + 2):
            for fill in stage_fillers[step:step + 1]:
                fill()
            for n, (hh, d, u, rows, lanes) in enumerate(chains):
                if u == step:
                    x_ref = ff_ref if d == 0 else fb_ref
                    stage_a[n] = _hgrn_chunk_a(x_ref[rows, lanes], lb_ref[d, hh], b_ref.at[n], tri_ref[d])
            ready, pending = pending, []
            for n, (hh, d, u, rows, lanes) in enumerate(chains):
                if u != step - 1:
                    continue
                b, f, k = stage_a[n]
                idx = 2 * hh + d
                st = st_ref[idx] if u == 0 else states[idx]
                q0, stb, attn_b, vb, diag, st = _hgrn_chunk_b(b, f, k, q_ref[rows, lanes], v_ref[rows, lanes], st,
                                                              b_ref.at[n], mask_ref, d)
                states[idx] = st
                if u == U - 1:
                    st_ref[idx] = st
                pending.append((rows, lanes, q0, stb, attn_b, vb, diag))
            for rows, lanes, q0, stb, attn_b, vb, diag in ready:
                oacc_ref[rows, lanes] += (_dot_nt(q0, stb) + _dot(attn_b, vb)) + diag
        assert not pending

    if nc == U:
        run_trip(0, list(fillers))
    else:
        for fill in fillers:
            fill()

        def body(c, carry):
            run_trip(c, [])
            return carry

        lax.fori_loop(0, nc // U, body, 0)

    for hh in range(HGRN_HG):
        lanes = slice(hh * LANE, (hh + 1) * LANE)
        if s_out_ref is not None:
            for d in range(2):
                s_out_ref[d, hh] = st_ref[2 * hh + d].T
        o = oacc_ref[:, lanes]
        ms = jnp.mean(o * o, axis=-1, keepdims=True) * (q_scale * q_scale)
        gate = g_ref[:, lanes]
        y = o * (q_scale * lax.rsqrt(ms + EPS)) * gnw_ref[...] * (gate * _sigmoid(gate))
        o_ref[:, lanes] = y.astype(BF16)


def _hgrn_kernel(*refs, T, has_s0, emit_state, n_alias):
    n_in = 9 + int(has_s0)
    n_out = 2 if emit_state else 1
    assert len(refs) == n_in + n_alias + n_out + 3
    s0_ref = refs[9] if has_s0 else None
    outs = refs[n_in + n_alias:n_in + n_alias + n_out]
    _hgrn_unit(*refs[:9], s0_ref, outs[0], outs[1] if emit_state else None, *refs[-3:], T=T)


def _hgrn_io(st, proj, lbs, gnorm_w, tables, s0, layer, states_prev, unit):
    hg = HGRN_HG
    wide = hg * LANE
    T = st.seq
    C = HGRN_CHUNK

    def at(fn):
        return lambda *g: fn(*unit(*g))

    def col_spec(col0):
        return pl.BlockSpec((T, wide), at(lambda b, h: (b, col0 // hg + h)))

    tri, masks = tables
    in_specs = [col_spec(COL_HQ), col_spec(COL_HI), col_spec(COL_HFF), col_spec(COL_HFB), col_spec(COL_HG),
                pl.BlockSpec((2, None, hg, 1, LANE), at(lambda b, h: (0, layer, h, 0, 0))),
                pl.BlockSpec((None, 1, LANE), at(lambda b, h: (layer, 0, 0))),
                pl.BlockSpec(tri.shape, at(lambda b, h: (0, 0, 0))),
                pl.BlockSpec(masks.shape, at(lambda b, h: (0, 0, 0)))]
    args = [proj, proj, proj, proj, proj, lbs, gnorm_w, tri, masks]
    state_spec = pl.BlockSpec((None, None, 2, hg, HGRN_DK, HGRN_DV), at(lambda b, h: (b, layer, 0, h, 0, 0)))
    if st.latent:
        in_specs.append(state_spec)
        args.append(s0)
    out_specs = [pl.BlockSpec((T, wide), at(lambda b, h: (b, h)))]
    out_shape = [jax.ShapeDtypeStruct((_ntok(st), MIX_WIDTH), BF16)]
    aliases = {}
    if not st.latent:
        out_specs.append(state_spec)
        out_shape.append(jax.ShapeDtypeStruct((st.nseq, DEPTH, 2, HGRN_HEADS, HGRN_DK, HGRN_DV), F32))
        if states_prev is not None:
            aliases = {len(args): 1}
            in_specs.append(pl.BlockSpec(memory_space=pl.ANY))
            args.append(states_prev)
    scratch = [pltpu.VMEM((T, wide), F32), pltpu.VMEM((2 * hg * HGRN_UNROLL, C, LANE), F32),
               pltpu.VMEM((2 * hg, HGRN_DV, HGRN_DK), F32)]
    return in_specs, args, out_specs, out_shape, scratch, aliases


def _hgrn(st, proj, lbs, gnorm_w, tables, s0, layer, states_prev):
    in_specs, args, out_specs, out_shape, scratch, aliases = _hgrn_io(
        st, proj, lbs, gnorm_w, tables, s0, layer, states_prev, lambda b, h: (b, h))
    return pl.pallas_call(
        functools.partial(_hgrn_kernel, T=st.seq, has_s0=st.latent, emit_state=not st.latent, n_alias=len(aliases)),
        grid=(st.nseq, HGRN_HEADS // HGRN_HG),
        in_specs=in_specs,
        out_specs=out_specs,
        out_shape=out_shape,
        scratch_shapes=scratch,
        input_output_aliases=aliases,
        compiler_params=_cparams(("arbitrary", "arbitrary")),
        name="hgrn_" + st.name,
    )(*args)


def _inproj_hgrn_kernel(*refs, T, n_hgrn_in):
    x_ref, sh_ref, sc_ref, nw_ref, w_ref = refs[:5]
    hgrn_in = refs[5:5 + n_hgrn_in]
    o_ref, mix_ref, s_out_ref = refs[5 + n_hgrn_in:5 + n_hgrn_in + 3]
    h_ref = refs[-4]

    @pl.when(pl.program_id(1) == 0)
    def _():
        _norm_modulate(x_ref, h_ref, nw_ref, sh_ref, sc_ref)

    def panel(n):
        cols = slice(n * IN_PANEL, (n + 1) * IN_PANEL)

        def fill():
            o_ref[:, cols] = _dot(h_ref[...], w_ref[:, cols])

        return fill

    _hgrn_unit(*hgrn_in[:9], None, mix_ref, s_out_ref, *refs[-3:], T=T,
               fillers=[panel(n) for n in range(IN_TN // IN_PANEL)])


def _inproj_hgrn(st, x, mods4, layer, norm_w, w_bf16, st2, proj2, lbs, gnorm_w, tables, states_prev):
    assert st.latent and not st2.latent
    ntok = _ntok(st)
    ni, nj = ntok // IN_TM, IN_PROJ_WIDTH // IN_TN
    nh = HGRN_HEADS // HGRN_HG
    assert ni * nj == st2.nseq * nh

    def unit(i, j):
        s = i * nj + j
        return s // nh, s % nh

    h_in, h_args, h_out, h_shape, h_scratch, h_alias = _hgrn_io(st2, proj2, lbs, gnorm_w, tables, None, layer,
                                                                states_prev, unit)
    in_specs = [
        pl.BlockSpec((IN_TM, D_MODEL), lambda i, j: (i, 0)),
        _mod_spec(st, layer, 0, IN_TM, 2),
        _mod_spec(st, layer, 1, IN_TM, 2),
        _layer_row_spec(layer, D_MODEL, 2),
        pl.BlockSpec((D_MODEL, IN_TN), lambda i, j: (0, j)),
    ] + h_in
    args = [x, mods4, mods4, norm_w, w_bf16] + h_args
    return pl.pallas_call(
        functools.partial(_inproj_hgrn_kernel, T=st2.seq, n_hgrn_in=len(h_args)),
        grid=(ni, nj),
        in_specs=in_specs,
        out_specs=[pl.BlockSpec((IN_TM, IN_TN), lambda i, j: (i, j))] + h_out,
        out_shape=[jax.ShapeDtypeStruct((ntok, IN_PROJ_WIDTH), F32)] + h_shape,
        scratch_shapes=[pltpu.VMEM((IN_TM, D_MODEL), BF16)] + h_scratch,
        input_output_aliases={5 + a: 1 + o for a, o in h_alias.items()},
        compiler_params=_cparams(("arbitrary", "arbitrary")),
        name="inproj_" + st.name + "_hgrn_" + st2.name,
    )(*args)


def _conv_kernel(cb_ref, cc_ref, cx_ref, w_ref, mix_ref, o_ref, *, T):
    del mix_ref
    u = cc_ref[...] * cx_ref[...]
    row = lax.broadcasted_iota(jnp.int32, (T, CONV_WIDTH), 0)
    u_prev = jnp.where(row == 0, 0.0, pltpu.roll(u, 1, axis=0))
    u_next = jnp.where(row == T - 1, 0.0, pltpu.roll(u, T - 1, axis=0))
    y = u_prev * w_ref[0:1, :] + u * w_ref[1:2, :] + u_next * w_ref[2:3, :]
    o_ref[...] = (cb_ref[...] * y).astype(BF16)


def _conv(st, proj, conv_w, layer, mix):
    cw = CONV_WIDTH // LANE
    T = st.seq

    def col_spec(col0):
        return pl.BlockSpec((T, CONV_WIDTH), lambda b: (b, col0 // cw))

    return pl.pallas_call(
        functools.partial(_conv_kernel, T=T),
        grid=(st.nseq,),
        in_specs=[col_spec(COL_CB), col_spec(COL_CC), col_spec(COL_CX),
                  pl.BlockSpec((None, 3, CONV_WIDTH), lambda b: (layer, 0, 0)),
                  pl.BlockSpec(memory_space=pl.ANY)],
        out_specs=pl.BlockSpec((T, CONV_WIDTH), lambda b: (b, HGRN_WIDTH // CONV_WIDTH)),
        out_shape=jax.ShapeDtypeStruct(mix.shape, mix.dtype),
        input_output_aliases={4: 0},
        compiler_params=_cparams(("arbitrary",)),
        name="conv_" + st.name,
    )(proj, proj, proj, conv_w, mix)


MIX_COL_ATTN = (HGRN_WIDTH + CONV_WIDTH) // LANE


def _ctx_attn_kernel(q_ref, k_ref, v_ref, mix_ref, o_ref):
    del mix_ref
    scale = NA_HEAD_DIM ** -0.5
    heads = [slice(h * LANE, (h + 1) * LANE) for h in range(NA_HEADS)]
    scores = [_dot_nt(q_ref[:, hs].astype(BF16), k_ref[:, hs].astype(BF16)) * scale for hs in heads]
    probs = []
    for s in scores:
        p = jnp.exp(s - jnp.max(s, axis=-1, keepdims=True))
        probs.append((p.astype(BF16), jnp.sum(p, axis=-1, keepdims=True)))
    for hs, (p, l) in zip(heads, probs):
        o_ref[:, hs] = (_dot(p, v_ref[:, hs].astype(BF16)) / l).astype(BF16)


def _ctx_attn(proj, mix):
    def col_spec(col0):
        return pl.BlockSpec((SEQ, NA_WIDTH), lambda b: (b, col0 * LANE // NA_WIDTH))

    return pl.pallas_call(
        _ctx_attn_kernel,
        grid=(BATCH,),
        in_specs=[col_spec(COL_NQ), col_spec(COL_NK), col_spec(COL_NV), pl.BlockSpec(memory_space=pl.ANY)],
        out_specs=pl.BlockSpec((SEQ, NA_WIDTH), lambda b: (b, MIX_COL_ATTN * LANE // NA_WIDTH)),
        out_shape=jax.ShapeDtypeStruct(mix.shape, mix.dtype),
        input_output_aliases={3: 0},
        compiler_params=_cparams(("arbitrary",)),
        name="ctx_attn",
    )(proj, proj, proj, mix)


_KH = min(NA_KH, GRID_ROWS)
_ROW_START = [int(v) for v in np.clip(np.arange(GRID_ROWS) - _KH // 2, 0, GRID_ROWS - _KH)]
NLOC = _KH * GRID_W
RPB_ROWS = 2 * NA_KH - 1
RPB_COLS = 2 * NA_KW - 1


def _nat_kernel(q_ref, k_ref, v_ref, kc_ref, vc_ref, bias_ref, mix_ref, o_ref, sctx_ref, sloc_ref, pctx_ref,
                ploc_ref, den_ref):
    del mix_ref
    scale = NA_HEAD_DIM ** -0.5
    q_all = q_ref[...].astype(BF16)
    k_all = k_ref[...].astype(BF16)
    v_all = v_ref[...].astype(BF16)
    row_slices = [slice(r * GRID_W, (r + 1) * GRID_W) for r in range(GRID_ROWS)]
    bands = [slice(rs * GRID_W, rs * GRID_W + NLOC) for rs in _ROW_START]
    sctx_ref[...] = _dot_nt(q_all, kc_ref[...].astype(BF16)) * scale
    for r, (rows, band) in enumerate(zip(row_slices, bands)):
        sloc_ref[rows, :] = _dot_nt(q_all[rows], k_all[band]) * scale + bias_ref[r - _ROW_START[r]]
    for rows in row_slices:
        s_loc = sloc_ref[rows, :]
        s_ctx = sctx_ref[rows, :]
        m = jnp.maximum(jnp.max(s_loc, axis=-1, keepdims=True), jnp.max(s_ctx, axis=-1, keepdims=True))
        p_loc = jnp.exp(s_loc - m)
        p_ctx = jnp.exp(s_ctx - m)
        den_ref[rows, :] = jnp.sum(p_loc, axis=-1, keepdims=True) + jnp.sum(p_ctx, axis=-1, keepdims=True)
        ploc_ref[rows, :] = p_loc.astype(BF16)
        pctx_ref[rows, :] = p_ctx.astype(BF16)
    o_ctx = _dot(pctx_ref[...], vc_ref[...].astype(BF16))
    for rows, band in zip(row_slices, bands):
        o = (_dot(ploc_ref[rows, :], v_all[band]) + o_ctx[rows]) / den_ref[rows, :]
        o_ref[rows, :] = o.astype(BF16)


def _nat_bias_kernel(rpb_ref, o_ref):
    base = (pl.program_id(0) * NA_HEADS + pl.program_id(1)) * (RPB_ROWS * RPB_COLS)
    q = lax.broadcasted_iota(jnp.int32, (GRID_W, GRID_W), 0)
    k = lax.broadcasted_iota(jnp.int32, (GRID_W, GRID_W), 1)
    col_start = jnp.clip(q - NA_KW // 2, 0, GRID_W - NA_KW)
    in_window = (k >= col_start) & (k < col_start + NA_KW)
    dc = jnp.clip(k - q + NA_KW - 1, 0, RPB_COLS - 1)
    tabs = []
    for dr in range(RPB_ROWS):
        t = jnp.zeros((GRID_W, GRID_W), F32)
        for c in range(RPB_COLS):
            t = jnp.where(dc == c, rpb_ref[base + dr * RPB_COLS + c], t)
        tabs.append(jnp.where(in_window, t, NEG_BIG))
    for off in range(_KH):
        for p in range(_KH // 2):
            pair = [tabs[kr - off + NA_KH - 1] for kr in (2 * p, 2 * p + 1)]
            o_ref[off, :, 2 * p * GRID_W:(2 * p + 2) * GRID_W] = jnp.concatenate(pair, axis=1)


def _nat_bias(na_rpb):
    return pl.pallas_call(
        _nat_bias_kernel,
        grid=(DEPTH, NA_HEADS),
        in_specs=[pl.BlockSpec(memory_space=pltpu.SMEM)],
        out_specs=pl.BlockSpec((None, None, _KH, GRID_W, NLOC), lambda l, h: (l, h, 0, 0, 0)),
        out_shape=jax.ShapeDtypeStruct((DEPTH, NA_HEADS, _KH, GRID_W, NLOC), F32),
        compiler_params=_cparams(("arbitrary", "arbitrary")),
        name="nat_bias",
    )(na_rpb.reshape(-1))


def _nat(proj, cache_k, cache_v, bias, layer, mix):
    def col_spec(col0):
        return pl.BlockSpec((DEC_SEQ, LANE), lambda b, h: (b, col0 + h))

    cache_spec = pl.BlockSpec((None, None, PAST_LEN, LANE), lambda b, h: (b, layer, 0, h))
    return pl.pallas_call(
        _nat_kernel,
        grid=(DEC_BATCH, NA_HEADS),
        in_specs=[col_spec(COL_NQ), col_spec(COL_NK), col_spec(COL_NV), cache_spec, cache_spec,
                  pl.BlockSpec((None, None, _KH, GRID_W, NLOC), lambda b, h: (layer, h, 0, 0, 0)),
                  pl.BlockSpec(memory_space=pl.ANY)],
        out_specs=pl.BlockSpec((DEC_SEQ, LANE), lambda b, h: (b, MIX_COL_ATTN + h)),
        out_shape=jax.ShapeDtypeStruct(mix.shape, mix.dtype),
        scratch_shapes=[pltpu.VMEM((DEC_SEQ, PAST_LEN), F32), pltpu.VMEM((DEC_SEQ, NLOC), F32),
                        pltpu.VMEM((DEC_SEQ, PAST_LEN), BF16), pltpu.VMEM((DEC_SEQ, NLOC), BF16),
                        pltpu.VMEM((DEC_SEQ, 1), F32)],
        input_output_aliases={6: 0},
        compiler_params=_cparams(("arbitrary", "arbitrary")),
        name="nat",
    )(proj, proj, proj, cache_k, cache_v, bias, mix)


def kernel(x_prompt, x_sample, cache_na_k, cache_na_v, state_hgrn, c, c_ctx, w_ada, b_ada, norm_mix_w, w_in,
           hgrn_lb_raw, hgrn_gnorm_w, conv_w, na_rpb, w_out, norm_ffn_w, w_ffn_gate, w_ffn_up, w_ffn_down,
           final_norm_w):
    xs = {LATENT: x_sample.reshape(_ntok(LATENT), D_MODEL), CONTEXT: x_prompt.reshape(_ntok(CONTEXT), D_MODEL)}
    cvecs = jnp.concatenate([c, c_ctx[None, :], jnp.zeros((MOD_ROWS - DEC_BATCH - 1, D_MODEL), F32)], axis=0)
    mods4 = _adaln(cvecs, w_ada, b_ada).reshape(DEPTH, MOD_ROWS, 1, 6 * D_MODEL)

    p_lb = jax.nn.softmax(hgrn_lb_raw.astype(F32), axis=1)
    cp = jnp.cumsum(p_lb, axis=1)
    lbs = (cp - cp[:, :1]).reshape(2, DEPTH, HGRN_HEADS, 1, LANE)
    gnorm_w = hgrn_gnorm_w.reshape(DEPTH, 1, LANE)
    norm_mix = norm_mix_w.reshape(DEPTH, 1, D_MODEL)
    norm_ffn = norm_ffn_w.reshape(DEPTH, 1, D_MODEL)

    cache_k4 = cache_na_k.reshape(DEC_BATCH, DEPTH, PAST_LEN, NA_WIDTH)
    cache_v4 = cache_na_v.reshape(DEC_BATCH, DEPTH, PAST_LEN, NA_WIDTH)

    w_out_b = w_out.astype(BF16)

    nat_bias = _nat_bias(na_rpb)
    tables = _hgrn_tables()

    kv = None
    states = None
    for l in range(DEPTH):
        last = l == DEPTH - 1
        proj_c, k_c, v_c, w_in_l = _inproj(CONTEXT, xs[CONTEXT], mods4, l, norm_mix, w_in, kv, cast_weights=True,
                                           nblocks=1)
        proj_c, *kv = _inproj(CONTEXT, xs[CONTEXT], mods4, l, norm_mix, w_in_l, (k_c, v_c), block0=1,
                              nblocks=_ntok(CONTEXT) // IN_TM - 1, proj_prev=proj_c)
        proj_l, mix_c, states = _inproj_hgrn(LATENT, xs[LATENT], mods4, l, norm_mix, w_in_l, CONTEXT, proj_c, lbs,
                                             gnorm_w, tables, states)

        mix_l = _hgrn(LATENT, proj_l, lbs, gnorm_w, tables, state_hgrn, l, None)[0]
        mix_l = _conv(LATENT, proj_l, conv_w, l, mix_l)
        mix_l = _nat(proj_l, cache_k4, cache_v4, nat_bias, l, mix_l)
        x1 = _outproj(LATENT, xs[LATENT], mix_l, mods4, l, w_out_b)
        y, *ffn_w = _ffn(LATENT, x1, mods4, l, norm_ffn, final_norm_w, (w_ffn_gate, w_ffn_up, w_ffn_down), last,
                         cast_weights=True, nblocks=1)
        xs[LATENT] = _ffn(LATENT, x1, mods4, l, norm_ffn, final_norm_w, ffn_w, last, block0=1,
                          nblocks=_ntok(LATENT) // FFN_TM - 1, out_prev=y)[0]

        mix_c = _conv(CONTEXT, proj_c, conv_w, l, mix_c)
        mix_c = _ctx_attn(proj_c, mix_c)
        x1 = _outproj(CONTEXT, xs[CONTEXT], mix_c, mods4, l, w_out_b)
        xs[CONTEXT] = _ffn(CONTEXT, x1, mods4, l, norm_ffn, final_norm_w, ffn_w, last)[0]

    y_sample = xs[LATENT].reshape(DEC_BATCH, DEC_SEQ, D_MODEL)
    y_prompt = xs[CONTEXT].reshape(BATCH, SEQ, D_MODEL)
    new_k = kv[0].reshape(BATCH, DEPTH, SEQ, NA_HEADS, NA_HEAD_DIM)
    new_v = kv[1].reshape(BATCH, DEPTH, SEQ, NA_HEADS, NA_HEAD_DIM)
    return (y_prompt, y_sample, new_k, new_v, states)
```

```python
import collections
import functools

import numpy as np
import jax
import jax.numpy as jnp
from jax import lax
from jax.experimental import pallas as pl
from jax.experimental.pallas import tpu as pltpu

F32 = jnp.float32
BF16 = jnp.bfloat16

D_MODEL = 2048
BATCH = 16
SEQ = 256
DEPTH = 2
DEC_BATCH = 8
DEC_SEQ = 1024
PAST_LEN = 512
GRID_W = 64
HGRN_HEADS = 8
HGRN_DK = 128
HGRN_DV = 128
HGRN_WIDTH = HGRN_HEADS * HGRN_DV
CONV_WIDTH = 512
NA_HEADS = 4
NA_HEAD_DIM = 128
NA_WIDTH = NA_HEADS * NA_HEAD_DIM
NA_KH = 8
NA_KW = 16
MIX_WIDTH = HGRN_WIDTH + CONV_WIDTH + NA_WIDTH
IN_PROJ_WIDTH = 5 * HGRN_WIDTH + 3 * CONV_WIDTH + 3 * NA_WIDTH
FFN_HIDDEN = ((8 * D_MODEL + 3 * 256 - 1) // (3 * 256)) * 256
EPS = 1e-6

MOD_ROWS = 16
CTX_ROW = DEC_BATCH
LANE = 128
SUBLANE = 8
GRID_ROWS = DEC_SEQ // GRID_W

COL_HQ = 0
COL_HI = HGRN_WIDTH // LANE
COL_HFF = 2 * HGRN_WIDTH // LANE
COL_HFB = 3 * HGRN_WIDTH // LANE
COL_HG = 4 * HGRN_WIDTH // LANE
COL_CB = 5 * HGRN_WIDTH // LANE
COL_CC = COL_CB + CONV_WIDTH // LANE
COL_CX = COL_CC + CONV_WIDTH // LANE
COL_NQ = COL_CX + CONV_WIDTH // LANE
COL_NK = COL_NQ + NA_WIDTH // LANE
COL_NV = COL_NK + NA_WIDTH // LANE

NEG_BIG = -1e30
VMEM_LIMIT = 60 * 1024 * 1024

Stream = collections.namedtuple("Stream", ["name", "seq", "nseq", "latent"])
LATENT = Stream("lat", DEC_SEQ, DEC_BATCH, True)
CONTEXT = Stream("ctx", SEQ, BATCH, False)


def _ntok(st):
    return st.seq * st.nseq


def _cparams(sem):
    return pltpu.CompilerParams(dimension_semantics=sem, vmem_limit_bytes=VMEM_LIMIT)


def _dot(a, b):
    return jnp.dot(a, b, preferred_element_type=F32)


def _dot_nt(a, b):
    return lax.dot_general(a, b, (((1,), (1,)), ((), ())), preferred_element_type=F32)


def _dot_tn(a, b):
    return lax.dot_general(a, b, (((0,), (0,)), ((), ())), preferred_element_type=F32)


def _sigmoid(x):
    return 1.0 / (1.0 + jnp.exp(-x))


ADA_TN = 1024


def _adaln_kernel(c_ref, w_ref, b_ref, o_ref):
    c = c_ref[...]
    s = (c * _sigmoid(c)).astype(BF16)
    o_ref[...] = _dot(s, w_ref[...].astype(BF16)) + b_ref[...]


def _adaln(cvecs, w_ada, b_ada):
    n = 6 * D_MODEL
    return pl.pallas_call(
        _adaln_kernel,
        grid=(DEPTH, n // ADA_TN),
        in_specs=[
            pl.BlockSpec((MOD_ROWS, D_MODEL), lambda l, j: (0, 0)),
            pl.BlockSpec((None, D_MODEL, ADA_TN), lambda l, j: (l, 0, j)),
            pl.BlockSpec((None, 1, ADA_TN), lambda l, j: (l, 0, j)),
        ],
        out_specs=pl.BlockSpec((None, MOD_ROWS, ADA_TN), lambda l, j: (l, 0, j)),
        out_shape=jax.ShapeDtypeStruct((DEPTH, MOD_ROWS, n), F32),
        compiler_params=_cparams(("arbitrary", "arbitrary")),
        name="adaln",
    )(cvecs, w_ada, b_ada.reshape(DEPTH, 1, n))


IN_TM = 1024
IN_TN = 1024
IN_PANEL = 256
TM = 1024
FFN_TM = 1024
FFN_TF = 512


def _mod_spec(st, layer, chunk, tm, ngrid, block0=0):
    def row(i):
        return ((block0 + i) * tm) // DEC_SEQ if st.latent else CTX_ROW

    if ngrid == 1:
        return pl.BlockSpec((None, None, 1, D_MODEL), lambda i: (layer, row(i), 0, chunk))
    return pl.BlockSpec((None, None, 1, D_MODEL), lambda i, j: (layer, row(i), 0, chunk))


PROLOGUE_ROWS = 128


def _norm_modulate(x_ref, h_ref, nw_ref, sh_ref, sc_ref):
    gain = nw_ref[...] * (1.0 + sc_ref[...])
    shift = sh_ref[...]

    def chunk(r, carry):
        rows = pl.ds(pl.multiple_of(r * PROLOGUE_ROWS, PROLOGUE_ROWS), PROLOGUE_ROWS)
        x = x_ref[rows, :]
        ms = jnp.mean(x * x, axis=-1, keepdims=True)
        h_ref[rows, :] = (x * lax.rsqrt(ms + EPS) * gain + shift).astype(BF16)
        return carry

    lax.fori_loop(0, x_ref.shape[0] // PROLOGUE_ROWS, chunk, 0)


IN_CAST_TN = 512


def _inproj_kernel(*refs, emit_kv, cast_weights, n_alias):
    x_ref, sh_ref, sc_ref, nw_ref, w_ref = refs[:5]
    outs = refs[5 + n_alias:-1]
    o_ref = outs[0]
    h_ref = refs[-1]
    j = pl.program_id(1)
    nj = pl.num_programs(1)

    @pl.when(j == 0)
    def _():
        _norm_modulate(x_ref, h_ref, nw_ref, sh_ref, sc_ref)

    if cast_weights:
        w = w_ref[...].astype(BF16)
        outs[-1][...] = w
    else:
        w = w_ref[...]
    o_ref[...] = _dot(h_ref[...], w)

    if emit_kv:
        k_ref, v_ref = outs[1], outs[2]
        tn = o_ref.shape[1]
        if tn == 2 * NA_WIDTH:
            @pl.when(j == nj - 1)
            def _():
                k_ref[...] = o_ref[:, :NA_WIDTH].reshape(k_ref.shape)
                v_ref[...] = o_ref[:, NA_WIDTH:].reshape(v_ref.shape)
        else:
            assert tn == NA_WIDTH

            @pl.when(j == nj - 2)
            def _():
                k_ref[...] = o_ref[...].reshape(k_ref.shape)

            @pl.when(j == nj - 1)
            def _():
                v_ref[...] = o_ref[...].reshape(v_ref.shape)


def _layer_row_spec(layer, width, ngrid):
    if ngrid == 1:
        return pl.BlockSpec((None, 1, width), lambda i: (layer, 0, 0))
    return pl.BlockSpec((None, 1, width), lambda i, j: (layer, 0, 0))


def _inproj(st, x, mods4, layer, norm_w, w, kv_prev, *, cast_weights=False, block0=0, nblocks=None, proj_prev=None):
    emit_kv = not st.latent
    tn = IN_CAST_TN if cast_weights else IN_TN
    assert COL_NK * LANE == IN_PROJ_WIDTH - 2 * NA_WIDTH and tn in (NA_WIDTH, 2 * NA_WIDTH)
    ntok = _ntok(st)
    nblocks = ntok // IN_TM if nblocks is None else nblocks
    if cast_weights:
        w_spec = pl.BlockSpec((None, D_MODEL, tn), lambda i, j: (layer, 0, j))
    else:
        w_spec = pl.BlockSpec((D_MODEL, tn), lambda i, j: (0, j))
    in_specs = [
        pl.BlockSpec((IN_TM, D_MODEL), lambda i, j: (block0 + i, 0)),
        _mod_spec(st, layer, 0, IN_TM, 2, block0),
        _mod_spec(st, layer, 1, IN_TM, 2, block0),
        _layer_row_spec(layer, D_MODEL, 2),
        w_spec,
    ]
    args = [x, mods4, mods4, norm_w, w]
    out_specs = [pl.BlockSpec((IN_TM, tn), lambda i, j: (block0 + i, j))]
    out_shape = [jax.ShapeDtypeStruct((ntok, IN_PROJ_WIDTH), F32)]
    aliases = {}
    if proj_prev is not None:
        aliases[len(args)] = 0
        in_specs.append(pl.BlockSpec(memory_space=pl.ANY))
        args.append(proj_prev)
    if emit_kv:
        nb = IN_TM // st.seq
        kv_spec = pl.BlockSpec((nb, None, st.seq, NA_WIDTH), lambda i, j: (block0 + i, layer, 0, 0))
        out_specs += [kv_spec, kv_spec]
        out_shape += [jax.ShapeDtypeStruct((st.nseq, DEPTH, st.seq, NA_WIDTH), F32)] * 2
        if kv_prev is not None:
            aliases.update({len(args): 1, len(args) + 1: 2})
            in_specs += [pl.BlockSpec(memory_space=pl.ANY)] * 2
            args += list(kv_prev)
    if cast_weights:
        out_specs.append(pl.BlockSpec((D_MODEL, tn), lambda i, j: (0, j)))
        out_shape.append(jax.ShapeDtypeStruct((D_MODEL, IN_PROJ_WIDTH), BF16))
    return pl.pallas_call(
        functools.partial(_inproj_kernel, emit_kv=emit_kv, cast_weights=cast_weights, n_alias=len(aliases)),
        grid=(nblocks, IN_PROJ_WIDTH // tn),
        in_specs=in_specs,
        out_specs=out_specs,
        out_shape=out_shape,
        scratch_shapes=[pltpu.VMEM((IN_TM, D_MODEL), BF16)],
        input_output_aliases=aliases,
        compiler_params=_cparams(("arbitrary", "arbitrary")),
        name="inproj_" + st.name + ("_cast" if cast_weights else ""),
    )(*args)


def _outproj_kernel(x_ref, m_ref, g_ref, w_ref, o_ref):
    o_ref[...] = x_ref[...] + g_ref[...] * _dot(m_ref[...], w_ref[...])


def _outproj(st, x, mix, mods4, layer, w_bf16):
    ntok = _ntok(st)
    return pl.pallas_call(
        _outproj_kernel,
        grid=(ntok // TM,),
        in_specs=[
            pl.BlockSpec((TM, D_MODEL), lambda i: (i, 0)),
            pl.BlockSpec((TM, MIX_WIDTH), lambda i: (i, 0)),
            _mod_spec(st, layer, 2, TM, 1),
            pl.BlockSpec((None, MIX_WIDTH, D_MODEL), lambda i: (layer, 0, 0), pipeline_mode=pl.Buffered(1)),
        ],
        out_specs=pl.BlockSpec((TM, D_MODEL), lambda i: (i, 0)),
        out_shape=jax.ShapeDtypeStruct((ntok, D_MODEL), F32),
        compiler_params=_cparams(("arbitrary",)),
        name="outproj_" + st.name,
    )(x, mix, mods4, w_bf16)


FFN_CAST_TF = 256
FFN_PANEL = 256


def _ffn_prologue(x_ref, h_ref, o_ref, nw_ref, sh_ref, sc_ref):
    _norm_modulate(x_ref, h_ref, nw_ref, sh_ref, sc_ref)
    o_ref[...] = jnp.zeros_like(o_ref)


def _ffn_fillers(h_ref, wg, wu, wd, o_ref):
    hidden = []

    def gate_up(n):
        def fill():
            h = h_ref[...]
            a = _dot(h, wg[:, n:n + FFN_PANEL])
            u = _dot(h, wu[:, n:n + FFN_PANEL])
            hidden.append((a * _sigmoid(a) * u).astype(BF16))
        return fill

    def down(n):
        def fill():
            t = hidden[0] if len(hidden) == 1 else jnp.concatenate(hidden, axis=1)
            o_ref[:, n:n + FFN_TF] += _dot(t, wd[:, n:n + FFN_TF])
        return fill

    return [gate_up(n) for n in range(0, wg.shape[1], FFN_PANEL)] + [down(n) for n in range(0, D_MODEL, FFN_TF)]


def _ffn_epilogue(x_ref, o_ref, g_ref, fw_ref, final_norm):
    gate = g_ref[...]

    def chunk(r, carry):
        rows = pl.ds(pl.multiple_of(r * PROLOGUE_ROWS, PROLOGUE_ROWS), PROLOGUE_ROWS)
        y = x_ref[rows, :] + gate * o_ref[rows, :]
        if final_norm:
            ms = jnp.mean(y * y, axis=-1, keepdims=True)
            y = y * lax.rsqrt(ms + EPS) * fw_ref[...]
        o_ref[rows, :] = y
        return carry

    lax.fori_loop(0, x_ref.shape[0] // PROLOGUE_ROWS, chunk, 0)


def _ffn_kernel(*refs, final_norm, cast_weights, n_alias):
    x_ref, sh_ref, sc_ref, g_ref, nw_ref, fw_ref, wg_ref, wu_ref, wd_ref = refs[:9]
    outs = refs[9 + n_alias:-1]
    o_ref = outs[0]
    h_ref = refs[-1]
    j = pl.program_id(1)

    @pl.when(j == 0)
    def _():
        _ffn_prologue(x_ref, h_ref, o_ref, nw_ref, sh_ref, sc_ref)

    if cast_weights:
        wg, wu, wd = (w[...].astype(BF16) for w in (wg_ref, wu_ref, wd_ref))
        for w, w_out_ref in zip((wg, wu, wd), outs[1:]):
            w_out_ref[...] = w
    else:
        wg, wu, wd = wg_ref[...], wu_ref[...], wd_ref[...]
    for fill in _ffn_fillers(h_ref, wg, wu, wd, o_ref):
        fill()

    @pl.when(j == pl.num_programs(1) - 1)
    def _():
        _ffn_epilogue(x_ref, o_ref, g_ref, fw_ref, final_norm)


def _ffn(st, x, mods4, layer, norm_w, final_w, weights, final_norm, *, cast_weights=False, block0=0, nblocks=None,
         out_prev=None, tm=FFN_TM):
    ntok = _ntok(st)
    TM = tm
    tf = FFN_CAST_TF if cast_weights else FFN_TF
    nblocks = ntok // TM if nblocks is None else nblocks
    if cast_weights:
        w_specs = [pl.BlockSpec((None, D_MODEL, tf), lambda i, j: (layer, 0, j)),
                   pl.BlockSpec((None, D_MODEL, tf), lambda i, j: (layer, 0, j)),
                   pl.BlockSpec((None, tf, D_MODEL), lambda i, j: (layer, j, 0))]
    else:
        w_specs = [pl.BlockSpec((D_MODEL, tf), lambda i, j: (0, j)),
                   pl.BlockSpec((D_MODEL, tf), lambda i, j: (0, j)),
                   pl.BlockSpec((tf, D_MODEL), lambda i, j: (j, 0))]
    in_specs = [
        pl.BlockSpec((TM, D_MODEL), lambda i, j: (block0 + i, 0)),
        _mod_spec(st, layer, 3, TM, 2, block0),
        _mod_spec(st, layer, 4, TM, 2, block0),
        _mod_spec(st, layer, 5, TM, 2, block0),
        _layer_row_spec(layer, D_MODEL, 2),
        pl.BlockSpec((1, D_MODEL), lambda i, j: (0, 0)),
    ] + w_specs
    args = [x, mods4, mods4, mods4, norm_w, final_w.reshape(1, D_MODEL)] + list(weights)
    out_specs = [pl.BlockSpec((TM, D_MODEL), lambda i, j: (block0 + i, 0))]
    out_shape = [jax.ShapeDtypeStruct((ntok, D_MODEL), F32)]
    if cast_weights:
        out_specs += [pl.BlockSpec((D_MODEL, tf), lambda i, j: (0, j)),
                      pl.BlockSpec((D_MODEL, tf), lambda i, j: (0, j)),
                      pl.BlockSpec((tf, D_MODEL), lambda i, j: (j, 0))]
        out_shape += [jax.ShapeDtypeStruct((D_MODEL, FFN_HIDDEN), BF16)] * 2
        out_shape += [jax.ShapeDtypeStruct((FFN_HIDDEN, D_MODEL), BF16)]
    aliases = {}
    if out_prev is not None:
        aliases = {len(args): 0}
        in_specs.append(pl.BlockSpec(memory_space=pl.ANY))
        args.append(out_prev)
    return pl.pallas_call(
        functools.partial(_ffn_kernel, final_norm=final_norm, cast_weights=cast_weights, n_alias=len(aliases)),
        grid=(nblocks, FFN_HIDDEN // tf),
        in_specs=in_specs,
        out_specs=out_specs,
        out_shape=out_shape,
        scratch_shapes=[pltpu.VMEM((TM, D_MODEL), BF16)],
        input_output_aliases=aliases,
        compiler_params=_cparams(("arbitrary", "arbitrary")),
        name="ffn_" + st.name + ("_cast" if cast_weights else ""),
    )(*args)


HGRN_CHUNK = 64
HGRN_HG = 2
HGRN_UNROLL = 4
HGRN_LEVELS = [HGRN_CHUNK >> (i + 1) for i in range(HGRN_CHUNK.bit_length() - 1)]


LOG2E = 1.4426950408889634


def _hgrn_gates(x, lb):
    e = jnp.exp(-jnp.abs(x))
    r = 1.0 / (1.0 + e)
    er = e * r
    pos = x >= 0.0
    sig = jnp.where(pos, r, er)
    nsig = jnp.where(pos, er, r)
    ls2 = jnp.minimum(x, 0.0) * LOG2E + jnp.log2(r)
    f = lb + (1.0 - lb) * sig
    log2_f = jnp.where(lb > 0.0, jnp.log2(f), ls2)
    return log2_f, f, (1.0 - lb) * nsig


def _hgrn_chunk_a(x, lb, b_ref, tri):
    g, f, k = _hgrn_gates(x, lb)
    g_hi = g.astype(BF16)
    g_lo = (g - g_hi.astype(F32)).astype(BF16)
    b = _dot(tri, jnp.concatenate([g_hi, g_lo], axis=0))
    b_ref[...] = b
    return b, f, k


def _hgrn_chunk_b(b, f, k, q, v, st, b_ref, mask_ref, dirn):
    C = HGRN_CHUNK
    fwd = dirn == 0
    b_end = b_ref[pl.ds(C - 1 if fwd else 0, 1), :]
    vb = v.astype(BF16)
    q0 = (q * jnp.exp2(b)).astype(BF16)
    stb = st.astype(BF16)
    kd = (k * jnp.exp2(b_end - b)).astype(BF16)
    st_new = jnp.exp2(b_end) * st + _dot_tn(vb, kd)

    row = lax.broadcasted_iota(jnp.int32, (C, LANE), 0)
    sub = lax.broadcasted_iota(jnp.int32, (SUBLANE, LANE), 0)
    attn = [None] * (C // SUBLANE)

    def add_rows(first_row, term):
        for t in range(term.shape[0] // SUBLANE):
            blk = first_row // SUBLANE + t
            piece = term[t * SUBLANE:(t + 1) * SUBLANE]
            attn[blk] = piece if attn[blk] is None else attn[blk] + piece

    for lvl, m in enumerate(HGRN_LEVELS):
        mask_idx = dirn * len(HGRN_LEVELS) + lvl
        if m >= SUBLANE:
            parts, readers = [], []
            for p in range(C // (2 * m)):
                lo = p * 2 * m
                left = slice(lo, lo + m)
                right = slice(lo + m, lo + 2 * m)
                beta = b_ref[pl.ds(lo + (m - 1 if fwd else m), 1), :]
                if fwd:
                    parts += [k[left] * jnp.exp2(beta - b[left]), q[right] * jnp.exp2(b[right] - beta)]
                    readers.append((lo + m, parts[-1]))
                else:
                    parts += [q[left] * jnp.exp2(b[left] - beta), k[right] * jnp.exp2(beta - b[right])]
                    readers.append((lo, parts[-2]))
            wb = jnp.concatenate(parts, axis=0).astype(BF16)
            wq = jnp.concatenate([r for _, r in readers], axis=0).astype(BF16)
            term = _dot_nt(wq, wb)
            for n, (first_row, _) in enumerate(readers):
                add_rows(first_row, term[n * m:(n + 1) * m] * mask_ref[mask_idx, first_row:first_row + m, :])
        else:
            reads = ((row & m) != 0) if fwd else ((row & m) == 0)
            if m == 1:
                w = jnp.where(reads, q * f, k)
            else:
                betas = []
                for u in range(C // SUBLANE):
                    rows = []
                    for p in range(SUBLANE // (2 * m)):
                        rows.append(b_ref[pl.ds(u * SUBLANE + p * 2 * m + (m - 1 if fwd else m), 1), :])
                    beta_u = rows[-1]
                    for p in range(len(rows) - 2, -1, -1):
                        beta_u = jnp.where(sub < (p + 1) * 2 * m, rows[p], beta_u)
                    betas.append(jnp.broadcast_to(beta_u, (SUBLANE, LANE)))
                beta = jnp.concatenate(betas, axis=0)
                w = jnp.where(reads, q, k) * jnp.exp2(-jnp.abs(b - beta))
            wb = w.astype(BF16)
            add_rows(0, _dot_nt(wb, wb) * mask_ref[mask_idx])
    attn_b = jnp.concatenate(attn, axis=0).astype(BF16)
    diag = jnp.sum(q * k, axis=-1, keepdims=True) * v
    return q0, stb, attn_b, vb, diag, st_new


def _hgrn_tables():
    C = HGRN_CHUNK
    row, col = np.indices((C, C))
    tri = np.stack([col <= row, col >= row])
    tri = np.concatenate([tri, tri], axis=2)
    fwd, bwd = [], []
    for m in HGRN_LEVELS:
        same = (row // (2 * m)) == (col // (2 * m))
        fwd.append(same & ((row & m) != 0) & ((col & m) == 0))
        bwd.append(same & ((row & m) == 0) & ((col & m) != 0))
    return jnp.asarray(tri, BF16), jnp.asarray(np.stack(fwd + bwd), F32)


def _hgrn_unit(q_ref, v_ref, ff_ref, fb_ref, g_ref, lb_ref, gnw_ref, tri_ref, mask_ref, s0_ref, o_ref, s_out_ref,
               oacc_ref, b_ref, st_ref, *, T, fillers=()):
    C = HGRN_CHUNK
    U = HGRN_UNROLL
    nc = T // C
    assert nc % U == 0 and len(fillers) <= U + 2
    q_scale = HGRN_DK ** -0.5

    oacc_ref[...] = jnp.zeros_like(oacc_ref)
    for hh in range(HGRN_HG):
        for d in range(2):
            if s0_ref is not None:
                st_ref[2 * hh + d] = s0_ref[d, hh].T
            else:
                st_ref[2 * hh + d] = jnp.zeros((HGRN_DV, HGRN_DK), F32)

    def run_trip(c, stage_fillers):
        chains = []
        for hh in range(HGRN_HG):
            lanes = slice(hh * LANE, (hh + 1) * LANE)
            for d in range(2):
                for u in range(U):
                    cc = c * U + u if d == 0 else nc - 1 - (c * U + u)
                    rows = pl.ds(pl.multiple_of(cc * C, C), C)
                    chains.append((hh, d, u, rows, lanes))
        stage_a = {}
        states = {}
        pending = []
        for step in range(U + 2):
            for fill in stage_fillers[step:step + 1]:
                fill()
            for n, (hh, d, u, rows, lanes) in enumerate(chains):
                if u == step:
                    x_ref = ff_ref if d == 0 else fb_ref
                    stage_a[n] = _hgrn_chunk_a(x_ref[rows, lanes], lb_ref[d, hh], b_ref.at[n], tri_ref[d])
            ready, pending = pending, []
            for n, (hh, d, u, rows, lanes) in enumerate(chains):
                if u != step - 1:
                    continue
                b, f, k = stage_a[n]
                idx = 2 * hh + d
                st = st_ref[idx] if u == 0 else states[idx]
                q0, stb, attn_b, vb, diag, st = _hgrn_chunk_b(b, f, k, q_ref[rows, lanes], v_ref[rows, lanes], st,
                                                              b_ref.at[n], mask_ref, d)
                states[idx] = st
                if u == U - 1:
                    st_ref[idx] = st
                pending.append((rows, lanes, q0, stb, attn_b, vb, diag))
            for rows, lanes, q0, stb, attn_b, vb, diag in ready:
                oacc_ref[rows, lanes] += (_dot_nt(q0, stb) + _dot(attn_b, vb)) + diag
        assert not pending

    if nc == U:
        run_trip(0, list(fillers))
    else:
        for fill in fillers:
            fill()

        def body(c, carry):
            run_trip(c, [])
            return carry

        lax.fori_loop(0, nc // U, body, 0)

    for hh in range(HGRN_HG):
        lanes = slice(hh * LANE, (hh + 1) * LANE)
        if s_out_ref is not None:
            for d in range(2):
                s_out_ref[d, hh] = st_ref[2 * hh + d].T
        o = oacc_ref[:, lanes]
        ms = jnp.mean(o * o, axis=-1, keepdims=True) * (q_scale * q_scale)
        gate = g_ref[:, lanes]
        y = o * (q_scale * lax.rsqrt(ms + EPS)) * gnw_ref[...] * (gate * _sigmoid(gate))
        o_ref[:, lanes] = y.astype(BF16)


def _hgrn_kernel(*refs, T, has_s0, emit_state, n_alias):
    n_in = 9 + int(has_s0)
    n_out = 2 if emit_state else 1
    assert len(refs) == n_in + n_alias + n_out + 3
    s0_ref = refs[9] if has_s0 else None
    outs = refs[n_in + n_alias:n_in + n_alias + n_out]
    _hgrn_unit(*refs[:9], s0_ref, outs[0], outs[1] if emit_state else None, *refs[-3:], T=T)


def _hgrn_io(st, proj, lbs, gnorm_w, tables, s0, layer, states_prev, unit):
    hg = HGRN_HG
    wide = hg * LANE
    T = st.seq
    C = HGRN_CHUNK

    def at(fn):
        return lambda *g: fn(*unit(*g))

    def col_spec(col0):
        return pl.BlockSpec((T, wide), at(lambda b, h: (b, col0 // hg + h)))

    tri, masks = tables
    in_specs = [col_spec(COL_HQ), col_spec(COL_HI), col_spec(COL_HFF), col_spec(COL_HFB), col_spec(COL_HG),
                pl.BlockSpec((2, None, hg, 1, LANE), at(lambda b, h: (0, layer, h, 0, 0))),
                pl.BlockSpec((None, 1, LANE), at(lambda b, h: (layer, 0, 0))),
                pl.BlockSpec(tri.shape, at(lambda b, h: (0, 0, 0))),
                pl.BlockSpec(masks.shape, at(lambda b, h: (0, 0, 0)))]
    args = [proj, proj, proj, proj, proj, lbs, gnorm_w, tri, masks]
    state_spec = pl.BlockSpec((None, None, 2, hg, HGRN_DK, HGRN_DV), at(lambda b, h: (b, layer, 0, h, 0, 0)))
    if st.latent:
        in_specs.append(state_spec)
        args.append(s0)
    out_specs = [pl.BlockSpec((T, wide), at(lambda b, h: (b, h)))]
    out_shape = [jax.ShapeDtypeStruct((_ntok(st), MIX_WIDTH), BF16)]
    aliases = {}
    if not st.latent:
        out_specs.append(state_spec)
        out_shape.append(jax.ShapeDtypeStruct((st.nseq, DEPTH, 2, HGRN_HEADS, HGRN_DK, HGRN_DV), F32))
        if states_prev is not None:
            aliases = {len(args): 1}
            in_specs.append(pl.BlockSpec(memory_space=pl.ANY))
            args.append(states_prev)
    scratch = [pltpu.VMEM((T, wide), F32), pltpu.VMEM((2 * hg * HGRN_UNROLL, C, LANE), F32),
               pltpu.VMEM((2 * hg, HGRN_DV, HGRN_DK), F32)]
    return in_specs, args, out_specs, out_shape, scratch, aliases


def _hgrn(st, proj, lbs, gnorm_w, tables, s0, layer, states_prev):
    in_specs, args, out_specs, out_shape, scratch, aliases = _hgrn_io(
        st, proj, lbs, gnorm_w, tables, s0, layer, states_prev, lambda b, h: (b, h))
    return pl.pallas_call(
        functools.partial(_hgrn_kernel, T=st.seq, has_s0=st.latent, emit_state=not st.latent, n_alias=len(aliases)),
        grid=(st.nseq, HGRN_HEADS // HGRN_HG),
        in_specs=in_specs,
        out_specs=out_specs,
        out_shape=out_shape,
        scratch_shapes=scratch,
        input_output_aliases=aliases,
        compiler_params=_cparams(("arbitrary", "arbitrary")),
        name="hgrn_" + st.name,
    )(*args)


def _inproj_hgrn_kernel(*refs, T, n_hgrn_in):
    x_ref, sh_ref, sc_ref, nw_ref, w_ref = refs[:5]
    hgrn_in = refs[5:5 + n_hgrn_in]
    o_ref, mix_ref, s_out_ref = refs[5 + n_hgrn_in:5 + n_hgrn_in + 3]
    h_ref = refs[-4]

    @pl.when(pl.program_id(1) == 0)
    def _():
        _norm_modulate(x_ref, h_ref, nw_ref, sh_ref, sc_ref)

    def panel(n):
        cols = slice(n * IN_PANEL, (n + 1) * IN_PANEL)

        def fill():
            o_ref[:, cols] = _dot(h_ref[...], w_ref[:, cols])

        return fill

    _hgrn_unit(*hgrn_in[:9], None, mix_ref, s_out_ref, *refs[-3:], T=T,
               fillers=[panel(n) for n in range(IN_TN // IN_PANEL)])


def _inproj_hgrn(st, x, mods4, layer, norm_w, w_bf16, st2, proj2, lbs, gnorm_w, tables, states_prev):
    assert st.latent and not st2.latent
    ntok = _ntok(st)
    ni, nj = ntok // IN_TM, IN_PROJ_WIDTH // IN_TN
    nh = HGRN_HEADS // HGRN_HG
    assert ni * nj == st2.nseq * nh

    def unit(i, j):
        s = i * nj + j
        return s // nh, s % nh

    h_in, h_args, h_out, h_shape, h_scratch, h_alias = _hgrn_io(st2, proj2, lbs, gnorm_w, tables, None, layer,
                                                                states_prev, unit)
    in_specs = [
        pl.BlockSpec((IN_TM, D_MODEL), lambda i, j: (i, 0)),
        _mod_spec(st, layer, 0, IN_TM, 2),
        _mod_spec(st, layer, 1, IN_TM, 2),
        _layer_row_spec(layer, D_MODEL, 2),
        pl.BlockSpec((D_MODEL, IN_TN), lambda i, j: (0, j)),
    ] + h_in
    args = [x, mods4, mods4, norm_w, w_bf16] + h_args
    return pl.pallas_call(
        functools.partial(_inproj_hgrn_kernel, T=st2.seq, n_hgrn_in=len(h_args)),
        grid=(ni, nj),
        in_specs=in_specs,
        out_specs=[pl.BlockSpec((IN_TM, IN_TN), lambda i, j: (i, j))] + h_out,
        out_shape=[jax.ShapeDtypeStruct((ntok, IN_PROJ_WIDTH), F32)] + h_shape,
        scratch_shapes=[pltpu.VMEM((IN_TM, D_MODEL), BF16)] + h_scratch,
        input_output_aliases={5 + a: 1 + o for a, o in h_alias.items()},
        compiler_params=_cparams(("arbitrary", "arbitrary")),
        name="inproj_" + st.name + "_hgrn_" + st2.name,
    )(*args)


def _conv_kernel(cb_ref, cc_ref, cx_ref, w_ref, mix_ref, o_ref, *, T):
    del mix_ref
    u = cc_ref[...] * cx_ref[...]
    row = lax.broadcasted_iota(jnp.int32, (T, CONV_WIDTH), 0)
    u_prev = jnp.where(row == 0, 0.0, pltpu.roll(u, 1, axis=0))
    u_next = jnp.where(row == T - 1, 0.0, pltpu.roll(u, T - 1, axis=0))
    y = u_prev * w_ref[0:1, :] + u * w_ref[1:2, :] + u_next * w_ref[2:3, :]
    o_ref[...] = (cb_ref[...] * y).astype(BF16)


def _conv(st, proj, conv_w, layer, mix):
    cw = CONV_WIDTH // LANE
    T = st.seq

    def col_spec(col0):
        return pl.BlockSpec((T, CONV_WIDTH), lambda b: (b, col0 // cw))

    return pl.pallas_call(
        functools.partial(_conv_kernel, T=T),
        grid=(st.nseq,),
        in_specs=[col_spec(COL_CB), col_spec(COL_CC), col_spec(COL_CX),
                  pl.BlockSpec((None, 3, CONV_WIDTH), lambda b: (layer, 0, 0)),
                  pl.BlockSpec(memory_space=pl.ANY)],
        out_specs=pl.BlockSpec((T, CONV_WIDTH), lambda b: (b, HGRN_WIDTH // CONV_WIDTH)),
        out_shape=jax.ShapeDtypeStruct(mix.shape, mix.dtype),
        input_output_aliases={4: 0},
        compiler_params=_cparams(("arbitrary",)),
        name="conv_" + st.name,
    )(proj, proj, proj, conv_w, mix)


MIX_COL_ATTN = (HGRN_WIDTH + CONV_WIDTH) // LANE


def _ctx_attn_kernel(q_ref, k_ref, v_ref, mix_ref, o_ref):
    del mix_ref
    scale = NA_HEAD_DIM ** -0.5
    heads = [slice(h * LANE, (h + 1) * LANE) for h in range(NA_HEADS)]
    scores = [_dot_nt(q_ref[:, hs].astype(BF16), k_ref[:, hs].astype(BF16)) * scale for hs in heads]
    probs = []
    for s in scores:
        p = jnp.exp(s - jnp.max(s, axis=-1, keepdims=True))
        probs.append((p.astype(BF16), jnp.sum(p, axis=-1, keepdims=True)))
    for hs, (p, l) in zip(heads, probs):
        o_ref[:, hs] = (_dot(p, v_ref[:, hs].astype(BF16)) / l).astype(BF16)


def _ctx_attn(proj, mix):
    def col_spec(col0):
        return pl.BlockSpec((SEQ, NA_WIDTH), lambda b: (b, col0 * LANE // NA_WIDTH))

    return pl.pallas_call(
        _ctx_attn_kernel,
        grid=(BATCH,),
        in_specs=[col_spec(COL_NQ), col_spec(COL_NK), col_spec(COL_NV), pl.BlockSpec(memory_space=pl.ANY)],
        out_specs=pl.BlockSpec((SEQ, NA_WIDTH), lambda b: (b, MIX_COL_ATTN * LANE // NA_WIDTH)),
        out_shape=jax.ShapeDtypeStruct(mix.shape, mix.dtype),
        input_output_aliases={3: 0},
        compiler_params=_cparams(("arbitrary",)),
        name="ctx_attn",
    )(proj, proj, proj, mix)


_KH = min(NA_KH, GRID_ROWS)
_ROW_START = [int(v) for v in np.clip(np.arange(GRID_ROWS) - _KH // 2, 0, GRID_ROWS - _KH)]
NLOC = _KH * GRID_W
RPB_ROWS = 2 * NA_KH - 1
RPB_COLS = 2 * NA_KW - 1


def _nat_kernel(q_ref, k_ref, v_ref, kc_ref, vc_ref, bias_ref, mix_ref, o_ref, sctx_ref, sloc_ref, pctx_ref,
                ploc_ref, den_ref):
    del mix_ref
    scale = NA_HEAD_DIM ** -0.5
    row_slices = [slice(r * GRID_W, (r + 1) * GRID_W) for r in range(GRID_ROWS)]
    bands = [slice(rs * GRID_W, rs * GRID_W + NLOC) for rs in _ROW_START]
    for h in range(NA_HEADS):
        lanes = slice(h * LANE, (h + 1) * LANE)
        q_all = q_ref[:, lanes].astype(BF16)
        k_all = k_ref[:, lanes].astype(BF16)
        v_all = v_ref[:, lanes].astype(BF16)
        sctx_ref[...] = _dot_nt(q_all, kc_ref[:, h, :].astype(BF16)) * scale
        for r, (rows, band) in enumerate(zip(row_slices, bands)):
            sloc_ref[rows, :] = _dot_nt(q_all[rows], k_all[band]) * scale + bias_ref[h, r - _ROW_START[r]]
        for rows in row_slices:
            s_loc = sloc_ref[rows, :]
            s_ctx = sctx_ref[rows, :]
            m = jnp.maximum(jnp.max(s_loc, axis=-1, keepdims=True), jnp.max(s_ctx, axis=-1, keepdims=True))
            p_loc = jnp.exp(s_loc - m)
            p_ctx = jnp.exp(s_ctx - m)
            den_ref[rows, :] = jnp.sum(p_loc, axis=-1, keepdims=True) + jnp.sum(p_ctx, axis=-1, keepdims=True)
            ploc_ref[rows, :] = p_loc.astype(BF16)
            pctx_ref[rows, :] = p_ctx.astype(BF16)
        o_ctx = _dot(pctx_ref[...], vc_ref[:, h, :].astype(BF16))
        for rows, band in zip(row_slices, bands):
            o = (_dot(ploc_ref[rows, :], v_all[band]) + o_ctx[rows]) / den_ref[rows, :]
            o_ref[rows, lanes] = o.astype(BF16)


def _nat_bias_kernel(rpb_ref, o_ref):
    base = (pl.program_id(0) * NA_HEADS + pl.program_id(1)) * (RPB_ROWS * RPB_COLS)
    q = lax.broadcasted_iota(jnp.int32, (GRID_W, GRID_W), 0)
    k = lax.broadcasted_iota(jnp.int32, (GRID_W, GRID_W), 1)
    col_start = jnp.clip(q - NA_KW // 2, 0, GRID_W - NA_KW)
    in_window = (k >= col_start) & (k < col_start + NA_KW)
    dc = jnp.clip(k - q + NA_KW - 1, 0, RPB_COLS - 1)
    tabs = []
    for dr in range(RPB_ROWS):
        t = jnp.zeros((GRID_W, GRID_W), F32)
        for c in range(RPB_COLS):
            t = jnp.where(dc == c, rpb_ref[base + dr * RPB_COLS + c], t)
        tabs.append(jnp.where(in_window, t, NEG_BIG))
    for off in range(_KH):
        for p in range(_KH // 2):
            pair = [tabs[kr - off + NA_KH - 1] for kr in (2 * p, 2 * p + 1)]
            o_ref[off, :, 2 * p * GRID_W:(2 * p + 2) * GRID_W] = jnp.concatenate(pair, axis=1)


def _nat_bias(na_rpb):
    return pl.pallas_call(
        _nat_bias_kernel,
        grid=(DEPTH, NA_HEADS),
        in_specs=[pl.BlockSpec(memory_space=pltpu.SMEM)],
        out_specs=pl.BlockSpec((None, None, _KH, GRID_W, NLOC), lambda l, h: (l, h, 0, 0, 0)),
        out_shape=jax.ShapeDtypeStruct((DEPTH, NA_HEADS, _KH, GRID_W, NLOC), F32),
        compiler_params=_cparams(("arbitrary", "arbitrary")),
        name="nat_bias",
    )(na_rpb.reshape(-1))


def _nat(proj, cache_k, cache_v, bias, layer, mix):
    def col_spec(col0):
        return pl.BlockSpec((DEC_SEQ, NA_WIDTH), lambda b: (b, col0 * LANE // NA_WIDTH))

    cache_spec = pl.BlockSpec((None, None, PAST_LEN, NA_HEADS, NA_HEAD_DIM), lambda b: (b, layer, 0, 0, 0))
    return pl.pallas_call(
        _nat_kernel,
        grid=(DEC_BATCH,),
        in_specs=[col_spec(COL_NQ), col_spec(COL_NK), col_spec(COL_NV), cache_spec, cache_spec,
                  pl.BlockSpec((None, NA_HEADS, _KH, GRID_W, NLOC), lambda b: (layer, 0, 0, 0, 0)),
                  pl.BlockSpec(memory_space=pl.ANY)],
        out_specs=pl.BlockSpec((DEC_SEQ, NA_WIDTH), lambda b: (b, MIX_COL_ATTN * LANE // NA_WIDTH)),
        out_shape=jax.ShapeDtypeStruct(mix.shape, mix.dtype),
        scratch_shapes=[pltpu.VMEM((DEC_SEQ, PAST_LEN), F32), pltpu.VMEM((DEC_SEQ, NLOC), F32),
                        pltpu.VMEM((DEC_SEQ, PAST_LEN), BF16), pltpu.VMEM((DEC_SEQ, NLOC), BF16),
                        pltpu.VMEM((DEC_SEQ, 1), F32)],
        input_output_aliases={6: 0},
        compiler_params=_cparams(("arbitrary",)),
        name="nat",
    )(proj, proj, proj, cache_k, cache_v, bias, mix)


def kernel(x_prompt, x_sample, cache_na_k, cache_na_v, state_hgrn, c, c_ctx, w_ada, b_ada, norm_mix_w, w_in,
           hgrn_lb_raw, hgrn_gnorm_w, conv_w, na_rpb, w_out, norm_ffn_w, w_ffn_gate, w_ffn_up, w_ffn_down,
           final_norm_w):
    xs = {LATENT: x_sample.reshape(_ntok(LATENT), D_MODEL), CONTEXT: x_prompt.reshape(_ntok(CONTEXT), D_MODEL)}
    cvecs = jnp.concatenate([c, c_ctx[None, :], jnp.zeros((MOD_ROWS - DEC_BATCH - 1, D_MODEL), F32)], axis=0)
    mods4 = _adaln(cvecs, w_ada, b_ada).reshape(DEPTH, MOD_ROWS, 1, 6 * D_MODEL)

    p_lb = jax.nn.softmax(hgrn_lb_raw.astype(F32), axis=1)
    cp = jnp.cumsum(p_lb, axis=1)
    lbs = (cp - cp[:, :1]).reshape(2, DEPTH, HGRN_HEADS, 1, LANE)
    gnorm_w = hgrn_gnorm_w.reshape(DEPTH, 1, LANE)
    norm_mix = norm_mix_w.reshape(DEPTH, 1, D_MODEL)
    norm_ffn = norm_ffn_w.reshape(DEPTH, 1, D_MODEL)

    w_out_b = w_out.astype(BF16)

    nat_bias = _nat_bias(na_rpb)
    tables = _hgrn_tables()

    kv = None
    states = None
    for l in range(DEPTH):
        last = l == DEPTH - 1
        proj_c, k_c, v_c, w_in_l = _inproj(CONTEXT, xs[CONTEXT], mods4, l, norm_mix, w_in, kv, cast_weights=True,
                                           nblocks=1)
        proj_c, *kv = _inproj(CONTEXT, xs[CONTEXT], mods4, l, norm_mix, w_in_l, (k_c, v_c), block0=1,
                              nblocks=_ntok(CONTEXT) // IN_TM - 1, proj_prev=proj_c)
        proj_l, mix_c, states = _inproj_hgrn(LATENT, xs[LATENT], mods4, l, norm_mix, w_in_l, CONTEXT, proj_c, lbs,
                                             gnorm_w, tables, states)

        mix_l = _hgrn(LATENT, proj_l, lbs, gnorm_w, tables, state_hgrn, l, None)[0]
        mix_l = _conv(LATENT, proj_l, conv_w, l, mix_l)
        mix_l = _nat(proj_l, cache_na_k, cache_na_v, nat_bias, l, mix_l)
        x1 = _outproj(LATENT, xs[LATENT], mix_l, mods4, l, w_out_b)
        y, *ffn_w = _ffn(LATENT, x1, mods4, l, norm_ffn, final_norm_w, (w_ffn_gate, w_ffn_up, w_ffn_down), last,
                         cast_weights=True, nblocks=1)
        xs[LATENT] = _ffn(LATENT, x1, mods4, l, norm_ffn, final_norm_w, ffn_w, last, block0=1,
                          nblocks=_ntok(LATENT) // FFN_TM - 1, out_prev=y)[0]

        mix_c = _conv(CONTEXT, proj_c, conv_w, l, mix_c)
        mix_c = _ctx_attn(proj_c, mix_c)
        x1 = _outproj(CONTEXT, xs[CONTEXT], mix_c, mods4, l, w_out_b)
        xs[CONTEXT] = _ffn(CONTEXT, x1, mods4, l, norm_ffn, final_norm_w, ffn_w, last)[0]

    y_sample = xs[LATENT].reshape(DEC_BATCH, DEC_SEQ, D_MODEL)
    y_prompt = xs[CONTEXT].reshape(BATCH, SEQ, D_MODEL)
    new_k = kv[0].reshape(BATCH, DEPTH, SEQ, NA_HEADS, NA_HEAD_DIM)
    new_v = kv[1].reshape(BATCH, DEPTH, SEQ, NA_HEADS, NA_HEAD_DIM)
    return (y_prompt, y_sample, new_k, new_v, states)
```

```python
import collections
import functools

import numpy as np
import jax
import jax.numpy as jnp
from jax import lax
from jax.experimental import pallas as pl
from jax.experimental.pallas import tpu as pltpu

F32 = jnp.float32
BF16 = jnp.bfloat16

D_MODEL = 2048
BATCH = 16
SEQ = 256
DEPTH = 2
DEC_BATCH = 8
DEC_SEQ = 1024
PAST_LEN = 512
GRID_W = 64
HGRN_HEADS = 8
HGRN_DK = 128
HGRN_DV = 128
HGRN_WIDTH = HGRN_HEADS * HGRN_DV
CONV_WIDTH = 512
NA_HEADS = 4
NA_HEAD_DIM = 128
NA_WIDTH = NA_HEADS * NA_HEAD_DIM
NA_KH = 8
NA_KW = 16
MIX_WIDTH = HGRN_WIDTH + CONV_WIDTH + NA_WIDTH
IN_PROJ_WIDTH = 5 * HGRN_WIDTH + 3 * CONV_WIDTH + 3 * NA_WIDTH
FFN_HIDDEN = ((8 * D_MODEL + 3 * 256 - 1) // (3 * 256)) * 256
EPS = 1e-6

MOD_ROWS = 16
CTX_ROW = DEC_BATCH
LANE = 128
SUBLANE = 8
GRID_ROWS = DEC_SEQ // GRID_W

COL_HQ = 0
COL_HI = HGRN_WIDTH // LANE
COL_HFF = 2 * HGRN_WIDTH // LANE
COL_HFB = 3 * HGRN_WIDTH // LANE
COL_HG = 4 * HGRN_WIDTH // LANE
COL_CB = 5 * HGRN_WIDTH // LANE
COL_CC = COL_CB + CONV_WIDTH // LANE
COL_CX = COL_CC + CONV_WIDTH // LANE
COL_NQ = COL_CX + CONV_WIDTH // LANE
COL_NK = COL_NQ + NA_WIDTH // LANE
COL_NV = COL_NK + NA_WIDTH // LANE

NEG_BIG = -1e30
VMEM_LIMIT = 60 * 1024 * 1024

Stream = collections.namedtuple("Stream", ["name", "seq", "nseq", "latent"])
LATENT = Stream("lat", DEC_SEQ, DEC_BATCH, True)
CONTEXT = Stream("ctx", SEQ, BATCH, False)


def _ntok(st):
    return st.seq * st.nseq


def _cparams(sem):
    return pltpu.CompilerParams(dimension_semantics=sem, vmem_limit_bytes=VMEM_LIMIT)


def _dot(a, b):
    return jnp.dot(a, b, preferred_element_type=F32)


def _dot_nt(a, b):
    return lax.dot_general(a, b, (((1,), (1,)), ((), ())), preferred_element_type=F32)


def _dot_tn(a, b):
    return lax.dot_general(a, b, (((0,), (0,)), ((), ())), preferred_element_type=F32)


def _sigmoid(x):
    return 1.0 / (1.0 + jnp.exp(-x))


ADA_TN = 1024


def _adaln_kernel(c_ref, w_ref, b_ref, o_ref):
    c = c_ref[...]
    s = (c * _sigmoid(c)).astype(BF16)
    o_ref[...] = _dot(s, w_ref[...].astype(BF16)) + b_ref[...]


def _adaln(cvecs, w_ada, b_ada):
    n = 6 * D_MODEL
    return pl.pallas_call(
        _adaln_kernel,
        grid=(DEPTH, n // ADA_TN),
        in_specs=[
            pl.BlockSpec((MOD_ROWS, D_MODEL), lambda l, j: (0, 0)),
            pl.BlockSpec((None, D_MODEL, ADA_TN), lambda l, j: (l, 0, j)),
            pl.BlockSpec((None, 1, ADA_TN), lambda l, j: (l, 0, j)),
        ],
        out_specs=pl.BlockSpec((None, MOD_ROWS, ADA_TN), lambda l, j: (l, 0, j)),
        out_shape=jax.ShapeDtypeStruct((DEPTH, MOD_ROWS, n), F32),
        compiler_params=_cparams(("arbitrary", "arbitrary")),
        name="adaln",
    )(cvecs, w_ada, b_ada.reshape(DEPTH, 1, n))


IN_TM = 1024
IN_TN = 1024
IN_PANEL = 256
TM = 512
FFN_TM = 1024
FFN_TF = 512


def _mod_spec(st, layer, chunk, tm, ngrid, block0=0):
    def row(i):
        return ((block0 + i) * tm) // DEC_SEQ if st.latent else CTX_ROW

    if ngrid == 1:
        return pl.BlockSpec((None, None, 1, D_MODEL), lambda i: (layer, row(i), 0, chunk))
    return pl.BlockSpec((None, None, 1, D_MODEL), lambda i, j: (layer, row(i), 0, chunk))


PROLOGUE_ROWS = 128


def _norm_modulate(x_ref, h_ref, nw_ref, sh_ref, sc_ref):
    gain = nw_ref[...] * (1.0 + sc_ref[...])
    shift = sh_ref[...]

    def chunk(r, carry):
        rows = pl.ds(pl.multiple_of(r * PROLOGUE_ROWS, PROLOGUE_ROWS), PROLOGUE_ROWS)
        x = x_ref[rows, :]
        ms = jnp.mean(x * x, axis=-1, keepdims=True)
        h_ref[rows, :] = (x * lax.rsqrt(ms + EPS) * gain + shift).astype(BF16)
        return carry

    lax.fori_loop(0, x_ref.shape[0] // PROLOGUE_ROWS, chunk, 0)


IN_CAST_TN = 512


def _inproj_kernel(*refs, emit_kv, cast_weights, n_alias):
    x_ref, sh_ref, sc_ref, nw_ref, w_ref = refs[:5]
    outs = refs[5 + n_alias:-1]
    o_ref = outs[0]
    h_ref = refs[-1]
    j = pl.program_id(1)
    nj = pl.num_programs(1)

    @pl.when(j == 0)
    def _():
        _norm_modulate(x_ref, h_ref, nw_ref, sh_ref, sc_ref)

    if cast_weights:
        w = w_ref[...].astype(BF16)
        outs[-1][...] = w
    else:
        w = w_ref[...]
    o_ref[...] = _dot(h_ref[...], w)

    if emit_kv:
        k_ref, v_ref = outs[1], outs[2]
        tn = o_ref.shape[1]
        if tn == 2 * NA_WIDTH:
            @pl.when(j == nj - 1)
            def _():
                k_ref[...] = o_ref[:, :NA_WIDTH].reshape(k_ref.shape)
                v_ref[...] = o_ref[:, NA_WIDTH:].reshape(v_ref.shape)
        else:
            assert tn == NA_WIDTH

            @pl.when(j == nj - 2)
            def _():
                k_ref[...] = o_ref[...].reshape(k_ref.shape)

            @pl.when(j == nj - 1)
            def _():
                v_ref[...] = o_ref[...].reshape(v_ref.shape)


def _layer_row_spec(layer, width, ngrid):
    if ngrid == 1:
        return pl.BlockSpec((None, 1, width), lambda i: (layer, 0, 0))
    return pl.BlockSpec((None, 1, width), lambda i, j: (layer, 0, 0))


def _inproj(st, x, mods4, layer, norm_w, w, kv_prev, *, cast_weights=False, block0=0, nblocks=None, proj_prev=None):
    emit_kv = not st.latent
    tn = IN_CAST_TN if cast_weights else IN_TN
    assert COL_NK * LANE == IN_PROJ_WIDTH - 2 * NA_WIDTH and tn in (NA_WIDTH, 2 * NA_WIDTH)
    ntok = _ntok(st)
    nblocks = ntok // IN_TM if nblocks is None else nblocks
    if cast_weights:
        w_spec = pl.BlockSpec((None, D_MODEL, tn), lambda i, j: (layer, 0, j))
    else:
        w_spec = pl.BlockSpec((D_MODEL, tn), lambda i, j: (0, j))
    in_specs = [
        pl.BlockSpec((IN_TM, D_MODEL), lambda i, j: (block0 + i, 0)),
        _mod_spec(st, layer, 0, IN_TM, 2, block0),
        _mod_spec(st, layer, 1, IN_TM, 2, block0),
        _layer_row_spec(layer, D_MODEL, 2),
        w_spec,
    ]
    args = [x, mods4, mods4, norm_w, w]
    out_specs = [pl.BlockSpec((IN_TM, tn), lambda i, j: (block0 + i, j))]
    out_shape = [jax.ShapeDtypeStruct((ntok, IN_PROJ_WIDTH), F32)]
    aliases = {}
    if proj_prev is not None:
        aliases[len(args)] = 0
        in_specs.append(pl.BlockSpec(memory_space=pl.ANY))
        args.append(proj_prev)
    if emit_kv:
        nb = IN_TM // st.seq
        kv_spec = pl.BlockSpec((nb, None, st.seq, NA_WIDTH), lambda i, j: (block0 + i, layer, 0, 0))
        out_specs += [kv_spec, kv_spec]
        out_shape += [jax.ShapeDtypeStruct((st.nseq, DEPTH, st.seq, NA_WIDTH), F32)] * 2
        if kv_prev is not None:
            aliases.update({len(args): 1, len(args) + 1: 2})
            in_specs += [pl.BlockSpec(memory_space=pl.ANY)] * 2
            args += list(kv_prev)
    if cast_weights:
        out_specs.append(pl.BlockSpec((D_MODEL, tn), lambda i, j: (0, j)))
        out_shape.append(jax.ShapeDtypeStruct((D_MODEL, IN_PROJ_WIDTH), BF16))
    return pl.pallas_call(
        functools.partial(_inproj_kernel, emit_kv=emit_kv, cast_weights=cast_weights, n_alias=len(aliases)),
        grid=(nblocks, IN_PROJ_WIDTH // tn),
        in_specs=in_specs,
        out_specs=out_specs,
        out_shape=out_shape,
        scratch_shapes=[pltpu.VMEM((IN_TM, D_MODEL), BF16)],
        input_output_aliases=aliases,
        compiler_params=_cparams(("arbitrary", "arbitrary")),
        name="inproj_" + st.name + ("_cast" if cast_weights else ""),
    )(*args)


def _outproj_kernel(x_ref, m_ref, g_ref, w_ref, o_ref):
    o_ref[...] = x_ref[...] + g_ref[...] * _dot(m_ref[...], w_ref[...])


def _outproj(st, x, mix, mods4, layer, w_bf16):
    ntok = _ntok(st)
    return pl.pallas_call(
        _outproj_kernel,
        grid=(ntok // TM,),
        in_specs=[
            pl.BlockSpec((TM, D_MODEL), lambda i: (i, 0)),
            pl.BlockSpec((TM, MIX_WIDTH), lambda i: (i, 0)),
            _mod_spec(st, layer, 2, TM, 1),
            pl.BlockSpec((None, MIX_WIDTH, D_MODEL), lambda i: (layer, 0, 0)),
        ],
        out_specs=pl.BlockSpec((TM, D_MODEL), lambda i: (i, 0)),
        out_shape=jax.ShapeDtypeStruct((ntok, D_MODEL), F32),
        compiler_params=_cparams(("arbitrary",)),
        name="outproj_" + st.name,
    )(x, mix, mods4, w_bf16)


FFN_CAST_TF = 256
FFN_PANEL = 256


def _ffn_prologue(x_ref, h_ref, o_ref, nw_ref, sh_ref, sc_ref):
    _norm_modulate(x_ref, h_ref, nw_ref, sh_ref, sc_ref)
    o_ref[...] = jnp.zeros_like(o_ref)


def _ffn_fillers(h_ref, wg, wu, wd, o_ref):
    hidden = []

    def gate_up(n):
        def fill():
            h = h_ref[...]
            a = _dot(h, wg[:, n:n + FFN_PANEL])
            u = _dot(h, wu[:, n:n + FFN_PANEL])
            hidden.append((a * _sigmoid(a) * u).astype(BF16))
        return fill

    def down(n):
        def fill():
            t = hidden[0] if len(hidden) == 1 else jnp.concatenate(hidden, axis=1)
            o_ref[:, n:n + FFN_TF] += _dot(t, wd[:, n:n + FFN_TF])
        return fill

    return [gate_up(n) for n in range(0, wg.shape[1], FFN_PANEL)] + [down(n) for n in range(0, D_MODEL, FFN_TF)]


def _ffn_epilogue(x_ref, o_ref, g_ref, fw_ref, final_norm):
    gate = g_ref[...]

    def chunk(r, carry):
        rows = pl.ds(pl.multiple_of(r * PROLOGUE_ROWS, PROLOGUE_ROWS), PROLOGUE_ROWS)
        y = x_ref[rows, :] + gate * o_ref[rows, :]
        if final_norm:
            ms = jnp.mean(y * y, axis=-1, keepdims=True)
            y = y * lax.rsqrt(ms + EPS) * fw_ref[...]
        o_ref[rows, :] = y
        return carry

    lax.fori_loop(0, x_ref.shape[0] // PROLOGUE_ROWS, chunk, 0)


def _ffn_kernel(*refs, final_norm, cast_weights, n_alias):
    x_ref, sh_ref, sc_ref, g_ref, nw_ref, fw_ref, wg_ref, wu_ref, wd_ref = refs[:9]
    outs = refs[9 + n_alias:-1]
    o_ref = outs[0]
    h_ref = refs[-1]
    j = pl.program_id(1)

    @pl.when(j == 0)
    def _():
        _ffn_prologue(x_ref, h_ref, o_ref, nw_ref, sh_ref, sc_ref)

    if cast_weights:
        wg, wu, wd = (w[...].astype(BF16) for w in (wg_ref, wu_ref, wd_ref))
        for w, w_out_ref in zip((wg, wu, wd), outs[1:]):
            w_out_ref[...] = w
    else:
        wg, wu, wd = wg_ref[...], wu_ref[...], wd_ref[...]
    for fill in _ffn_fillers(h_ref, wg, wu, wd, o_ref):
        fill()

    @pl.when(j == pl.num_programs(1) - 1)
    def _():
        _ffn_epilogue(x_ref, o_ref, g_ref, fw_ref, final_norm)


def _ffn(st, x, mods4, layer, norm_w, final_w, weights, final_norm, *, cast_weights=False, block0=0, nblocks=None,
         out_prev=None, tm=FFN_TM):
    ntok = _ntok(st)
    TM = tm
    tf = FFN_CAST_TF if cast_weights else FFN_TF
    nblocks = ntok // TM if nblocks is None else nblocks
    if cast_weights:
        w_specs = [pl.BlockSpec((None, D_MODEL, tf), lambda i, j: (layer, 0, j)),
                   pl.BlockSpec((None, D_MODEL, tf), lambda i, j: (layer, 0, j)),
                   pl.BlockSpec((None, tf, D_MODEL), lambda i, j: (layer, j, 0))]
    else:
        w_specs = [pl.BlockSpec((D_MODEL, tf), lambda i, j: (0, j)),
                   pl.BlockSpec((D_MODEL, tf), lambda i, j: (0, j)),
                   pl.BlockSpec((tf, D_MODEL), lambda i, j: (j, 0))]
    in_specs = [
        pl.BlockSpec((TM, D_MODEL), lambda i, j: (block0 + i, 0)),
        _mod_spec(st, layer, 3, TM, 2, block0),
        _mod_spec(st, layer, 4, TM, 2, block0),
        _mod_spec(st, layer, 5, TM, 2, block0),
        _layer_row_spec(layer, D_MODEL, 2),
        pl.BlockSpec((1, D_MODEL), lambda i, j: (0, 0)),
    ] + w_specs
    args = [x, mods4, mods4, mods4, norm_w, final_w.reshape(1, D_MODEL)] + list(weights)
    out_specs = [pl.BlockSpec((TM, D_MODEL), lambda i, j: (block0 + i, 0))]
    out_shape = [jax.ShapeDtypeStruct((ntok, D_MODEL), F32)]
    if cast_weights:
        out_specs += [pl.BlockSpec((D_MODEL, tf), lambda i, j: (0, j)),
                      pl.BlockSpec((D_MODEL, tf), lambda i, j: (0, j)),
                      pl.BlockSpec((tf, D_MODEL), lambda i, j: (j, 0))]
        out_shape += [jax.ShapeDtypeStruct((D_MODEL, FFN_HIDDEN), BF16)] * 2
        out_shape += [jax.ShapeDtypeStruct((FFN_HIDDEN, D_MODEL), BF16)]
    aliases = {}
    if out_prev is not None:
        aliases = {len(args): 0}
        in_specs.append(pl.BlockSpec(memory_space=pl.ANY))
        args.append(out_prev)
    return pl.pallas_call(
        functools.partial(_ffn_kernel, final_norm=final_norm, cast_weights=cast_weights, n_alias=len(aliases)),
        grid=(nblocks, FFN_HIDDEN // tf),
        in_specs=in_specs,
        out_specs=out_specs,
        out_shape=out_shape,
        scratch_shapes=[pltpu.VMEM((TM, D_MODEL), BF16)],
        input_output_aliases=aliases,
        compiler_params=_cparams(("arbitrary", "arbitrary")),
        name="ffn_" + st.name + ("_cast" if cast_weights else ""),
    )(*args)


HGRN_CHUNK = 64
HGRN_HG = 2
HGRN_UNROLL = 8
HGRN_LEVELS = [HGRN_CHUNK >> (i + 1) for i in range(HGRN_CHUNK.bit_length() - 1)]


LOG2E = 1.4426950408889634


def _hgrn_gates(x, lb):
    e = jnp.exp(-jnp.abs(x))
    r = 1.0 / (1.0 + e)
    er = e * r
    pos = x >= 0.0
    sig = jnp.where(pos, r, er)
    nsig = jnp.where(pos, er, r)
    ls2 = jnp.minimum(x, 0.0) * LOG2E + jnp.log2(r)
    f = lb + (1.0 - lb) * sig
    log2_f = jnp.where(lb > 0.0, jnp.log2(f), ls2)
    return log2_f, f, (1.0 - lb) * nsig


def _hgrn_chunk_a(x, lb, b_ref, tri):
    g, f, k = _hgrn_gates(x, lb)
    g_hi = g.astype(BF16)
    g_lo = (g - g_hi.astype(F32)).astype(BF16)
    b = _dot(tri, jnp.concatenate([g_hi, g_lo], axis=0))
    b_ref[...] = b
    return b, f, k


def _hgrn_chunk_b(b, f, k, q, v, st, b_ref, mask_ref, dirn):
    C = HGRN_CHUNK
    fwd = dirn == 0
    b_end = b_ref[pl.ds(C - 1 if fwd else 0, 1), :]
    vb = v.astype(BF16)
    q0 = (q * jnp.exp2(b)).astype(BF16)
    stb = st.astype(BF16)
    kd = (k * jnp.exp2(b_end - b)).astype(BF16)
    st_new = jnp.exp2(b_end) * st + _dot_tn(vb, kd)

    row = lax.broadcasted_iota(jnp.int32, (C, LANE), 0)
    sub = lax.broadcasted_iota(jnp.int32, (SUBLANE, LANE), 0)
    attn = [None] * (C // SUBLANE)

    def add_rows(first_row, term):
        for t in range(term.shape[0] // SUBLANE):
            blk = first_row // SUBLANE + t
            piece = term[t * SUBLANE:(t + 1) * SUBLANE]
            attn[blk] = piece if attn[blk] is None else attn[blk] + piece

    for lvl, m in enumerate(HGRN_LEVELS):
        mask_idx = dirn * len(HGRN_LEVELS) + lvl
        if m >= SUBLANE:
            parts, readers = [], []
            for p in range(C // (2 * m)):
                lo = p * 2 * m
                left = slice(lo, lo + m)
                right = slice(lo + m, lo + 2 * m)
                beta = b_ref[pl.ds(lo + (m - 1 if fwd else m), 1), :]
                if fwd:
                    parts += [k[left] * jnp.exp2(beta - b[left]), q[right] * jnp.exp2(b[right] - beta)]
                    readers.append((lo + m, parts[-1]))
                else:
                    parts += [q[left] * jnp.exp2(b[left] - beta), k[right] * jnp.exp2(beta - b[right])]
                    readers.append((lo, parts[-2]))
            wb = jnp.concatenate(parts, axis=0).astype(BF16)
            wq = jnp.concatenate([r for _, r in readers], axis=0).astype(BF16)
            term = _dot_nt(wq, wb)
            for n, (first_row, _) in enumerate(readers):
                add_rows(first_row, term[n * m:(n + 1) * m] * mask_ref[mask_idx, first_row:first_row + m, :])
        else:
            reads = ((row & m) != 0) if fwd else ((row & m) == 0)
            if m == 1:
                w = jnp.where(reads, q * f, k)
            else:
                betas = []
                for u in range(C // SUBLANE):
                    rows = []
                    for p in range(SUBLANE // (2 * m)):
                        rows.append(b_ref[pl.ds(u * SUBLANE + p * 2 * m + (m - 1 if fwd else m), 1), :])
                    beta_u = rows[-1]
                    for p in range(len(rows) - 2, -1, -1):
                        beta_u = jnp.where(sub < (p + 1) * 2 * m, rows[p], beta_u)
                    betas.append(jnp.broadcast_to(beta_u, (SUBLANE, LANE)))
                beta = jnp.concatenate(betas, axis=0)
                w = jnp.where(reads, q, k) * jnp.exp2(-jnp.abs(b - beta))
            wb = w.astype(BF16)
            add_rows(0, _dot_nt(wb, wb) * mask_ref[mask_idx])
    attn_b = jnp.concatenate(attn, axis=0).astype(BF16)
    diag = jnp.sum(q * k, axis=-1, keepdims=True) * v
    return q0, stb, attn_b, vb, diag, st_new


def _hgrn_tables():
    C = HGRN_CHUNK
    row, col = np.indices((C, C))
    tri = np.stack([col <= row, col >= row])
    tri = np.concatenate([tri, tri], axis=2)
    fwd, bwd = [], []
    for m in HGRN_LEVELS:
        same = (row // (2 * m)) == (col // (2 * m))
        fwd.append(same & ((row & m) != 0) & ((col & m) == 0))
        bwd.append(same & ((row & m) == 0) & ((col & m) != 0))
    return jnp.asarray(tri, BF16), jnp.asarray(np.stack(fwd + bwd), F32)


def _hgrn_unit(q_ref, v_ref, ff_ref, fb_ref, g_ref, lb_ref, gnw_ref, tri_ref, mask_ref, s0_ref, o_ref, s_out_ref,
               oacc_ref, b_ref, st_ref, *, T, fillers=()):
    C = HGRN_CHUNK
    U = min(HGRN_UNROLL, T // C)
    nc = T // C
    assert nc % U == 0 and len(fillers) <= U + 2
    q_scale = HGRN_DK ** -0.5

    oacc_ref[...] = jnp.zeros_like(oacc_ref)
    for hh in range(HGRN_HG):
        for d in range(2):
            if s0_ref is not None:
                st_ref[2 * hh + d] = s0_ref[d, hh].T
            else:
                st_ref[2 * hh + d] = jnp.zeros((HGRN_DV, HGRN_DK), F32)

    def run_trip(c, stage_fillers):
        chains = []
        for hh in range(HGRN_HG):
            lanes = slice(hh * LANE, (hh + 1) * LANE)
            for d in range(2):
                for u in range(U):
                    cc = c * U + u if d == 0 else nc - 1 - (c * U + u)
                    rows = pl.ds(pl.multiple_of(cc * C, C), C)
                    chains.append((hh, d, u, rows, lanes))
        stage_a = {}
        states = {}
        pending = []
        for step in range(U + 2):
            for fill in stage_fillers[step:step + 1]:
                fill()
            for n, (hh, d, u, rows, lanes) in enumerate(chains):
                if u == step:
                    x_ref = ff_ref if d == 0 else fb_ref
                    stage_a[n] = _hgrn_chunk_a(x_ref[rows, lanes], lb_ref[d, hh], b_ref.at[n], tri_ref[d])
            ready, pending = pending, []
            for n, (hh, d, u, rows, lanes) in enumerate(chains):
                if u != step - 1:
                    continue
                b, f, k = stage_a[n]
                idx = 2 * hh + d
                st = st_ref[idx] if u == 0 else states[idx]
                q0, stb, attn_b, vb, diag, st = _hgrn_chunk_b(b, f, k, q_ref[rows, lanes], v_ref[rows, lanes], st,
                                                              b_ref.at[n], mask_ref, d)
                states[idx] = st
                if u == U - 1:
                    st_ref[idx] = st
                pending.append((rows, lanes, q0, stb, attn_b, vb, diag))
            for rows, lanes, q0, stb, attn_b, vb, diag in ready:
                oacc_ref[rows, lanes] += (_dot_nt(q0, stb) + _dot(attn_b, vb)) + diag
        assert not pending

    if nc == U:
        run_trip(0, list(fillers))
    else:
        for fill in fillers:
            fill()

        def body(c, carry):
            run_trip(c, [])
            return carry

        lax.fori_loop(0, nc // U, body, 0)

    for hh in range(HGRN_HG):
        lanes = slice(hh * LANE, (hh + 1) * LANE)
        if s_out_ref is not None:
            for d in range(2):
                s_out_ref[d, hh] = st_ref[2 * hh + d].T
        o = oacc_ref[:, lanes]
        ms = jnp.mean(o * o, axis=-1, keepdims=True) * (q_scale * q_scale)
        gate = g_ref[:, lanes]
        y = o * (q_scale * lax.rsqrt(ms + EPS)) * gnw_ref[...] * (gate * _sigmoid(gate))
        o_ref[:, lanes] = y.astype(BF16)


def _hgrn_kernel(*refs, T, has_s0, emit_state, n_alias):
    n_in = 9 + int(has_s0)
    n_out = 2 if emit_state else 1
    assert len(refs) == n_in + n_alias + n_out + 3
    s0_ref = refs[9] if has_s0 else None
    outs = refs[n_in + n_alias:n_in + n_alias + n_out]
    _hgrn_unit(*refs[:9], s0_ref, outs[0], outs[1] if emit_state else None, *refs[-3:], T=T)


def _hgrn_io(st, proj, lbs, gnorm_w, tables, s0, layer, states_prev, unit):
    hg = HGRN_HG
    wide = hg * LANE
    T = st.seq
    C = HGRN_CHUNK

    def at(fn):
        return lambda *g: fn(*unit(*g))

    def col_spec(col0):
        return pl.BlockSpec((T, wide), at(lambda b, h: (b, col0 // hg + h)))

    tri, masks = tables
    in_specs = [col_spec(COL_HQ), col_spec(COL_HI), col_spec(COL_HFF), col_spec(COL_HFB), col_spec(COL_HG),
                pl.BlockSpec((2, None, hg, 1, LANE), at(lambda b, h: (0, layer, h, 0, 0))),
                pl.BlockSpec((None, 1, LANE), at(lambda b, h: (layer, 0, 0))),
                pl.BlockSpec(tri.shape, at(lambda b, h: (0, 0, 0))),
                pl.BlockSpec(masks.shape, at(lambda b, h: (0, 0, 0)))]
    args = [proj, proj, proj, proj, proj, lbs, gnorm_w, tri, masks]
    state_spec = pl.BlockSpec((None, None, 2, hg, HGRN_DK, HGRN_DV), at(lambda b, h: (b, layer, 0, h, 0, 0)))
    if st.latent:
        in_specs.append(state_spec)
        args.append(s0)
    out_specs = [pl.BlockSpec((T, wide), at(lambda b, h: (b, h)))]
    out_shape = [jax.ShapeDtypeStruct((_ntok(st), MIX_WIDTH), BF16)]
    aliases = {}
    if not st.latent:
        out_specs.append(state_spec)
        out_shape.append(jax.ShapeDtypeStruct((st.nseq, DEPTH, 2, HGRN_HEADS, HGRN_DK, HGRN_DV), F32))
        if states_prev is not None:
            aliases = {len(args): 1}
            in_specs.append(pl.BlockSpec(memory_space=pl.ANY))
            args.append(states_prev)
    scratch = [pltpu.VMEM((T, wide), F32), pltpu.VMEM((2 * hg * min(HGRN_UNROLL, T // C), C, LANE), F32),
               pltpu.VMEM((2 * hg, HGRN_DV, HGRN_DK), F32)]
    return in_specs, args, out_specs, out_shape, scratch, aliases


def _hgrn(st, proj, lbs, gnorm_w, tables, s0, layer, states_prev):
    in_specs, args, out_specs, out_shape, scratch, aliases = _hgrn_io(
        st, proj, lbs, gnorm_w, tables, s0, layer, states_prev, lambda b, h: (b, h))
    return pl.pallas_call(
        functools.partial(_hgrn_kernel, T=st.seq, has_s0=st.latent, emit_state=not st.latent, n_alias=len(aliases)),
        grid=(st.nseq, HGRN_HEADS // HGRN_HG),
        in_specs=in_specs,
        out_specs=out_specs,
        out_shape=out_shape,
        scratch_shapes=scratch,
        input_output_aliases=aliases,
        compiler_params=_cparams(("arbitrary", "arbitrary")),
        name="hgrn_" + st.name,
    )(*args)


def _inproj_hgrn_kernel(*refs, T, n_hgrn_in):
    x_ref, sh_ref, sc_ref, nw_ref, w_ref = refs[:5]
    hgrn_in = refs[5:5 + n_hgrn_in]
    o_ref, mix_ref, s_out_ref = refs[5 + n_hgrn_in:5 + n_hgrn_in + 3]
    h_ref = refs[-4]

    @pl.when(pl.program_id(1) == 0)
    def _():
        _norm_modulate(x_ref, h_ref, nw_ref, sh_ref, sc_ref)

    def panel(n):
        cols = slice(n * IN_PANEL, (n + 1) * IN_PANEL)

        def fill():
            o_ref[:, cols] = _dot(h_ref[...], w_ref[:, cols])

        return fill

    _hgrn_unit(*hgrn_in[:9], None, mix_ref, s_out_ref, *refs[-3:], T=T,
               fillers=[panel(n) for n in range(IN_TN // IN_PANEL)])


def _inproj_hgrn(st, x, mods4, layer, norm_w, w_bf16, st2, proj2, lbs, gnorm_w, tables, states_prev):
    assert st.latent and not st2.latent
    ntok = _ntok(st)
    ni, nj = ntok // IN_TM, IN_PROJ_WIDTH // IN_TN
    nh = HGRN_HEADS // HGRN_HG
    assert ni * nj == st2.nseq * nh

    def unit(i, j):
        s = i * nj + j
        return s // nh, s % nh

    h_in, h_args, h_out, h_shape, h_scratch, h_alias = _hgrn_io(st2, proj2, lbs, gnorm_w, tables, None, layer,
                                                                states_prev, unit)
    in_specs = [
        pl.BlockSpec((IN_TM, D_MODEL), lambda i, j: (i, 0)),
        _mod_spec(st, layer, 0, IN_TM, 2),
        _mod_spec(st, layer, 1, IN_TM, 2),
        _layer_row_spec(layer, D_MODEL, 2),
        pl.BlockSpec((D_MODEL, IN_TN), lambda i, j: (0, j)),
    ] + h_in
    args = [x, mods4, mods4, norm_w, w_bf16] + h_args
    return pl.pallas_call(
        functools.partial(_inproj_hgrn_kernel, T=st2.seq, n_hgrn_in=len(h_args)),
        grid=(ni, nj),
        in_specs=in_specs,
        out_specs=[pl.BlockSpec((IN_TM, IN_TN), lambda i, j: (i, j))] + h_out,
        out_shape=[jax.ShapeDtypeStruct((ntok, IN_PROJ_WIDTH), F32)] + h_shape,
        scratch_shapes=[pltpu.VMEM((IN_TM, D_MODEL), BF16)] + h_scratch,
        input_output_aliases={5 + a: 1 + o for a, o in h_alias.items()},
        compiler_params=_cparams(("arbitrary", "arbitrary")),
        name="inproj_" + st.name + "_hgrn_" + st2.name,
    )(*args)


def _conv_kernel(cb_ref, cc_ref, cx_ref, w_ref, mix_ref, o_ref, *, T):
    del mix_ref
    u = cc_ref[...] * cx_ref[...]
    row = lax.broadcasted_iota(jnp.int32, (T, CONV_WIDTH), 0)
    u_prev = jnp.where(row == 0, 0.0, pltpu.roll(u, 1, axis=0))
    u_next = jnp.where(row == T - 1, 0.0, pltpu.roll(u, T - 1, axis=0))
    y = u_prev * w_ref[0:1, :] + u * w_ref[1:2, :] + u_next * w_ref[2:3, :]
    o_ref[...] = (cb_ref[...] * y).astype(BF16)


def _conv(st, proj, conv_w, layer, mix):
    cw = CONV_WIDTH // LANE
    T = st.seq

    def col_spec(col0):
        return pl.BlockSpec((T, CONV_WIDTH), lambda b: (b, col0 // cw))

    return pl.pallas_call(
        functools.partial(_conv_kernel, T=T),
        grid=(st.nseq,),
        in_specs=[col_spec(COL_CB), col_spec(COL_CC), col_spec(COL_CX),
                  pl.BlockSpec((None, 3, CONV_WIDTH), lambda b: (layer, 0, 0)),
                  pl.BlockSpec(memory_space=pl.ANY)],
        out_specs=pl.BlockSpec((T, CONV_WIDTH), lambda b: (b, HGRN_WIDTH // CONV_WIDTH)),
        out_shape=jax.ShapeDtypeStruct(mix.shape, mix.dtype),
        input_output_aliases={4: 0},
        compiler_params=_cparams(("arbitrary",)),
        name="conv_" + st.name,
    )(proj, proj, proj, conv_w, mix)


MIX_COL_ATTN = (HGRN_WIDTH + CONV_WIDTH) // LANE


def _ctx_attn_kernel(q_ref, k_ref, v_ref, mix_ref, o_ref):
    del mix_ref
    scale = NA_HEAD_DIM ** -0.5
    heads = [slice(h * LANE, (h + 1) * LANE) for h in range(NA_HEADS)]
    scores = [_dot_nt(q_ref[:, hs].astype(BF16), k_ref[:, hs].astype(BF16)) * scale for hs in heads]
    probs = []
    for s in scores:
        p = jnp.exp(s - jnp.max(s, axis=-1, keepdims=True))
        probs.append((p.astype(BF16), jnp.sum(p, axis=-1, keepdims=True)))
    for hs, (p, l) in zip(heads, probs):
        o_ref[:, hs] = (_dot(p, v_ref[:, hs].astype(BF16)) / l).astype(BF16)


def _ctx_attn(proj, mix):
    def col_spec(col0):
        return pl.BlockSpec((SEQ, NA_WIDTH), lambda b: (b, col0 * LANE // NA_WIDTH))

    return pl.pallas_call(
        _ctx_attn_kernel,
        grid=(BATCH,),
        in_specs=[col_spec(COL_NQ), col_spec(COL_NK), col_spec(COL_NV), pl.BlockSpec(memory_space=pl.ANY)],
        out_specs=pl.BlockSpec((SEQ, NA_WIDTH), lambda b: (b, MIX_COL_ATTN * LANE // NA_WIDTH)),
        out_shape=jax.ShapeDtypeStruct(mix.shape, mix.dtype),
        input_output_aliases={3: 0},
        compiler_params=_cparams(("arbitrary",)),
        name="ctx_attn",
    )(proj, proj, proj, mix)


_KH = min(NA_KH, GRID_ROWS)
_ROW_START = [int(v) for v in np.clip(np.arange(GRID_ROWS) - _KH // 2, 0, GRID_ROWS - _KH)]
NLOC = _KH * GRID_W
RPB_ROWS = 2 * NA_KH - 1
RPB_COLS = 2 * NA_KW - 1


def _nat_kernel(q_ref, k_ref, v_ref, kc_ref, vc_ref, bias_ref, mix_ref, o_ref, sctx_ref, sloc_ref, pctx_ref,
                ploc_ref, den_ref):
    del mix_ref
    scale = NA_HEAD_DIM ** -0.5
    row_slices = [slice(r * GRID_W, (r + 1) * GRID_W) for r in range(GRID_ROWS)]
    bands = [slice(rs * GRID_W, rs * GRID_W + NLOC) for rs in _ROW_START]
    for h in range(NA_HEADS):
        lanes = slice(h * LANE, (h + 1) * LANE)
        q_all = q_ref[:, lanes].astype(BF16)
        k_all = k_ref[:, lanes].astype(BF16)
        v_all = v_ref[:, lanes].astype(BF16)
        sctx_ref[...] = _dot_nt(q_all, kc_ref[:, h, :].astype(BF16)) * scale
        for r, (rows, band) in enumerate(zip(row_slices, bands)):
            sloc_ref[rows, :] = _dot_nt(q_all[rows], k_all[band]) * scale + bias_ref[h, r - _ROW_START[r]]
        for rows in row_slices:
            s_loc = sloc_ref[rows, :]
            s_ctx = sctx_ref[rows, :]
            m = jnp.maximum(jnp.max(s_loc, axis=-1, keepdims=True), jnp.max(s_ctx, axis=-1, keepdims=True))
            p_loc = jnp.exp(s_loc - m)
            p_ctx = jnp.exp(s_ctx - m)
            den_ref[rows, :] = jnp.sum(p_loc, axis=-1, keepdims=True) + jnp.sum(p_ctx, axis=-1, keepdims=True)
            ploc_ref[rows, :] = p_loc.astype(BF16)
            pctx_ref[rows, :] = p_ctx.astype(BF16)
        o_ctx = _dot(pctx_ref[...], vc_ref[:, h, :].astype(BF16))
        for rows, band in zip(row_slices, bands):
            o = (_dot(ploc_ref[rows, :], v_all[band]) + o_ctx[rows]) / den_ref[rows, :]
            o_ref[rows, lanes] = o.astype(BF16)


def _nat_bias_kernel(rpb_ref, o_ref):
    base = (pl.program_id(0) * NA_HEADS + pl.program_id(1)) * (RPB_ROWS * RPB_COLS)
    q = lax.broadcasted_iota(jnp.int32, (GRID_W, GRID_W), 0)
    k = lax.broadcasted_iota(jnp.int32, (GRID_W, GRID_W), 1)
    col_start = jnp.clip(q - NA_KW // 2, 0, GRID_W - NA_KW)
    in_window = (k >= col_start) & (k < col_start + NA_KW)
    dc = jnp.clip(k - q + NA_KW - 1, 0, RPB_COLS - 1)
    tabs = []
    for dr in range(RPB_ROWS):
        t = jnp.zeros((GRID_W, GRID_W), F32)
        for c in range(RPB_COLS):
            t = jnp.where(dc == c, rpb_ref[base + dr * RPB_COLS + c], t)
        tabs.append(jnp.where(in_window, t, NEG_BIG))
    for off in range(_KH):
        for p in range(_KH // 2):
            pair = [tabs[kr - off + NA_KH - 1] for kr in (2 * p, 2 * p + 1)]
            o_ref[off, :, 2 * p * GRID_W:(2 * p + 2) * GRID_W] = jnp.concatenate(pair, axis=1)


def _nat_bias(na_rpb):
    return pl.pallas_call(
        _nat_bias_kernel,
        grid=(DEPTH, NA_HEADS),
        in_specs=[pl.BlockSpec(memory_space=pltpu.SMEM)],
        out_specs=pl.BlockSpec((None, None, _KH, GRID_W, NLOC), lambda l, h: (l, h, 0, 0, 0)),
        out_shape=jax.ShapeDtypeStruct((DEPTH, NA_HEADS, _KH, GRID_W, NLOC), F32),
        compiler_params=_cparams(("arbitrary", "arbitrary")),
        name="nat_bias",
    )(na_rpb.reshape(-1))


def _nat(proj, cache_k, cache_v, bias, layer, mix):
    def col_spec(col0):
        return pl.BlockSpec((DEC_SEQ, NA_WIDTH), lambda b: (b, col0 * LANE // NA_WIDTH))

    cache_spec = pl.BlockSpec((None, None, PAST_LEN, NA_HEADS, NA_HEAD_DIM), lambda b: (b, layer, 0, 0, 0))
    return pl.pallas_call(
        _nat_kernel,
        grid=(DEC_BATCH,),
        in_specs=[col_spec(COL_NQ), col_spec(COL_NK), col_spec(COL_NV), cache_spec, cache_spec,
                  pl.BlockSpec((None, NA_HEADS, _KH, GRID_W, NLOC), lambda b: (layer, 0, 0, 0, 0)),
                  pl.BlockSpec(memory_space=pl.ANY)],
        out_specs=pl.BlockSpec((DEC_SEQ, NA_WIDTH), lambda b: (b, MIX_COL_ATTN * LANE // NA_WIDTH)),
        out_shape=jax.ShapeDtypeStruct(mix.shape, mix.dtype),
        scratch_shapes=[pltpu.VMEM((DEC_SEQ, PAST_LEN), F32), pltpu.VMEM((DEC_SEQ, NLOC), F32),
                        pltpu.VMEM((DEC_SEQ, PAST_LEN), BF16), pltpu.VMEM((DEC_SEQ, NLOC), BF16),
                        pltpu.VMEM((DEC_SEQ, 1), F32)],
        input_output_aliases={6: 0},
        compiler_params=_cparams(("arbitrary",)),
        name="nat",
    )(proj, proj, proj, cache_k, cache_v, bias, mix)


def kernel(x_prompt, x_sample, cache_na_k, cache_na_v, state_hgrn, c, c_ctx, w_ada, b_ada, norm_mix_w, w_in,
           hgrn_lb_raw, hgrn_gnorm_w, conv_w, na_rpb, w_out, norm_ffn_w, w_ffn_gate, w_ffn_up, w_ffn_down,
           final_norm_w):
    xs = {LATENT: x_sample.reshape(_ntok(LATENT), D_MODEL), CONTEXT: x_prompt.reshape(_ntok(CONTEXT), D_MODEL)}
    cvecs = jnp.concatenate([c, c_ctx[None, :], jnp.zeros((MOD_ROWS - DEC_BATCH - 1, D_MODEL), F32)], axis=0)
    mods4 = _adaln(cvecs, w_ada, b_ada).reshape(DEPTH, MOD_ROWS, 1, 6 * D_MODEL)

    p_lb = jax.nn.softmax(hgrn_lb_raw.astype(F32), axis=1)
    cp = jnp.cumsum(p_lb, axis=1)
    lbs = (cp - cp[:, :1]).reshape(2, DEPTH, HGRN_HEADS, 1, LANE)
    gnorm_w = hgrn_gnorm_w.reshape(DEPTH, 1, LANE)
    norm_mix = norm_mix_w.reshape(DEPTH, 1, D_MODEL)
    norm_ffn = norm_ffn_w.reshape(DEPTH, 1, D_MODEL)

    w_out_b = w_out.astype(BF16)

    nat_bias = _nat_bias(na_rpb)
    tables = _hgrn_tables()

    kv = None
    states = None
    for l in range(DEPTH):
        last = l == DEPTH - 1
        proj_c, k_c, v_c, w_in_l = _inproj(CONTEXT, xs[CONTEXT], mods4, l, norm_mix, w_in, kv, cast_weights=True,
                                           nblocks=1)
        proj_c, *kv = _inproj(CONTEXT, xs[CONTEXT], mods4, l, norm_mix, w_in_l, (k_c, v_c), block0=1,
                              nblocks=_ntok(CONTEXT) // IN_TM - 1, proj_prev=proj_c)
        proj_l, mix_c, states = _inproj_hgrn(LATENT, xs[LATENT], mods4, l, norm_mix, w_in_l, CONTEXT, proj_c, lbs,
                                             gnorm_w, tables, states)

        mix_l = _hgrn(LATENT, proj_l, lbs, gnorm_w, tables, state_hgrn, l, None)[0]
        mix_l = _conv(LATENT, proj_l, conv_w, l, mix_l)
        mix_l = _nat(proj_l, cache_na_k, cache_na_v, nat_bias, l, mix_l)
        x1 = _outproj(LATENT, xs[LATENT], mix_l, mods4, l, w_out_b)
        y, *ffn_w = _ffn(LATENT, x1, mods4, l, norm_ffn, final_norm_w, (w_ffn_gate, w_ffn_up, w_ffn_down), last,
                         cast_weights=True, nblocks=1)
        xs[LATENT] = _ffn(LATENT, x1, mods4, l, norm_ffn, final_norm_w, ffn_w, last, block0=1,
                          nblocks=_ntok(LATENT) // FFN_TM - 1, out_prev=y)[0]

        mix_c = _conv(CONTEXT, proj_c, conv_w, l, mix_c)
        mix_c = _ctx_attn(proj_c, mix_c)
        x1 = _outproj(CONTEXT, xs[CONTEXT], mix_c, mods4, l, w_out_b)
        xs[CONTEXT] = _ffn(CONTEXT, x1, mods4, l, norm_ffn, final_norm_w, ffn_w, last)[0]

    y_sample = xs[LATENT].reshape(DEC_BATCH, DEC_SEQ, D_MODEL)
    y_prompt = xs[CONTEXT].reshape(BATCH, SEQ, D_MODEL)
    new_k = kv[0].reshape(BATCH, DEPTH, SEQ, NA_HEADS, NA_HEAD_DIM)
    new_v = kv[1].reshape(BATCH, DEPTH, SEQ, NA_HEADS, NA_HEAD_DIM)
    return (y_prompt, y_sample, new_k, new_v, states)
```

```python
import collections
import functools

import numpy as np
import jax
import jax.numpy as jnp
from jax import lax
from jax.experimental import pallas as pl
from jax.experimental.pallas import tpu as pltpu

F32 = jnp.float32
BF16 = jnp.bfloat16

D_MODEL = 2048
BATCH = 16
SEQ = 256
DEPTH = 2
DEC_BATCH = 8
DEC_SEQ = 1024
PAST_LEN = 512
GRID_W = 64
HGRN_HEADS = 8
HGRN_DK = 128
HGRN_DV = 128
HGRN_WIDTH = HGRN_HEADS * HGRN_DV
CONV_WIDTH = 512
NA_HEADS = 4
NA_HEAD_DIM = 128
NA_WIDTH = NA_HEADS * NA_HEAD_DIM
NA_KH = 8
NA_KW = 16
MIX_WIDTH = HGRN_WIDTH + CONV_WIDTH + NA_WIDTH
IN_PROJ_WIDTH = 5 * HGRN_WIDTH + 3 * CONV_WIDTH + 3 * NA_WIDTH
FFN_HIDDEN = ((8 * D_MODEL + 3 * 256 - 1) // (3 * 256)) * 256
EPS = 1e-6

MOD_ROWS = 16
CTX_ROW = DEC_BATCH
LANE = 128
SUBLANE = 8
GRID_ROWS = DEC_SEQ // GRID_W

COL_HQ = 0
COL_HI = HGRN_WIDTH // LANE
COL_HFF = 2 * HGRN_WIDTH // LANE
COL_HFB = 3 * HGRN_WIDTH // LANE
COL_HG = 4 * HGRN_WIDTH // LANE
COL_CB = 5 * HGRN_WIDTH // LANE
COL_CC = COL_CB + CONV_WIDTH // LANE
COL_CX = COL_CC + CONV_WIDTH // LANE
COL_NQ = COL_CX + CONV_WIDTH // LANE
COL_NK = COL_NQ + NA_WIDTH // LANE
COL_NV = COL_NK + NA_WIDTH // LANE

NEG_BIG = -1e30
VMEM_LIMIT = 60 * 1024 * 1024

Stream = collections.namedtuple("Stream", ["name", "seq", "nseq", "latent"])
LATENT = Stream("lat", DEC_SEQ, DEC_BATCH, True)
CONTEXT = Stream("ctx", SEQ, BATCH, False)


def _ntok(st):
    return st.seq * st.nseq


def _cparams(sem):
    return pltpu.CompilerParams(dimension_semantics=sem, vmem_limit_bytes=VMEM_LIMIT)


def _dot(a, b):
    return jnp.dot(a, b, preferred_element_type=F32)


def _dot_nt(a, b):
    return lax.dot_general(a, b, (((1,), (1,)), ((), ())), preferred_element_type=F32)


def _dot_tn(a, b):
    return lax.dot_general(a, b, (((0,), (0,)), ((), ())), preferred_element_type=F32)


def _sigmoid(x):
    return 1.0 / (1.0 + jnp.exp(-x))


ADA_TN = 1024


def _adaln_kernel(c_ref, w_ref, b_ref, o_ref):
    c = c_ref[...]
    s = (c * _sigmoid(c)).astype(BF16)
    o_ref[...] = _dot(s, w_ref[...].astype(BF16)) + b_ref[...]


def _adaln(cvecs, w_ada, b_ada):
    n = 6 * D_MODEL
    return pl.pallas_call(
        _adaln_kernel,
        grid=(DEPTH, n // ADA_TN),
        in_specs=[
            pl.BlockSpec((MOD_ROWS, D_MODEL), lambda l, j: (0, 0)),
            pl.BlockSpec((None, D_MODEL, ADA_TN), lambda l, j: (l, 0, j)),
            pl.BlockSpec((None, 1, ADA_TN), lambda l, j: (l, 0, j)),
        ],
        out_specs=pl.BlockSpec((None, MOD_ROWS, ADA_TN), lambda l, j: (l, 0, j)),
        out_shape=jax.ShapeDtypeStruct((DEPTH, MOD_ROWS, n), F32),
        compiler_params=_cparams(("arbitrary", "arbitrary")),
        name="adaln",
    )(cvecs, w_ada, b_ada.reshape(DEPTH, 1, n))


IN_TM = 1024
IN_TN = 1024
IN_PANEL = 256
TM = 512
FFN_TM = 1024
FFN_TF = 512


def _mod_spec(st, layer, chunk, tm, ngrid, block0=0):
    def row(i):
        return ((block0 + i) * tm) // DEC_SEQ if st.latent else CTX_ROW

    if ngrid == 1:
        return pl.BlockSpec((None, None, 1, D_MODEL), lambda i: (layer, row(i), 0, chunk))
    return pl.BlockSpec((None, None, 1, D_MODEL), lambda i, j: (layer, row(i), 0, chunk))


PROLOGUE_ROWS = 128


def _norm_modulate(x_ref, h_ref, nw_ref, sh_ref, sc_ref):
    gain = nw_ref[...] * (1.0 + sc_ref[...])
    shift = sh_ref[...]

    def chunk(r, carry):
        rows = pl.ds(pl.multiple_of(r * PROLOGUE_ROWS, PROLOGUE_ROWS), PROLOGUE_ROWS)
        x = x_ref[rows, :]
        ms = jnp.mean(x * x, axis=-1, keepdims=True)
        h_ref[rows, :] = (x * lax.rsqrt(ms + EPS) * gain + shift).astype(BF16)
        return carry

    lax.fori_loop(0, x_ref.shape[0] // PROLOGUE_ROWS, chunk, 0)


IN_CAST_TN = 512


def _inproj_kernel(*refs, emit_kv, cast_weights, n_alias):
    x_ref, sh_ref, sc_ref, nw_ref, w_ref = refs[:5]
    outs = refs[5 + n_alias:-1]
    o_ref = outs[0]
    h_ref = refs[-1]
    j = pl.program_id(1)
    nj = pl.num_programs(1)

    @pl.when(j == 0)
    def _():
        _norm_modulate(x_ref, h_ref, nw_ref, sh_ref, sc_ref)

    if cast_weights:
        w = w_ref[...].astype(BF16)
        outs[-1][...] = w
    else:
        w = w_ref[...]
    o_ref[...] = _dot(h_ref[...], w)

    if emit_kv:
        k_ref, v_ref = outs[1], outs[2]
        tn = o_ref.shape[1]
        if tn == 2 * NA_WIDTH:
            @pl.when(j == nj - 1)
            def _():
                k_ref[...] = o_ref[:, :NA_WIDTH].reshape(k_ref.shape)
                v_ref[...] = o_ref[:, NA_WIDTH:].reshape(v_ref.shape)
        else:
            assert tn == NA_WIDTH

            @pl.when(j == nj - 2)
            def _():
                k_ref[...] = o_ref[...].reshape(k_ref.shape)

            @pl.when(j == nj - 1)
            def _():
                v_ref[...] = o_ref[...].reshape(v_ref.shape)


def _layer_row_spec(layer, width, ngrid):
    if ngrid == 1:
        return pl.BlockSpec((None, 1, width), lambda i: (layer, 0, 0))
    return pl.BlockSpec((None, 1, width), lambda i, j: (layer, 0, 0))


def _inproj(st, x, mods4, layer, norm_w, w, kv_prev, *, cast_weights=False, block0=0, nblocks=None, proj_prev=None):
    emit_kv = not st.latent
    tn = IN_CAST_TN if cast_weights else IN_TN
    assert COL_NK * LANE == IN_PROJ_WIDTH - 2 * NA_WIDTH and tn in (NA_WIDTH, 2 * NA_WIDTH)
    ntok = _ntok(st)
    nblocks = ntok // IN_TM if nblocks is None else nblocks
    if cast_weights:
        w_spec = pl.BlockSpec((None, D_MODEL, tn), lambda i, j: (layer, 0, j))
    else:
        w_spec = pl.BlockSpec((D_MODEL, tn), lambda i, j: (0, j))
    in_specs = [
        pl.BlockSpec((IN_TM, D_MODEL), lambda i, j: (block0 + i, 0)),
        _mod_spec(st, layer, 0, IN_TM, 2, block0),
        _mod_spec(st, layer, 1, IN_TM, 2, block0),
        _layer_row_spec(layer, D_MODEL, 2),
        w_spec,
    ]
    args = [x, mods4, mods4, norm_w, w]
    out_specs = [pl.BlockSpec((IN_TM, tn), lambda i, j: (block0 + i, j))]
    out_shape = [jax.ShapeDtypeStruct((ntok, IN_PROJ_WIDTH), F32)]
    aliases = {}
    if proj_prev is not None:
        aliases[len(args)] = 0
        in_specs.append(pl.BlockSpec(memory_space=pl.ANY))
        args.append(proj_prev)
    if emit_kv:
        nb = IN_TM // st.seq
        kv_spec = pl.BlockSpec((nb, None, st.seq, NA_WIDTH), lambda i, j: (block0 + i, layer, 0, 0))
        out_specs += [kv_spec, kv_spec]
        out_shape += [jax.ShapeDtypeStruct((st.nseq, DEPTH, st.seq, NA_WIDTH), F32)] * 2
        if kv_prev is not None:
            aliases.update({len(args): 1, len(args) + 1: 2})
            in_specs += [pl.BlockSpec(memory_space=pl.ANY)] * 2
            args += list(kv_prev)
    if cast_weights:
        out_specs.append(pl.BlockSpec((D_MODEL, tn), lambda i, j: (0, j)))
        out_shape.append(jax.ShapeDtypeStruct((D_MODEL, IN_PROJ_WIDTH), BF16))
    return pl.pallas_call(
        functools.partial(_inproj_kernel, emit_kv=emit_kv, cast_weights=cast_weights, n_alias=len(aliases)),
        grid=(nblocks, IN_PROJ_WIDTH // tn),
        in_specs=in_specs,
        out_specs=out_specs,
        out_shape=out_shape,
        scratch_shapes=[pltpu.VMEM((IN_TM, D_MODEL), BF16)],
        input_output_aliases=aliases,
        compiler_params=_cparams(("arbitrary", "arbitrary")),
        name="inproj_" + st.name + ("_cast" if cast_weights else ""),
    )(*args)


def _outproj_kernel(x_ref, m_ref, g_ref, w_ref, o_ref):
    o_ref[...] = x_ref[...] + g_ref[...] * _dot(m_ref[...], w_ref[...])


def _outproj(st, x, mix, mods4, layer, w_bf16):
    ntok = _ntok(st)
    return pl.pallas_call(
        _outproj_kernel,
        grid=(ntok // TM,),
        in_specs=[
            pl.BlockSpec((TM, D_MODEL), lambda i: (i, 0)),
            pl.BlockSpec((TM, MIX_WIDTH), lambda i: (i, 0)),
            _mod_spec(st, layer, 2, TM, 1),
            pl.BlockSpec((None, MIX_WIDTH, D_MODEL), lambda i: (layer, 0, 0)),
        ],
        out_specs=pl.BlockSpec((TM, D_MODEL), lambda i: (i, 0)),
        out_shape=jax.ShapeDtypeStruct((ntok, D_MODEL), F32),
        compiler_params=_cparams(("arbitrary",)),
        name="outproj_" + st.name,
    )(x, mix, mods4, w_bf16)


FFN_CAST_TF = 256
FFN_PANEL = 256


def _ffn_prologue(x_ref, h_ref, o_ref, nw_ref, sh_ref, sc_ref):
    _norm_modulate(x_ref, h_ref, nw_ref, sh_ref, sc_ref)
    o_ref[...] = jnp.zeros_like(o_ref)


def _ffn_fillers(h_ref, wg, wu, wd, o_ref):
    hidden = []

    def gate_up(n):
        def fill():
            h = h_ref[...]
            a = _dot(h, wg[:, n:n + FFN_PANEL])
            u = _dot(h, wu[:, n:n + FFN_PANEL])
            hidden.append((a * _sigmoid(a) * u).astype(BF16))
        return fill

    def down(n):
        def fill():
            t = hidden[0] if len(hidden) == 1 else jnp.concatenate(hidden, axis=1)
            o_ref[:, n:n + FFN_TF] += _dot(t, wd[:, n:n + FFN_TF])
        return fill

    return [gate_up(n) for n in range(0, wg.shape[1], FFN_PANEL)] + [down(n) for n in range(0, D_MODEL, FFN_TF)]


def _ffn_epilogue(x_ref, o_ref, g_ref, fw_ref, final_norm):
    gate = g_ref[...]

    def chunk(r, carry):
        rows = pl.ds(pl.multiple_of(r * PROLOGUE_ROWS, PROLOGUE_ROWS), PROLOGUE_ROWS)
        y = x_ref[rows, :] + gate * o_ref[rows, :]
        if final_norm:
            ms = jnp.mean(y * y, axis=-1, keepdims=True)
            y = y * lax.rsqrt(ms + EPS) * fw_ref[...]
        o_ref[rows, :] = y
        return carry

    lax.fori_loop(0, x_ref.shape[0] // PROLOGUE_ROWS, chunk, 0)


def _ffn_kernel(*refs, final_norm, cast_weights, n_alias):
    x_ref, sh_ref, sc_ref, g_ref, nw_ref, fw_ref, wg_ref, wu_ref, wd_ref = refs[:9]
    outs = refs[9 + n_alias:-1]
    o_ref = outs[0]
    h_ref = refs[-1]
    j = pl.program_id(1)

    @pl.when(j == 0)
    def _():
        _ffn_prologue(x_ref, h_ref, o_ref, nw_ref, sh_ref, sc_ref)

    if cast_weights:
        wg, wu, wd = (w[...].astype(BF16) for w in (wg_ref, wu_ref, wd_ref))
        for w, w_out_ref in zip((wg, wu, wd), outs[1:]):
            w_out_ref[...] = w
    else:
        wg, wu, wd = wg_ref[...], wu_ref[...], wd_ref[...]
    for fill in _ffn_fillers(h_ref, wg, wu, wd, o_ref):
        fill()

    @pl.when(j == pl.num_programs(1) - 1)
    def _():
        _ffn_epilogue(x_ref, o_ref, g_ref, fw_ref, final_norm)


def _ffn(st, x, mods4, layer, norm_w, final_w, weights, final_norm, *, cast_weights=False, block0=0, nblocks=None,
         out_prev=None, tm=FFN_TM):
    ntok = _ntok(st)
    TM = tm
    tf = FFN_CAST_TF if cast_weights else FFN_TF
    nblocks = ntok // TM if nblocks is None else nblocks
    if cast_weights:
        w_specs = [pl.BlockSpec((None, D_MODEL, tf), lambda i, j: (layer, 0, j)),
                   pl.BlockSpec((None, D_MODEL, tf), lambda i, j: (layer, 0, j)),
                   pl.BlockSpec((None, tf, D_MODEL), lambda i, j: (layer, j, 0))]
    else:
        w_specs = [pl.BlockSpec((D_MODEL, tf), lambda i, j: (0, j)),
                   pl.BlockSpec((D_MODEL, tf), lambda i, j: (0, j)),
                   pl.BlockSpec((tf, D_MODEL), lambda i, j: (j, 0))]
    in_specs = [
        pl.BlockSpec((TM, D_MODEL), lambda i, j: (block0 + i, 0)),
        _mod_spec(st, layer, 3, TM, 2, block0),
        _mod_spec(st, layer, 4, TM, 2, block0),
        _mod_spec(st, layer, 5, TM, 2, block0),
        _layer_row_spec(layer, D_MODEL, 2),
        pl.BlockSpec((1, D_MODEL), lambda i, j: (0, 0)),
    ] + w_specs
    args = [x, mods4, mods4, mods4, norm_w, final_w.reshape(1, D_MODEL)] + list(weights)
    out_specs = [pl.BlockSpec((TM, D_MODEL), lambda i, j: (block0 + i, 0))]
    out_shape = [jax.ShapeDtypeStruct((ntok, D_MODEL), F32)]
    if cast_weights:
        out_specs += [pl.BlockSpec((D_MODEL, tf), lambda i, j: (0, j)),
                      pl.BlockSpec((D_MODEL, tf), lambda i, j: (0, j)),
                      pl.BlockSpec((tf, D_MODEL), lambda i, j: (j, 0))]
        out_shape += [jax.ShapeDtypeStruct((D_MODEL, FFN_HIDDEN), BF16)] * 2
        out_shape += [jax.ShapeDtypeStruct((FFN_HIDDEN, D_MODEL), BF16)]
    aliases = {}
    if out_prev is not None:
        aliases = {len(args): 0}
        in_specs.append(pl.BlockSpec(memory_space=pl.ANY))
        args.append(out_prev)
    return pl.pallas_call(
        functools.partial(_ffn_kernel, final_norm=final_norm, cast_weights=cast_weights, n_alias=len(aliases)),
        grid=(nblocks, FFN_HIDDEN // tf),
        in_specs=in_specs,
        out_specs=out_specs,
        out_shape=out_shape,
        scratch_shapes=[pltpu.VMEM((TM, D_MODEL), BF16)],
        input_output_aliases=aliases,
        compiler_params=_cparams(("arbitrary", "arbitrary")),
        name="ffn_" + st.name + ("_cast" if cast_weights else ""),
    )(*args)


HGRN_CHUNK = 64
HGRN_HG = 2
HGRN_UNROLL = 16
HGRN_LEVELS = [HGRN_CHUNK >> (i + 1) for i in range(HGRN_CHUNK.bit_length() - 1)]


LOG2E = 1.4426950408889634


def _hgrn_gates(x, lb):
    e = jnp.exp(-jnp.abs(x))
    r = 1.0 / (1.0 + e)
    er = e * r
    pos = x >= 0.0
    sig = jnp.where(pos, r, er)
    nsig = jnp.where(pos, er, r)
    ls2 = jnp.minimum(x, 0.0) * LOG2E + jnp.log2(r)
    f = lb + (1.0 - lb) * sig
    log2_f = jnp.where(lb > 0.0, jnp.log2(f), ls2)
    return log2_f, f, (1.0 - lb) * nsig


def _hgrn_chunk_a(x, lb, b_ref, tri):
    g, f, k = _hgrn_gates(x, lb)
    g_hi = g.astype(BF16)
    g_lo = (g - g_hi.astype(F32)).astype(BF16)
    b = _dot(tri, jnp.concatenate([g_hi, g_lo], axis=0))
    b_ref[...] = b
    return b, f, k


def _hgrn_chunk_b(b, f, k, q, v, st, b_ref, mask_ref, dirn):
    C = HGRN_CHUNK
    fwd = dirn == 0
    b_end = b_ref[pl.ds(C - 1 if fwd else 0, 1), :]
    vb = v.astype(BF16)
    q0 = (q * jnp.exp2(b)).astype(BF16)
    stb = st.astype(BF16)
    kd = (k * jnp.exp2(b_end - b)).astype(BF16)
    st_new = jnp.exp2(b_end) * st + _dot_tn(vb, kd)

    row = lax.broadcasted_iota(jnp.int32, (C, LANE), 0)
    sub = lax.broadcasted_iota(jnp.int32, (SUBLANE, LANE), 0)
    attn = [None] * (C // SUBLANE)

    def add_rows(first_row, term):
        for t in range(term.shape[0] // SUBLANE):
            blk = first_row // SUBLANE + t
            piece = term[t * SUBLANE:(t + 1) * SUBLANE]
            attn[blk] = piece if attn[blk] is None else attn[blk] + piece

    for lvl, m in enumerate(HGRN_LEVELS):
        mask_idx = dirn * len(HGRN_LEVELS) + lvl
        if m >= SUBLANE:
            parts, readers = [], []
            for p in range(C // (2 * m)):
                lo = p * 2 * m
                left = slice(lo, lo + m)
                right = slice(lo + m, lo + 2 * m)
                beta = b_ref[pl.ds(lo + (m - 1 if fwd else m), 1), :]
                if fwd:
                    parts += [k[left] * jnp.exp2(beta - b[left]), q[right] * jnp.exp2(b[right] - beta)]
                    readers.append((lo + m, parts[-1]))
                else:
                    parts += [q[left] * jnp.exp2(b[left] - beta), k[right] * jnp.exp2(beta - b[right])]
                    readers.append((lo, parts[-2]))
            wb = jnp.concatenate(parts, axis=0).astype(BF16)
            wq = jnp.concatenate([r for _, r in readers], axis=0).astype(BF16)
            term = _dot_nt(wq, wb)
            for n, (first_row, _) in enumerate(readers):
                add_rows(first_row, term[n * m:(n + 1) * m] * mask_ref[mask_idx, first_row:first_row + m, :])
        else:
            reads = ((row & m) != 0) if fwd else ((row & m) == 0)
            if m == 1:
                w = jnp.where(reads, q * f, k)
            else:
                betas = []
                for u in range(C // SUBLANE):
                    rows = []
                    for p in range(SUBLANE // (2 * m)):
                        rows.append(b_ref[pl.ds(u * SUBLANE + p * 2 * m + (m - 1 if fwd else m), 1), :])
                    beta_u = rows[-1]
                    for p in range(len(rows) - 2, -1, -1):
                        beta_u = jnp.where(sub < (p + 1) * 2 * m, rows[p], beta_u)
                    betas.append(jnp.broadcast_to(beta_u, (SUBLANE, LANE)))
                beta = jnp.concatenate(betas, axis=0)
                w = jnp.where(reads, q, k) * jnp.exp2(-jnp.abs(b - beta))
            wb = w.astype(BF16)
            add_rows(0, _dot_nt(wb, wb) * mask_ref[mask_idx])
    attn_b = jnp.concatenate(attn, axis=0).astype(BF16)
    diag = jnp.sum(q * k, axis=-1, keepdims=True) * v
    return q0, stb, attn_b, vb, diag, st_new


def _hgrn_tables():
    C = HGRN_CHUNK
    row, col = np.indices((C, C))
    tri = np.stack([col <= row, col >= row])
    tri = np.concatenate([tri, tri], axis=2)
    fwd, bwd = [], []
    for m in HGRN_LEVELS:
        same = (row // (2 * m)) == (col // (2 * m))
        fwd.append(same & ((row & m) != 0) & ((col & m) == 0))
        bwd.append(same & ((row & m) == 0) & ((col & m) != 0))
    return jnp.asarray(tri, BF16), jnp.asarray(np.stack(fwd + bwd), F32)


def _hgrn_unit(q_ref, v_ref, ff_ref, fb_ref, g_ref, lb_ref, gnw_ref, tri_ref, mask_ref, s0_ref, o_ref, s_out_ref,
               oacc_ref, b_ref, st_ref, *, T, fillers=()):
    C = HGRN_CHUNK
    U = min(HGRN_UNROLL, T // C)
    nc = T // C
    assert nc % U == 0 and len(fillers) <= U + 2
    q_scale = HGRN_DK ** -0.5

    oacc_ref[...] = jnp.zeros_like(oacc_ref)
    for hh in range(HGRN_HG):
        for d in range(2):
            if s0_ref is not None:
                st_ref[2 * hh + d] = s0_ref[d, hh].T
            else:
                st_ref[2 * hh + d] = jnp.zeros((HGRN_DV, HGRN_DK), F32)

    def run_trip(c, stage_fillers):
        chains = []
        for hh in range(HGRN_HG):
            lanes = slice(hh * LANE, (hh + 1) * LANE)
            for d in range(2):
                for u in range(U):
                    cc = c * U + u if d == 0 else nc - 1 - (c * U + u)
                    rows = pl.ds(pl.multiple_of(cc * C, C), C)
                    chains.append((hh, d, u, rows, lanes))
        stage_a = {}
        states = {}
        pending = []
        for step in range(U + 2):
            for fill in stage_fillers[step:step + 1]:
                fill()
            for n, (hh, d, u, rows, lanes) in enumerate(chains):
                if u == step:
                    x_ref = ff_ref if d == 0 else fb_ref
                    stage_a[n] = _hgrn_chunk_a(x_ref[rows, lanes], lb_ref[d, hh], b_ref.at[n], tri_ref[d])
            ready, pending = pending, []
            for n, (hh, d, u, rows, lanes) in enumerate(chains):
                if u != step - 1:
                    continue
                b, f, k = stage_a[n]
                idx = 2 * hh + d
                st = st_ref[idx] if u == 0 else states[idx]
                q0, stb, attn_b, vb, diag, st = _hgrn_chunk_b(b, f, k, q_ref[rows, lanes], v_ref[rows, lanes], st,
                                                              b_ref.at[n], mask_ref, d)
                states[idx] = st
                if u == U - 1:
                    st_ref[idx] = st
                pending.append((rows, lanes, q0, stb, attn_b, vb, diag))
            for rows, lanes, q0, stb, attn_b, vb, diag in ready:
                oacc_ref[rows, lanes] += (_dot_nt(q0, stb) + _dot(attn_b, vb)) + diag
        assert not pending

    if nc == U:
        run_trip(0, list(fillers))
    else:
        for fill in fillers:
            fill()

        def body(c, carry):
            run_trip(c, [])
            return carry

        lax.fori_loop(0, nc // U, body, 0)

    for hh in range(HGRN_HG):
        lanes = slice(hh * LANE, (hh + 1) * LANE)
        if s_out_ref is not None:
            for d in range(2):
                s_out_ref[d, hh] = st_ref[2 * hh + d].T
        o = oacc_ref[:, lanes]
        ms = jnp.mean(o * o, axis=-1, keepdims=True) * (q_scale * q_scale)
        gate = g_ref[:, lanes]
        y = o * (q_scale * lax.rsqrt(ms + EPS)) * gnw_ref[...] * (gate * _sigmoid(gate))
        o_ref[:, lanes] = y.astype(BF16)


def _hgrn_kernel(*refs, T, has_s0, emit_state, n_alias):
    n_in = 9 + int(has_s0)
    n_out = 2 if emit_state else 1
    assert len(refs) == n_in + n_alias + n_out + 3
    s0_ref = refs[9] if has_s0 else None
    outs = refs[n_in + n_alias:n_in + n_alias + n_out]
    _hgrn_unit(*refs[:9], s0_ref, outs[0], outs[1] if emit_state else None, *refs[-3:], T=T)


def _hgrn_io(st, proj, lbs, gnorm_w, tables, s0, layer, states_prev, unit):
    hg = HGRN_HG
    wide = hg * LANE
    T = st.seq
    C = HGRN_CHUNK

    def at(fn):
        return lambda *g: fn(*unit(*g))

    def col_spec(col0):
        return pl.BlockSpec((T, wide), at(lambda b, h: (b, col0 // hg + h)))

    tri, masks = tables
    in_specs = [col_spec(COL_HQ), col_spec(COL_HI), col_spec(COL_HFF), col_spec(COL_HFB), col_spec(COL_HG),
                pl.BlockSpec((2, None, hg, 1, LANE), at(lambda b, h: (0, layer, h, 0, 0))),
                pl.BlockSpec((None, 1, LANE), at(lambda b, h: (layer, 0, 0))),
                pl.BlockSpec(tri.shape, at(lambda b, h: (0, 0, 0))),
                pl.BlockSpec(masks.shape, at(lambda b, h: (0, 0, 0)))]
    args = [proj, proj, proj, proj, proj, lbs, gnorm_w, tri, masks]
    state_spec = pl.BlockSpec((None, None, 2, hg, HGRN_DK, HGRN_DV), at(lambda b, h: (b, layer, 0, h, 0, 0)))
    if st.latent:
        in_specs.append(state_spec)
        args.append(s0)
    out_specs = [pl.BlockSpec((T, wide), at(lambda b, h: (b, h)))]
    out_shape = [jax.ShapeDtypeStruct((_ntok(st), MIX_WIDTH), BF16)]
    aliases = {}
    if not st.latent:
        out_specs.append(state_spec)
        out_shape.append(jax.ShapeDtypeStruct((st.nseq, DEPTH, 2, HGRN_HEADS, HGRN_DK, HGRN_DV), F32))
        if states_prev is not None:
            aliases = {len(args): 1}
            in_specs.append(pl.BlockSpec(memory_space=pl.ANY))
            args.append(states_prev)
    scratch = [pltpu.VMEM((T, wide), F32), pltpu.VMEM((2 * hg * min(HGRN_UNROLL, T // C), C, LANE), F32),
               pltpu.VMEM((2 * hg, HGRN_DV, HGRN_DK), F32)]
    return in_specs, args, out_specs, out_shape, scratch, aliases


def _hgrn(st, proj, lbs, gnorm_w, tables, s0, layer, states_prev):
    in_specs, args, out_specs, out_shape, scratch, aliases = _hgrn_io(
        st, proj, lbs, gnorm_w, tables, s0, layer, states_prev, lambda b, h: (b, h))
    return pl.pallas_call(
        functools.partial(_hgrn_kernel, T=st.seq, has_s0=st.latent, emit_state=not st.latent, n_alias=len(aliases)),
        grid=(st.nseq, HGRN_HEADS // HGRN_HG),
        in_specs=in_specs,
        out_specs=out_specs,
        out_shape=out_shape,
        scratch_shapes=scratch,
        input_output_aliases=aliases,
        compiler_params=_cparams(("arbitrary", "arbitrary")),
        name="hgrn_" + st.name,
    )(*args)


def _inproj_hgrn_kernel(*refs, T, n_hgrn_in):
    x_ref, sh_ref, sc_ref, nw_ref, w_ref = refs[:5]
    hgrn_in = refs[5:5 + n_hgrn_in]
    o_ref, mix_ref, s_out_ref = refs[5 + n_hgrn_in:5 + n_hgrn_in + 3]
    h_ref = refs[-4]

    @pl.when(pl.program_id(1) == 0)
    def _():
        _norm_modulate(x_ref, h_ref, nw_ref, sh_ref, sc_ref)

    def panel(n):
        cols = slice(n * IN_PANEL, (n + 1) * IN_PANEL)

        def fill():
            o_ref[:, cols] = _dot(h_ref[...], w_ref[:, cols])

        return fill

    _hgrn_unit(*hgrn_in[:9], None, mix_ref, s_out_ref, *refs[-3:], T=T,
               fillers=[panel(n) for n in range(IN_TN // IN_PANEL)])


def _inproj_hgrn(st, x, mods4, layer, norm_w, w_bf16, st2, proj2, lbs, gnorm_w, tables, states_prev):
    assert st.latent and not st2.latent
    ntok = _ntok(st)
    ni, nj = ntok // IN_TM, IN_PROJ_WIDTH // IN_TN
    nh = HGRN_HEADS // HGRN_HG
    assert ni * nj == st2.nseq * nh

    def unit(i, j):
        s = i * nj + j
        return s // nh, s % nh

    h_in, h_args, h_out, h_shape, h_scratch, h_alias = _hgrn_io(st2, proj2, lbs, gnorm_w, tables, None, layer,
                                                                states_prev, unit)
    in_specs = [
        pl.BlockSpec((IN_TM, D_MODEL), lambda i, j: (i, 0)),
        _mod_spec(st, layer, 0, IN_TM, 2),
        _mod_spec(st, layer, 1, IN_TM, 2),
        _layer_row_spec(layer, D_MODEL, 2),
        pl.BlockSpec((D_MODEL, IN_TN), lambda i, j: (0, j)),
    ] + h_in
    args = [x, mods4, mods4, norm_w, w_bf16] + h_args
    return pl.pallas_call(
        functools.partial(_inproj_hgrn_kernel, T=st2.seq, n_hgrn_in=len(h_args)),
        grid=(ni, nj),
        in_specs=in_specs,
        out_specs=[pl.BlockSpec((IN_TM, IN_TN), lambda i, j: (i, j))] + h_out,
        out_shape=[jax.ShapeDtypeStruct((ntok, IN_PROJ_WIDTH), F32)] + h_shape,
        scratch_shapes=[pltpu.VMEM((IN_TM, D_MODEL), BF16)] + h_scratch,
        input_output_aliases={5 + a: 1 + o for a, o in h_alias.items()},
        compiler_params=_cparams(("arbitrary", "arbitrary")),
        name="inproj_" + st.name + "_hgrn_" + st2.name,
    )(*args)


def _conv_kernel(cb_ref, cc_ref, cx_ref, w_ref, mix_ref, o_ref, *, T):
    del mix_ref
    u = cc_ref[...] * cx_ref[...]
    row = lax.broadcasted_iota(jnp.int32, (T, CONV_WIDTH), 0)
    u_prev = jnp.where(row == 0, 0.0, pltpu.roll(u, 1, axis=0))
    u_next = jnp.where(row == T - 1, 0.0, pltpu.roll(u, T - 1, axis=0))
    y = u_prev * w_ref[0:1, :] + u * w_ref[1:2, :] + u_next * w_ref[2:3, :]
    o_ref[...] = (cb_ref[...] * y).astype(BF16)


def _conv(st, proj, conv_w, layer, mix):
    cw = CONV_WIDTH // LANE
    T = st.seq

    def col_spec(col0):
        return pl.BlockSpec((T, CONV_WIDTH), lambda b: (b, col0 // cw))

    return pl.pallas_call(
        functools.partial(_conv_kernel, T=T),
        grid=(st.nseq,),
        in_specs=[col_spec(COL_CB), col_spec(COL_CC), col_spec(COL_CX),
                  pl.BlockSpec((None, 3, CONV_WIDTH), lambda b: (layer, 0, 0)),
                  pl.BlockSpec(memory_space=pl.ANY)],
        out_specs=pl.BlockSpec((T, CONV_WIDTH), lambda b: (b, HGRN_WIDTH // CONV_WIDTH)),
        out_shape=jax.ShapeDtypeStruct(mix.shape, mix.dtype),
        input_output_aliases={4: 0},
        compiler_params=_cparams(("arbitrary",)),
        name="conv_" + st.name,
    )(proj, proj, proj, conv_w, mix)


MIX_COL_ATTN = (HGRN_WIDTH + CONV_WIDTH) // LANE


def _ctx_attn_kernel(q_ref, k_ref, v_ref, mix_ref, o_ref):
    del mix_ref
    scale = NA_HEAD_DIM ** -0.5
    heads = [slice(h * LANE, (h + 1) * LANE) for h in range(NA_HEADS)]
    scores = [_dot_nt(q_ref[:, hs].astype(BF16), k_ref[:, hs].astype(BF16)) * scale for hs in heads]
    probs = []
    for s in scores:
        p = jnp.exp(s - jnp.max(s, axis=-1, keepdims=True))
        probs.append((p.astype(BF16), jnp.sum(p, axis=-1, keepdims=True)))
    for hs, (p, l) in zip(heads, probs):
        o_ref[:, hs] = (_dot(p, v_ref[:, hs].astype(BF16)) / l).astype(BF16)


def _ctx_attn(proj, mix):
    def col_spec(col0):
        return pl.BlockSpec((SEQ, NA_WIDTH), lambda b: (b, col0 * LANE // NA_WIDTH))

    return pl.pallas_call(
        _ctx_attn_kernel,
        grid=(BATCH,),
        in_specs=[col_spec(COL_NQ), col_spec(COL_NK), col_spec(COL_NV), pl.BlockSpec(memory_space=pl.ANY)],
        out_specs=pl.BlockSpec((SEQ, NA_WIDTH), lambda b: (b, MIX_COL_ATTN * LANE // NA_WIDTH)),
        out_shape=jax.ShapeDtypeStruct(mix.shape, mix.dtype),
        input_output_aliases={3: 0},
        compiler_params=_cparams(("arbitrary",)),
        name="ctx_attn",
    )(proj, proj, proj, mix)


_KH = min(NA_KH, GRID_ROWS)
_ROW_START = [int(v) for v in np.clip(np.arange(GRID_ROWS) - _KH // 2, 0, GRID_ROWS - _KH)]
NLOC = _KH * GRID_W
RPB_ROWS = 2 * NA_KH - 1
RPB_COLS = 2 * NA_KW - 1


def _nat_kernel(q_ref, k_ref, v_ref, kc_ref, vc_ref, bias_ref, mix_ref, o_ref, sctx_ref, sloc_ref, pctx_ref,
                ploc_ref, den_ref):
    del mix_ref
    scale = NA_HEAD_DIM ** -0.5
    row_slices = [slice(r * GRID_W, (r + 1) * GRID_W) for r in range(GRID_ROWS)]
    bands = [slice(rs * GRID_W, rs * GRID_W + NLOC) for rs in _ROW_START]
    for h in range(NA_HEADS):
        lanes = slice(h * LANE, (h + 1) * LANE)
        q_all = q_ref[:, lanes].astype(BF16)
        k_all = k_ref[:, lanes].astype(BF16)
        v_all = v_ref[:, lanes].astype(BF16)
        sctx_ref[...] = _dot_nt(q_all, kc_ref[:, h, :].astype(BF16)) * scale
        for r, (rows, band) in enumerate(zip(row_slices, bands)):
            sloc_ref[rows, :] = _dot_nt(q_all[rows], k_all[band]) * scale + bias_ref[h, r - _ROW_START[r]]
        for rows in row_slices:
            s_loc = sloc_ref[rows, :]
            s_ctx = sctx_ref[rows, :]
            m = jnp.maximum(jnp.max(s_loc, axis=-1, keepdims=True), jnp.max(s_ctx, axis=-1, keepdims=True))
            p_loc = jnp.exp(s_loc - m)
            p_ctx = jnp.exp(s_ctx - m)
            den_ref[rows, :] = jnp.sum(p_loc, axis=-1, keepdims=True) + jnp.sum(p_ctx, axis=-1, keepdims=True)
            ploc_ref[rows, :] = p_loc.astype(BF16)
            pctx_ref[rows, :] = p_ctx.astype(BF16)
        o_ctx = _dot(pctx_ref[...], vc_ref[:, h, :].astype(BF16))
        for rows, band in zip(row_slices, bands):
            o = (_dot(ploc_ref[rows, :], v_all[band]) + o_ctx[rows]) / den_ref[rows, :]
            o_ref[rows, lanes] = o.astype(BF16)


def _nat_bias_kernel(rpb_ref, o_ref):
    base = (pl.program_id(0) * NA_HEADS + pl.program_id(1)) * (RPB_ROWS * RPB_COLS)
    q = lax.broadcasted_iota(jnp.int32, (GRID_W, GRID_W), 0)
    k = lax.broadcasted_iota(jnp.int32, (GRID_W, GRID_W), 1)
    col_start = jnp.clip(q - NA_KW // 2, 0, GRID_W - NA_KW)
    in_window = (k >= col_start) & (k < col_start + NA_KW)
    dc = jnp.clip(k - q + NA_KW - 1, 0, RPB_COLS - 1)
    tabs = []
    for dr in range(RPB_ROWS):
        t = jnp.zeros((GRID_W, GRID_W), F32)
        for c in range(RPB_COLS):
            t = jnp.where(dc == c, rpb_ref[base + dr * RPB_COLS + c], t)
        tabs.append(jnp.where(in_window, t, NEG_BIG))
    for off in range(_KH):
        for p in range(_KH // 2):
            pair = [tabs[kr - off + NA_KH - 1] for kr in (2 * p, 2 * p + 1)]
            o_ref[off, :, 2 * p * GRID_W:(2 * p + 2) * GRID_W] = jnp.concatenate(pair, axis=1)


def _nat_bias(na_rpb):
    return pl.pallas_call(
        _nat_bias_kernel,
        grid=(DEPTH, NA_HEADS),
        in_specs=[pl.BlockSpec(memory_space=pltpu.SMEM)],
        out_specs=pl.BlockSpec((None, None, _KH, GRID_W, NLOC), lambda l, h: (l, h, 0, 0, 0)),
        out_shape=jax.ShapeDtypeStruct((DEPTH, NA_HEADS, _KH, GRID_W, NLOC), F32),
        compiler_params=_cparams(("arbitrary", "arbitrary")),
        name="nat_bias",
    )(na_rpb.reshape(-1))


def _nat(proj, cache_k, cache_v, bias, layer, mix):
    def col_spec(col0):
        return pl.BlockSpec((DEC_SEQ, NA_WIDTH), lambda b: (b, col0 * LANE // NA_WIDTH))

    cache_spec = pl.BlockSpec((None, None, PAST_LEN, NA_HEADS, NA_HEAD_DIM), lambda b: (b, layer, 0, 0, 0))
    return pl.pallas_call(
        _nat_kernel,
        grid=(DEC_BATCH,),
        in_specs=[col_spec(COL_NQ), col_spec(COL_NK), col_spec(COL_NV), cache_spec, cache_spec,
                  pl.BlockSpec((None, NA_HEADS, _KH, GRID_W, NLOC), lambda b: (layer, 0, 0, 0, 0)),
                  pl.BlockSpec(memory_space=pl.ANY)],
        out_specs=pl.BlockSpec((DEC_SEQ, NA_WIDTH), lambda b: (b, MIX_COL_ATTN * LANE // NA_WIDTH)),
        out_shape=jax.ShapeDtypeStruct(mix.shape, mix.dtype),
        scratch_shapes=[pltpu.VMEM((DEC_SEQ, PAST_LEN), F32), pltpu.VMEM((DEC_SEQ, NLOC), F32),
                        pltpu.VMEM((DEC_SEQ, PAST_LEN), BF16), pltpu.VMEM((DEC_SEQ, NLOC), BF16),
                        pltpu.VMEM((DEC_SEQ, 1), F32)],
        input_output_aliases={6: 0},
        compiler_params=_cparams(("arbitrary",)),
        name="nat",
    )(proj, proj, proj, cache_k, cache_v, bias, mix)


def kernel(x_prompt, x_sample, cache_na_k, cache_na_v, state_hgrn, c, c_ctx, w_ada, b_ada, norm_mix_w, w_in,
           hgrn_lb_raw, hgrn_gnorm_w, conv_w, na_rpb, w_out, norm_ffn_w, w_ffn_gate, w_ffn_up, w_ffn_down,
           final_norm_w):
    xs = {LATENT: x_sample.reshape(_ntok(LATENT), D_MODEL), CONTEXT: x_prompt.reshape(_ntok(CONTEXT), D_MODEL)}
    cvecs = jnp.concatenate([c, c_ctx[None, :], jnp.zeros((MOD_ROWS - DEC_BATCH - 1, D_MODEL), F32)], axis=0)
    mods4 = _adaln(cvecs, w_ada, b_ada).reshape(DEPTH, MOD_ROWS, 1, 6 * D_MODEL)

    p_lb = jax.nn.softmax(hgrn_lb_raw.astype(F32), axis=1)
    cp = jnp.cumsum(p_lb, axis=1)
    lbs = (cp - cp[:, :1]).reshape(2, DEPTH, HGRN_HEADS, 1, LANE)
    gnorm_w = hgrn_gnorm_w.reshape(DEPTH, 1, LANE)
    norm_mix = norm_mix_w.reshape(DEPTH, 1, D_MODEL)
    norm_ffn = norm_ffn_w.reshape(DEPTH, 1, D_MODEL)

    w_out_b = w_out.astype(BF16)

    nat_bias = _nat_bias(na_rpb)
    tables = _hgrn_tables()

    kv = None
    states = None
    for l in range(DEPTH):
        last = l == DEPTH - 1
        proj_c, k_c, v_c, w_in_l = _inproj(CONTEXT, xs[CONTEXT], mods4, l, norm_mix, w_in, kv, cast_weights=True,
                                           nblocks=1)
        proj_c, *kv = _inproj(CONTEXT, xs[CONTEXT], mods4, l, norm_mix, w_in_l, (k_c, v_c), block0=1,
                              nblocks=_ntok(CONTEXT) // IN_TM - 1, proj_prev=proj_c)
        proj_l, mix_c, states = _inproj_hgrn(LATENT, xs[LATENT], mods4, l, norm_mix, w_in_l, CONTEXT, proj_c, lbs,
                                             gnorm_w, tables, states)

        mix_l = _hgrn(LATENT, proj_l, lbs, gnorm_w, tables, state_hgrn, l, None)[0]
        mix_l = _conv(LATENT, proj_l, conv_w, l, mix_l)
        mix_l = _nat(proj_l, cache_na_k, cache_na_v, nat_bias, l, mix_l)
        x1 = _outproj(LATENT, xs[LATENT], mix_l, mods4, l, w_out_b)
        y, *ffn_w = _ffn(LATENT, x1, mods4, l, norm_ffn, final_norm_w, (w_ffn_gate, w_ffn_up, w_ffn_down), last,
                         cast_weights=True, nblocks=1)
        xs[LATENT] = _ffn(LATENT, x1, mods4, l, norm_ffn, final_norm_w, ffn_w, last, block0=1,
                          nblocks=_ntok(LATENT) // FFN_TM - 1, out_prev=y)[0]

        mix_c = _conv(CONTEXT, proj_c, conv_w, l, mix_c)
        mix_c = _ctx_attn(proj_c, mix_c)
        x1 = _outproj(CONTEXT, xs[CONTEXT], mix_c, mods4, l, w_out_b)
        xs[CONTEXT] = _ffn(CONTEXT, x1, mods4, l, norm_ffn, final_norm_w, ffn_w, last)[0]

    y_sample = xs[LATENT].reshape(DEC_BATCH, DEC_SEQ, D_MODEL)
    y_prompt = xs[CONTEXT].reshape(BATCH, SEQ, D_MODEL)
    new_k = kv[0].reshape(BATCH, DEPTH, SEQ, NA_HEADS, NA_HEAD_DIM)
    new_v = kv[1].reshape(BATCH, DEPTH, SEQ, NA_HEADS, NA_HEAD_DIM)
    return (y_prompt, y_sample, new_k, new_v, states)
```

```python
import collections
import functools

import numpy as np
import jax
import jax.numpy as jnp
from jax import lax
from jax.experimental import pallas as pl
from jax.experimental.pallas import tpu as pltpu

F32 = jnp.float32
BF16 = jnp.bfloat16

D_MODEL = 2048
BATCH = 16
SEQ = 256
DEPTH = 2
DEC_BATCH = 8
DEC_SEQ = 1024
PAST_LEN = 512
GRID_W = 64
HGRN_HEADS = 8
HGRN_DK = 128
HGRN_DV = 128
HGRN_WIDTH = HGRN_HEADS * HGRN_DV
CONV_WIDTH = 512
NA_HEADS = 4
NA_HEAD_DIM = 128
NA_WIDTH = NA_HEADS * NA_HEAD_DIM
NA_KH = 8
NA_KW = 16
MIX_WIDTH = HGRN_WIDTH + CONV_WIDTH + NA_WIDTH
IN_PROJ_WIDTH = 5 * HGRN_WIDTH + 3 * CONV_WIDTH + 3 * NA_WIDTH
FFN_HIDDEN = ((8 * D_MODEL + 3 * 256 - 1) // (3 * 256)) * 256
EPS = 1e-6

MOD_ROWS = 16
CTX_ROW = DEC_BATCH
LANE = 128
SUBLANE = 8
GRID_ROWS = DEC_SEQ // GRID_W

COL_HQ = 0
COL_HI = HGRN_WIDTH // LANE
COL_HFF = 2 * HGRN_WIDTH // LANE
COL_HFB = 3 * HGRN_WIDTH // LANE
COL_HG = 4 * HGRN_WIDTH // LANE
COL_CB = 5 * HGRN_WIDTH // LANE
COL_CC = COL_CB + CONV_WIDTH // LANE
COL_CX = COL_CC + CONV_WIDTH // LANE
COL_NQ = COL_CX + CONV_WIDTH // LANE
COL_NK = COL_NQ + NA_WIDTH // LANE
COL_NV = COL_NK + NA_WIDTH // LANE

NEG_BIG = -1e30
VMEM_LIMIT = 60 * 1024 * 1024

Stream = collections.namedtuple("Stream", ["name", "seq", "nseq", "latent"])
LATENT = Stream("lat", DEC_SEQ, DEC_BATCH, True)
CONTEXT = Stream("ctx", SEQ, BATCH, False)


def _ntok(st):
    return st.seq * st.nseq


def _cparams(sem):
    return pltpu.CompilerParams(dimension_semantics=sem, vmem_limit_bytes=VMEM_LIMIT)


def _dot(a, b):
    return jnp.dot(a, b, preferred_element_type=F32)


def _dot_nt(a, b):
    return lax.dot_general(a, b, (((1,), (1,)), ((), ())), preferred_element_type=F32)


def _dot_tn(a, b):
    return lax.dot_general(a, b, (((0,), (0,)), ((), ())), preferred_element_type=F32)


def _sigmoid(x):
    return 1.0 / (1.0 + jnp.exp(-x))


ADA_TN = 1024


def _adaln_kernel(c_ref, w_ref, b_ref, o_ref):
    c = c_ref[...]
    s = (c * _sigmoid(c)).astype(BF16)
    o_ref[...] = _dot(s, w_ref[...].astype(BF16)) + b_ref[...]


def _adaln(cvecs, w_ada, b_ada):
    n = 6 * D_MODEL
    return pl.pallas_call(
        _adaln_kernel,
        grid=(DEPTH, n // ADA_TN),
        in_specs=[
            pl.BlockSpec((MOD_ROWS, D_MODEL), lambda l, j: (0, 0)),
            pl.BlockSpec((None, D_MODEL, ADA_TN), lambda l, j: (l, 0, j)),
            pl.BlockSpec((None, 1, ADA_TN), lambda l, j: (l, 0, j)),
        ],
        out_specs=pl.BlockSpec((None, MOD_ROWS, ADA_TN), lambda l, j: (l, 0, j)),
        out_shape=jax.ShapeDtypeStruct((DEPTH, MOD_ROWS, n), F32),
        compiler_params=_cparams(("arbitrary", "arbitrary")),
        name="adaln",
    )(cvecs, w_ada, b_ada.reshape(DEPTH, 1, n))


IN_TM = 1024
IN_TN = 1024
IN_PANEL = 256
TM = 512
FFN_TM = 1024
FFN_TF = 512


def _mod_spec(st, layer, chunk, tm, ngrid, block0=0):
    def row(i):
        return ((block0 + i) * tm) // DEC_SEQ if st.latent else CTX_ROW

    if ngrid == 1:
        return pl.BlockSpec((None, None, 1, D_MODEL), lambda i: (layer, row(i), 0, chunk))
    return pl.BlockSpec((None, None, 1, D_MODEL), lambda i, j: (layer, row(i), 0, chunk))


PROLOGUE_ROWS = 128


def _norm_modulate(x_ref, h_ref, nw_ref, sh_ref, sc_ref):
    gain = nw_ref[...] * (1.0 + sc_ref[...])
    shift = sh_ref[...]

    def chunk(r, carry):
        rows = pl.ds(pl.multiple_of(r * PROLOGUE_ROWS, PROLOGUE_ROWS), PROLOGUE_ROWS)
        x = x_ref[rows, :]
        ms = jnp.mean(x * x, axis=-1, keepdims=True)
        h_ref[rows, :] = (x * lax.rsqrt(ms + EPS) * gain + shift).astype(BF16)
        return carry

    lax.fori_loop(0, x_ref.shape[0] // PROLOGUE_ROWS, chunk, 0)


IN_CAST_TN = 512


def _inproj_kernel(*refs, emit_kv, cast_weights, n_alias):
    x_ref, sh_ref, sc_ref, nw_ref, w_ref = refs[:5]
    outs = refs[5 + n_alias:-1]
    o_ref = outs[0]
    h_ref = refs[-1]
    j = pl.program_id(1)
    nj = pl.num_programs(1)

    @pl.when(j == 0)
    def _():
        _norm_modulate(x_ref, h_ref, nw_ref, sh_ref, sc_ref)

    if cast_weights:
        w = w_ref[...].astype(BF16)
        outs[-1][...] = w
    else:
        w = w_ref[...]
    o_ref[...] = _dot(h_ref[...], w)

    if emit_kv:
        k_ref, v_ref = outs[1], outs[2]
        tn = o_ref.shape[1]
        if tn == 2 * NA_WIDTH:
            @pl.when(j == nj - 1)
            def _():
                k_ref[...] = o_ref[:, :NA_WIDTH].reshape(k_ref.shape)
                v_ref[...] = o_ref[:, NA_WIDTH:].reshape(v_ref.shape)
        else:
            assert tn == NA_WIDTH

            @pl.when(j == nj - 2)
            def _():
                k_ref[...] = o_ref[...].reshape(k_ref.shape)

            @pl.when(j == nj - 1)
            def _():
                v_ref[...] = o_ref[...].reshape(v_ref.shape)


def _layer_row_spec(layer, width, ngrid):
    if ngrid == 1:
        return pl.BlockSpec((None, 1, width), lambda i: (layer, 0, 0))
    return pl.BlockSpec((None, 1, width), lambda i, j: (layer, 0, 0))


def _inproj(st, x, mods4, layer, norm_w, w, kv_prev, *, cast_weights=False, block0=0, nblocks=None, proj_prev=None):
    emit_kv = not st.latent
    tn = IN_CAST_TN if cast_weights else IN_TN
    assert COL_NK * LANE == IN_PROJ_WIDTH - 2 * NA_WIDTH and tn in (NA_WIDTH, 2 * NA_WIDTH)
    ntok = _ntok(st)
    nblocks = ntok // IN_TM if nblocks is None else nblocks
    if cast_weights:
        w_spec = pl.BlockSpec((None, D_MODEL, tn), lambda i, j: (layer, 0, j))
    else:
        w_spec = pl.BlockSpec((D_MODEL, tn), lambda i, j: (0, j))
    in_specs = [
        pl.BlockSpec((IN_TM, D_MODEL), lambda i, j: (block0 + i, 0)),
        _mod_spec(st, layer, 0, IN_TM, 2, block0),
        _mod_spec(st, layer, 1, IN_TM, 2, block0),
        _layer_row_spec(layer, D_MODEL, 2),
        w_spec,
    ]
    args = [x, mods4, mods4, norm_w, w]
    out_specs = [pl.BlockSpec((IN_TM, tn), lambda i, j: (block0 + i, j))]
    out_shape = [jax.ShapeDtypeStruct((ntok, IN_PROJ_WIDTH), F32)]
    aliases = {}
    if proj_prev is not None:
        aliases[len(args)] = 0
        in_specs.append(pl.BlockSpec(memory_space=pl.ANY))
        args.append(proj_prev)
    if emit_kv:
        nb = IN_TM // st.seq
        kv_spec = pl.BlockSpec((nb, None, st.seq, NA_WIDTH), lambda i, j: (block0 + i, layer, 0, 0))
        out_specs += [kv_spec, kv_spec]
        out_shape += [jax.ShapeDtypeStruct((st.nseq, DEPTH, st.seq, NA_WIDTH), F32)] * 2
        if kv_prev is not None:
            aliases.update({len(args): 1, len(args) + 1: 2})
            in_specs += [pl.BlockSpec(memory_space=pl.ANY)] * 2
            args += list(kv_prev)
    if cast_weights:
        out_specs.append(pl.BlockSpec((D_MODEL, tn), lambda i, j: (0, j)))
        out_shape.append(jax.ShapeDtypeStruct((D_MODEL, IN_PROJ_WIDTH), BF16))
    return pl.pallas_call(
        functools.partial(_inproj_kernel, emit_kv=emit_kv, cast_weights=cast_weights, n_alias=len(aliases)),
        grid=(nblocks, IN_PROJ_WIDTH // tn),
        in_specs=in_specs,
        out_specs=out_specs,
        out_shape=out_shape,
        scratch_shapes=[pltpu.VMEM((IN_TM, D_MODEL), BF16)],
        input_output_aliases=aliases,
        compiler_params=_cparams(("arbitrary", "arbitrary")),
        name="inproj_" + st.name + ("_cast" if cast_weights else ""),
    )(*args)


OUT_PANEL = 512


def _outproj_kernel(x_ref, m_ref, g_ref, w_ref, o_ref):
    for n in range(0, D_MODEL, OUT_PANEL):
        cols = slice(n, n + OUT_PANEL)
        o_ref[:, cols] = x_ref[:, cols] + g_ref[:, cols] * _dot(m_ref[...], w_ref[:, cols])


def _outproj(st, x, mix, mods4, layer, w_bf16):
    ntok = _ntok(st)
    return pl.pallas_call(
        _outproj_kernel,
        grid=(ntok // TM,),
        in_specs=[
            pl.BlockSpec((TM, D_MODEL), lambda i: (i, 0)),
            pl.BlockSpec((TM, MIX_WIDTH), lambda i: (i, 0)),
            _mod_spec(st, layer, 2, TM, 1),
            pl.BlockSpec((None, MIX_WIDTH, D_MODEL), lambda i: (layer, 0, 0)),
        ],
        out_specs=pl.BlockSpec((TM, D_MODEL), lambda i: (i, 0)),
        out_shape=jax.ShapeDtypeStruct((ntok, D_MODEL), F32),
        compiler_params=_cparams(("arbitrary",)),
        name="outproj_" + st.name,
    )(x, mix, mods4, w_bf16)


FFN_CAST_TF = 256
FFN_PANEL = 256


def _ffn_prologue(x_ref, h_ref, o_ref, nw_ref, sh_ref, sc_ref):
    _norm_modulate(x_ref, h_ref, nw_ref, sh_ref, sc_ref)
    o_ref[...] = jnp.zeros_like(o_ref)


def _ffn_fillers(h_ref, wg, wu, wd, o_ref):
    hidden = []

    def gate_up(n):
        def fill():
            h = h_ref[...]
            a = _dot(h, wg[:, n:n + FFN_PANEL])
            u = _dot(h, wu[:, n:n + FFN_PANEL])
            hidden.append((a * _sigmoid(a) * u).astype(BF16))
        return fill

    def down(n):
        def fill():
            t = hidden[0] if len(hidden) == 1 else jnp.concatenate(hidden, axis=1)
            o_ref[:, n:n + FFN_TF] += _dot(t, wd[:, n:n + FFN_TF])
        return fill

    return [gate_up(n) for n in range(0, wg.shape[1], FFN_PANEL)] + [down(n) for n in range(0, D_MODEL, FFN_TF)]


def _ffn_epilogue(x_ref, o_ref, g_ref, fw_ref, final_norm):
    gate = g_ref[...]

    def chunk(r, carry):
        rows = pl.ds(pl.multiple_of(r * PROLOGUE_ROWS, PROLOGUE_ROWS), PROLOGUE_ROWS)
        y = x_ref[rows, :] + gate * o_ref[rows, :]
        if final_norm:
            ms = jnp.mean(y * y, axis=-1, keepdims=True)
            y = y * lax.rsqrt(ms + EPS) * fw_ref[...]
        o_ref[rows, :] = y
        return carry

    lax.fori_loop(0, x_ref.shape[0] // PROLOGUE_ROWS, chunk, 0)


def _ffn_kernel(*refs, final_norm, cast_weights, n_alias):
    x_ref, sh_ref, sc_ref, g_ref, nw_ref, fw_ref, wg_ref, wu_ref, wd_ref = refs[:9]
    outs = refs[9 + n_alias:-1]
    o_ref = outs[0]
    h_ref = refs[-1]
    j = pl.program_id(1)

    @pl.when(j == 0)
    def _():
        _ffn_prologue(x_ref, h_ref, o_ref, nw_ref, sh_ref, sc_ref)

    if cast_weights:
        wg, wu, wd = (w[...].astype(BF16) for w in (wg_ref, wu_ref, wd_ref))
        for w, w_out_ref in zip((wg, wu, wd), outs[1:]):
            w_out_ref[...] = w
    else:
        wg, wu, wd = wg_ref[...], wu_ref[...], wd_ref[...]
    for fill in _ffn_fillers(h_ref, wg, wu, wd, o_ref):
        fill()

    @pl.when(j == pl.num_programs(1) - 1)
    def _():
        _ffn_epilogue(x_ref, o_ref, g_ref, fw_ref, final_norm)


def _ffn(st, x, mods4, layer, norm_w, final_w, weights, final_norm, *, cast_weights=False, block0=0, nblocks=None,
         out_prev=None, tm=FFN_TM):
    ntok = _ntok(st)
    TM = tm
    tf = FFN_CAST_TF if cast_weights else FFN_TF
    nblocks = ntok // TM if nblocks is None else nblocks
    if cast_weights:
        w_specs = [pl.BlockSpec((None, D_MODEL, tf), lambda i, j: (layer, 0, j)),
                   pl.BlockSpec((None, D_MODEL, tf), lambda i, j: (layer, 0, j)),
                   pl.BlockSpec((None, tf, D_MODEL), lambda i, j: (layer, j, 0))]
    else:
        w_specs = [pl.BlockSpec((D_MODEL, tf), lambda i, j: (0, j)),
                   pl.BlockSpec((D_MODEL, tf), lambda i, j: (0, j)),
                   pl.BlockSpec((tf, D_MODEL), lambda i, j: (j, 0))]
    in_specs = [
        pl.BlockSpec((TM, D_MODEL), lambda i, j: (block0 + i, 0)),
        _mod_spec(st, layer, 3, TM, 2, block0),
        _mod_spec(st, layer, 4, TM, 2, block0),
        _mod_spec(st, layer, 5, TM, 2, block0),
        _layer_row_spec(layer, D_MODEL, 2),
        pl.BlockSpec((1, D_MODEL), lambda i, j: (0, 0)),
    ] + w_specs
    args = [x, mods4, mods4, mods4, norm_w, final_w.reshape(1, D_MODEL)] + list(weights)
    out_specs = [pl.BlockSpec((TM, D_MODEL), lambda i, j: (block0 + i, 0))]
    out_shape = [jax.ShapeDtypeStruct((ntok, D_MODEL), F32)]
    if cast_weights:
        out_specs += [pl.BlockSpec((D_MODEL, tf), lambda i, j: (0, j)),
                      pl.BlockSpec((D_MODEL, tf), lambda i, j: (0, j)),
                      pl.BlockSpec((tf, D_MODEL), lambda i, j: (j, 0))]
        out_shape += [jax.ShapeDtypeStruct((D_MODEL, FFN_HIDDEN), BF16)] * 2
        out_shape += [jax.ShapeDtypeStruct((FFN_HIDDEN, D_MODEL), BF16)]
    aliases = {}
    if out_prev is not None:
        aliases = {len(args): 0}
        in_specs.append(pl.BlockSpec(memory_space=pl.ANY))
        args.append(out_prev)
    return pl.pallas_call(
        functools.partial(_ffn_kernel, final_norm=final_norm, cast_weights=cast_weights, n_alias=len(aliases)),
        grid=(nblocks, FFN_HIDDEN // tf),
        in_specs=in_specs,
        out_specs=out_specs,
        out_shape=out_shape,
        scratch_shapes=[pltpu.VMEM((TM, D_MODEL), BF16)],
        input_output_aliases=aliases,
        compiler_params=_cparams(("arbitrary", "arbitrary")),
        name="ffn_" + st.name + ("_cast" if cast_weights else ""),
    )(*args)


HGRN_CHUNK = 64
HGRN_HG = 2
HGRN_UNROLL = 16
HGRN_LEVELS = [HGRN_CHUNK >> (i + 1) for i in range(HGRN_CHUNK.bit_length() - 1)]


LOG2E = 1.4426950408889634


def _hgrn_gates(x, lb):
    e = jnp.exp(-jnp.abs(x))
    r = 1.0 / (1.0 + e)
    er = e * r
    pos = x >= 0.0
    sig = jnp.where(pos, r, er)
    nsig = jnp.where(pos, er, r)
    ls2 = jnp.minimum(x, 0.0) * LOG2E + jnp.log2(r)
    f = lb + (1.0 - lb) * sig
    log2_f = jnp.where(lb > 0.0, jnp.log2(f), ls2)
    return log2_f, f, (1.0 - lb) * nsig


def _hgrn_chunk_a(x, lb, b_ref, tri):
    g, f, k = _hgrn_gates(x, lb)
    g_hi = g.astype(BF16)
    g_lo = (g - g_hi.astype(F32)).astype(BF16)
    b = _dot(tri, jnp.concatenate([g_hi, g_lo], axis=0))
    b_ref[...] = b
    return b, f, k


def _hgrn_chunk_b(b, f, k, q, v, st, b_ref, mask_ref, dirn):
    C = HGRN_CHUNK
    fwd = dirn == 0
    b_end = b_ref[pl.ds(C - 1 if fwd else 0, 1), :]
    vb = v.astype(BF16)
    q0 = (q * jnp.exp2(b)).astype(BF16)
    stb = st.astype(BF16)
    kd = (k * jnp.exp2(b_end - b)).astype(BF16)
    st_new = jnp.exp2(b_end) * st + _dot_tn(vb, kd)

    row = lax.broadcasted_iota(jnp.int32, (C, LANE), 0)
    sub = lax.broadcasted_iota(jnp.int32, (SUBLANE, LANE), 0)
    attn = [None] * (C // SUBLANE)

    def add_rows(first_row, term):
        for t in range(term.shape[0] // SUBLANE):
            blk = first_row // SUBLANE + t
            piece = term[t * SUBLANE:(t + 1) * SUBLANE]
            attn[blk] = piece if attn[blk] is None else attn[blk] + piece

    for lvl, m in enumerate(HGRN_LEVELS):
        mask_idx = dirn * len(HGRN_LEVELS) + lvl
        if m >= SUBLANE:
            parts, readers = [], []
            for p in range(C // (2 * m)):
                lo = p * 2 * m
                left = slice(lo, lo + m)
                right = slice(lo + m, lo + 2 * m)
                beta = b_ref[pl.ds(lo + (m - 1 if fwd else m), 1), :]
                if fwd:
                    parts += [k[left] * jnp.exp2(beta - b[left]), q[right] * jnp.exp2(b[right] - beta)]
                    readers.append((lo + m, parts[-1]))
                else:
                    parts += [q[left] * jnp.exp2(b[left] - beta), k[right] * jnp.exp2(beta - b[right])]
                    readers.append((lo, parts[-2]))
            wb = jnp.concatenate(parts, axis=0).astype(BF16)
            wq = jnp.concatenate([r for _, r in readers], axis=0).astype(BF16)
            term = _dot_nt(wq, wb)
            for n, (first_row, _) in enumerate(readers):
                add_rows(first_row, term[n * m:(n + 1) * m] * mask_ref[mask_idx, first_row:first_row + m, :])
        else:
            reads = ((row & m) != 0) if fwd else ((row & m) == 0)
            if m == 1:
                w = jnp.where(reads, q * f, k)
            else:
                betas = []
                for u in range(C // SUBLANE):
                    rows = []
                    for p in range(SUBLANE // (2 * m)):
                        rows.append(b_ref[pl.ds(u * SUBLANE + p * 2 * m + (m - 1 if fwd else m), 1), :])
                    beta_u = rows[-1]
                    for p in range(len(rows) - 2, -1, -1):
                        beta_u = jnp.where(sub < (p + 1) * 2 * m, rows[p], beta_u)
                    betas.append(jnp.broadcast_to(beta_u, (SUBLANE, LANE)))
                beta = jnp.concatenate(betas, axis=0)
                w = jnp.where(reads, q, k) * jnp.exp2(-jnp.abs(b - beta))
            wb = w.astype(BF16)
            add_rows(0, _dot_nt(wb, wb) * mask_ref[mask_idx])
    attn_b = jnp.concatenate(attn, axis=0).astype(BF16)
    diag = jnp.sum(q * k, axis=-1, keepdims=True) * v
    return q0, stb, attn_b, vb, diag, st_new


def _hgrn_tables():
    C = HGRN_CHUNK
    row, col = np.indices((C, C))
    tri = np.stack([col <= row, col >= row])
    tri = np.concatenate([tri, tri], axis=2)
    fwd, bwd = [], []
    for m in HGRN_LEVELS:
        same = (row // (2 * m)) == (col // (2 * m))
        fwd.append(same & ((row & m) != 0) & ((col & m) == 0))
        bwd.append(same & ((row & m) == 0) & ((col & m) != 0))
    return jnp.asarray(tri, BF16), jnp.asarray(np.stack(fwd + bwd), F32)


def _hgrn_unit(q_ref, v_ref, ff_ref, fb_ref, g_ref, lb_ref, gnw_ref, tri_ref, mask_ref, s0_ref, o_ref, s_out_ref,
               oacc_ref, b_ref, st_ref, *, T, fillers=()):
    C = HGRN_CHUNK
    U = min(HGRN_UNROLL, T // C)
    nc = T // C
    assert nc % U == 0 and len(fillers) <= U + 2
    q_scale = HGRN_DK ** -0.5

    oacc_ref[...] = jnp.zeros_like(oacc_ref)
    for hh in range(HGRN_HG):
        for d in range(2):
            if s0_ref is not None:
                st_ref[2 * hh + d] = s0_ref[d, hh].T
            else:
                st_ref[2 * hh + d] = jnp.zeros((HGRN_DV, HGRN_DK), F32)

    def run_trip(c, stage_fillers):
        chains = []
        for hh in range(HGRN_HG):
            lanes = slice(hh * LANE, (hh + 1) * LANE)
            for d in range(2):
                for u in range(U):
                    cc = c * U + u if d == 0 else nc - 1 - (c * U + u)
                    rows = pl.ds(pl.multiple_of(cc * C, C), C)
                    chains.append((hh, d, u, rows, lanes))
        stage_a = {}
        states = {}
        pending = []
        for step in range(U + 2):
            for fill in stage_fillers[step:step + 1]:
                fill()
            for n, (hh, d, u, rows, lanes) in enumerate(chains):
                if u == step:
                    x_ref = ff_ref if d == 0 else fb_ref
                    stage_a[n] = _hgrn_chunk_a(x_ref[rows, lanes], lb_ref[d, hh], b_ref.at[n], tri_ref[d])
            ready, pending = pending, []
            for n, (hh, d, u, rows, lanes) in enumerate(chains):
                if u != step - 1:
                    continue
                b, f, k = stage_a[n]
                idx = 2 * hh + d
                st = st_ref[idx] if u == 0 else states[idx]
                q0, stb, attn_b, vb, diag, st = _hgrn_chunk_b(b, f, k, q_ref[rows, lanes], v_ref[rows, lanes], st,
                                                              b_ref.at[n], mask_ref, d)
                states[idx] = st
                if u == U - 1:
                    st_ref[idx] = st
                pending.append((rows, lanes, q0, stb, attn_b, vb, diag))
            for rows, lanes, q0, stb, attn_b, vb, diag in ready:
                oacc_ref[rows, lanes] += (_dot_nt(q0, stb) + _dot(attn_b, vb)) + diag
        assert not pending

    if nc == U:
        run_trip(0, list(fillers))
    else:
        for fill in fillers:
            fill()

        def body(c, carry):
            run_trip(c, [])
            return carry

        lax.fori_loop(0, nc // U, body, 0)

    for hh in range(HGRN_HG):
        lanes = slice(hh * LANE, (hh + 1) * LANE)
        if s_out_ref is not None:
            for d in range(2):
                s_out_ref[d, hh] = st_ref[2 * hh + d].T
        o = oacc_ref[:, lanes]
        ms = jnp.mean(o * o, axis=-1, keepdims=True) * (q_scale * q_scale)
        gate = g_ref[:, lanes]
        y = o * (q_scale * lax.rsqrt(ms + EPS)) * gnw_ref[...] * (gate * _sigmoid(gate))
        o_ref[:, lanes] = y.astype(BF16)


def _hgrn_kernel(*refs, T, has_s0, emit_state, n_alias):
    n_in = 9 + int(has_s0)
    n_out = 2 if emit_state else 1
    assert len(refs) == n_in + n_alias + n_out + 3
    s0_ref = refs[9] if has_s0 else None
    outs = refs[n_in + n_alias:n_in + n_alias + n_out]
    _hgrn_unit(*refs[:9], s0_ref, outs[0], outs[1] if emit_state else None, *refs[-3:], T=T)


def _hgrn_io(st, proj, lbs, gnorm_w, tables, s0, layer, states_prev, unit):
    hg = HGRN_HG
    wide = hg * LANE
    T = st.seq
    C = HGRN_CHUNK

    def at(fn):
        return lambda *g: fn(*unit(*g))

    def col_spec(col0):
        return pl.BlockSpec((T, wide), at(lambda b, h: (b, col0 // hg + h)))

    tri, masks = tables
    in_specs = [col_spec(COL_HQ), col_spec(COL_HI), col_spec(COL_HFF), col_spec(COL_HFB), col_spec(COL_HG),
                pl.BlockSpec((2, None, hg, 1, LANE), at(lambda b, h: (0, layer, h, 0, 0))),
                pl.BlockSpec((None, 1, LANE), at(lambda b, h: (layer, 0, 0))),
                pl.BlockSpec(tri.shape, at(lambda b, h: (0, 0, 0))),
                pl.BlockSpec(masks.shape, at(lambda b, h: (0, 0, 0)))]
    args = [proj, proj, proj, proj, proj, lbs, gnorm_w, tri, masks]
    state_spec = pl.BlockSpec((None, None, 2, hg, HGRN_DK, HGRN_DV), at(lambda b, h: (b, layer, 0, h, 0, 0)))
    if st.latent:
        in_specs.append(state_spec)
        args.append(s0)
    out_specs = [pl.BlockSpec((T, wide), at(lambda b, h: (b, h)))]
    out_shape = [jax.ShapeDtypeStruct((_ntok(st), MIX_WIDTH), BF16)]
    aliases = {}
    if not st.latent:
        out_specs.append(state_spec)
        out_shape.append(jax.ShapeDtypeStruct((st.nseq, DEPTH, 2, HGRN_HEADS, HGRN_DK, HGRN_DV), F32))
        if states_prev is not None:
            aliases = {len(args): 1}
            in_specs.append(pl.BlockSpec(memory_space=pl.ANY))
            args.append(states_prev)
    scratch = [pltpu.VMEM((T, wide), F32), pltpu.VMEM((2 * hg * min(HGRN_UNROLL, T // C), C, LANE), F32),
               pltpu.VMEM((2 * hg, HGRN_DV, HGRN_DK), F32)]
    return in_specs, args, out_specs, out_shape, scratch, aliases


def _hgrn(st, proj, lbs, gnorm_w, tables, s0, layer, states_prev):
    in_specs, args, out_specs, out_shape, scratch, aliases = _hgrn_io(
        st, proj, lbs, gnorm_w, tables, s0, layer, states_prev, lambda b, h: (b, h))
    return pl.pallas_call(
        functools.partial(_hgrn_kernel, T=st.seq, has_s0=st.latent, emit_state=not st.latent, n_alias=len(aliases)),
        grid=(st.nseq, HGRN_HEADS // HGRN_HG),
        in_specs=in_specs,
        out_specs=out_specs,
        out_shape=out_shape,
        scratch_shapes=scratch,
        input_output_aliases=aliases,
        compiler_params=_cparams(("arbitrary", "arbitrary")),
        name="hgrn_" + st.name,
    )(*args)


def _inproj_hgrn_kernel(*refs, T, n_hgrn_in):
    x_ref, sh_ref, sc_ref, nw_ref, w_ref = refs[:5]
    hgrn_in = refs[5:5 + n_hgrn_in]
    o_ref, mix_ref, s_out_ref = refs[5 + n_hgrn_in:5 + n_hgrn_in + 3]
    h_ref = refs[-4]

    @pl.when(pl.program_id(1) == 0)
    def _():
        _norm_modulate(x_ref, h_ref, nw_ref, sh_ref, sc_ref)

    def panel(n):
        cols = slice(n * IN_PANEL, (n + 1) * IN_PANEL)

        def fill():
            o_ref[:, cols] = _dot(h_ref[...], w_ref[:, cols])

        return fill

    _hgrn_unit(*hgrn_in[:9], None, mix_ref, s_out_ref, *refs[-3:], T=T,
               fillers=[panel(n) for n in range(IN_TN // IN_PANEL)])


def _inproj_hgrn(st, x, mods4, layer, norm_w, w_bf16, st2, proj2, lbs, gnorm_w, tables, states_prev):
    assert st.latent and not st2.latent
    ntok = _ntok(st)
    ni, nj = ntok // IN_TM, IN_PROJ_WIDTH // IN_TN
    nh = HGRN_HEADS // HGRN_HG
    assert ni * nj == st2.nseq * nh

    def unit(i, j):
        s = i * nj + j
        return s // nh, s % nh

    h_in, h_args, h_out, h_shape, h_scratch, h_alias = _hgrn_io(st2, proj2, lbs, gnorm_w, tables, None, layer,
                                                                states_prev, unit)
    in_specs = [
        pl.BlockSpec((IN_TM, D_MODEL), lambda i, j: (i, 0)),
        _mod_spec(st, layer, 0, IN_TM, 2),
        _mod_spec(st, layer, 1, IN_TM, 2),
        _layer_row_spec(layer, D_MODEL, 2),
        pl.BlockSpec((D_MODEL, IN_TN), lambda i, j: (0, j)),
    ] + h_in
    args = [x, mods4, mods4, norm_w, w_bf16] + h_args
    return pl.pallas_call(
        functools.partial(_inproj_hgrn_kernel, T=st2.seq, n_hgrn_in=len(h_args)),
        grid=(ni, nj),
        in_specs=in_specs,
        out_specs=[pl.BlockSpec((IN_TM, IN_TN), lambda i, j: (i, j))] + h_out,
        out_shape=[jax.ShapeDtypeStruct((ntok, IN_PROJ_WIDTH), F32)] + h_shape,
        scratch_shapes=[pltpu.VMEM((IN_TM, D_MODEL), BF16)] + h_scratch,
        input_output_aliases={5 + a: 1 + o for a, o in h_alias.items()},
        compiler_params=_cparams(("arbitrary", "arbitrary")),
        name="inproj_" + st.name + "_hgrn_" + st2.name,
    )(*args)


def _conv_seq(cb_ref, cc_ref, cx_ref, w_ref):
    T = cb_ref.shape[0]
    u = cc_ref[...] * cx_ref[...]
    row = lax.broadcasted_iota(jnp.int32, (T, CONV_WIDTH), 0)
    u_prev = jnp.where(row == 0, 0.0, pltpu.roll(u, 1, axis=0))
    u_next = jnp.where(row == T - 1, 0.0, pltpu.roll(u, T - 1, axis=0))
    y = u_prev * w_ref[0:1, :] + u * w_ref[1:2, :] + u_next * w_ref[2:3, :]
    return (cb_ref[...] * y).astype(BF16)


def _conv_specs(T, layer):
    cw = CONV_WIDTH // LANE
    cols = [pl.BlockSpec((T, CONV_WIDTH), lambda b, c=c: (b, c // cw)) for c in (COL_CB, COL_CC, COL_CX)]
    return cols + [pl.BlockSpec((None, 3, CONV_WIDTH), lambda b: (layer, 0, 0))]


MIX_TAIL = CONV_WIDTH + NA_WIDTH
assert HGRN_WIDTH % MIX_TAIL == 0


def _ctx_attn_kernel(q_ref, k_ref, v_ref, cb_ref, cc_ref, cx_ref, cw_ref, mix_ref, o_ref):
    del mix_ref
    o_ref[:, :CONV_WIDTH] = _conv_seq(cb_ref, cc_ref, cx_ref, cw_ref)
    scale = NA_HEAD_DIM ** -0.5
    heads = [slice(h * LANE, (h + 1) * LANE) for h in range(NA_HEADS)]
    scores = [_dot_nt(q_ref[:, hs].astype(BF16), k_ref[:, hs].astype(BF16)) * scale for hs in heads]
    probs = []
    for s in scores:
        p = jnp.exp(s - jnp.max(s, axis=-1, keepdims=True))
        probs.append((p.astype(BF16), jnp.sum(p, axis=-1, keepdims=True)))
    for h, (hs, (p, l)) in enumerate(zip(heads, probs)):
        o_ref[:, CONV_WIDTH + h * LANE:CONV_WIDTH + (h + 1) * LANE] = (
            _dot(p, v_ref[:, hs].astype(BF16)) / l).astype(BF16)


def _ctx_attn(proj, conv_w, layer, mix):
    def col_spec(col0):
        return pl.BlockSpec((SEQ, NA_WIDTH), lambda b: (b, col0 * LANE // NA_WIDTH))

    return pl.pallas_call(
        _ctx_attn_kernel,
        grid=(BATCH,),
        in_specs=[col_spec(COL_NQ), col_spec(COL_NK), col_spec(COL_NV)] + _conv_specs(SEQ, layer)
        + [pl.BlockSpec(memory_space=pl.ANY)],
        out_specs=pl.BlockSpec((SEQ, MIX_TAIL), lambda b: (b, HGRN_WIDTH // MIX_TAIL)),
        out_shape=jax.ShapeDtypeStruct(mix.shape, mix.dtype),
        input_output_aliases={7: 0},
        compiler_params=_cparams(("arbitrary",)),
        name="ctx_attn",
    )(proj, proj, proj, proj, proj, proj, conv_w, mix)


_KH = min(NA_KH, GRID_ROWS)
_ROW_START = [int(v) for v in np.clip(np.arange(GRID_ROWS) - _KH // 2, 0, GRID_ROWS - _KH)]
NLOC = _KH * GRID_W
RPB_ROWS = 2 * NA_KH - 1
RPB_COLS = 2 * NA_KW - 1


def _nat_kernel(q_ref, k_ref, v_ref, kc_ref, vc_ref, bias_ref, cb_ref, cc_ref, cx_ref, cw_ref, mix_ref, o_ref,
                sctx_ref, sloc_ref, pctx_ref, ploc_ref, den_ref):
    del mix_ref
    o_ref[:, :CONV_WIDTH] = _conv_seq(cb_ref, cc_ref, cx_ref, cw_ref)
    scale = NA_HEAD_DIM ** -0.5
    row_slices = [slice(r * GRID_W, (r + 1) * GRID_W) for r in range(GRID_ROWS)]
    bands = [slice(rs * GRID_W, rs * GRID_W + NLOC) for rs in _ROW_START]
    for h in range(NA_HEADS):
        lanes = slice(h * LANE, (h + 1) * LANE)
        q_all = q_ref[:, lanes].astype(BF16)
        k_all = k_ref[:, lanes].astype(BF16)
        v_all = v_ref[:, lanes].astype(BF16)
        sctx_ref[...] = _dot_nt(q_all, kc_ref[:, h, :].astype(BF16)) * scale
        for r, (rows, band) in enumerate(zip(row_slices, bands)):
            sloc_ref[rows, :] = _dot_nt(q_all[rows], k_all[band]) * scale + bias_ref[h, r - _ROW_START[r]]
        for rows in row_slices:
            s_loc = sloc_ref[rows, :]
            s_ctx = sctx_ref[rows, :]
            m = jnp.maximum(jnp.max(s_loc, axis=-1, keepdims=True), jnp.max(s_ctx, axis=-1, keepdims=True))
            p_loc = jnp.exp(s_loc - m)
            p_ctx = jnp.exp(s_ctx - m)
            den_ref[rows, :] = jnp.sum(p_loc, axis=-1, keepdims=True) + jnp.sum(p_ctx, axis=-1, keepdims=True)
            ploc_ref[rows, :] = p_loc.astype(BF16)
            pctx_ref[rows, :] = p_ctx.astype(BF16)
        o_ctx = _dot(pctx_ref[...], vc_ref[:, h, :].astype(BF16))
        for rows, band in zip(row_slices, bands):
            o = (_dot(ploc_ref[rows, :], v_all[band]) + o_ctx[rows]) / den_ref[rows, :]
            o_ref[rows, CONV_WIDTH + h * LANE:CONV_WIDTH + (h + 1) * LANE] = o.astype(BF16)


def _nat_bias_kernel(rpb_ref, o_ref):
    base = (pl.program_id(0) * NA_HEADS + pl.program_id(1)) * (RPB_ROWS * RPB_COLS)
    q = lax.broadcasted_iota(jnp.int32, (GRID_W, GRID_W), 0)
    k = lax.broadcasted_iota(jnp.int32, (GRID_W, GRID_W), 1)
    col_start = jnp.clip(q - NA_KW // 2, 0, GRID_W - NA_KW)
    in_window = (k >= col_start) & (k < col_start + NA_KW)
    dc = jnp.clip(k - q + NA_KW - 1, 0, RPB_COLS - 1)
    tabs = []
    for dr in range(RPB_ROWS):
        t = jnp.zeros((GRID_W, GRID_W), F32)
        for c in range(RPB_COLS):
            t = jnp.where(dc == c, rpb_ref[base + dr * RPB_COLS + c], t)
        tabs.append(jnp.where(in_window, t, NEG_BIG))
    for off in range(_KH):
        for p in range(_KH // 2):
            pair = [tabs[kr - off + NA_KH - 1] for kr in (2 * p, 2 * p + 1)]
            o_ref[off, :, 2 * p * GRID_W:(2 * p + 2) * GRID_W] = jnp.concatenate(pair, axis=1)


def _nat_bias(na_rpb):
    return pl.pallas_call(
        _nat_bias_kernel,
        grid=(DEPTH, NA_HEADS),
        in_specs=[pl.BlockSpec(memory_space=pltpu.SMEM)],
        out_specs=pl.BlockSpec((None, None, _KH, GRID_W, NLOC), lambda l, h: (l, h, 0, 0, 0)),
        out_shape=jax.ShapeDtypeStruct((DEPTH, NA_HEADS, _KH, GRID_W, NLOC), F32),
        compiler_params=_cparams(("arbitrary", "arbitrary")),
        name="nat_bias",
    )(na_rpb.reshape(-1))


def _nat(proj, cache_k, cache_v, bias, conv_w, layer, mix):
    def col_spec(col0):
        return pl.BlockSpec((DEC_SEQ, NA_WIDTH), lambda b: (b, col0 * LANE // NA_WIDTH))

    cache_spec = pl.BlockSpec((None, None, PAST_LEN, NA_HEADS, NA_HEAD_DIM), lambda b: (b, layer, 0, 0, 0))
    return pl.pallas_call(
        _nat_kernel,
        grid=(DEC_BATCH,),
        in_specs=[col_spec(COL_NQ), col_spec(COL_NK), col_spec(COL_NV), cache_spec, cache_spec,
                  pl.BlockSpec((None, NA_HEADS, _KH, GRID_W, NLOC), lambda b: (layer, 0, 0, 0, 0))]
        + _conv_specs(DEC_SEQ, layer) + [pl.BlockSpec(memory_space=pl.ANY)],
        out_specs=pl.BlockSpec((DEC_SEQ, MIX_TAIL), lambda b: (b, HGRN_WIDTH // MIX_TAIL)),
        out_shape=jax.ShapeDtypeStruct(mix.shape, mix.dtype),
        scratch_shapes=[pltpu.VMEM((DEC_SEQ, PAST_LEN), F32), pltpu.VMEM((DEC_SEQ, NLOC), F32),
                        pltpu.VMEM((DEC_SEQ, PAST_LEN), BF16), pltpu.VMEM((DEC_SEQ, NLOC), BF16),
                        pltpu.VMEM((DEC_SEQ, 1), F32)],
        input_output_aliases={10: 0},
        compiler_params=_cparams(("arbitrary",)),
        name="nat",
    )(proj, proj, proj, cache_k, cache_v, bias, proj, proj, proj, conv_w, mix)


def kernel(x_prompt, x_sample, cache_na_k, cache_na_v, state_hgrn, c, c_ctx, w_ada, b_ada, norm_mix_w, w_in,
           hgrn_lb_raw, hgrn_gnorm_w, conv_w, na_rpb, w_out, norm_ffn_w, w_ffn_gate, w_ffn_up, w_ffn_down,
           final_norm_w):
    xs = {LATENT: x_sample.reshape(_ntok(LATENT), D_MODEL), CONTEXT: x_prompt.reshape(_ntok(CONTEXT), D_MODEL)}
    cvecs = jnp.concatenate([c, c_ctx[None, :], jnp.zeros((MOD_ROWS - DEC_BATCH - 1, D_MODEL), F32)], axis=0)
    mods4 = _adaln(cvecs, w_ada, b_ada).reshape(DEPTH, MOD_ROWS, 1, 6 * D_MODEL)

    p_lb = jax.nn.softmax(hgrn_lb_raw.astype(F32), axis=1)
    cp = jnp.cumsum(p_lb, axis=1)
    lbs = (cp - cp[:, :1]).reshape(2, DEPTH, HGRN_HEADS, 1, LANE)
    gnorm_w = hgrn_gnorm_w.reshape(DEPTH, 1, LANE)
    norm_mix = norm_mix_w.reshape(DEPTH, 1, D_MODEL)
    norm_ffn = norm_ffn_w.reshape(DEPTH, 1, D_MODEL)

    w_out_b = w_out.astype(BF16)

    nat_bias = _nat_bias(na_rpb)
    tables = _hgrn_tables()

    kv = None
    states = None
    for l in range(DEPTH):
        last = l == DEPTH - 1
        proj_c, k_c, v_c, w_in_l = _inproj(CONTEXT, xs[CONTEXT], mods4, l, norm_mix, w_in, kv, cast_weights=True,
                                           nblocks=1)
        proj_c, *kv = _inproj(CONTEXT, xs[CONTEXT], mods4, l, norm_mix, w_in_l, (k_c, v_c), block0=1,
                              nblocks=_ntok(CONTEXT) // IN_TM - 1, proj_prev=proj_c)
        proj_l, mix_c, states = _inproj_hgrn(LATENT, xs[LATENT], mods4, l, norm_mix, w_in_l, CONTEXT, proj_c, lbs,
                                             gnorm_w, tables, states)

        mix_l = _hgrn(LATENT, proj_l, lbs, gnorm_w, tables, state_hgrn, l, None)[0]
        mix_l = _nat(proj_l, cache_na_k, cache_na_v, nat_bias, conv_w, l, mix_l)
        x1 = _outproj(LATENT, xs[LATENT], mix_l, mods4, l, w_out_b)
        y, *ffn_w = _ffn(LATENT, x1, mods4, l, norm_ffn, final_norm_w, (w_ffn_gate, w_ffn_up, w_ffn_down), last,
                         cast_weights=True, nblocks=1)
        xs[LATENT] = _ffn(LATENT, x1, mods4, l, norm_ffn, final_norm_w, ffn_w, last, block0=1,
                          nblocks=_ntok(LATENT) // FFN_TM - 1, out_prev=y)[0]

        mix_c = _ctx_attn(proj_c, conv_w, l, mix_c)
        x1 = _outproj(CONTEXT, xs[CONTEXT], mix_c, mods4, l, w_out_b)
        xs[CONTEXT] = _ffn(CONTEXT, x1, mods4, l, norm_ffn, final_norm_w, ffn_w, last)[0]

    y_sample = xs[LATENT].reshape(DEC_BATCH, DEC_SEQ, D_MODEL)
    y_prompt = xs[CONTEXT].reshape(BATCH, SEQ, D_MODEL)
    new_k = kv[0].reshape(BATCH, DEPTH, SEQ, NA_HEADS, NA_HEAD_DIM)
    new_v = kv[1].reshape(BATCH, DEPTH, SEQ, NA_HEADS, NA_HEAD_DIM)
    return (y_prompt, y_sample, new_k, new_v, states)
```

```python
import collections
import functools

import numpy as np
import jax
import jax.numpy as jnp
from jax import lax
from jax.experimental import pallas as pl
from jax.experimental.pallas import tpu as pltpu

F32 = jnp.float32
BF16 = jnp.bfloat16

D_MODEL = 2048
BATCH = 16
SEQ = 256
DEPTH = 2
DEC_BATCH = 8
DEC_SEQ = 1024
PAST_LEN = 512
GRID_W = 64
HGRN_HEADS = 8
HGRN_DK = 128
HGRN_DV = 128
HGRN_WIDTH = HGRN_HEADS * HGRN_DV
CONV_WIDTH = 512
NA_HEADS = 4
NA_HEAD_DIM = 128
NA_WIDTH = NA_HEADS * NA_HEAD_DIM
NA_KH = 8
NA_KW = 16
MIX_WIDTH = HGRN_WIDTH + CONV_WIDTH + NA_WIDTH
IN_PROJ_WIDTH = 5 * HGRN_WIDTH + 3 * CONV_WIDTH + 3 * NA_WIDTH
FFN_HIDDEN = ((8 * D_MODEL + 3 * 256 - 1) // (3 * 256)) * 256
EPS = 1e-6

MOD_ROWS = 16
CTX_ROW = DEC_BATCH
LANE = 128
SUBLANE = 8
GRID_ROWS = DEC_SEQ // GRID_W

COL_HQ = 0
COL_HI = HGRN_WIDTH // LANE
COL_HFF = 2 * HGRN_WIDTH // LANE
COL_HFB = 3 * HGRN_WIDTH // LANE
COL_HG = 4 * HGRN_WIDTH // LANE
COL_CB = 5 * HGRN_WIDTH // LANE
COL_CC = COL_CB + CONV_WIDTH // LANE
COL_CX = COL_CC + CONV_WIDTH // LANE
COL_NQ = COL_CX + CONV_WIDTH // LANE
COL_NK = COL_NQ + NA_WIDTH // LANE
COL_NV = COL_NK + NA_WIDTH // LANE

NEG_BIG = -1e30
VMEM_LIMIT = 60 * 1024 * 1024

Stream = collections.namedtuple("Stream", ["name", "seq", "nseq", "latent"])
LATENT = Stream("lat", DEC_SEQ, DEC_BATCH, True)
CONTEXT = Stream("ctx", SEQ, BATCH, False)


def _ntok(st):
    return st.seq * st.nseq


def _cparams(sem):
    return pltpu.CompilerParams(dimension_semantics=sem, vmem_limit_bytes=VMEM_LIMIT)


def _dot(a, b):
    return jnp.dot(a, b, preferred_element_type=F32)


def _dot_nt(a, b):
    return lax.dot_general(a, b, (((1,), (1,)), ((), ())), preferred_element_type=F32)


def _dot_tn(a, b):
    return lax.dot_general(a, b, (((0,), (0,)), ((), ())), preferred_element_type=F32)


def _sigmoid(x):
    return 1.0 / (1.0 + jnp.exp(-x))


ADA_TN = 1024


def _adaln_kernel(c_ref, w_ref, b_ref, o_ref):
    c = c_ref[...]
    s = (c * _sigmoid(c)).astype(BF16)
    o_ref[...] = _dot(s, w_ref[...].astype(BF16)) + b_ref[...]


def _adaln(cvecs, w_ada, b_ada):
    n = 6 * D_MODEL
    return pl.pallas_call(
        _adaln_kernel,
        grid=(DEPTH, n // ADA_TN),
        in_specs=[
            pl.BlockSpec((MOD_ROWS, D_MODEL), lambda l, j: (0, 0)),
            pl.BlockSpec((None, D_MODEL, ADA_TN), lambda l, j: (l, 0, j)),
            pl.BlockSpec((None, 1, ADA_TN), lambda l, j: (l, 0, j)),
        ],
        out_specs=pl.BlockSpec((None, MOD_ROWS, ADA_TN), lambda l, j: (l, 0, j)),
        out_shape=jax.ShapeDtypeStruct((DEPTH, MOD_ROWS, n), F32),
        compiler_params=_cparams(("arbitrary", "arbitrary")),
        name="adaln",
    )(cvecs, w_ada, b_ada.reshape(DEPTH, 1, n))


IN_TM = 1024
IN_TN = 1024
IN_PANEL = 256
TM = 512
FFN_TM = 1024
FFN_TF = 512


def _mod_spec(st, layer, chunk, tm, ngrid, block0=0):
    def row(i):
        return ((block0 + i) * tm) // DEC_SEQ if st.latent else CTX_ROW

    if ngrid == 1:
        return pl.BlockSpec((None, None, 1, D_MODEL), lambda i: (layer, row(i), 0, chunk))
    return pl.BlockSpec((None, None, 1, D_MODEL), lambda i, j: (layer, row(i), 0, chunk))


PROLOGUE_ROWS = 128


def _norm_modulate(x_ref, h_ref, nw_ref, sh_ref, sc_ref):
    gain = nw_ref[...] * (1.0 + sc_ref[...])
    shift = sh_ref[...]

    def chunk(r, carry):
        rows = pl.ds(pl.multiple_of(r * PROLOGUE_ROWS, PROLOGUE_ROWS), PROLOGUE_ROWS)
        x = x_ref[rows, :]
        ms = jnp.mean(x * x, axis=-1, keepdims=True)
        h_ref[rows, :] = (x * lax.rsqrt(ms + EPS) * gain + shift).astype(BF16)
        return carry

    lax.fori_loop(0, x_ref.shape[0] // PROLOGUE_ROWS, chunk, 0)


IN_CAST_TN = 512


def _inproj_kernel(*refs, emit_kv, cast_weights, n_alias):
    x_ref, sh_ref, sc_ref, nw_ref, w_ref = refs[:5]
    outs = refs[5 + n_alias:-1]
    o_ref = outs[0]
    h_ref = refs[-1]
    j = pl.program_id(1)
    nj = pl.num_programs(1)

    @pl.when(j == 0)
    def _():
        _norm_modulate(x_ref, h_ref, nw_ref, sh_ref, sc_ref)

    if cast_weights:
        w = w_ref[...].astype(BF16)
        outs[-1][...] = w
    else:
        w = w_ref[...]
    o_ref[...] = _dot(h_ref[...], w)

    if emit_kv:
        k_ref, v_ref = outs[1], outs[2]
        tn = o_ref.shape[1]
        if tn == 2 * NA_WIDTH:
            @pl.when(j == nj - 1)
            def _():
                k_ref[...] = o_ref[:, :NA_WIDTH].reshape(k_ref.shape)
                v_ref[...] = o_ref[:, NA_WIDTH:].reshape(v_ref.shape)
        else:
            assert tn == NA_WIDTH

            @pl.when(j == nj - 2)
            def _():
                k_ref[...] = o_ref[...].reshape(k_ref.shape)

            @pl.when(j == nj - 1)
            def _():
                v_ref[...] = o_ref[...].reshape(v_ref.shape)


def _layer_row_spec(layer, width, ngrid):
    if ngrid == 1:
        return pl.BlockSpec((None, 1, width), lambda i: (layer, 0, 0))
    return pl.BlockSpec((None, 1, width), lambda i, j: (layer, 0, 0))


def _inproj(st, x, mods4, layer, norm_w, w, kv_prev, *, cast_weights=False, block0=0, nblocks=None, proj_prev=None):
    emit_kv = not st.latent
    tn = IN_CAST_TN if cast_weights else IN_TN
    assert COL_NK * LANE == IN_PROJ_WIDTH - 2 * NA_WIDTH and tn in (NA_WIDTH, 2 * NA_WIDTH)
    ntok = _ntok(st)
    nblocks = ntok // IN_TM if nblocks is None else nblocks
    if cast_weights:
        w_spec = pl.BlockSpec((None, D_MODEL, tn), lambda i, j: (layer, 0, j))
    else:
        w_spec = pl.BlockSpec((D_MODEL, tn), lambda i, j: (0, j))
    in_specs = [
        pl.BlockSpec((IN_TM, D_MODEL), lambda i, j: (block0 + i, 0)),
        _mod_spec(st, layer, 0, IN_TM, 2, block0),
        _mod_spec(st, layer, 1, IN_TM, 2, block0),
        _layer_row_spec(layer, D_MODEL, 2),
        w_spec,
    ]
    args = [x, mods4, mods4, norm_w, w]
    out_specs = [pl.BlockSpec((IN_TM, tn), lambda i, j: (block0 + i, j))]
    out_shape = [jax.ShapeDtypeStruct((ntok, IN_PROJ_WIDTH), F32)]
    aliases = {}
    if proj_prev is not None:
        aliases[len(args)] = 0
        in_specs.append(pl.BlockSpec(memory_space=pl.ANY))
        args.append(proj_prev)
    if emit_kv:
        nb = IN_TM // st.seq
        kv_spec = pl.BlockSpec((nb, None, st.seq, NA_WIDTH), lambda i, j: (block0 + i, layer, 0, 0))
        out_specs += [kv_spec, kv_spec]
        out_shape += [jax.ShapeDtypeStruct((st.nseq, DEPTH, st.seq, NA_WIDTH), F32)] * 2
        if kv_prev is not None:
            aliases.update({len(args): 1, len(args) + 1: 2})
            in_specs += [pl.BlockSpec(memory_space=pl.ANY)] * 2
            args += list(kv_prev)
    if cast_weights:
        out_specs.append(pl.BlockSpec((D_MODEL, tn), lambda i, j: (0, j)))
        out_shape.append(jax.ShapeDtypeStruct((D_MODEL, IN_PROJ_WIDTH), BF16))
    return pl.pallas_call(
        functools.partial(_inproj_kernel, emit_kv=emit_kv, cast_weights=cast_weights, n_alias=len(aliases)),
        grid=(nblocks, IN_PROJ_WIDTH // tn),
        in_specs=in_specs,
        out_specs=out_specs,
        out_shape=out_shape,
        scratch_shapes=[pltpu.VMEM((IN_TM, D_MODEL), BF16)],
        input_output_aliases=aliases,
        compiler_params=_cparams(("arbitrary", "arbitrary")),
        name="inproj_" + st.name + ("_cast" if cast_weights else ""),
    )(*args)


OUT_PANEL = 512


def _outproj_kernel(x_ref, m_ref, g_ref, w_ref, o_ref):
    for n in range(0, D_MODEL, OUT_PANEL):
        cols = slice(n, n + OUT_PANEL)
        o_ref[:, cols] = x_ref[:, cols] + g_ref[:, cols] * _dot(m_ref[...], w_ref[:, cols])


def _outproj(st, x, mix, mods4, layer, w_bf16):
    ntok = _ntok(st)
    return pl.pallas_call(
        _outproj_kernel,
        grid=(ntok // TM,),
        in_specs=[
            pl.BlockSpec((TM, D_MODEL), lambda i: (i, 0)),
            pl.BlockSpec((TM, MIX_WIDTH), lambda i: (i, 0)),
            _mod_spec(st, layer, 2, TM, 1),
            pl.BlockSpec((None, MIX_WIDTH, D_MODEL), lambda i: (layer, 0, 0)),
        ],
        out_specs=pl.BlockSpec((TM, D_MODEL), lambda i: (i, 0)),
        out_shape=jax.ShapeDtypeStruct((ntok, D_MODEL), F32),
        compiler_params=_cparams(("arbitrary",)),
        name="outproj_" + st.name,
    )(x, mix, mods4, w_bf16)


FFN_CAST_TF = 256
FFN_PANEL = 256


def _ffn_prologue(x_ref, h_ref, o_ref, nw_ref, sh_ref, sc_ref):
    _norm_modulate(x_ref, h_ref, nw_ref, sh_ref, sc_ref)
    o_ref[...] = jnp.zeros_like(o_ref)


def _ffn_fillers(h_ref, wg, wu, wd, o_ref):
    hidden = []

    def gate_up(n):
        def fill():
            h = h_ref[...]
            a = _dot(h, wg[:, n:n + FFN_PANEL])
            u = _dot(h, wu[:, n:n + FFN_PANEL])
            hidden.append((a * _sigmoid(a) * u).astype(BF16))
        return fill

    def down(n):
        def fill():
            t = hidden[0] if len(hidden) == 1 else jnp.concatenate(hidden, axis=1)
            o_ref[:, n:n + FFN_TF] += _dot(t, wd[:, n:n + FFN_TF])
        return fill

    return [gate_up(n) for n in range(0, wg.shape[1], FFN_PANEL)] + [down(n) for n in range(0, D_MODEL, FFN_TF)]


def _ffn_epilogue(x_ref, o_ref, g_ref, fw_ref, final_norm):
    gate = g_ref[...]

    def chunk(r, carry):
        rows = pl.ds(pl.multiple_of(r * PROLOGUE_ROWS, PROLOGUE_ROWS), PROLOGUE_ROWS)
        y = x_ref[rows, :] + gate * o_ref[rows, :]
        if final_norm:
            ms = jnp.mean(y * y, axis=-1, keepdims=True)
            y = y * lax.rsqrt(ms + EPS) * fw_ref[...]
        o_ref[rows, :] = y
        return carry

    lax.fori_loop(0, x_ref.shape[0] // PROLOGUE_ROWS, chunk, 0)


def _ffn_kernel(*refs, final_norm, cast_weights, n_alias):
    x_ref, sh_ref, sc_ref, g_ref, nw_ref, fw_ref, wg_ref, wu_ref, wd_ref = refs[:9]
    outs = refs[9 + n_alias:-1]
    o_ref = outs[0]
    h_ref = refs[-1]
    j = pl.program_id(1)

    @pl.when(j == 0)
    def _():
        _ffn_prologue(x_ref, h_ref, o_ref, nw_ref, sh_ref, sc_ref)

    if cast_weights:
        wg, wu, wd = (w[...].astype(BF16) for w in (wg_ref, wu_ref, wd_ref))
        for w, w_out_ref in zip((wg, wu, wd), outs[1:]):
            w_out_ref[...] = w
    else:
        wg, wu, wd = wg_ref[...], wu_ref[...], wd_ref[...]
    for fill in _ffn_fillers(h_ref, wg, wu, wd, o_ref):
        fill()

    @pl.when(j == pl.num_programs(1) - 1)
    def _():
        _ffn_epilogue(x_ref, o_ref, g_ref, fw_ref, final_norm)


def _ffn(st, x, mods4, layer, norm_w, final_w, weights, final_norm, *, cast_weights=False, block0=0, nblocks=None,
         out_prev=None, tm=FFN_TM):
    ntok = _ntok(st)
    TM = tm
    tf = FFN_CAST_TF if cast_weights else FFN_TF
    nblocks = ntok // TM if nblocks is None else nblocks
    if cast_weights:
        w_specs = [pl.BlockSpec((None, D_MODEL, tf), lambda i, j: (layer, 0, j)),
                   pl.BlockSpec((None, D_MODEL, tf), lambda i, j: (layer, 0, j)),
                   pl.BlockSpec((None, tf, D_MODEL), lambda i, j: (layer, j, 0))]
    else:
        w_specs = [pl.BlockSpec((D_MODEL, tf), lambda i, j: (0, j)),
                   pl.BlockSpec((D_MODEL, tf), lambda i, j: (0, j)),
                   pl.BlockSpec((tf, D_MODEL), lambda i, j: (j, 0))]
    in_specs = [
        pl.BlockSpec((TM, D_MODEL), lambda i, j: (block0 + i, 0)),
        _mod_spec(st, layer, 3, TM, 2, block0),
        _mod_spec(st, layer, 4, TM, 2, block0),
        _mod_spec(st, layer, 5, TM, 2, block0),
        _layer_row_spec(layer, D_MODEL, 2),
        pl.BlockSpec((1, D_MODEL), lambda i, j: (0, 0)),
    ] + w_specs
    args = [x, mods4, mods4, mods4, norm_w, final_w.reshape(1, D_MODEL)] + list(weights)
    out_specs = [pl.BlockSpec((TM, D_MODEL), lambda i, j: (block0 + i, 0))]
    out_shape = [jax.ShapeDtypeStruct((ntok, D_MODEL), F32)]
    if cast_weights:
        out_specs += [pl.BlockSpec((D_MODEL, tf), lambda i, j: (0, j)),
                      pl.BlockSpec((D_MODEL, tf), lambda i, j: (0, j)),
                      pl.BlockSpec((tf, D_MODEL), lambda i, j: (j, 0))]
        out_shape += [jax.ShapeDtypeStruct((D_MODEL, FFN_HIDDEN), BF16)] * 2
        out_shape += [jax.ShapeDtypeStruct((FFN_HIDDEN, D_MODEL), BF16)]
    aliases = {}
    if out_prev is not None:
        aliases = {len(args): 0}
        in_specs.append(pl.BlockSpec(memory_space=pl.ANY))
        args.append(out_prev)
    return pl.pallas_call(
        functools.partial(_ffn_kernel, final_norm=final_norm, cast_weights=cast_weights, n_alias=len(aliases)),
        grid=(nblocks, FFN_HIDDEN // tf),
        in_specs=in_specs,
        out_specs=out_specs,
        out_shape=out_shape,
        scratch_shapes=[pltpu.VMEM((TM, D_MODEL), BF16)],
        input_output_aliases=aliases,
        compiler_params=_cparams(("arbitrary", "arbitrary")),
        name="ffn_" + st.name + ("_cast" if cast_weights else ""),
    )(*args)


HGRN_CHUNK = 64
HGRN_HG = 2
HGRN_UNROLL = 16
HGRN_LEVELS = [HGRN_CHUNK >> (i + 1) for i in range(HGRN_CHUNK.bit_length() - 1)]


LOG2E = 1.4426950408889634


def _hgrn_gates(x, lb):
    e = jnp.exp(-jnp.abs(x))
    r = 1.0 / (1.0 + e)
    er = e * r
    pos = x >= 0.0
    sig = jnp.where(pos, r, er)
    nsig = jnp.where(pos, er, r)
    ls2 = jnp.minimum(x, 0.0) * LOG2E + jnp.log2(r)
    f = lb + (1.0 - lb) * sig
    log2_f = jnp.where(lb > 0.0, jnp.log2(f), ls2)
    return log2_f, f, (1.0 - lb) * nsig


def _hgrn_chunk_a(x, lb, b_ref, tri):
    g, f, k = _hgrn_gates(x, lb)
    g_hi = g.astype(BF16)
    g_lo = (g - g_hi.astype(F32)).astype(BF16)
    b = _dot(tri, jnp.concatenate([g_hi, g_lo], axis=0))
    b_ref[...] = b
    return b, f, k


def _hgrn_chunk_b(b, f, k, q, v, st, b_ref, mask_ref, dirn):
    C = HGRN_CHUNK
    fwd = dirn == 0
    b_end = b_ref[pl.ds(C - 1 if fwd else 0, 1), :]
    vb = v.astype(BF16)
    q0 = (q * jnp.exp2(b)).astype(BF16)
    stb = st.astype(BF16)
    kd = (k * jnp.exp2(b_end - b)).astype(BF16)
    st_new = jnp.exp2(b_end) * st + _dot_tn(vb, kd)

    row = lax.broadcasted_iota(jnp.int32, (C, LANE), 0)
    sub = lax.broadcasted_iota(jnp.int32, (SUBLANE, LANE), 0)
    attn = [None] * (C // SUBLANE)

    def add_rows(first_row, term):
        for t in range(term.shape[0] // SUBLANE):
            blk = first_row // SUBLANE + t
            piece = term[t * SUBLANE:(t + 1) * SUBLANE]
            attn[blk] = piece if attn[blk] is None else attn[blk] + piece

    for lvl, m in enumerate(HGRN_LEVELS):
        mask_idx = dirn * len(HGRN_LEVELS) + lvl
        if m >= SUBLANE:
            parts, readers = [], []
            for p in range(C // (2 * m)):
                lo = p * 2 * m
                left = slice(lo, lo + m)
                right = slice(lo + m, lo + 2 * m)
                beta = b_ref[pl.ds(lo + (m - 1 if fwd else m), 1), :]
                if fwd:
                    parts += [k[left] * jnp.exp2(beta - b[left]), q[right] * jnp.exp2(b[right] - beta)]
                    readers.append((lo + m, parts[-1]))
                else:
                    parts += [q[left] * jnp.exp2(b[left] - beta), k[right] * jnp.exp2(beta - b[right])]
                    readers.append((lo, parts[-2]))
            wb = jnp.concatenate(parts, axis=0).astype(BF16)
            wq = jnp.concatenate([r for _, r in readers], axis=0).astype(BF16)
            term = _dot_nt(wq, wb)
            for n, (first_row, _) in enumerate(readers):
                add_rows(first_row, term[n * m:(n + 1) * m] * mask_ref[mask_idx, first_row:first_row + m, :])
        else:
            reads = ((row & m) != 0) if fwd else ((row & m) == 0)
            if m == 1:
                w = jnp.where(reads, q * f, k)
            else:
                betas = []
                for u in range(C // SUBLANE):
                    rows = []
                    for p in range(SUBLANE // (2 * m)):
                        rows.append(b_ref[pl.ds(u * SUBLANE + p * 2 * m + (m - 1 if fwd else m), 1), :])
                    beta_u = rows[-1]
                    for p in range(len(rows) - 2, -1, -1):
                        beta_u = jnp.where(sub < (p + 1) * 2 * m, rows[p], beta_u)
                    betas.append(jnp.broadcast_to(beta_u, (SUBLANE, LANE)))
                beta = jnp.concatenate(betas, axis=0)
                w = jnp.where(reads, q, k) * jnp.exp2(-jnp.abs(b - beta))
            wb = w.astype(BF16)
            add_rows(0, _dot_nt(wb, wb) * mask_ref[mask_idx])
    attn_b = jnp.concatenate(attn, axis=0).astype(BF16)
    diag = jnp.sum(q * k, axis=-1, keepdims=True) * v
    return q0, stb, attn_b, vb, diag, st_new


def _hgrn_tables():
    C = HGRN_CHUNK
    row, col = np.indices((C, C))
    tri = np.stack([col <= row, col >= row])
    tri = np.concatenate([tri, tri], axis=2)
    fwd, bwd = [], []
    for m in HGRN_LEVELS:
        same = (row // (2 * m)) == (col // (2 * m))
        fwd.append(same & ((row & m) != 0) & ((col & m) == 0))
        bwd.append(same & ((row & m) == 0) & ((col & m) != 0))
    return jnp.asarray(tri, BF16), jnp.asarray(np.stack(fwd + bwd), F32)


def _hgrn_unit(q_ref, v_ref, ff_ref, fb_ref, g_ref, lb_ref, gnw_ref, tri_ref, mask_ref, s0_ref, o_ref, s_out_ref,
               oacc_ref, b_ref, st_ref, *, T, fillers=()):
    C = HGRN_CHUNK
    U = min(HGRN_UNROLL, T // C)
    nc = T // C
    assert nc % U == 0 and len(fillers) <= U + 2
    q_scale = HGRN_DK ** -0.5

    oacc_ref[...] = jnp.zeros_like(oacc_ref)
    for hh in range(HGRN_HG):
        for d in range(2):
            if s0_ref is not None:
                st_ref[2 * hh + d] = s0_ref[d, hh].T
            else:
                st_ref[2 * hh + d] = jnp.zeros((HGRN_DV, HGRN_DK), F32)

    def run_trip(c, stage_fillers):
        chains = []
        for hh in range(HGRN_HG):
            lanes = slice(hh * LANE, (hh + 1) * LANE)
            for d in range(2):
                for u in range(U):
                    cc = c * U + u if d == 0 else nc - 1 - (c * U + u)
                    rows = pl.ds(pl.multiple_of(cc * C, C), C)
                    chains.append((hh, d, u, rows, lanes))
        stage_a = {}
        states = {}
        pending = []
        for step in range(U + 2):
            for fill in stage_fillers[step:step + 1]:
                fill()
            for n, (hh, d, u, rows, lanes) in enumerate(chains):
                if u == step:
                    x_ref = ff_ref if d == 0 else fb_ref
                    stage_a[n] = _hgrn_chunk_a(x_ref[rows, lanes], lb_ref[d, hh], b_ref.at[n], tri_ref[d])
            ready, pending = pending, []
            for n, (hh, d, u, rows, lanes) in enumerate(chains):
                if u != step - 1:
                    continue
                b, f, k = stage_a[n]
                idx = 2 * hh + d
                st = st_ref[idx] if u == 0 else states[idx]
                q0, stb, attn_b, vb, diag, st = _hgrn_chunk_b(b, f, k, q_ref[rows, lanes], v_ref[rows, lanes], st,
                                                              b_ref.at[n], mask_ref, d)
                states[idx] = st
                if u == U - 1:
                    st_ref[idx] = st
                pending.append((rows, lanes, q0, stb, attn_b, vb, diag))
            for rows, lanes, q0, stb, attn_b, vb, diag in ready:
                oacc_ref[rows, lanes] += (_dot_nt(q0, stb) + _dot(attn_b, vb)) + diag
        assert not pending

    if nc == U:
        run_trip(0, list(fillers))
    else:
        for fill in fillers:
            fill()

        def body(c, carry):
            run_trip(c, [])
            return carry

        lax.fori_loop(0, nc // U, body, 0)

    for hh in range(HGRN_HG):
        lanes = slice(hh * LANE, (hh + 1) * LANE)
        if s_out_ref is not None:
            for d in range(2):
                s_out_ref[d, hh] = st_ref[2 * hh + d].T
        o = oacc_ref[:, lanes]
        ms = jnp.mean(o * o, axis=-1, keepdims=True) * (q_scale * q_scale)
        gate = g_ref[:, lanes]
        y = o * (q_scale * lax.rsqrt(ms + EPS)) * gnw_ref[...] * (gate * _sigmoid(gate))
        o_ref[:, lanes] = y.astype(BF16)


def _hgrn_kernel(*refs, T, has_s0, emit_state, n_alias):
    n_in = 9 + int(has_s0)
    n_out = 2 if emit_state else 1
    assert len(refs) == n_in + n_alias + n_out + 3
    s0_ref = refs[9] if has_s0 else None
    outs = refs[n_in + n_alias:n_in + n_alias + n_out]
    _hgrn_unit(*refs[:9], s0_ref, outs[0], outs[1] if emit_state else None, *refs[-3:], T=T)


def _hgrn_io(st, proj, lbs, gnorm_w, tables, s0, layer, states_prev, unit):
    hg = HGRN_HG
    wide = hg * LANE
    T = st.seq
    C = HGRN_CHUNK

    def at(fn):
        return lambda *g: fn(*unit(*g))

    def col_spec(col0):
        return pl.BlockSpec((T, wide), at(lambda b, h: (b, col0 // hg + h)))

    tri, masks = tables
    in_specs = [col_spec(COL_HQ), col_spec(COL_HI), col_spec(COL_HFF), col_spec(COL_HFB), col_spec(COL_HG),
                pl.BlockSpec((2, None, hg, 1, LANE), at(lambda b, h: (0, layer, h, 0, 0))),
                pl.BlockSpec((None, 1, LANE), at(lambda b, h: (layer, 0, 0))),
                pl.BlockSpec(tri.shape, at(lambda b, h: (0, 0, 0))),
                pl.BlockSpec(masks.shape, at(lambda b, h: (0, 0, 0)))]
    args = [proj, proj, proj, proj, proj, lbs, gnorm_w, tri, masks]
    state_spec = pl.BlockSpec((None, None, 2, hg, HGRN_DK, HGRN_DV), at(lambda b, h: (b, layer, 0, h, 0, 0)))
    if st.latent:
        in_specs.append(state_spec)
        args.append(s0)
    out_specs = [pl.BlockSpec((T, wide), at(lambda b, h: (b, h)))]
    out_shape = [jax.ShapeDtypeStruct((_ntok(st), MIX_WIDTH), BF16)]
    aliases = {}
    if not st.latent:
        out_specs.append(state_spec)
        out_shape.append(jax.ShapeDtypeStruct((st.nseq, DEPTH, 2, HGRN_HEADS, HGRN_DK, HGRN_DV), F32))
        if states_prev is not None:
            aliases = {len(args): 1}
            in_specs.append(pl.BlockSpec(memory_space=pl.ANY))
            args.append(states_prev)
    scratch = [pltpu.VMEM((T, wide), F32), pltpu.VMEM((2 * hg * min(HGRN_UNROLL, T // C), C, LANE), F32),
               pltpu.VMEM((2 * hg, HGRN_DV, HGRN_DK), F32)]
    return in_specs, args, out_specs, out_shape, scratch, aliases


def _hgrn(st, proj, lbs, gnorm_w, tables, s0, layer, states_prev):
    in_specs, args, out_specs, out_shape, scratch, aliases = _hgrn_io(
        st, proj, lbs, gnorm_w, tables, s0, layer, states_prev, lambda b, h: (b, h))
    return pl.pallas_call(
        functools.partial(_hgrn_kernel, T=st.seq, has_s0=st.latent, emit_state=not st.latent, n_alias=len(aliases)),
        grid=(st.nseq, HGRN_HEADS // HGRN_HG),
        in_specs=in_specs,
        out_specs=out_specs,
        out_shape=out_shape,
        scratch_shapes=scratch,
        input_output_aliases=aliases,
        compiler_params=_cparams(("arbitrary", "arbitrary")),
        name="hgrn_" + st.name,
    )(*args)


def _inproj_hgrn_kernel(*refs, T, n_hgrn_in):
    x_ref, sh_ref, sc_ref, nw_ref, w_ref = refs[:5]
    hgrn_in = refs[5:5 + n_hgrn_in]
    o_ref, mix_ref, s_out_ref = refs[5 + n_hgrn_in:5 + n_hgrn_in + 3]
    h_ref = refs[-4]

    @pl.when(pl.program_id(1) == 0)
    def _():
        _norm_modulate(x_ref, h_ref, nw_ref, sh_ref, sc_ref)

    def panel(n):
        cols = slice(n * IN_PANEL, (n + 1) * IN_PANEL)

        def fill():
            o_ref[:, cols] = _dot(h_ref[...], w_ref[:, cols])

        return fill

    _hgrn_unit(*hgrn_in[:9], None, mix_ref, s_out_ref, *refs[-3:], T=T,
               fillers=[panel(n) for n in range(IN_TN // IN_PANEL)])


def _inproj_hgrn(st, x, mods4, layer, norm_w, w_bf16, st2, proj2, lbs, gnorm_w, tables, states_prev):
    assert st.latent and not st2.latent
    ntok = _ntok(st)
    ni, nj = ntok // IN_TM, IN_PROJ_WIDTH // IN_TN
    nh = HGRN_HEADS // HGRN_HG
    assert ni * nj == st2.nseq * nh

    def unit(i, j):
        s = i * nj + j
        return s // nh, s % nh

    h_in, h_args, h_out, h_shape, h_scratch, h_alias = _hgrn_io(st2, proj2, lbs, gnorm_w, tables, None, layer,
                                                                states_prev, unit)
    in_specs = [
        pl.BlockSpec((IN_TM, D_MODEL), lambda i, j: (i, 0)),
        _mod_spec(st, layer, 0, IN_TM, 2),
        _mod_spec(st, layer, 1, IN_TM, 2),
        _layer_row_spec(layer, D_MODEL, 2),
        pl.BlockSpec((D_MODEL, IN_TN), lambda i, j: (0, j)),
    ] + h_in
    args = [x, mods4, mods4, norm_w, w_bf16] + h_args
    return pl.pallas_call(
        functools.partial(_inproj_hgrn_kernel, T=st2.seq, n_hgrn_in=len(h_args)),
        grid=(ni, nj),
        in_specs=in_specs,
        out_specs=[pl.BlockSpec((IN_TM, IN_TN), lambda i, j: (i, j))] + h_out,
        out_shape=[jax.ShapeDtypeStruct((ntok, IN_PROJ_WIDTH), F32)] + h_shape,
        scratch_shapes=[pltpu.VMEM((IN_TM, D_MODEL), BF16)] + h_scratch,
        input_output_aliases={5 + a: 1 + o for a, o in h_alias.items()},
        compiler_params=_cparams(("arbitrary", "arbitrary")),
        name="inproj_" + st.name + "_hgrn_" + st2.name,
    )(*args)


def _conv_seq(cb_ref, cc_ref, cx_ref, w_ref):
    T = cb_ref.shape[0]
    u = cc_ref[...] * cx_ref[...]
    row = lax.broadcasted_iota(jnp.int32, (T, CONV_WIDTH), 0)
    u_prev = jnp.where(row == 0, 0.0, pltpu.roll(u, 1, axis=0))
    u_next = jnp.where(row == T - 1, 0.0, pltpu.roll(u, T - 1, axis=0))
    y = u_prev * w_ref[0:1, :] + u * w_ref[1:2, :] + u_next * w_ref[2:3, :]
    return (cb_ref[...] * y).astype(BF16)


def _conv_specs(T, layer):
    cw = CONV_WIDTH // LANE
    cols = [pl.BlockSpec((T, CONV_WIDTH), lambda b, c=c: (b, c // cw)) for c in (COL_CB, COL_CC, COL_CX)]
    return cols + [pl.BlockSpec((None, 3, CONV_WIDTH), lambda b: (layer, 0, 0))]


MIX_TAIL = CONV_WIDTH + NA_WIDTH
assert HGRN_WIDTH % MIX_TAIL == 0


def _ctx_attn_kernel(q_ref, k_ref, v_ref, cb_ref, cc_ref, cx_ref, cw_ref, mix_ref, o_ref):
    del mix_ref
    o_ref[:, :CONV_WIDTH] = _conv_seq(cb_ref, cc_ref, cx_ref, cw_ref)
    scale = NA_HEAD_DIM ** -0.5
    heads = [slice(h * LANE, (h + 1) * LANE) for h in range(NA_HEADS)]
    scores = [_dot_nt(q_ref[:, hs].astype(BF16), k_ref[:, hs].astype(BF16)) * scale for hs in heads]
    probs = []
    for s in scores:
        p = jnp.exp(s - jnp.max(s, axis=-1, keepdims=True))
        probs.append((p.astype(BF16), jnp.sum(p, axis=-1, keepdims=True)))
    for h, (hs, (p, l)) in enumerate(zip(heads, probs)):
        o_ref[:, CONV_WIDTH + h * LANE:CONV_WIDTH + (h + 1) * LANE] = (
            _dot(p, v_ref[:, hs].astype(BF16)) / l).astype(BF16)


def _ctx_attn(proj, conv_w, layer, mix):
    def col_spec(col0):
        return pl.BlockSpec((SEQ, NA_WIDTH), lambda b: (b, col0 * LANE // NA_WIDTH))

    return pl.pallas_call(
        _ctx_attn_kernel,
        grid=(BATCH,),
        in_specs=[col_spec(COL_NQ), col_spec(COL_NK), col_spec(COL_NV)] + _conv_specs(SEQ, layer)
        + [pl.BlockSpec(memory_space=pl.ANY)],
        out_specs=pl.BlockSpec((SEQ, MIX_TAIL), lambda b: (b, HGRN_WIDTH // MIX_TAIL)),
        out_shape=jax.ShapeDtypeStruct(mix.shape, mix.dtype),
        input_output_aliases={7: 0},
        compiler_params=_cparams(("arbitrary",)),
        name="ctx_attn",
    )(proj, proj, proj, proj, proj, proj, conv_w, mix)


_KH = min(NA_KH, GRID_ROWS)
_ROW_START = [int(v) for v in np.clip(np.arange(GRID_ROWS) - _KH // 2, 0, GRID_ROWS - _KH)]
NLOC = _KH * GRID_W
RPB_ROWS = 2 * NA_KH - 1
RPB_COLS = 2 * NA_KW - 1


def _nat_kernel(q_ref, k_ref, v_ref, kc_ref, vc_ref, bias_ref, cb_ref, cc_ref, cx_ref, cw_ref, mix_ref, o_ref,
                sctx_ref, sloc_ref, pctx_ref, ploc_ref, den_ref):
    del mix_ref
    o_ref[:, :CONV_WIDTH] = _conv_seq(cb_ref, cc_ref, cx_ref, cw_ref)
    scale = NA_HEAD_DIM ** -0.5
    row_slices = [slice(r * GRID_W, (r + 1) * GRID_W) for r in range(GRID_ROWS)]
    bands = [slice(rs * GRID_W, rs * GRID_W + NLOC) for rs in _ROW_START]
    for h in range(NA_HEADS):
        lanes = slice(h * LANE, (h + 1) * LANE)
        q_all = q_ref[:, lanes].astype(BF16)
        k_all = k_ref[:, lanes].astype(BF16)
        v_all = v_ref[:, lanes].astype(BF16)
        sctx_ref[...] = _dot_nt(q_all, kc_ref[:, lanes].astype(BF16)) * scale
        for r, (rows, band) in enumerate(zip(row_slices, bands)):
            sloc_ref[rows, :] = _dot_nt(q_all[rows], k_all[band]) * scale + bias_ref[h, r - _ROW_START[r]]
        for rows in row_slices:
            s_loc = sloc_ref[rows, :]
            s_ctx = sctx_ref[rows, :]
            m = jnp.maximum(jnp.max(s_loc, axis=-1, keepdims=True), jnp.max(s_ctx, axis=-1, keepdims=True))
            p_loc = jnp.exp(s_loc - m)
            p_ctx = jnp.exp(s_ctx - m)
            den_ref[rows, :] = jnp.sum(p_loc, axis=-1, keepdims=True) + jnp.sum(p_ctx, axis=-1, keepdims=True)
            ploc_ref[rows, :] = p_loc.astype(BF16)
            pctx_ref[rows, :] = p_ctx.astype(BF16)
        o_ctx = _dot(pctx_ref[...], vc_ref[:, lanes].astype(BF16))
        for rows, band in zip(row_slices, bands):
            o = (_dot(ploc_ref[rows, :], v_all[band]) + o_ctx[rows]) / den_ref[rows, :]
            o_ref[rows, CONV_WIDTH + h * LANE:CONV_WIDTH + (h + 1) * LANE] = o.astype(BF16)


def _nat_bias_kernel(rpb_ref, o_ref):
    base = (pl.program_id(0) * NA_HEADS + pl.program_id(1)) * (RPB_ROWS * RPB_COLS)
    q = lax.broadcasted_iota(jnp.int32, (GRID_W, GRID_W), 0)
    k = lax.broadcasted_iota(jnp.int32, (GRID_W, GRID_W), 1)
    col_start = jnp.clip(q - NA_KW // 2, 0, GRID_W - NA_KW)
    in_window = (k >= col_start) & (k < col_start + NA_KW)
    dc = jnp.clip(k - q + NA_KW - 1, 0, RPB_COLS - 1)
    tabs = []
    for dr in range(RPB_ROWS):
        t = jnp.zeros((GRID_W, GRID_W), F32)
        for c in range(RPB_COLS):
            t = jnp.where(dc == c, rpb_ref[base + dr * RPB_COLS + c], t)
        tabs.append(jnp.where(in_window, t, NEG_BIG))
    for off in range(_KH):
        for p in range(_KH // 2):
            pair = [tabs[kr - off + NA_KH - 1] for kr in (2 * p, 2 * p + 1)]
            o_ref[off, :, 2 * p * GRID_W:(2 * p + 2) * GRID_W] = jnp.concatenate(pair, axis=1)


def _nat_bias(na_rpb):
    return pl.pallas_call(
        _nat_bias_kernel,
        grid=(DEPTH, NA_HEADS),
        in_specs=[pl.BlockSpec(memory_space=pltpu.SMEM)],
        out_specs=pl.BlockSpec((None, None, _KH, GRID_W, NLOC), lambda l, h: (l, h, 0, 0, 0)),
        out_shape=jax.ShapeDtypeStruct((DEPTH, NA_HEADS, _KH, GRID_W, NLOC), F32),
        compiler_params=_cparams(("arbitrary", "arbitrary")),
        name="nat_bias",
    )(na_rpb.reshape(-1))


def _nat(proj, cache_k, cache_v, bias, conv_w, layer, mix):
    def col_spec(col0):
        return pl.BlockSpec((DEC_SEQ, NA_WIDTH), lambda b: (b, col0 * LANE // NA_WIDTH))

    cache_spec = pl.BlockSpec((None, None, PAST_LEN, NA_WIDTH), lambda b: (b, layer, 0, 0))
    return pl.pallas_call(
        _nat_kernel,
        grid=(DEC_BATCH,),
        in_specs=[col_spec(COL_NQ), col_spec(COL_NK), col_spec(COL_NV), cache_spec, cache_spec,
                  pl.BlockSpec((None, NA_HEADS, _KH, GRID_W, NLOC), lambda b: (layer, 0, 0, 0, 0))]
        + _conv_specs(DEC_SEQ, layer) + [pl.BlockSpec(memory_space=pl.ANY)],
        out_specs=pl.BlockSpec((DEC_SEQ, MIX_TAIL), lambda b: (b, HGRN_WIDTH // MIX_TAIL)),
        out_shape=jax.ShapeDtypeStruct(mix.shape, mix.dtype),
        scratch_shapes=[pltpu.VMEM((DEC_SEQ, PAST_LEN), F32), pltpu.VMEM((DEC_SEQ, NLOC), F32),
                        pltpu.VMEM((DEC_SEQ, PAST_LEN), BF16), pltpu.VMEM((DEC_SEQ, NLOC), BF16),
                        pltpu.VMEM((DEC_SEQ, 1), F32)],
        input_output_aliases={10: 0},
        compiler_params=_cparams(("arbitrary",)),
        name="nat",
    )(proj, proj, proj, cache_k, cache_v, bias, proj, proj, proj, conv_w, mix)


def kernel(x_prompt, x_sample, cache_na_k, cache_na_v, state_hgrn, c, c_ctx, w_ada, b_ada, norm_mix_w, w_in,
           hgrn_lb_raw, hgrn_gnorm_w, conv_w, na_rpb, w_out, norm_ffn_w, w_ffn_gate, w_ffn_up, w_ffn_down,
           final_norm_w):
    xs = {LATENT: x_sample.reshape(_ntok(LATENT), D_MODEL), CONTEXT: x_prompt.reshape(_ntok(CONTEXT), D_MODEL)}
    cvecs = jnp.concatenate([c, c_ctx[None, :], jnp.zeros((MOD_ROWS - DEC_BATCH - 1, D_MODEL), F32)], axis=0)
    mods4 = _adaln(cvecs, w_ada, b_ada).reshape(DEPTH, MOD_ROWS, 1, 6 * D_MODEL)

    p_lb = jax.nn.softmax(hgrn_lb_raw.astype(F32), axis=1)
    cp = jnp.cumsum(p_lb, axis=1)
    lbs = (cp - cp[:, :1]).reshape(2, DEPTH, HGRN_HEADS, 1, LANE)
    gnorm_w = hgrn_gnorm_w.reshape(DEPTH, 1, LANE)
    norm_mix = norm_mix_w.reshape(DEPTH, 1, D_MODEL)
    norm_ffn = norm_ffn_w.reshape(DEPTH, 1, D_MODEL)

    w_out_b = w_out.astype(BF16)
    cache_k4 = cache_na_k.reshape(DEC_BATCH, DEPTH, PAST_LEN, NA_WIDTH)
    cache_v4 = cache_na_v.reshape(DEC_BATCH, DEPTH, PAST_LEN, NA_WIDTH)

    nat_bias = _nat_bias(na_rpb)
    tables = _hgrn_tables()

    kv = None
    states = None
    for l in range(DEPTH):
        last = l == DEPTH - 1
        proj_c, k_c, v_c, w_in_l = _inproj(CONTEXT, xs[CONTEXT], mods4, l, norm_mix, w_in, kv, cast_weights=True,
                                           nblocks=1)
        proj_c, *kv = _inproj(CONTEXT, xs[CONTEXT], mods4, l, norm_mix, w_in_l, (k_c, v_c), block0=1,
                              nblocks=_ntok(CONTEXT) // IN_TM - 1, proj_prev=proj_c)
        proj_l, mix_c, states = _inproj_hgrn(LATENT, xs[LATENT], mods4, l, norm_mix, w_in_l, CONTEXT, proj_c, lbs,
                                             gnorm_w, tables, states)

        mix_l = _hgrn(LATENT, proj_l, lbs, gnorm_w, tables, state_hgrn, l, None)[0]
        mix_l = _nat(proj_l, cache_k4, cache_v4, nat_bias, conv_w, l, mix_l)
        x1 = _outproj(LATENT, xs[LATENT], mix_l, mods4, l, w_out_b)
        y, *ffn_w = _ffn(LATENT, x1, mods4, l, norm_ffn, final_norm_w, (w_ffn_gate, w_ffn_up, w_ffn_down), last,
                         cast_weights=True, nblocks=1)
        xs[LATENT] = _ffn(LATENT, x1, mods4, l, norm_ffn, final_norm_w, ffn_w, last, block0=1,
                          nblocks=_ntok(LATENT) // FFN_TM - 1, out_prev=y)[0]

        mix_c = _ctx_attn(proj_c, conv_w, l, mix_c)
        x1 = _outproj(CONTEXT, xs[CONTEXT], mix_c, mods4, l, w_out_b)
        xs[CONTEXT] = _ffn(CONTEXT, x1, mods4, l, norm_ffn, final_norm_w, ffn_w, last)[0]

    y_sample = xs[LATENT].reshape(DEC_BATCH, DEC_SEQ, D_MODEL)
    y_prompt = xs[CONTEXT].reshape(BATCH, SEQ, D_MODEL)
    new_k = kv[0].reshape(BATCH, DEPTH, SEQ, NA_HEADS, NA_HEAD_DIM)
    new_v = kv[1].reshape(BATCH, DEPTH, SEQ, NA_HEADS, NA_HEAD_DIM)
    return (y_prompt, y_sample, new_k, new_v, states)
```

```python
import collections
import functools

import numpy as np
import jax
import jax.numpy as jnp
from jax import lax
from jax.experimental import pallas as pl
from jax.experimental.pallas import tpu as pltpu

F32 = jnp.float32
BF16 = jnp.bfloat16

D_MODEL = 2048
BATCH = 16
SEQ = 256
DEPTH = 2
DEC_BATCH = 8
DEC_SEQ = 1024
PAST_LEN = 512
GRID_W = 64
HGRN_HEADS = 8
HGRN_DK = 128
HGRN_DV = 128
HGRN_WIDTH = HGRN_HEADS * HGRN_DV
CONV_WIDTH = 512
NA_HEADS = 4
NA_HEAD_DIM = 128
NA_WIDTH = NA_HEADS * NA_HEAD_DIM
NA_KH = 8
NA_KW = 16
MIX_WIDTH = HGRN_WIDTH + CONV_WIDTH + NA_WIDTH
IN_PROJ_WIDTH = 5 * HGRN_WIDTH + 3 * CONV_WIDTH + 3 * NA_WIDTH
FFN_HIDDEN = ((8 * D_MODEL + 3 * 256 - 1) // (3 * 256)) * 256
EPS = 1e-6

MOD_ROWS = 16
CTX_ROW = DEC_BATCH
LANE = 128
SUBLANE = 8
GRID_ROWS = DEC_SEQ // GRID_W

COL_HQ = 0
COL_HI = HGRN_WIDTH // LANE
COL_HFF = 2 * HGRN_WIDTH // LANE
COL_HFB = 3 * HGRN_WIDTH // LANE
COL_HG = 4 * HGRN_WIDTH // LANE
COL_CB = 5 * HGRN_WIDTH // LANE
COL_CC = COL_CB + CONV_WIDTH // LANE
COL_CX = COL_CC + CONV_WIDTH // LANE
COL_NQ = COL_CX + CONV_WIDTH // LANE
COL_NK = COL_NQ + NA_WIDTH // LANE
COL_NV = COL_NK + NA_WIDTH // LANE

NEG_BIG = -1e30
VMEM_LIMIT = 60 * 1024 * 1024

Stream = collections.namedtuple("Stream", ["name", "seq", "nseq", "latent"])
LATENT = Stream("lat", DEC_SEQ, DEC_BATCH, True)
CONTEXT = Stream("ctx", SEQ, BATCH, False)


def _ntok(st):
    return st.seq * st.nseq


def _cparams(sem):
    return pltpu.CompilerParams(dimension_semantics=sem, vmem_limit_bytes=VMEM_LIMIT)


def _dot(a, b):
    return jnp.dot(a, b, preferred_element_type=F32)


def _dot_nt(a, b):
    return lax.dot_general(a, b, (((1,), (1,)), ((), ())), preferred_element_type=F32)


def _dot_tn(a, b):
    return lax.dot_general(a, b, (((0,), (0,)), ((), ())), preferred_element_type=F32)


def _sigmoid(x):
    return 1.0 / (1.0 + jnp.exp(-x))


ADA_TN = 1024


def _adaln_kernel(c_ref, w_ref, b_ref, o_ref):
    c = c_ref[...]
    s = (c * _sigmoid(c)).astype(BF16)
    o_ref[...] = _dot(s, w_ref[...].astype(BF16)) + b_ref[...]


def _adaln(cvecs, w_ada, b_ada):
    n = 6 * D_MODEL
    return pl.pallas_call(
        _adaln_kernel,
        grid=(DEPTH, n // ADA_TN),
        in_specs=[
            pl.BlockSpec((MOD_ROWS, D_MODEL), lambda l, j: (0, 0)),
            pl.BlockSpec((None, D_MODEL, ADA_TN), lambda l, j: (l, 0, j)),
            pl.BlockSpec((None, 1, ADA_TN), lambda l, j: (l, 0, j)),
        ],
        out_specs=pl.BlockSpec((None, MOD_ROWS, ADA_TN), lambda l, j: (l, 0, j)),
        out_shape=jax.ShapeDtypeStruct((DEPTH, MOD_ROWS, n), F32),
        compiler_params=_cparams(("arbitrary", "arbitrary")),
        name="adaln",
    )(cvecs, w_ada, b_ada.reshape(DEPTH, 1, n))


IN_TM = 1024
IN_TN = 1024
IN_PANEL = 256
TM = 512
FFN_TM = 1024
FFN_TF = 512


def _mod_spec(st, layer, chunk, tm, ngrid, block0=0):
    def row(i):
        return ((block0 + i) * tm) // DEC_SEQ if st.latent else CTX_ROW

    if ngrid == 1:
        return pl.BlockSpec((None, None, 1, D_MODEL), lambda i: (layer, row(i), 0, chunk))
    return pl.BlockSpec((None, None, 1, D_MODEL), lambda i, j: (layer, row(i), 0, chunk))


PROLOGUE_ROWS = 128


def _norm_modulate(x_ref, h_ref, nw_ref, sh_ref, sc_ref):
    gain = nw_ref[...] * (1.0 + sc_ref[...])
    shift = sh_ref[...]

    def chunk(r, carry):
        rows = pl.ds(pl.multiple_of(r * PROLOGUE_ROWS, PROLOGUE_ROWS), PROLOGUE_ROWS)
        x = x_ref[rows, :]
        ms = jnp.mean(x * x, axis=-1, keepdims=True)
        h_ref[rows, :] = (x * lax.rsqrt(ms + EPS) * gain + shift).astype(BF16)
        return carry

    lax.fori_loop(0, x_ref.shape[0] // PROLOGUE_ROWS, chunk, 0)


IN_CAST_TN = 512


def _inproj_kernel(*refs, emit_kv, cast_weights, n_alias):
    x_ref, sh_ref, sc_ref, nw_ref, w_ref = refs[:5]
    outs = refs[5 + n_alias:-1]
    o_ref = outs[0]
    h_ref = refs[-1]
    j = pl.program_id(1)
    nj = pl.num_programs(1)

    @pl.when(j == 0)
    def _():
        _norm_modulate(x_ref, h_ref, nw_ref, sh_ref, sc_ref)

    if cast_weights:
        w = w_ref[...].astype(BF16)
        outs[-1][...] = w
    else:
        w = w_ref[...]
    o_ref[...] = _dot(h_ref[...], w)

    if emit_kv:
        k_ref, v_ref = outs[1], outs[2]
        tn = o_ref.shape[1]
        if tn == 2 * NA_WIDTH:
            @pl.when(j == nj - 1)
            def _():
                k_ref[...] = o_ref[:, :NA_WIDTH].reshape(k_ref.shape)
                v_ref[...] = o_ref[:, NA_WIDTH:].reshape(v_ref.shape)
        else:
            assert tn == NA_WIDTH

            @pl.when(j == nj - 2)
            def _():
                k_ref[...] = o_ref[...].reshape(k_ref.shape)

            @pl.when(j == nj - 1)
            def _():
                v_ref[...] = o_ref[...].reshape(v_ref.shape)


def _layer_row_spec(layer, width, ngrid):
    if ngrid == 1:
        return pl.BlockSpec((None, 1, width), lambda i: (layer, 0, 0))
    return pl.BlockSpec((None, 1, width), lambda i, j: (layer, 0, 0))


def _inproj(st, x, mods4, layer, norm_w, w, kv_prev, *, cast_weights=False, block0=0, nblocks=None, proj_prev=None):
    emit_kv = not st.latent
    tn = IN_CAST_TN if cast_weights else IN_TN
    assert COL_NK * LANE == IN_PROJ_WIDTH - 2 * NA_WIDTH and tn in (NA_WIDTH, 2 * NA_WIDTH)
    ntok = _ntok(st)
    nblocks = ntok // IN_TM if nblocks is None else nblocks
    if cast_weights:
        w_spec = pl.BlockSpec((None, D_MODEL, tn), lambda i, j: (layer, 0, j))
    else:
        w_spec = pl.BlockSpec((D_MODEL, tn), lambda i, j: (0, j))
    in_specs = [
        pl.BlockSpec((IN_TM, D_MODEL), lambda i, j: (block0 + i, 0)),
        _mod_spec(st, layer, 0, IN_TM, 2, block0),
        _mod_spec(st, layer, 1, IN_TM, 2, block0),
        _layer_row_spec(layer, D_MODEL, 2),
        w_spec,
    ]
    args = [x, mods4, mods4, norm_w, w]
    out_specs = [pl.BlockSpec((IN_TM, tn), lambda i, j: (block0 + i, j))]
    out_shape = [jax.ShapeDtypeStruct((ntok, IN_PROJ_WIDTH), F32)]
    aliases = {}
    if proj_prev is not None:
        aliases[len(args)] = 0
        in_specs.append(pl.BlockSpec(memory_space=pl.ANY))
        args.append(proj_prev)
    if emit_kv:
        nb = IN_TM // st.seq
        kv_spec = pl.BlockSpec((nb, None, st.seq, NA_WIDTH), lambda i, j: (block0 + i, layer, 0, 0))
        out_specs += [kv_spec, kv_spec]
        out_shape += [jax.ShapeDtypeStruct((st.nseq, DEPTH, st.seq, NA_WIDTH), F32)] * 2
        if kv_prev is not None:
            aliases.update({len(args): 1, len(args) + 1: 2})
            in_specs += [pl.BlockSpec(memory_space=pl.ANY)] * 2
            args += list(kv_prev)
    if cast_weights:
        out_specs.append(pl.BlockSpec((D_MODEL, tn), lambda i, j: (0, j)))
        out_shape.append(jax.ShapeDtypeStruct((D_MODEL, IN_PROJ_WIDTH), BF16))
    return pl.pallas_call(
        functools.partial(_inproj_kernel, emit_kv=emit_kv, cast_weights=cast_weights, n_alias=len(aliases)),
        grid=(nblocks, IN_PROJ_WIDTH // tn),
        in_specs=in_specs,
        out_specs=out_specs,
        out_shape=out_shape,
        scratch_shapes=[pltpu.VMEM((IN_TM, D_MODEL), BF16)],
        input_output_aliases=aliases,
        compiler_params=_cparams(("arbitrary", "arbitrary")),
        name="inproj_" + st.name + ("_cast" if cast_weights else ""),
    )(*args)


OUT_PANEL = 512


def _outproj_kernel(x_ref, m_ref, g_ref, w_ref, o_ref):
    for n in range(0, D_MODEL, OUT_PANEL):
        cols = slice(n, n + OUT_PANEL)
        o_ref[:, cols] = x_ref[:, cols] + g_ref[:, cols] * _dot(m_ref[...], w_ref[:, cols])


def _outproj(st, x, mix, mods4, layer, w_bf16):
    ntok = _ntok(st)
    return pl.pallas_call(
        _outproj_kernel,
        grid=(ntok // TM,),
        in_specs=[
            pl.BlockSpec((TM, D_MODEL), lambda i: (i, 0)),
            pl.BlockSpec((TM, MIX_WIDTH), lambda i: (i, 0)),
            _mod_spec(st, layer, 2, TM, 1),
            pl.BlockSpec((None, MIX_WIDTH, D_MODEL), lambda i: (layer, 0, 0), pipeline_mode=pl.Buffered(1)),
        ],
        out_specs=pl.BlockSpec((TM, D_MODEL), lambda i: (i, 0)),
        out_shape=jax.ShapeDtypeStruct((ntok, D_MODEL), F32),
        compiler_params=_cparams(("arbitrary",)),
        name="outproj_" + st.name,
    )(x, mix, mods4, w_bf16)


FFN_CAST_TF = 256
FFN_PANEL = 256


def _ffn_prologue(x_ref, h_ref, o_ref, nw_ref, sh_ref, sc_ref):
    _norm_modulate(x_ref, h_ref, nw_ref, sh_ref, sc_ref)
    o_ref[...] = jnp.zeros_like(o_ref)


def _ffn_fillers(h_ref, wg, wu, wd, o_ref):
    hidden = []

    def gate_up(n):
        def fill():
            h = h_ref[...]
            a = _dot(h, wg[:, n:n + FFN_PANEL])
            u = _dot(h, wu[:, n:n + FFN_PANEL])
            hidden.append((a * _sigmoid(a) * u).astype(BF16))
        return fill

    def down(n):
        def fill():
            t = hidden[0] if len(hidden) == 1 else jnp.concatenate(hidden, axis=1)
            o_ref[:, n:n + FFN_TF] += _dot(t, wd[:, n:n + FFN_TF])
        return fill

    return [gate_up(n) for n in range(0, wg.shape[1], FFN_PANEL)] + [down(n) for n in range(0, D_MODEL, FFN_TF)]


def _ffn_epilogue(x_ref, o_ref, g_ref, fw_ref, final_norm):
    gate = g_ref[...]

    def chunk(r, carry):
        rows = pl.ds(pl.multiple_of(r * PROLOGUE_ROWS, PROLOGUE_ROWS), PROLOGUE_ROWS)
        y = x_ref[rows, :] + gate * o_ref[rows, :]
        if final_norm:
            ms = jnp.mean(y * y, axis=-1, keepdims=True)
            y = y * lax.rsqrt(ms + EPS) * fw_ref[...]
        o_ref[rows, :] = y
        return carry

    lax.fori_loop(0, x_ref.shape[0] // PROLOGUE_ROWS, chunk, 0)


def _ffn_kernel(*refs, final_norm, cast_weights, n_alias):
    x_ref, sh_ref, sc_ref, g_ref, nw_ref, fw_ref, wg_ref, wu_ref, wd_ref = refs[:9]
    outs = refs[9 + n_alias:-1]
    o_ref = outs[0]
    h_ref = refs[-1]
    j = pl.program_id(1)

    @pl.when(j == 0)
    def _():
        _ffn_prologue(x_ref, h_ref, o_ref, nw_ref, sh_ref, sc_ref)

    if cast_weights:
        wg, wu, wd = (w[...].astype(BF16) for w in (wg_ref, wu_ref, wd_ref))
        for w, w_out_ref in zip((wg, wu, wd), outs[1:]):
            w_out_ref[...] = w
    else:
        wg, wu, wd = wg_ref[...], wu_ref[...], wd_ref[...]
    for fill in _ffn_fillers(h_ref, wg, wu, wd, o_ref):
        fill()

    @pl.when(j == pl.num_programs(1) - 1)
    def _():
        _ffn_epilogue(x_ref, o_ref, g_ref, fw_ref, final_norm)


def _ffn(st, x, mods4, layer, norm_w, final_w, weights, final_norm, *, cast_weights=False, block0=0, nblocks=None,
         out_prev=None, tm=FFN_TM):
    ntok = _ntok(st)
    TM = tm
    tf = FFN_CAST_TF if cast_weights else FFN_TF
    nblocks = ntok // TM if nblocks is None else nblocks
    if cast_weights:
        w_specs = [pl.BlockSpec((None, D_MODEL, tf), lambda i, j: (layer, 0, j)),
                   pl.BlockSpec((None, D_MODEL, tf), lambda i, j: (layer, 0, j)),
                   pl.BlockSpec((None, tf, D_MODEL), lambda i, j: (layer, j, 0))]
    else:
        w_specs = [pl.BlockSpec((D_MODEL, tf), lambda i, j: (0, j)),
                   pl.BlockSpec((D_MODEL, tf), lambda i, j: (0, j)),
                   pl.BlockSpec((tf, D_MODEL), lambda i, j: (j, 0))]
    in_specs = [
        pl.BlockSpec((TM, D_MODEL), lambda i, j: (block0 + i, 0)),
        _mod_spec(st, layer, 3, TM, 2, block0),
        _mod_spec(st, layer, 4, TM, 2, block0),
        _mod_spec(st, layer, 5, TM, 2, block0),
        _layer_row_spec(layer, D_MODEL, 2),
        pl.BlockSpec((1, D_MODEL), lambda i, j: (0, 0)),
    ] + w_specs
    args = [x, mods4, mods4, mods4, norm_w, final_w.reshape(1, D_MODEL)] + list(weights)
    out_specs = [pl.BlockSpec((TM, D_MODEL), lambda i, j: (block0 + i, 0))]
    out_shape = [jax.ShapeDtypeStruct((ntok, D_MODEL), F32)]
    if cast_weights:
        out_specs += [pl.BlockSpec((D_MODEL, tf), lambda i, j: (0, j)),
                      pl.BlockSpec((D_MODEL, tf), lambda i, j: (0, j)),
                      pl.BlockSpec((tf, D_MODEL), lambda i, j: (j, 0))]
        out_shape += [jax.ShapeDtypeStruct((D_MODEL, FFN_HIDDEN), BF16)] * 2
        out_shape += [jax.ShapeDtypeStruct((FFN_HIDDEN, D_MODEL), BF16)]
    aliases = {}
    if out_prev is not None:
        aliases = {len(args): 0}
        in_specs.append(pl.BlockSpec(memory_space=pl.ANY))
        args.append(out_prev)
    return pl.pallas_call(
        functools.partial(_ffn_kernel, final_norm=final_norm, cast_weights=cast_weights, n_alias=len(aliases)),
        grid=(nblocks, FFN_HIDDEN // tf),
        in_specs=in_specs,
        out_specs=out_specs,
        out_shape=out_shape,
        scratch_shapes=[pltpu.VMEM((TM, D_MODEL), BF16)],
        input_output_aliases=aliases,
        compiler_params=_cparams(("arbitrary", "arbitrary")),
        name="ffn_" + st.name + ("_cast" if cast_weights else ""),
    )(*args)


HGRN_CHUNK = 64
HGRN_HG = 2
HGRN_UNROLL = 16
HGRN_LEVELS = [HGRN_CHUNK >> (i + 1) for i in range(HGRN_CHUNK.bit_length() - 1)]


LOG2E = 1.4426950408889634


def _hgrn_gates(x, lb):
    e = jnp.exp(-jnp.abs(x))
    r = 1.0 / (1.0 + e)
    er = e * r
    pos = x >= 0.0
    sig = jnp.where(pos, r, er)
    nsig = jnp.where(pos, er, r)
    ls2 = jnp.minimum(x, 0.0) * LOG2E + jnp.log2(r)
    f = lb + (1.0 - lb) * sig
    log2_f = jnp.where(lb > 0.0, jnp.log2(f), ls2)
    return log2_f, f, (1.0 - lb) * nsig


def _hgrn_chunk_a(x, lb, b_ref, tri):
    g, f, k = _hgrn_gates(x, lb)
    g_hi = g.astype(BF16)
    g_lo = (g - g_hi.astype(F32)).astype(BF16)
    b = _dot(tri, jnp.concatenate([g_hi, g_lo], axis=0))
    b_ref[...] = b
    return b, f, k


def _hgrn_chunk_b(b, f, k, q, v, st, b_ref, mask_ref, dirn):
    C = HGRN_CHUNK
    fwd = dirn == 0
    b_end = b_ref[pl.ds(C - 1 if fwd else 0, 1), :]
    vb = v.astype(BF16)
    q0 = (q * jnp.exp2(b)).astype(BF16)
    stb = st.astype(BF16)
    kd = (k * jnp.exp2(b_end - b)).astype(BF16)
    st_new = jnp.exp2(b_end) * st + _dot_tn(vb, kd)

    row = lax.broadcasted_iota(jnp.int32, (C, LANE), 0)
    sub = lax.broadcasted_iota(jnp.int32, (SUBLANE, LANE), 0)
    attn = [None] * (C // SUBLANE)

    def add_rows(first_row, term):
        for t in range(term.shape[0] // SUBLANE):
            blk = first_row // SUBLANE + t
            piece = term[t * SUBLANE:(t + 1) * SUBLANE]
            attn[blk] = piece if attn[blk] is None else attn[blk] + piece

    for lvl, m in enumerate(HGRN_LEVELS):
        mask_idx = dirn * len(HGRN_LEVELS) + lvl
        if m >= SUBLANE:
            parts, readers = [], []
            for p in range(C // (2 * m)):
                lo = p * 2 * m
                left = slice(lo, lo + m)
                right = slice(lo + m, lo + 2 * m)
                beta = b_ref[pl.ds(lo + (m - 1 if fwd else m), 1), :]
                if fwd:
                    parts += [k[left] * jnp.exp2(beta - b[left]), q[right] * jnp.exp2(b[right] - beta)]
                    readers.append((lo + m, parts[-1]))
                else:
                    parts += [q[left] * jnp.exp2(b[left] - beta), k[right] * jnp.exp2(beta - b[right])]
                    readers.append((lo, parts[-2]))
            wb = jnp.concatenate(parts, axis=0).astype(BF16)
            wq = jnp.concatenate([r for _, r in readers], axis=0).astype(BF16)
            term = _dot_nt(wq, wb)
            for n, (first_row, _) in enumerate(readers):
                add_rows(first_row, term[n * m:(n + 1) * m] * mask_ref[mask_idx, first_row:first_row + m, :])
        else:
            reads = ((row & m) != 0) if fwd else ((row & m) == 0)
            if m == 1:
                w = jnp.where(reads, q * f, k)
            else:
                betas = []
                for u in range(C // SUBLANE):
                    rows = []
                    for p in range(SUBLANE // (2 * m)):
                        rows.append(b_ref[pl.ds(u * SUBLANE + p * 2 * m + (m - 1 if fwd else m), 1), :])
                    beta_u = rows[-1]
                    for p in range(len(rows) - 2, -1, -1):
                        beta_u = jnp.where(sub < (p + 1) * 2 * m, rows[p], beta_u)
                    betas.append(jnp.broadcast_to(beta_u, (SUBLANE, LANE)))
                beta = jnp.concatenate(betas, axis=0)
                w = jnp.where(reads, q, k) * jnp.exp2(-jnp.abs(b - beta))
            wb = w.astype(BF16)
            add_rows(0, _dot_nt(wb, wb) * mask_ref[mask_idx])
    attn_b = jnp.concatenate(attn, axis=0).astype(BF16)
    diag = jnp.sum(q * k, axis=-1, keepdims=True) * v
    return q0, stb, attn_b, vb, diag, st_new


def _hgrn_tables():
    C = HGRN_CHUNK
    row, col = np.indices((C, C))
    tri = np.stack([col <= row, col >= row])
    tri = np.concatenate([tri, tri], axis=2)
    fwd, bwd = [], []
    for m in HGRN_LEVELS:
        same = (row // (2 * m)) == (col // (2 * m))
        fwd.append(same & ((row & m) != 0) & ((col & m) == 0))
        bwd.append(same & ((row & m) == 0) & ((col & m) != 0))
    return jnp.asarray(tri, BF16), jnp.asarray(np.stack(fwd + bwd), F32)


def _hgrn_unit(q_ref, v_ref, ff_ref, fb_ref, g_ref, lb_ref, gnw_ref, tri_ref, mask_ref, s0_ref, o_ref, s_out_ref,
               oacc_ref, b_ref, st_ref, *, T, fillers=()):
    C = HGRN_CHUNK
    U = min(HGRN_UNROLL, T // C)
    nc = T // C
    assert nc % U == 0 and len(fillers) <= U + 2
    q_scale = HGRN_DK ** -0.5

    oacc_ref[...] = jnp.zeros_like(oacc_ref)
    for hh in range(HGRN_HG):
        for d in range(2):
            if s0_ref is not None:
                st_ref[2 * hh + d] = s0_ref[d, hh].T
            else:
                st_ref[2 * hh + d] = jnp.zeros((HGRN_DV, HGRN_DK), F32)

    def run_trip(c, stage_fillers):
        chains = []
        for hh in range(HGRN_HG):
            lanes = slice(hh * LANE, (hh + 1) * LANE)
            for d in range(2):
                for u in range(U):
                    cc = c * U + u if d == 0 else nc - 1 - (c * U + u)
                    rows = pl.ds(pl.multiple_of(cc * C, C), C)
                    chains.append((hh, d, u, rows, lanes))
        stage_a = {}
        states = {}
        pending = []
        for step in range(U + 2):
            for fill in stage_fillers[step:step + 1]:
                fill()
            for n, (hh, d, u, rows, lanes) in enumerate(chains):
                if u == step:
                    x_ref = ff_ref if d == 0 else fb_ref
                    stage_a[n] = _hgrn_chunk_a(x_ref[rows, lanes], lb_ref[d, hh], b_ref.at[n], tri_ref[d])
            ready, pending = pending, []
            for n, (hh, d, u, rows, lanes) in enumerate(chains):
                if u != step - 1:
                    continue
                b, f, k = stage_a[n]
                idx = 2 * hh + d
                st = st_ref[idx] if u == 0 else states[idx]
                q0, stb, attn_b, vb, diag, st = _hgrn_chunk_b(b, f, k, q_ref[rows, lanes], v_ref[rows, lanes], st,
                                                              b_ref.at[n], mask_ref, d)
                states[idx] = st
                if u == U - 1:
                    st_ref[idx] = st
                pending.append((rows, lanes, q0, stb, attn_b, vb, diag))
            for rows, lanes, q0, stb, attn_b, vb, diag in ready:
                oacc_ref[rows, lanes] += (_dot_nt(q0, stb) + _dot(attn_b, vb)) + diag
        assert not pending

    if nc == U:
        run_trip(0, list(fillers))
    else:
        for fill in fillers:
            fill()

        def body(c, carry):
            run_trip(c, [])
            return carry

        lax.fori_loop(0, nc // U, body, 0)

    for hh in range(HGRN_HG):
        lanes = slice(hh * LANE, (hh + 1) * LANE)
        if s_out_ref is not None:
            for d in range(2):
                s_out_ref[d, hh] = st_ref[2 * hh + d].T
        o = oacc_ref[:, lanes]
        ms = jnp.mean(o * o, axis=-1, keepdims=True) * (q_scale * q_scale)
        gate = g_ref[:, lanes]
        y = o * (q_scale * lax.rsqrt(ms + EPS)) * gnw_ref[...] * (gate * _sigmoid(gate))
        o_ref[:, lanes] = y.astype(BF16)


def _hgrn_kernel(*refs, T, has_s0, emit_state, n_alias):
    n_in = 9 + int(has_s0)
    n_out = 2 if emit_state else 1
    assert len(refs) == n_in + n_alias + n_out + 3
    s0_ref = refs[9] if has_s0 else None
    outs = refs[n_in + n_alias:n_in + n_alias + n_out]
    _hgrn_unit(*refs[:9], s0_ref, outs[0], outs[1] if emit_state else None, *refs[-3:], T=T)


def _hgrn_io(st, proj, lbs, gnorm_w, tables, s0, layer, states_prev, unit):
    hg = HGRN_HG
    wide = hg * LANE
    T = st.seq
    C = HGRN_CHUNK

    def at(fn):
        return lambda *g: fn(*unit(*g))

    def col_spec(col0):
        return pl.BlockSpec((T, wide), at(lambda b, h: (b, col0 // hg + h)))

    tri, masks = tables
    in_specs = [col_spec(COL_HQ), col_spec(COL_HI), col_spec(COL_HFF), col_spec(COL_HFB), col_spec(COL_HG),
                pl.BlockSpec((2, None, hg, 1, LANE), at(lambda b, h: (0, layer, h, 0, 0))),
                pl.BlockSpec((None, 1, LANE), at(lambda b, h: (layer, 0, 0))),
                pl.BlockSpec(tri.shape, at(lambda b, h: (0, 0, 0))),
                pl.BlockSpec(masks.shape, at(lambda b, h: (0, 0, 0)))]
    args = [proj, proj, proj, proj, proj, lbs, gnorm_w, tri, masks]
    state_spec = pl.BlockSpec((None, None, 2, hg, HGRN_DK, HGRN_DV), at(lambda b, h: (b, layer, 0, h, 0, 0)))
    if st.latent:
        in_specs.append(state_spec)
        args.append(s0)
    out_specs = [pl.BlockSpec((T, wide), at(lambda b, h: (b, h)))]
    out_shape = [jax.ShapeDtypeStruct((_ntok(st), MIX_WIDTH), BF16)]
    aliases = {}
    if not st.latent:
        out_specs.append(state_spec)
        out_shape.append(jax.ShapeDtypeStruct((st.nseq, DEPTH, 2, HGRN_HEADS, HGRN_DK, HGRN_DV), F32))
        if states_prev is not None:
            aliases = {len(args): 1}
            in_specs.append(pl.BlockSpec(memory_space=pl.ANY))
            args.append(states_prev)
    scratch = [pltpu.VMEM((T, wide), F32), pltpu.VMEM((2 * hg * min(HGRN_UNROLL, T // C), C, LANE), F32),
               pltpu.VMEM((2 * hg, HGRN_DV, HGRN_DK), F32)]
    return in_specs, args, out_specs, out_shape, scratch, aliases


def _hgrn(st, proj, lbs, gnorm_w, tables, s0, layer, states_prev):
    in_specs, args, out_specs, out_shape, scratch, aliases = _hgrn_io(
        st, proj, lbs, gnorm_w, tables, s0, layer, states_prev, lambda b, h: (b, h))
    return pl.pallas_call(
        functools.partial(_hgrn_kernel, T=st.seq, has_s0=st.latent, emit_state=not st.latent, n_alias=len(aliases)),
        grid=(st.nseq, HGRN_HEADS // HGRN_HG),
        in_specs=in_specs,
        out_specs=out_specs,
        out_shape=out_shape,
        scratch_shapes=scratch,
        input_output_aliases=aliases,
        compiler_params=_cparams(("arbitrary", "arbitrary")),
        name="hgrn_" + st.name,
    )(*args)


def _inproj_hgrn_kernel(*refs, T, n_hgrn_in):
    x_ref, sh_ref, sc_ref, nw_ref, w_ref = refs[:5]
    hgrn_in = refs[5:5 + n_hgrn_in]
    o_ref, mix_ref, s_out_ref = refs[5 + n_hgrn_in:5 + n_hgrn_in + 3]
    h_ref = refs[-4]

    @pl.when(pl.program_id(1) == 0)
    def _():
        _norm_modulate(x_ref, h_ref, nw_ref, sh_ref, sc_ref)

    def panel(n):
        cols = slice(n * IN_PANEL, (n + 1) * IN_PANEL)

        def fill():
            o_ref[:, cols] = _dot(h_ref[...], w_ref[:, cols])

        return fill

    _hgrn_unit(*hgrn_in[:9], None, mix_ref, s_out_ref, *refs[-3:], T=T,
               fillers=[panel(n) for n in range(IN_TN // IN_PANEL)])


def _inproj_hgrn(st, x, mods4, layer, norm_w, w_bf16, st2, proj2, lbs, gnorm_w, tables, states_prev):
    assert st.latent and not st2.latent
    ntok = _ntok(st)
    ni, nj = ntok // IN_TM, IN_PROJ_WIDTH // IN_TN
    nh = HGRN_HEADS // HGRN_HG
    assert ni * nj == st2.nseq * nh

    def unit(i, j):
        s = i * nj + j
        return s // nh, s % nh

    h_in, h_args, h_out, h_shape, h_scratch, h_alias = _hgrn_io(st2, proj2, lbs, gnorm_w, tables, None, layer,
                                                                states_prev, unit)
    in_specs = [
        pl.BlockSpec((IN_TM, D_MODEL), lambda i, j: (i, 0)),
        _mod_spec(st, layer, 0, IN_TM, 2),
        _mod_spec(st, layer, 1, IN_TM, 2),
        _layer_row_spec(layer, D_MODEL, 2),
        pl.BlockSpec((D_MODEL, IN_TN), lambda i, j: (0, j)),
    ] + h_in
    args = [x, mods4, mods4, norm_w, w_bf16] + h_args
    return pl.pallas_call(
        functools.partial(_inproj_hgrn_kernel, T=st2.seq, n_hgrn_in=len(h_args)),
        grid=(ni, nj),
        in_specs=in_specs,
        out_specs=[pl.BlockSpec((IN_TM, IN_TN), lambda i, j: (i, j))] + h_out,
        out_shape=[jax.ShapeDtypeStruct((ntok, IN_PROJ_WIDTH), F32)] + h_shape,
        scratch_shapes=[pltpu.VMEM((IN_TM, D_MODEL), BF16)] + h_scratch,
        input_output_aliases={5 + a: 1 + o for a, o in h_alias.items()},
        compiler_params=_cparams(("arbitrary", "arbitrary")),
        name="inproj_" + st.name + "_hgrn_" + st2.name,
    )(*args)


def _conv_seq(cb_ref, cc_ref, cx_ref, w_ref):
    T = cb_ref.shape[0]
    u = cc_ref[...] * cx_ref[...]
    row = lax.broadcasted_iota(jnp.int32, (T, CONV_WIDTH), 0)
    u_prev = jnp.where(row == 0, 0.0, pltpu.roll(u, 1, axis=0))
    u_next = jnp.where(row == T - 1, 0.0, pltpu.roll(u, T - 1, axis=0))
    y = u_prev * w_ref[0:1, :] + u * w_ref[1:2, :] + u_next * w_ref[2:3, :]
    return (cb_ref[...] * y).astype(BF16)


def _conv_specs(T, layer):
    cw = CONV_WIDTH // LANE
    cols = [pl.BlockSpec((T, CONV_WIDTH), lambda b, c=c: (b, c // cw)) for c in (COL_CB, COL_CC, COL_CX)]
    return cols + [pl.BlockSpec((None, 3, CONV_WIDTH), lambda b: (layer, 0, 0))]


MIX_TAIL = CONV_WIDTH + NA_WIDTH
assert HGRN_WIDTH % MIX_TAIL == 0


def _ctx_attn_kernel(q_ref, k_ref, v_ref, cb_ref, cc_ref, cx_ref, cw_ref, mix_ref, o_ref):
    del mix_ref
    o_ref[:, :CONV_WIDTH] = _conv_seq(cb_ref, cc_ref, cx_ref, cw_ref)
    scale = NA_HEAD_DIM ** -0.5
    heads = [slice(h * LANE, (h + 1) * LANE) for h in range(NA_HEADS)]
    scores = [_dot_nt(q_ref[:, hs].astype(BF16), k_ref[:, hs].astype(BF16)) * scale for hs in heads]
    probs = []
    for s in scores:
        p = jnp.exp(s - jnp.max(s, axis=-1, keepdims=True))
        probs.append((p.astype(BF16), jnp.sum(p, axis=-1, keepdims=True)))
    for h, (hs, (p, l)) in enumerate(zip(heads, probs)):
        o_ref[:, CONV_WIDTH + h * LANE:CONV_WIDTH + (h + 1) * LANE] = (
            _dot(p, v_ref[:, hs].astype(BF16)) / l).astype(BF16)


def _ctx_attn(proj, conv_w, layer, mix):
    def col_spec(col0):
        return pl.BlockSpec((SEQ, NA_WIDTH), lambda b: (b, col0 * LANE // NA_WIDTH))

    return pl.pallas_call(
        _ctx_attn_kernel,
        grid=(BATCH,),
        in_specs=[col_spec(COL_NQ), col_spec(COL_NK), col_spec(COL_NV)] + _conv_specs(SEQ, layer)
        + [pl.BlockSpec(memory_space=pl.ANY)],
        out_specs=pl.BlockSpec((SEQ, MIX_TAIL), lambda b: (b, HGRN_WIDTH // MIX_TAIL)),
        out_shape=jax.ShapeDtypeStruct(mix.shape, mix.dtype),
        input_output_aliases={7: 0},
        compiler_params=_cparams(("arbitrary",)),
        name="ctx_attn",
    )(proj, proj, proj, proj, proj, proj, conv_w, mix)


_KH = min(NA_KH, GRID_ROWS)
_ROW_START = [int(v) for v in np.clip(np.arange(GRID_ROWS) - _KH // 2, 0, GRID_ROWS - _KH)]
NLOC = _KH * GRID_W
RPB_ROWS = 2 * NA_KH - 1
RPB_COLS = 2 * NA_KW - 1


def _nat_kernel(q_ref, k_ref, v_ref, kc_ref, vc_ref, bias_ref, cb_ref, cc_ref, cx_ref, cw_ref, mix_ref, o_ref,
                sctx_ref, sloc_ref, pctx_ref, ploc_ref, den_ref):
    del mix_ref
    o_ref[:, :CONV_WIDTH] = _conv_seq(cb_ref, cc_ref, cx_ref, cw_ref)
    scale = NA_HEAD_DIM ** -0.5
    row_slices = [slice(r * GRID_W, (r + 1) * GRID_W) for r in range(GRID_ROWS)]
    bands = [slice(rs * GRID_W, rs * GRID_W + NLOC) for rs in _ROW_START]
    for h in range(NA_HEADS):
        lanes = slice(h * LANE, (h + 1) * LANE)
        q_all = q_ref[:, lanes].astype(BF16)
        k_all = k_ref[:, lanes].astype(BF16)
        v_all = v_ref[:, lanes].astype(BF16)
        sctx_ref[...] = _dot_nt(q_all, kc_ref[:, h, :].astype(BF16)) * scale
        for r, (rows, band) in enumerate(zip(row_slices, bands)):
            sloc_ref[rows, :] = _dot_nt(q_all[rows], k_all[band]) * scale + bias_ref[h, r - _ROW_START[r]]
        for rows in row_slices:
            s_loc = sloc_ref[rows, :]
            s_ctx = sctx_ref[rows, :]
            m = jnp.maximum(jnp.max(s_loc, axis=-1, keepdims=True), jnp.max(s_ctx, axis=-1, keepdims=True))
            p_loc = jnp.exp(s_loc - m)
            p_ctx = jnp.exp(s_ctx - m)
            den_ref[rows, :] = jnp.sum(p_loc, axis=-1, keepdims=True) + jnp.sum(p_ctx, axis=-1, keepdims=True)
            ploc_ref[rows, :] = p_loc.astype(BF16)
            pctx_ref[rows, :] = p_ctx.astype(BF16)
        o_ctx = _dot(pctx_ref[...], vc_ref[:, h, :].astype(BF16))
        for rows, band in zip(row_slices, bands):
            o = (_dot(ploc_ref[rows, :], v_all[band]) + o_ctx[rows]) / den_ref[rows, :]
            o_ref[rows, CONV_WIDTH + h * LANE:CONV_WIDTH + (h + 1) * LANE] = o.astype(BF16)


def _nat_bias_kernel(rpb_ref, o_ref):
    base = (pl.program_id(0) * NA_HEADS + pl.program_id(1)) * (RPB_ROWS * RPB_COLS)
    q = lax.broadcasted_iota(jnp.int32, (GRID_W, GRID_W), 0)
    k = lax.broadcasted_iota(jnp.int32, (GRID_W, GRID_W), 1)
    col_start = jnp.clip(q - NA_KW // 2, 0, GRID_W - NA_KW)
    in_window = (k >= col_start) & (k < col_start + NA_KW)
    dc = jnp.clip(k - q + NA_KW - 1, 0, RPB_COLS - 1)
    tabs = []
    for dr in range(RPB_ROWS):
        t = jnp.zeros((GRID_W, GRID_W), F32)
        for c in range(RPB_COLS):
            t = jnp.where(dc == c, rpb_ref[base + dr * RPB_COLS + c], t)
        tabs.append(jnp.where(in_window, t, NEG_BIG))
    for off in range(_KH):
        for p in range(_KH // 2):
            pair = [tabs[kr - off + NA_KH - 1] for kr in (2 * p, 2 * p + 1)]
            o_ref[off, :, 2 * p * GRID_W:(2 * p + 2) * GRID_W] = jnp.concatenate(pair, axis=1)


def _nat_bias(na_rpb):
    return pl.pallas_call(
        _nat_bias_kernel,
        grid=(DEPTH, NA_HEADS),
        in_specs=[pl.BlockSpec(memory_space=pltpu.SMEM)],
        out_specs=pl.BlockSpec((None, None, _KH, GRID_W, NLOC), lambda l, h: (l, h, 0, 0, 0)),
        out_shape=jax.ShapeDtypeStruct((DEPTH, NA_HEADS, _KH, GRID_W, NLOC), F32),
        compiler_params=_cparams(("arbitrary", "arbitrary")),
        name="nat_bias",
    )(na_rpb.reshape(-1))


def _nat(proj, cache_k, cache_v, bias, conv_w, layer, mix):
    def col_spec(col0):
        return pl.BlockSpec((DEC_SEQ, NA_WIDTH), lambda b: (b, col0 * LANE // NA_WIDTH))

    cache_spec = pl.BlockSpec((None, None, PAST_LEN, NA_HEADS, NA_HEAD_DIM), lambda b: (b, layer, 0, 0, 0))
    return pl.pallas_call(
        _nat_kernel,
        grid=(DEC_BATCH,),
        in_specs=[col_spec(COL_NQ), col_spec(COL_NK), col_spec(COL_NV), cache_spec, cache_spec,
                  pl.BlockSpec((None, NA_HEADS, _KH, GRID_W, NLOC), lambda b: (layer, 0, 0, 0, 0),
                               pipeline_mode=pl.Buffered(1))]
        + _conv_specs(DEC_SEQ, layer) + [pl.BlockSpec(memory_space=pl.ANY)],
        out_specs=pl.BlockSpec((DEC_SEQ, MIX_TAIL), lambda b: (b, HGRN_WIDTH // MIX_TAIL)),
        out_shape=jax.ShapeDtypeStruct(mix.shape, mix.dtype),
        scratch_shapes=[pltpu.VMEM((DEC_SEQ, PAST_LEN), F32), pltpu.VMEM((DEC_SEQ, NLOC), F32),
                        pltpu.VMEM((DEC_SEQ, PAST_LEN), BF16), pltpu.VMEM((DEC_SEQ, NLOC), BF16),
                        pltpu.VMEM((DEC_SEQ, 1), F32)],
        input_output_aliases={10: 0},
        compiler_params=_cparams(("arbitrary",)),
        name="nat",
    )(proj, proj, proj, cache_k, cache_v, bias, proj, proj, proj, conv_w, mix)


def kernel(x_prompt, x_sample, cache_na_k, cache_na_v, state_hgrn, c, c_ctx, w_ada, b_ada, norm_mix_w, w_in,
           hgrn_lb_raw, hgrn_gnorm_w, conv_w, na_rpb, w_out, norm_ffn_w, w_ffn_gate, w_ffn_up, w_ffn_down,
           final_norm_w):
    xs = {LATENT: x_sample.reshape(_ntok(LATENT), D_MODEL), CONTEXT: x_prompt.reshape(_ntok(CONTEXT), D_MODEL)}
    cvecs = jnp.concatenate([c, c_ctx[None, :], jnp.zeros((MOD_ROWS - DEC_BATCH - 1, D_MODEL), F32)], axis=0)
    mods4 = _adaln(cvecs, w_ada, b_ada).reshape(DEPTH, MOD_ROWS, 1, 6 * D_MODEL)

    p_lb = jax.nn.softmax(hgrn_lb_raw.astype(F32), axis=1)
    cp = jnp.cumsum(p_lb, axis=1)
    lbs = (cp - cp[:, :1]).reshape(2, DEPTH, HGRN_HEADS, 1, LANE)
    gnorm_w = hgrn_gnorm_w.reshape(DEPTH, 1, LANE)
    norm_mix = norm_mix_w.reshape(DEPTH, 1, D_MODEL)
    norm_ffn = norm_ffn_w.reshape(DEPTH, 1, D_MODEL)

    w_out_b = w_out.astype(BF16)

    nat_bias = _nat_bias(na_rpb)
    tables = _hgrn_tables()

    kv = None
    states = None
    for l in range(DEPTH):
        last = l == DEPTH - 1
        proj_c, k_c, v_c, w_in_l = _inproj(CONTEXT, xs[CONTEXT], mods4, l, norm_mix, w_in, kv, cast_weights=True,
                                           nblocks=1)
        proj_c, *kv = _inproj(CONTEXT, xs[CONTEXT], mods4, l, norm_mix, w_in_l, (k_c, v_c), block0=1,
                              nblocks=_ntok(CONTEXT) // IN_TM - 1, proj_prev=proj_c)
        proj_l, mix_c, states = _inproj_hgrn(LATENT, xs[LATENT], mods4, l, norm_mix, w_in_l, CONTEXT, proj_c, lbs,
                                             gnorm_w, tables, states)

        mix_l = _hgrn(LATENT, proj_l, lbs, gnorm_w, tables, state_hgrn, l, None)[0]
        mix_l = _nat(proj_l, cache_na_k, cache_na_v, nat_bias, conv_w, l, mix_l)
        x1 = _outproj(LATENT, xs[LATENT], mix_l, mods4, l, w_out_b)
        y, *ffn_w = _ffn(LATENT, x1, mods4, l, norm_ffn, final_norm_w, (w_ffn_gate, w_ffn_up, w_ffn_down), last,
                         cast_weights=True, nblocks=1)
        xs[LATENT] = _ffn(LATENT, x1, mods4, l, norm_ffn, final_norm_w, ffn_w, last, block0=1,
                          nblocks=_ntok(LATENT) // FFN_TM - 1, out_prev=y)[0]

        mix_c = _ctx_attn(proj_c, conv_w, l, mix_c)
        x1 = _outproj(CONTEXT, xs[CONTEXT], mix_c, mods4, l, w_out_b)
        xs[CONTEXT] = _ffn(CONTEXT, x1, mods4, l, norm_ffn, final_norm_w, ffn_w, last)[0]

    y_sample = xs[LATENT].reshape(DEC_BATCH, DEC_SEQ, D_MODEL)
    y_prompt = xs[CONTEXT].reshape(BATCH, SEQ, D_MODEL)
    new_k = kv[0].reshape(BATCH, DEPTH, SEQ, NA_HEADS, NA_HEAD_DIM)
    new_v = kv[1].reshape(BATCH, DEPTH, SEQ, NA_HEADS, NA_HEAD_DIM)
    return (y_prompt, y_sample, new_k, new_v, states)
```
